```python
import math
import jax, jax.numpy as jnp
from jax import lax
import numpy as np

D_MODEL = 1024
BATCH = 2
SEQ = 8192
DEPTH = 1

CHUNK = 64
EPS = 1e-6

SSD_D_INNER = D_MODEL
SSD_HEAD_DIM = 64
SSD_HEADS = SSD_D_INNER // SSD_HEAD_DIM
SSD_GROUPS = 2
SSD_HEADS_PER_GROUP = SSD_HEADS // SSD_GROUPS
SSD_STATE = 128
SSD_CONV = 4
SSD_CONV_DIM = SSD_D_INNER + 2 * SSD_GROUPS * SSD_STATE

MLA_HEADS = 8
MLA_Q_RANK = 384
MLA_KV_RANK = 256
MLA_NOPE = 128
MLA_ROPE = 64
MLA_QK_DIM = MLA_NOPE + MLA_ROPE
MLA_V = 128
ROPE_BASE = 10000.0
Q_BLOCK = 128

N_EXPERTS = 32
TOP_K = 4
EXPERT_FF = D_MODEL
SWIGLU_ALPHA = 1.702
SWIGLU_LIMIT = 7.0
MOE_BLOCK = 128

N_BRANCHES = 2
OFF_XBC = SSD_D_INNER
OFF_DT = OFF_XBC + SSD_CONV_DIM
OFF_QLAT = OFF_DT + SSD_HEADS
OFF_KVLAT = OFF_QLAT + MLA_Q_RANK
OFF_KROPE = OFF_KVLAT + MLA_KV_RANK
OFF_GATE = OFF_KROPE + MLA_ROPE
IN_DIM = OFF_GATE + N_BRANCHES * D_MODEL

kernel_name = 'hybrid_ssd_mla_moe_block'


def rms_norm(x, gain):
    xf = x.astype(jnp.float32)
    y = xf * lax.rsqrt(jnp.mean(xf * xf, axis=-1, keepdims=True) + EPS)
    return y.astype(x.dtype) * gain


def rope_tables(seq):
    pos = jnp.arange(seq, dtype=jnp.float32)
    inv = ROPE_BASE ** (-jnp.arange(0, MLA_ROPE, 2, dtype=jnp.float32) / MLA_ROPE)
    ang = pos[:, None] * inv[None, :]
    return jnp.cos(ang), jnp.sin(ang)


def apply_rope(x, cos, sin):
    half = MLA_ROPE // 2
    x1, x2 = x[..., :half], x[..., half:]
    c = cos[None, :, None, :].astype(x.dtype)
    s = sin[None, :, None, :].astype(x.dtype)
    return jnp.concatenate([x1 * c - x2 * s, x2 * c + x1 * s], axis=-1)


def ssd_scan(x, dt, a, bm, cm):
    b, s = x.shape[0], x.shape[1]
    nc = s // CHUNK
    xc = x.reshape(b, nc, CHUNK, SSD_GROUPS, SSD_HEADS_PER_GROUP, SSD_HEAD_DIM)
    dtc = dt.reshape(b, nc, CHUNK, SSD_GROUPS, SSD_HEADS_PER_GROUP)
    xdt = xc * dtc[..., None]
    da = (dtc * a.reshape(SSD_GROUPS, SSD_HEADS_PER_GROUP)).transpose(0, 3, 4, 1, 2)
    bc = bm.reshape(b, nc, CHUNK, SSD_GROUPS, SSD_STATE)
    cc = cm.reshape(b, nc, CHUNK, SSD_GROUPS, SSD_STATE)
    a_cum = jnp.cumsum(da, axis=-1)
    seg = a_cum[..., :, None] - a_cum[..., None, :]
    tri = jnp.tril(jnp.ones((CHUNK, CHUNK), dtype=bool))
    lmat = jnp.exp(jnp.where(tri, seg, -jnp.inf))
    cb = jnp.einsum('bclgn,bcsgn->bgcls', cc, bc)
    y_diag = jnp.einsum('bgecls,bcsgep->bclgep', cb[:, :, None] * lmat, xdt)
    decay_states = jnp.exp(a_cum[..., -1:] - a_cum)
    states = jnp.einsum('bcsgn,bgecs,bcsgep->cbgepn', bc, decay_states, xdt)
    chunk_decay = jnp.moveaxis(jnp.exp(a_cum[..., -1]), -1, 0)

    def step(h, inp):
        st, dec = inp
        return dec[..., None, None] * h + st, h

    _, prev = lax.scan(step, jnp.zeros_like(states[0]), (states, chunk_decay))
    y_off = jnp.einsum('bclgn,cbgepn,bgecl->bclgep', cc, prev, jnp.exp(a_cum))
    return (y_diag + y_off).reshape(b, s, SSD_HEADS, SSD_HEAD_DIM)


def ssd_branch(z, xbc, dt_raw, conv_w, conv_b, dt_bias, a_log, d_skip, g_out):
    b, s, _ = xbc.shape
    xbc = lax.conv_general_dilated(
        xbc, conv_w[:, None, :].astype(xbc.dtype), window_strides=(1,),
        padding=[(SSD_CONV - 1, 0)], dimension_numbers=('NWC', 'WIO', 'NWC'),
        feature_group_count=SSD_CONV_DIM) + conv_b
    xbc = jax.nn.silu(xbc)
    xs = xbc[..., :SSD_D_INNER]
    bm = xbc[..., SSD_D_INNER:SSD_D_INNER + SSD_GROUPS * SSD_STATE]
    cm = xbc[..., SSD_D_INNER + SSD_GROUPS * SSD_STATE:]
    xh = xs.reshape(b, s, SSD_HEADS, SSD_HEAD_DIM).astype(jnp.float32)
    dt = jax.nn.softplus(dt_raw.astype(jnp.float32) + dt_bias.astype(jnp.float32))
    a = -jnp.exp(a_log.astype(jnp.float32))
    y = ssd_scan(xh, dt, a,
                 bm.reshape(b, s, SSD_GROUPS, SSD_STATE).astype(jnp.float32),
                 cm.reshape(b, s, SSD_GROUPS, SSD_STATE).astype(jnp.float32))
    y = y + d_skip.astype(jnp.float32)[:, None] * xh
    y = y.reshape(b, s, SSD_D_INNER).astype(z.dtype) * jax.nn.silu(z)
    y = rms_norm(y.reshape(b, s, SSD_GROUPS, SSD_D_INNER // SSD_GROUPS),
                 g_out.reshape(SSD_GROUPS, SSD_D_INNER // SSD_GROUPS))
    return y.reshape(b, s, SSD_D_INNER)


def mla_branch(q_lat, kv_lat, k_rope, g_q_lat, w_q_up, g_kv_lat, w_kv_up, g_qk_q, g_qk_k):
    b, s, _ = q_lat.shape
    q = (rms_norm(q_lat, g_q_lat) @ w_q_up).reshape(b, s, MLA_HEADS, MLA_QK_DIM)
    kv = (rms_norm(kv_lat, g_kv_lat) @ w_kv_up).reshape(b, s, MLA_HEADS, MLA_NOPE + MLA_V)
    k_nope, v = kv[..., :MLA_NOPE], kv[..., MLA_NOPE:]
    k = jnp.concatenate(
        [k_nope, jnp.broadcast_to(k_rope[:, :, None, :], (b, s, MLA_HEADS, MLA_ROPE))], axis=-1)
    q = rms_norm(q, g_qk_q)
    k = rms_norm(k, g_qk_k)
    cos, sin = rope_tables(s)
    q = jnp.concatenate([q[..., :MLA_NOPE], apply_rope(q[..., MLA_NOPE:], cos, sin)], axis=-1)
    k = jnp.concatenate([k[..., :MLA_NOPE], apply_rope(k[..., MLA_NOPE:], cos, sin)], axis=-1)
    q = q.transpose(0, 2, 1, 3)
    k = k.transpose(0, 2, 1, 3)
    v = v.transpose(0, 2, 1, 3)
    scale = MLA_QK_DIM ** -0.5
    chunk_id = jnp.arange(s) // CHUNK
    outs = []
    for j in range(s // Q_BLOCK):
        lo, hi = j * Q_BLOCK, (j + 1) * Q_BLOCK
        sc = jnp.einsum('bhqd,bhkd->bhqk', q[:, :, lo:hi], k[:, :, :hi]).astype(jnp.float32) * scale
        mask = chunk_id[:hi][None, :] <= chunk_id[lo:hi][:, None]
        p = jax.nn.softmax(jnp.where(mask, sc, -jnp.inf), axis=-1)
        outs.append(jnp.einsum('bhqk,bhkd->bhqd', p.astype(v.dtype), v[:, :, :hi]))
    o = jnp.concatenate(outs, axis=2)
    return o.transpose(0, 2, 1, 3).reshape(b, s, MLA_HEADS * MLA_V)


def moe(h, w_router, b_router, w_gate_up, b_gate_up, w_down, b_down):
    t, d = h.shape
    logits = (h @ w_router + b_router).astype(jnp.float32)
    top_vals, top_idx = lax.top_k(logits, TOP_K)
    gates = jax.nn.softmax(top_vals, axis=-1).astype(h.dtype)
    n_assign = t * TOP_K
    flat_e = top_idx.reshape(-1)
    order = jnp.argsort(flat_e)
    sorted_e = flat_e[order]
    tok = order // TOP_K
    gates_sorted = gates.reshape(-1)[order]
    counts = jnp.bincount(flat_e, length=N_EXPERTS)
    padded = ((counts + MOE_BLOCK - 1) // MOE_BLOCK) * MOE_BLOCK
    pad_end = jnp.cumsum(padded)
    pad_start = pad_end - padded
    start = jnp.cumsum(counts) - counts
    dest = pad_start[sorted_e] + (jnp.arange(n_assign) - start[sorted_e])
    n_blocks = -(-n_assign // MOE_BLOCK) + N_EXPERTS
    rows = n_blocks * MOE_BLOCK
    xs = jnp.zeros((rows, d), h.dtype).at[dest].set(h[tok])
    block_e = jnp.minimum(
        jnp.searchsorted(pad_end, jnp.arange(n_blocks) * MOE_BLOCK, side='right'), N_EXPERTS - 1)

    def expert_block(args):
        xb, e = args
        gu = xb @ w_gate_up[e] + b_gate_up[e]
        glu = jnp.minimum(gu[..., :EXPERT_FF], SWIGLU_LIMIT)
        lin = jnp.clip(gu[..., EXPERT_FF:], -SWIGLU_LIMIT, SWIGLU_LIMIT)
        act = glu * jax.nn.sigmoid(SWIGLU_ALPHA * glu) * (lin + 1.0)
        return act @ w_down[e] + b_down[e]

    ys = lax.map(expert_block, (xs.reshape(n_blocks, MOE_BLOCK, d), block_e)).reshape(rows, d)
    y_assign = ys[dest] * gates_sorted[:, None]
    return jax.ops.segment_sum(y_assign, tok, num_segments=t)


def setup_inputs(seed: int = 0) -> dict:
    key = jax.random.key(seed)
    ks = jax.random.split(key, 32)
    f32 = jnp.float32
    L = DEPTH

    def normal(k, shape, fan_in):
        return jax.random.normal(k, shape, f32) * fan_in ** -0.5

    def gain(k, shape):
        return 1.0 + 0.1 * jax.random.normal(k, shape, f32)

    def bias(k, shape):
        return 0.01 * jax.random.normal(k, shape, f32)

    dt0 = jnp.exp(jax.random.uniform(ks[5], (L, SSD_HEADS), f32, math.log(1e-3), math.log(1e-1)))
    return {
        'x': jax.random.normal(ks[0], (BATCH, SEQ, D_MODEL), f32),
        'g_mix': gain(ks[1], (L, D_MODEL)),
        'w_in': normal(ks[2], (L, D_MODEL, IN_DIM), D_MODEL),
        'conv_w': normal(ks[3], (L, SSD_CONV, SSD_CONV_DIM), SSD_CONV),
        'conv_b': bias(ks[4], (L, SSD_CONV_DIM)),
        'dt_bias': dt0 + jnp.log(-jnp.expm1(-dt0)),
        'a_log': jnp.log(jax.random.uniform(ks[6], (L, SSD_HEADS), f32, 1.0, 16.0)),
        'd_skip': gain(ks[7], (L, SSD_HEADS)),
        'g_ssd_out': gain(ks[8], (L, SSD_D_INNER)),
        'w_ssd_out': normal(ks[9], (L, SSD_D_INNER, D_MODEL), SSD_D_INNER),
        'g_q_lat': gain(ks[10], (L, MLA_Q_RANK)),
        'w_q_up': normal(ks[11], (L, MLA_Q_RANK, MLA_HEADS * MLA_QK_DIM), MLA_Q_RANK),
        'g_kv_lat': gain(ks[12], (L, MLA_KV_RANK)),
        'w_kv_up': normal(ks[13], (L, MLA_KV_RANK, MLA_HEADS * (MLA_NOPE + MLA_V)), MLA_KV_RANK),
        'g_qk_q': gain(ks[14], (L, MLA_QK_DIM)),
        'g_qk_k': gain(ks[15], (L, MLA_QK_DIM)),
        'w_mla_out': normal(ks[16], (L, MLA_HEADS * MLA_V, D_MODEL), MLA_HEADS * MLA_V),
        'w_o': normal(ks[17], (L, D_MODEL, D_MODEL), D_MODEL),
        'g_ffn': gain(ks[18], (L, D_MODEL)),
        'w_router': normal(ks[19], (L, D_MODEL, N_EXPERTS), D_MODEL),
        'b_router': bias(ks[20], (L, N_EXPERTS)),
        'w_gate_up': normal(ks[21], (L, N_EXPERTS, D_MODEL, 2 * EXPERT_FF), D_MODEL),
        'b_gate_up': bias(ks[22], (L, N_EXPERTS, 2 * EXPERT_FF)),
        'w_down': normal(ks[23], (L, N_EXPERTS, EXPERT_FF, D_MODEL), EXPERT_FF),
        'b_down': bias(ks[24], (L, N_EXPERTS, D_MODEL)),
    }


def reference(x, g_mix, w_in, conv_w, conv_b, dt_bias, a_log, d_skip, g_ssd_out, w_ssd_out,
              g_q_lat, w_q_up, g_kv_lat, w_kv_up, g_qk_q, g_qk_k, w_mla_out, w_o,
              g_ffn, w_router, b_router, w_gate_up, b_gate_up, w_down, b_down):
    b, s, d = x.shape
    for l in range(DEPTH):
        h = rms_norm(x, g_mix[l])
        proj = h @ w_in[l]
        z = proj[..., :OFF_XBC]
        xbc = proj[..., OFF_XBC:OFF_DT]
        dt_raw = proj[..., OFF_DT:OFF_QLAT]
        q_lat = proj[..., OFF_QLAT:OFF_KVLAT]
        kv_lat = proj[..., OFF_KVLAT:OFF_KROPE]
        k_rope = proj[..., OFF_KROPE:OFF_GATE]
        gate_logits = proj[..., OFF_GATE:]
        y_ssd = ssd_branch(z, xbc, dt_raw, conv_w[l], conv_b[l], dt_bias[l], a_log[l],
                           d_skip[l], g_ssd_out[l]) @ w_ssd_out[l]
        y_mla = mla_branch(q_lat, kv_lat, k_rope, g_q_lat[l], w_q_up[l], g_kv_lat[l],
                           w_kv_up[l], g_qk_q[l], g_qk_k[l]) @ w_mla_out[l]
        g = jax.nn.sigmoid(gate_logits.astype(jnp.float32)).astype(x.dtype)
        merged = g[..., :d] * y_ssd + g[..., d:] * y_mla
        x = x + merged @ w_o[l]
        h2 = rms_norm(x, g_ffn[l]).reshape(b * s, d)
        x = x + moe(h2, w_router[l], b_router[l], w_gate_up[l], b_gate_up[l],
                    w_down[l], b_down[l]).reshape(b, s, d)
    return x
```

```python
import functools
import math

import jax
import jax.numpy as jnp
from jax import lax
from jax.experimental import pallas as pl
from jax.experimental.pallas import tpu as pltpu

F32 = jnp.float32
BF16 = jnp.bfloat16
I32 = jnp.int32

EPS = 1e-6
CHUNK = 64

SSD_HEADS = 16
SSD_HEAD_DIM = 64
SSD_GROUPS = 2
SSD_STATE = 128
SSD_CONV = 4
SSD_D_INNER = SSD_HEADS * SSD_HEAD_DIM
SSD_CONV_DIM = SSD_D_INNER + 2 * SSD_GROUPS * SSD_STATE

MLA_HEADS = 8
MLA_Q_RANK = 384
MLA_KV_RANK = 256
MLA_NOPE = 128
MLA_ROPE = 64
MLA_QK_DIM = MLA_NOPE + MLA_ROPE
MLA_V = 128
ROPE_BASE = 10000.0

N_EXPERTS = 32
TOP_K = 4
SWIGLU_ALPHA = 1.702
SWIGLU_LIMIT = 7.0

LANES = 128
VMEM_LIMIT = 56 * 1024 * 1024
NEG_BIG = -1e30

MOE_BLOCK = 256


def _cparams(semantics, **kw):
    return pltpu.CompilerParams(dimension_semantics=semantics,
                                vmem_limit_bytes=VMEM_LIMIT, **kw)


def _sigmoid(v):
    return 1.0 / (1.0 + jnp.exp(-v))


def _silu(v):
    return v * _sigmoid(v)


def _inproj_kernel(x_ref, g_ref, w_ref, z_ref, xbc_ref, ql_ref, kvl_ref, krdt_ref, gate_ref, *, segs):
    x = x_ref[...]
    ms = jnp.mean(x * x, axis=-1, keepdims=True)
    h = (x * lax.rsqrt(ms + EPS) * g_ref[...]).astype(BF16)
    outs = (z_ref, xbc_ref, ql_ref, kvl_ref, krdt_ref, gate_ref)
    for ref, (lo, hi) in zip(outs, segs):
        p = jnp.dot(h, w_ref[:, lo:hi], preferred_element_type=F32)
        if ref is gate_ref:
            p = _sigmoid(p)
        ref[...] = p.astype(ref.dtype)


def _in_proj(x2, g_mix, w_cat, widths, tm):
    t, d = x2.shape
    offs = [0]
    for w in widths:
        offs.append(offs[-1] + w)
    segs = tuple((offs[i], offs[i + 1]) for i in range(len(widths)))
    dts = (BF16, BF16, BF16, BF16, F32, BF16)
    out_shape = tuple(jax.ShapeDtypeStruct((t, w), dt) for w, dt in zip(widths, dts))
    out_specs = tuple(pl.BlockSpec((tm, w), lambda i: (i, 0)) for w in widths)
    return pl.pallas_call(
        functools.partial(_inproj_kernel, segs=segs),
        grid=(t // tm,),
        in_specs=[pl.BlockSpec((tm, d), lambda i: (i, 0)),
                  pl.BlockSpec((1, d), lambda i: (0, 0)),
                  pl.BlockSpec(w_cat.shape, lambda i: (0, 0))],
        out_specs=out_specs,
        out_shape=out_shape,
        compiler_params=_cparams(("parallel",)),
        name="in_proj",
    )(x2, g_mix.reshape(1, d), w_cat)


def _ssd_kernel(xbc_ref, z_ref, krdt_ref, convw_ref, convb_ref, dtb_ref, alog_ref, dskip_ref, gout_ref,
                y_ref, xext_scr, state_scr, *, lt):
    i = pl.program_id(1)
    halo = 8
    dt_lo = MLA_ROPE
    gw = SSD_D_INNER // SSD_GROUPS
    hpg = SSD_HEADS // SSD_GROUPS

    @pl.when(i == 0)
    def _():
        state_scr[...] = jnp.zeros_like(state_scr)
        xext_scr[0:halo, :] = jnp.zeros((halo, SSD_CONV_DIM), F32)

    xext_scr[halo:halo + lt, :] = xbc_ref[0].astype(F32)
    acc = jnp.broadcast_to(convb_ref[...], (lt, SSD_CONV_DIM))
    for j in range(SSD_CONV):
        acc = acc + convw_ref[j:j + 1, :] * xext_scr[pl.ds(halo - (SSD_CONV - 1) + j, lt), :]
    xext_scr[0:halo, :] = xext_scr[lt:lt + halo, :]
    xbc = _silu(acc)
    xs = xbc[:, :SSD_D_INNER]
    bm = xbc[:, SSD_D_INNER:SSD_D_INNER + SSD_GROUPS * SSD_STATE]
    cm = xbc[:, SSD_D_INNER + SSD_GROUPS * SSD_STATE:]

    lane = lax.broadcasted_iota(I32, (1, LANES), 1)
    head_lane = (lane >= dt_lo) & (lane < dt_lo + SSD_HEADS)
    v = krdt_ref[0] + dtb_ref[...]
    dt = jnp.maximum(v, 0.0) + jnp.log(1.0 + jnp.exp(-jnp.abs(v)))
    dt = jnp.where(head_lane, dt, 0.0)
    a = jnp.where(head_lane, -jnp.exp(alog_ref[...]), 0.0)
    da = dt * a
    row = lax.broadcasted_iota(I32, (lt, lt), 0)
    col = lax.broadcasted_iota(I32, (lt, lt), 1)
    tri = row >= col
    a_cum = jnp.dot(tri.astype(F32), da, preferred_element_type=F32,
                    precision=lax.Precision.HIGHEST)
    a_cum_t = a_cum.T
    a_last = a_cum[lt - 1:lt, :]
    exp_a = jnp.exp(a_cum)
    dec = jnp.exp(a_last - a_cum)

    er = lax.broadcasted_iota(I32, (LANES, SSD_D_INNER), 0)
    ec = lax.broadcasted_iota(I32, (LANES, SSD_D_INNER), 1)
    expand = ((er - dt_lo) == (ec // SSD_HEAD_DIM)).astype(F32)

    def widen(t):
        return jnp.dot(t, expand, preferred_element_type=F32, precision=lax.Precision.HIGHEST)

    dt_w = widen(dt)
    exp_a_w = widen(exp_a)
    dec_w = widen(dec)
    xdt = xs * dt_w
    xdt_b = xdt.astype(BF16)
    xdec_b = (xdt * dec_w).astype(BF16)
    lane_pair = lax.broadcasted_iota(I32, (lt, LANES), 1)

    y_parts = []
    for g in range(SSD_GROUPS):
        bg = bm[:, g * SSD_STATE:(g + 1) * SSD_STATE].astype(BF16)
        cg = cm[:, g * SSD_STATE:(g + 1) * SSD_STATE].astype(BF16)
        cb = lax.dot_general(cg, bg, (((1,), (1,)), ((), ())), preferred_element_type=F32)
        for pair in range(hpg // 2):
            h0 = g * hpg + 2 * pair
            rhs = xdt_b[:, h0 * SSD_HEAD_DIM:(h0 + 2) * SSD_HEAD_DIM]
            res = []
            for h in (h0, h0 + 1):
                seg = a_cum[:, dt_lo + h:dt_lo + h + 1] - a_cum_t[dt_lo + h:dt_lo + h + 1, :]
                lmat = jnp.exp(jnp.where(tri, seg, NEG_BIG))
                res.append(jnp.dot((cb * lmat).astype(BF16), rhs, preferred_element_type=F32))
            y_parts.append(jnp.where(lane_pair < SSD_HEAD_DIM, res[0], res[1]))
        st = state_scr[g]
        y_off = jnp.dot(cg, st.astype(BF16), preferred_element_type=F32)
        y_parts.append(y_off * exp_a_w[:, g * gw:(g + 1) * gw])
        new = lax.dot_general(bg, xdec_b[:, g * gw:(g + 1) * gw], (((0,), (0,)), ((), ())),
                              preferred_element_type=F32)
        state_scr[g] = st * exp_a_w[lt - 1:lt, g * gw:(g + 1) * gw] + new

    zs = _silu(z_ref[0].astype(F32))
    outs = []
    npg = hpg // 2 + 1
    for g in range(SSD_GROUPS):
        parts = y_parts[g * npg:(g + 1) * npg]
        yd = jnp.concatenate(parts[:-1], axis=-1)
        yg = yd + parts[-1] + dskip_ref[:, g * gw:(g + 1) * gw] * xs[:, g * gw:(g + 1) * gw]
        yg = yg * zs[:, g * gw:(g + 1) * gw]
        ms = jnp.mean(yg * yg, axis=-1, keepdims=True)
        outs.append(yg * lax.rsqrt(ms + EPS) * gout_ref[:, g * gw:(g + 1) * gw])
    y_ref[0] = jnp.concatenate(outs, axis=-1).astype(y_ref.dtype)


def _ssd(xbc, z, krdt, conv_w, conv_b, dtb, alog, dskip_w, g_out, lt):
    b, s, _ = xbc.shape
    gw = SSD_D_INNER // SSD_GROUPS
    full = lambda shape: pl.BlockSpec(shape, lambda bi, i: (0,) * len(shape))
    return pl.pallas_call(
        functools.partial(_ssd_kernel, lt=lt),
        grid=(b, s // lt),
        in_specs=[pl.BlockSpec((1, lt, SSD_CONV_DIM), lambda bi, i: (bi, i, 0)),
                  pl.BlockSpec((1, lt, SSD_D_INNER), lambda bi, i: (bi, i, 0)),
                  pl.BlockSpec((1, lt, LANES), lambda bi, i: (bi, i, 0)),
                  full((SSD_CONV, SSD_CONV_DIM)), full((1, SSD_CONV_DIM)),
                  full((1, LANES)), full((1, LANES)),
                  full((1, SSD_D_INNER)), full((1, SSD_D_INNER))],
        out_specs=pl.BlockSpec((1, lt, SSD_D_INNER), lambda bi, i: (bi, i, 0)),
        out_shape=jax.ShapeDtypeStruct((b, s, SSD_D_INNER), BF16),
        scratch_shapes=[pltpu.VMEM((lt + 8, SSD_CONV_DIM), F32),
                        pltpu.VMEM((SSD_GROUPS, SSD_STATE, gw), F32)],
        compiler_params=_cparams(("parallel", "arbitrary")),
        name="ssd_scan",
    )(xbc, z, krdt, conv_w, conv_b, dtb, alog, dskip_w, g_out)


def _swap_halves(t):
    half = MLA_ROPE // 2
    return pltpu.roll(t, half, 1) + pltpu.roll(t, LANES - half, 1)


def _qkv_kernel(ql_ref, kvl_ref, krdt_ref, cos_ref, sin_ref, gql_ref, wq_ref, gkvl_ref, wkv_ref,
                gq_ref, gk_ref, q_ref, k_ref, v_ref, *, scale):
    hw = 2 * LANES
    cosv = cos_ref[...]
    sinv = sin_ref[...]

    ql = ql_ref[...].astype(F32)
    ms = jnp.mean(ql * ql, axis=-1, keepdims=True)
    qn = (ql * lax.rsqrt(ms + EPS) * gql_ref[...]).astype(BF16)
    qf = jnp.dot(qn, wq_ref[...], preferred_element_type=F32)

    kvl = kvl_ref[...].astype(F32)
    ms = jnp.mean(kvl * kvl, axis=-1, keepdims=True)
    kvn = (kvl * lax.rsqrt(ms + EPS) * gkvl_ref[...]).astype(BF16)
    kvf = jnp.dot(kvn, wkv_ref[...], preferred_element_type=F32)

    lane = lax.broadcasted_iota(I32, (1, LANES), 1)
    kr = jnp.where(lane < MLA_ROPE, krdt_ref[...], 0.0)
    ss_r = jnp.sum(kr * kr, axis=-1, keepdims=True)
    krg = kr * gk_ref[:, LANES:]
    kr_rot = krg * cosv + _swap_halves(krg) * sinv

    for h in range(MLA_HEADS):
        qh = qf[:, h * hw:(h + 1) * hw]
        r = lax.rsqrt(jnp.sum(qh * qh, axis=-1, keepdims=True) * (1.0 / MLA_QK_DIM) + EPS)
        qs = qh * (r * scale) * gq_ref[...]
        q_ref[0, h, :, 0:LANES] = qs[:, :LANES].astype(q_ref.dtype)
        qr = qs[:, LANES:]
        qr = qr * cosv + _swap_halves(qr) * sinv
        q_ref[0, h, :, LANES:MLA_QK_DIM] = qr[:, :MLA_ROPE].astype(q_ref.dtype)

        kn = kvf[:, h * hw:h * hw + LANES]
        rk = lax.rsqrt((jnp.sum(kn * kn, axis=-1, keepdims=True) + ss_r) * (1.0 / MLA_QK_DIM) + EPS)
        k_ref[0, h, :, 0:LANES] = (kn * rk * gk_ref[:, :LANES]).astype(k_ref.dtype)
        k_ref[0, h, :, LANES:MLA_QK_DIM] = (kr_rot * rk)[:, :MLA_ROPE].astype(k_ref.dtype)
        v_ref[0, h] = kvf[:, h * hw + LANES:(h + 1) * hw].astype(v_ref.dtype)


def _qkv_prep(ql, kvl, krdt, cos_t, sin_t, g_q_lat, wq_pad, g_kv_lat, wkv, gq_pad, gk_pad, b, s, tm):
    t = b * s
    nst = s // tm
    full = lambda shape: pl.BlockSpec(shape, lambda bi, i: (0,) * len(shape))
    tok = lambda w: pl.BlockSpec((tm, w), lambda bi, i: (bi * nst + i, 0))
    hs = lambda w: pl.BlockSpec((1, MLA_HEADS, tm, w), lambda bi, i: (bi, 0, i, 0))
    return pl.pallas_call(
        functools.partial(_qkv_kernel, scale=MLA_QK_DIM ** -0.5),
        grid=(b, nst),
        in_specs=[tok(MLA_Q_RANK), tok(MLA_KV_RANK), tok(LANES),
                  pl.BlockSpec((tm, LANES), lambda bi, i: (i, 0)),
                  pl.BlockSpec((tm, LANES), lambda bi, i: (i, 0)),
                  full((1, MLA_Q_RANK)), full(wq_pad.shape), full((1, MLA_KV_RANK)), full(wkv.shape),
                  full((1, 2 * LANES)), full((1, 2 * LANES))],
        out_specs=(hs(MLA_QK_DIM), hs(MLA_QK_DIM), hs(MLA_V)),
        out_shape=(jax.ShapeDtypeStruct((b, MLA_HEADS, s, MLA_QK_DIM), BF16),
                   jax.ShapeDtypeStruct((b, MLA_HEADS, s, MLA_QK_DIM), BF16),
                   jax.ShapeDtypeStruct((b, MLA_HEADS, s, MLA_V), BF16)),
        compiler_params=_cparams(("parallel", "parallel")),
        name="qkv_prep",
    )(ql, kvl, krdt, cos_t, sin_t, g_q_lat, wq_pad, g_kv_lat, wkv, gq_pad, gk_pad)


def _flash_kernel(q_ref, k_ref, v_ref, o_ref, m_scr, l_scr, acc_scr, *, tq):
    i = pl.program_id(2)
    q = q_ref[0, 0]
    m_scr[...] = jnp.full(m_scr.shape, NEG_BIG, F32)
    l_scr[...] = jnp.zeros(l_scr.shape, F32)
    acc_scr[...] = jnp.zeros(acc_scr.shape, F32)

    def tile(j, masked):
        start = pl.multiple_of(j * tq, tq)
        ks = k_ref[0, 0, pl.ds(start, tq), :]
        vs = v_ref[0, 0, pl.ds(start, tq), :]
        s = lax.dot_general(q, ks, (((1,), (1,)), ((), ())), preferred_element_type=F32)
        if masked:
            row = lax.broadcasted_iota(I32, (tq, tq), 0) // CHUNK
            col = lax.broadcasted_iota(I32, (tq, tq), 1) // CHUNK
            s = jnp.where(col <= row, s, NEG_BIG)
        m_prev = m_scr[...]
        m_new = jnp.maximum(m_prev, jnp.max(s, axis=-1, keepdims=True))
        alpha = jnp.exp(m_prev - m_new)
        p = jnp.exp(s - m_new)
        l_scr[...] = alpha * l_scr[...] + jnp.sum(p, axis=-1, keepdims=True)
        acc_scr[...] = alpha * acc_scr[...] + jnp.dot(p.astype(BF16), vs, preferred_element_type=F32)
        m_scr[...] = m_new

    def body(j, carry):
        tile(j, False)
        return carry

    lax.fori_loop(0, i, body, 0)
    tile(i, True)
    o_ref[0] = (acc_scr[...] / l_scr[...]).astype(o_ref.dtype)


def _flash(q, k, v, tq):
    b, nh, s, _ = q.shape
    return pl.pallas_call(
        functools.partial(_flash_kernel, tq=tq),
        grid=(b, nh, s // tq),
        in_specs=[pl.BlockSpec((1, 1, tq, MLA_QK_DIM), lambda bi, h, i: (bi, h, i, 0)),
                  pl.BlockSpec((1, 1, s, MLA_QK_DIM), lambda bi, h, i: (bi, h, 0, 0)),
                  pl.BlockSpec((1, 1, s, MLA_V), lambda bi, h, i: (bi, h, 0, 0))],
        out_specs=pl.BlockSpec((1, tq, MLA_V), lambda bi, h, i: (bi, i, h)),
        out_shape=jax.ShapeDtypeStruct((b, s, nh * MLA_V), BF16),
        scratch_shapes=[pltpu.VMEM((tq, 1), F32), pltpu.VMEM((tq, 1), F32), pltpu.VMEM((tq, MLA_V), F32)],
        compiler_params=_cparams(("parallel", "parallel", "arbitrary")),
        name="flash_attn",
    )(q, k, v)


def _merge_kernel(x_ref, ys_ref, om_ref, gate_ref, wss_ref, wml_ref, wo_ref, gffn_ref, wrt_ref, brt_ref,
                  x1_ref, h2_ref, idx_ref, gates_ref, rank_ref, cnt_ref, carry_scr, *, tm):
    d = x_ref.shape[-1]

    @pl.when(pl.program_id(0) == 0)
    def _():
        carry_scr[...] = jnp.zeros_like(carry_scr)

    y1 = jnp.dot(ys_ref[...], wss_ref[...], preferred_element_type=F32)
    y2 = jnp.dot(om_ref[...], wml_ref[...], preferred_element_type=F32)
    g = gate_ref[...].astype(F32)
    merged = (g[:, :d] * y1 + g[:, d:] * y2).astype(BF16)
    x1 = x_ref[...] + jnp.dot(merged, wo_ref[...], preferred_element_type=F32)
    x1_ref[...] = x1
    ms = jnp.mean(x1 * x1, axis=-1, keepdims=True)
    h2 = x1 * lax.rsqrt(ms + EPS) * gffn_ref[...]
    h2_ref[...] = h2

    logits = lax.dot_general(wrt_ref[...], h2, (((1,), (1,)), ((), ())), preferred_element_type=F32,
                             precision=lax.Precision.HIGHEST) + brt_ref[...]
    eid = lax.broadcasted_iota(I32, (N_EXPERTS, tm), 0)
    cur = logits
    onehot = jnp.zeros((N_EXPERTS, tm), F32)
    vals, sels = [], []
    for k in range(TOP_K):
        mx = jnp.max(cur, axis=0, keepdims=True)
        idx = jnp.min(jnp.where(cur == mx, eid, N_EXPERTS), axis=0, keepdims=True)
        sel = eid == idx
        vals.append(mx)
        sels.append(sel)
        idx_ref[k:k + 1, :] = idx
        cur = jnp.where(sel, -jnp.inf, cur)
        onehot = onehot + sel.astype(F32)
    es = [jnp.exp(vk - vals[0]) for vk in vals]
    den = es[0] + es[1] + es[2] + es[3]
    for k in range(TOP_K):
        gates_ref[k:k + 1, :] = es[k] / den

    r = lax.broadcasted_iota(I32, (tm, tm), 0)
    c = lax.broadcasted_iota(I32, (tm, tm), 1)
    before = (r < c).astype(BF16)
    prefix = jnp.dot(onehot.astype(BF16), before, preferred_element_type=F32) + carry_scr[...]
    for k in range(TOP_K):
        rank_ref[k:k + 1, :] = jnp.sum(jnp.where(sels[k], prefix, 0.0), axis=0, keepdims=True).astype(I32)
    carry = carry_scr[...] + jnp.sum(onehot, axis=1, keepdims=True)
    carry_scr[...] = carry
    cnt_ref[...] = jnp.broadcast_to(carry, cnt_ref.shape)


def _merge(x2, y_ssd, o_mla, gate, w_ss, w_ml, w_o, g_ffn, w_rt, b_rt, tm):
    t, d = x2.shape
    full = lambda shape: pl.BlockSpec(shape, lambda i: (0,) * len(shape))
    tok = lambda w: pl.BlockSpec((tm, w), lambda i: (i, 0))
    sel = pl.BlockSpec((TOP_K, tm), lambda i: (0, i))
    return pl.pallas_call(
        functools.partial(_merge_kernel, tm=tm),
        grid=(t // tm,),
        in_specs=[tok(d), tok(d), tok(d), tok(2 * d), full((d, d)), full((d, d)), full((d, d)),
                  full((1, d)), full((N_EXPERTS, d)), full((N_EXPERTS, 1))],
        out_specs=(tok(d), tok(d), sel, sel, sel, full((N_EXPERTS, LANES))),
        out_shape=(jax.ShapeDtypeStruct((t, d), F32), jax.ShapeDtypeStruct((t, d), F32),
                   jax.ShapeDtypeStruct((TOP_K, t), I32), jax.ShapeDtypeStruct((TOP_K, t), F32),
                   jax.ShapeDtypeStruct((TOP_K, t), I32), jax.ShapeDtypeStruct((N_EXPERTS, LANES), F32)),
        scratch_shapes=[pltpu.VMEM((N_EXPERTS, 1), F32)],
        compiler_params=_cparams(("arbitrary",)),
        name="merge_route",
    )(x2, y_ssd, o_mla, gate, w_ss, w_ml, w_o, g_ffn, w_rt, b_rt)


def _tables_kernel(cnt_ref, idx_ref, rank_ref, dest_ref, blk_ref, seg_ref, *, nblk_pad):
    cnt = cnt_ref[...]
    padded = jnp.ceil(cnt * (1.0 / MOE_BLOCK)) * MOE_BLOCK
    r = lax.broadcasted_iota(I32, (N_EXPERTS, N_EXPERTS), 0)
    c = lax.broadcasted_iota(I32, (N_EXPERTS, N_EXPERTS), 1)
    start = jnp.dot((c < r).astype(F32), padded, preferred_element_type=F32,
                    precision=lax.Precision.HIGHEST)
    end = start + padded
    idx = idx_ref[...]
    dest = rank_ref[...]
    for e in range(N_EXPERTS):
        dest = dest + jnp.where(idx == e, start[e:e + 1, 0:1].astype(I32), 0)
    dest_ref[...] = dest
    pos = (lax.broadcasted_iota(I32, (N_EXPERTS, nblk_pad), 1) * MOE_BLOCK).astype(F32)
    nle = jnp.sum((end[:, 0:1] <= pos).astype(I32), axis=0, keepdims=True)
    blk_ref[...] = jnp.minimum(nle, N_EXPERTS - 1)
    lane = lax.broadcasted_iota(I32, (N_EXPERTS, LANES), 1)
    total = jnp.max(end, axis=0, keepdims=True)
    seg = jnp.where(lane == 0, start + cnt, jnp.where(lane == 1, end, total))
    seg_ref[...] = seg.astype(I32)


def _tables(cnt, idx_t, rank_t, nblk_pad):
    t = idx_t.shape[1]
    return pl.pallas_call(
        functools.partial(_tables_kernel, nblk_pad=nblk_pad),
        out_shape=(jax.ShapeDtypeStruct((TOP_K, t), I32),
                   jax.ShapeDtypeStruct((1, nblk_pad), I32),
                   jax.ShapeDtypeStruct((N_EXPERTS, LANES), I32)),
        compiler_params=pltpu.CompilerParams(vmem_limit_bytes=VMEM_LIMIT),
        name="route_tables",
    )(cnt, idx_t, rank_t)


def _row_copy(src_ref, src_row, dst_ref, dst_row, sem):
    return pltpu.make_async_copy(src_ref.at[pl.ds(src_row, 1), :], dst_ref.at[pl.ds(dst_row, 1), :], sem)


def _dispatch_kernel(dest_ref, padlo_ref, padhi_ref, used_ref, h_ref, zero_ref, xs_ref, sem, zsem, *, tt, t):
    i = pl.program_id(0)

    @pl.when(i == 0)
    def _():
        nblk = xs_ref.shape[0] // MOE_BLOCK

        def blk_copy(b):
            start = pl.multiple_of(b * MOE_BLOCK, MOE_BLOCK)
            return pltpu.make_async_copy(zero_ref, xs_ref.at[pl.ds(start, MOE_BLOCK), :], zsem)

        def issue_blk(b, carry):
            blk_copy(b).start()
            return carry

        def drain_blk(b, carry):
            blk_copy(b).wait()
            return carry

        lax.fori_loop(used_ref[0], nblk, issue_blk, 0)
        lax.fori_loop(used_ref[0], nblk, drain_blk, 0)

        for e in range(N_EXPERTS):
            lo = padlo_ref[e]
            hi = padhi_ref[e]

            def issue(r, carry):
                _row_copy(zero_ref, 0, xs_ref, r, zsem).start()
                return carry

            def drain(r, carry):
                _row_copy(zero_ref, 0, xs_ref, r, zsem).wait()
                return carry

            lax.fori_loop(lo, hi, issue, 0)
            lax.fori_loop(lo, hi, drain, 0)

    base = i * tt

    def issue(j, carry):
        tok = base + j
        for k in range(TOP_K):
            _row_copy(h_ref, tok, xs_ref, dest_ref[k * t + tok], sem).start()
        return carry

    lax.fori_loop(0, tt, issue, 0)
    for k in range(TOP_K):
        pltpu.make_async_copy(h_ref.at[pl.ds(0, tt), :], xs_ref.at[pl.ds(0, tt), :], sem).wait()


def _dispatch(dest_flat, pad_lo, pad_hi, used, h2, rows, tt):
    t, d = h2.shape
    zero_row = jnp.zeros((MOE_BLOCK, d), h2.dtype)
    return pl.pallas_call(
        functools.partial(_dispatch_kernel, tt=tt, t=t),
        grid_spec=pltpu.PrefetchScalarGridSpec(
            num_scalar_prefetch=4,
            grid=(t // tt,),
            in_specs=[pl.BlockSpec(memory_space=pl.ANY), pl.BlockSpec(memory_space=pl.ANY)],
            out_specs=pl.BlockSpec(memory_space=pl.ANY),
            scratch_shapes=[pltpu.SemaphoreType.DMA(()), pltpu.SemaphoreType.DMA(())]),
        out_shape=jax.ShapeDtypeStruct((rows, d), h2.dtype),
        compiler_params=_cparams(("arbitrary",)),
        name="moe_dispatch",
    )(dest_flat, pad_lo, pad_hi, used, h2, zero_row)


def _expert_kernel(blk_ref, used_ref, x_ref, wgu_ref, bgu_ref, wd_ref, bd_ref, y_ref, wgu_scr, wd_scr):
    b = pl.program_id(0)
    ff = wd_ref.shape[1]
    prev = blk_ref[jnp.maximum(b - 1, 0)]
    changed = (b == 0) | (blk_ref[b] != prev)

    @pl.when(changed)
    def _():
        wgu_scr[...] = wgu_ref[0].astype(BF16)
        wd_scr[...] = wd_ref[0].astype(BF16)

    @pl.when(b < used_ref[0])
    def _():
        xb = x_ref[...].astype(BF16)
        gu = jnp.dot(xb, wgu_scr[...], preferred_element_type=F32) + bgu_ref[0]
        glu = jnp.minimum(gu[:, :ff], SWIGLU_LIMIT)
        lin = jnp.clip(gu[:, ff:], -SWIGLU_LIMIT, SWIGLU_LIMIT)
        act = glu * _sigmoid(SWIGLU_ALPHA * glu) * (lin + 1.0)
        y_ref[...] = jnp.dot(act.astype(BF16), wd_scr[...], preferred_element_type=F32) + bd_ref[0]

    @pl.when(b >= used_ref[0])
    def _():
        y_ref[...] = jnp.zeros_like(y_ref)


def _experts(blk_e, used, xs, w_gate_up, b_gate_up, w_down, b_down):
    rows, d = xs.shape
    ne, _, ff2 = w_gate_up.shape
    ff = ff2 // 2
    nblk = rows // MOE_BLOCK
    row_map = lambda b, blk, used: (jnp.minimum(b, used[0] - 1), 0)
    return pl.pallas_call(
        _expert_kernel,
        grid_spec=pltpu.PrefetchScalarGridSpec(
            num_scalar_prefetch=2,
            grid=(nblk,),
            in_specs=[pl.BlockSpec((MOE_BLOCK, d), row_map),
                      pl.BlockSpec((1, d, ff2), lambda b, blk, used: (blk[b], 0, 0)),
                      pl.BlockSpec((1, 1, ff2), lambda b, blk, used: (blk[b], 0, 0)),
                      pl.BlockSpec((1, ff, d), lambda b, blk, used: (blk[b], 0, 0)),
                      pl.BlockSpec((1, 1, d), lambda b, blk, used: (blk[b], 0, 0))],
            out_specs=pl.BlockSpec((MOE_BLOCK, d), lambda b, blk, used: (b, 0)),
            scratch_shapes=[pltpu.VMEM((d, ff2), BF16), pltpu.VMEM((ff, d), BF16)]),
        out_shape=jax.ShapeDtypeStruct((rows, d), F32),
        compiler_params=_cparams(("arbitrary",)),
        name="moe_experts",
    )(blk_e, used, xs, w_gate_up, b_gate_up.reshape(ne, 1, ff2), w_down, b_down.reshape(ne, 1, d))


def _combine_kernel(dest_ref, x1_ref, gates_ref, ys_ref, o_ref, buf, sem, *, tt, t):
    base = pl.program_id(0) * tt

    def issue(j, carry):
        tok = base + j
        for k in range(TOP_K):
            pltpu.make_async_copy(ys_ref.at[pl.ds(dest_ref[k * t + tok], 1), :],
                                  buf.at[k, pl.ds(j, 1), :], sem).start()
        return carry

    lax.fori_loop(0, tt, issue, 0)
    for k in range(TOP_K):
        pltpu.make_async_copy(ys_ref.at[pl.ds(0, tt), :], buf.at[k], sem).wait()
    g = gates_ref[...]
    acc = x1_ref[...]
    for k in range(TOP_K):
        acc = acc + g[:, k:k + 1] * buf[k]
    o_ref[...] = acc


def _combine(dest_flat, x1, gates_tk, ys, tt):
    t, d = x1.shape
    return pl.pallas_call(
        functools.partial(_combine_kernel, tt=tt, t=t),
        grid_spec=pltpu.PrefetchScalarGridSpec(
            num_scalar_prefetch=1,
            grid=(t // tt,),
            in_specs=[pl.BlockSpec((tt, d), lambda i, dest: (i, 0)),
                      pl.BlockSpec((tt, TOP_K), lambda i, dest: (i, 0)),
                      pl.BlockSpec(memory_space=pl.ANY)],
            out_specs=pl.BlockSpec((tt, d), lambda i, dest: (i, 0)),
            scratch_shapes=[pltpu.VMEM((TOP_K, tt, d), F32), pltpu.SemaphoreType.DMA(())]),
        out_shape=jax.ShapeDtypeStruct((t, d), F32),
        compiler_params=_cparams(("arbitrary",)),
        name="moe_combine",
    )(dest_flat, x1, gates_tk, ys)


def _rope_tables(s):
    pos = jnp.arange(s, dtype=F32)
    inv = ROPE_BASE ** (-jnp.arange(0, MLA_ROPE, 2, dtype=F32) / MLA_ROPE)
    ang = pos[:, None] * inv[None, :]
    cos, sin = jnp.cos(ang), jnp.sin(ang)
    zeros = jnp.zeros((s, LANES - MLA_ROPE), F32)
    return (jnp.concatenate([cos, cos, zeros], axis=-1),
            jnp.concatenate([-sin, sin, zeros], axis=-1))


def _pad_heads(w, width):
    lead = w.shape[:-1]
    w = w.reshape(lead + (MLA_HEADS, width))
    w = jnp.pad(w, [(0, 0)] * len(lead) + [(0, 0), (0, 2 * LANES - width)])
    return w.reshape(lead + (MLA_HEADS * 2 * LANES,))


def _layer(x, g_mix, w_in, conv_w, conv_b, dt_bias, a_log, d_skip, g_ssd_out, w_ssd_out,
           g_q_lat, w_q_up, g_kv_lat, w_kv_up, g_qk_q, g_qk_k, w_mla_out, w_o,
           g_ffn, w_router, b_router, w_gate_up, b_gate_up, w_down, b_down):
    b, s, d = x.shape
    t = b * s
    x2 = x.reshape(t, d)

    off_xbc = SSD_D_INNER
    off_dt = off_xbc + SSD_CONV_DIM
    off_ql = off_dt + SSD_HEADS
    off_kvl = off_ql + MLA_Q_RANK
    off_kr = off_kvl + MLA_KV_RANK
    off_gate = off_kr + MLA_ROPE
    pad = LANES - MLA_ROPE - SSD_HEADS
    w_cat = jnp.concatenate([
        w_in[:, :off_dt], w_in[:, off_ql:off_kr],
        w_in[:, off_kr:off_gate], w_in[:, off_dt:off_ql], jnp.zeros((d, pad), w_in.dtype),
        w_in[:, off_gate:]], axis=1).astype(BF16)
    widths = (SSD_D_INNER, SSD_CONV_DIM, MLA_Q_RANK, MLA_KV_RANK, LANES, 2 * d)

    tm = min(512, s)
    z, xbc, ql, kvl, krdt, gate = _in_proj(x2, g_mix, w_cat, widths, tm)

    lane_pad = lambda vec: jnp.pad(vec, (MLA_ROPE, LANES - MLA_ROPE - SSD_HEADS)).reshape(1, LANES)
    lt = min(256, s)
    y_ssd = _ssd(xbc.reshape(b, s, SSD_CONV_DIM), z.reshape(b, s, SSD_D_INNER), krdt.reshape(b, s, LANES),
                 conv_w, conv_b.reshape(1, -1), lane_pad(dt_bias), lane_pad(a_log),
                 jnp.repeat(d_skip, SSD_HEAD_DIM).reshape(1, -1), g_ssd_out.reshape(1, -1), lt)

    cos_t, sin_t = _rope_tables(s)
    wq_pad = _pad_heads(w_q_up, MLA_QK_DIM).astype(BF16)
    gq_pad = jnp.pad(g_qk_q, (0, 2 * LANES - MLA_QK_DIM)).reshape(1, -1)
    gk_pad = jnp.pad(g_qk_k, (0, 2 * LANES - MLA_QK_DIM)).reshape(1, -1)
    tq = min(256, s)
    q, k, v = _qkv_prep(ql, kvl, krdt, cos_t, sin_t, g_q_lat.reshape(1, -1), wq_pad,
                        g_kv_lat.reshape(1, -1), w_kv_up.astype(BF16), gq_pad, gk_pad, b, s, tq)
    o_mla = _flash(q, k, v, min(512, s))

    x1, h2, idx_t, gates_t, rank_t, cnt = _merge(
        x2, y_ssd.reshape(t, d), o_mla.reshape(t, d), gate,
        w_ssd_out.astype(BF16), w_mla_out.astype(BF16), w_o.astype(BF16),
        g_ffn.reshape(1, -1), w_router.T, b_router.reshape(-1, 1), min(512, t))

    nblk = t * TOP_K // MOE_BLOCK + N_EXPERTS
    rows = nblk * MOE_BLOCK
    nblk_pad = -(-nblk // LANES) * LANES
    dest_t, blk_e, seg = _tables(cnt, idx_t, rank_t, nblk_pad)
    dest_flat = dest_t.reshape(-1)
    used = (seg[0:1, 2] // MOE_BLOCK).astype(I32)
    xs = _dispatch(dest_flat, seg[:, 0], seg[:, 1], used, h2, rows, min(512, t))
    ys = _experts(blk_e.reshape(-1)[:nblk], used, xs, w_gate_up, b_gate_up, w_down, b_down)
    out = _combine(dest_flat, x1, gates_t.T, ys, min(256, t))
    return out.reshape(b, s, d)


def kernel(x, g_mix, w_in, conv_w, conv_b, dt_bias, a_log, d_skip, g_ssd_out, w_ssd_out, g_q_lat, w_q_up, g_kv_lat, w_kv_up, g_qk_q, g_qk_k, w_mla_out, w_o, g_ffn, w_router, b_router, w_gate_up, b_gate_up, w_down, b_down):
    params = (g_mix, w_in, conv_w, conv_b, dt_bias, a_log, d_skip, g_ssd_out, w_ssd_out, g_q_lat, w_q_up,
              g_kv_lat, w_kv_up, g_qk_q, g_qk_k, w_mla_out, w_o, g_ffn, w_router, b_router,
              w_gate_up, b_gate_up, w_down, b_down)
    for l in range(g_mix.shape[0]):
        x = _layer(x, *(p[l] for p in params))
    return x
```

```python
import functools
import math

import jax
import jax.numpy as jnp
from jax import lax
from jax.experimental import pallas as pl
from jax.experimental.pallas import tpu as pltpu

F32 = jnp.float32
BF16 = jnp.bfloat16
I32 = jnp.int32

EPS = 1e-6
CHUNK = 64

SSD_HEADS = 16
SSD_HEAD_DIM = 64
SSD_GROUPS = 2
SSD_STATE = 128
SSD_CONV = 4
SSD_D_INNER = SSD_HEADS * SSD_HEAD_DIM
SSD_CONV_DIM = SSD_D_INNER + 2 * SSD_GROUPS * SSD_STATE

MLA_HEADS = 8
MLA_Q_RANK = 384
MLA_KV_RANK = 256
MLA_NOPE = 128
MLA_ROPE = 64
MLA_QK_DIM = MLA_NOPE + MLA_ROPE
MLA_V = 128
ROPE_BASE = 10000.0

N_EXPERTS = 32
TOP_K = 4
SWIGLU_ALPHA = 1.702
SWIGLU_LIMIT = 7.0

LANES = 128
VMEM_LIMIT = 56 * 1024 * 1024
NEG_BIG = -1e30

MOE_BLOCK = 256


def _cparams(semantics, **kw):
    return pltpu.CompilerParams(dimension_semantics=semantics,
                                vmem_limit_bytes=VMEM_LIMIT, **kw)


def _sigmoid(v):
    return 1.0 / (1.0 + jnp.exp(-v))


def _silu(v):
    return v * _sigmoid(v)


def _inproj_kernel(x_ref, g_ref, w_ref, z_ref, xbc_ref, ql_ref, kvl_ref, krdt_ref, gate_ref, *, segs):
    x = x_ref[...]
    ms = jnp.mean(x * x, axis=-1, keepdims=True)
    h = (x * lax.rsqrt(ms + EPS) * g_ref[...]).astype(BF16)
    outs = (z_ref, xbc_ref, ql_ref, kvl_ref, krdt_ref, gate_ref)
    for ref, (lo, hi) in zip(outs, segs):
        p = jnp.dot(h, w_ref[:, lo:hi], preferred_element_type=F32)
        if ref is gate_ref:
            p = _sigmoid(p)
        ref[...] = p.astype(ref.dtype)


def _in_proj(x2, g_mix, w_cat, widths, tm):
    t, d = x2.shape
    offs = [0]
    for w in widths:
        offs.append(offs[-1] + w)
    segs = tuple((offs[i], offs[i + 1]) for i in range(len(widths)))
    dts = (BF16, BF16, BF16, BF16, F32, BF16)
    out_shape = tuple(jax.ShapeDtypeStruct((t, w), dt) for w, dt in zip(widths, dts))
    out_specs = tuple(pl.BlockSpec((tm, w), lambda i: (i, 0)) for w in widths)
    return pl.pallas_call(
        functools.partial(_inproj_kernel, segs=segs),
        grid=(t // tm,),
        in_specs=[pl.BlockSpec((tm, d), lambda i: (i, 0)),
                  pl.BlockSpec((1, d), lambda i: (0, 0)),
                  pl.BlockSpec(w_cat.shape, lambda i: (0, 0))],
        out_specs=out_specs,
        out_shape=out_shape,
        compiler_params=_cparams(("parallel",)),
        name="in_proj",
    )(x2, g_mix.reshape(1, d), w_cat)


def _ssd_kernel(xbc_ref, z_ref, krdt_ref, convw_ref, convb_ref, dtb_ref, alog_ref, dskip_ref, gout_ref,
                y_ref, xext_scr, state_scr, *, lt):
    i = pl.program_id(1)
    halo = 8
    dt_lo = MLA_ROPE
    gw = SSD_D_INNER // SSD_GROUPS
    hpg = SSD_HEADS // SSD_GROUPS

    @pl.when(i == 0)
    def _():
        state_scr[...] = jnp.zeros_like(state_scr)
        xext_scr[0:halo, :] = jnp.zeros((halo, SSD_CONV_DIM), F32)

    xext_scr[halo:halo + lt, :] = xbc_ref[0].astype(F32)
    acc = jnp.broadcast_to(convb_ref[...], (lt, SSD_CONV_DIM))
    for j in range(SSD_CONV):
        acc = acc + convw_ref[j:j + 1, :] * xext_scr[pl.ds(halo - (SSD_CONV - 1) + j, lt), :]
    xext_scr[0:halo, :] = xext_scr[lt:lt + halo, :]
    xbc = _silu(acc)
    xs = xbc[:, :SSD_D_INNER]
    bm = xbc[:, SSD_D_INNER:SSD_D_INNER + SSD_GROUPS * SSD_STATE]
    cm = xbc[:, SSD_D_INNER + SSD_GROUPS * SSD_STATE:]

    lane = lax.broadcasted_iota(I32, (1, LANES), 1)
    head_lane = (lane >= dt_lo) & (lane < dt_lo + SSD_HEADS)
    v = krdt_ref[0] + dtb_ref[...]
    dt = jnp.maximum(v, 0.0) + jnp.log(1.0 + jnp.exp(-jnp.abs(v)))
    dt = jnp.where(head_lane, dt, 0.0)
    a = jnp.where(head_lane, -jnp.exp(alog_ref[...]), 0.0)
    da = dt * a
    row = lax.broadcasted_iota(I32, (lt, lt), 0)
    col = lax.broadcasted_iota(I32, (lt, lt), 1)
    tri = row >= col
    a_cum = jnp.dot(tri.astype(F32), da, preferred_element_type=F32,
                    precision=lax.Precision.HIGHEST)
    a_cum_t = a_cum.T
    a_last = a_cum[lt - 1:lt, :]
    exp_a = jnp.exp(a_cum)
    dec = jnp.exp(a_last - a_cum)

    er = lax.broadcasted_iota(I32, (LANES, SSD_D_INNER), 0)
    ec = lax.broadcasted_iota(I32, (LANES, SSD_D_INNER), 1)
    expand = ((er - dt_lo) == (ec // SSD_HEAD_DIM)).astype(F32)

    def widen(t):
        return jnp.dot(t, expand, preferred_element_type=F32, precision=lax.Precision.HIGHEST)

    dt_w = widen(dt)
    exp_a_w = widen(exp_a)
    dec_w = widen(dec)
    xdt = xs * dt_w
    xdt_b = xdt.astype(BF16)
    xdec_b = (xdt * dec_w).astype(BF16)
    lane_pair = lax.broadcasted_iota(I32, (lt, LANES), 1)

    y_parts = []
    for g in range(SSD_GROUPS):
        bg = bm[:, g * SSD_STATE:(g + 1) * SSD_STATE].astype(BF16)
        cg = cm[:, g * SSD_STATE:(g + 1) * SSD_STATE].astype(BF16)
        cb = lax.dot_general(cg, bg, (((1,), (1,)), ((), ())), preferred_element_type=F32)
        for pair in range(hpg // 2):
            h0 = g * hpg + 2 * pair
            rhs = xdt_b[:, h0 * SSD_HEAD_DIM:(h0 + 2) * SSD_HEAD_DIM]
            res = []
            for h in (h0, h0 + 1):
                seg = a_cum[:, dt_lo + h:dt_lo + h + 1] - a_cum_t[dt_lo + h:dt_lo + h + 1, :]
                lmat = jnp.exp(jnp.where(tri, seg, NEG_BIG))
                res.append(jnp.dot((cb * lmat).astype(BF16), rhs, preferred_element_type=F32))
            y_parts.append(jnp.where(lane_pair < SSD_HEAD_DIM, res[0], res[1]))
        st = state_scr[g]
        y_off = jnp.dot(cg, st.astype(BF16), preferred_element_type=F32)
        y_parts.append(y_off * exp_a_w[:, g * gw:(g + 1) * gw])
        new = lax.dot_general(bg, xdec_b[:, g * gw:(g + 1) * gw], (((0,), (0,)), ((), ())),
                              preferred_element_type=F32)
        state_scr[g] = st * exp_a_w[lt - 1:lt, g * gw:(g + 1) * gw] + new

    zs = _silu(z_ref[0].astype(F32))
    outs = []
    npg = hpg // 2 + 1
    for g in range(SSD_GROUPS):
        parts = y_parts[g * npg:(g + 1) * npg]
        yd = jnp.concatenate(parts[:-1], axis=-1)
        yg = yd + parts[-1] + dskip_ref[:, g * gw:(g + 1) * gw] * xs[:, g * gw:(g + 1) * gw]
        yg = yg * zs[:, g * gw:(g + 1) * gw]
        ms = jnp.mean(yg * yg, axis=-1, keepdims=True)
        outs.append(yg * lax.rsqrt(ms + EPS) * gout_ref[:, g * gw:(g + 1) * gw])
    y_ref[0] = jnp.concatenate(outs, axis=-1).astype(y_ref.dtype)


def _ssd(xbc, z, krdt, conv_w, conv_b, dtb, alog, dskip_w, g_out, lt):
    b, s, _ = xbc.shape
    gw = SSD_D_INNER // SSD_GROUPS
    full = lambda shape: pl.BlockSpec(shape, lambda bi, i: (0,) * len(shape))
    return pl.pallas_call(
        functools.partial(_ssd_kernel, lt=lt),
        grid=(b, s // lt),
        in_specs=[pl.BlockSpec((1, lt, SSD_CONV_DIM), lambda bi, i: (bi, i, 0)),
                  pl.BlockSpec((1, lt, SSD_D_INNER), lambda bi, i: (bi, i, 0)),
                  pl.BlockSpec((1, lt, LANES), lambda bi, i: (bi, i, 0)),
                  full((SSD_CONV, SSD_CONV_DIM)), full((1, SSD_CONV_DIM)),
                  full((1, LANES)), full((1, LANES)),
                  full((1, SSD_D_INNER)), full((1, SSD_D_INNER))],
        out_specs=pl.BlockSpec((1, lt, SSD_D_INNER), lambda bi, i: (bi, i, 0)),
        out_shape=jax.ShapeDtypeStruct((b, s, SSD_D_INNER), BF16),
        scratch_shapes=[pltpu.VMEM((lt + 8, SSD_CONV_DIM), F32),
                        pltpu.VMEM((SSD_GROUPS, SSD_STATE, gw), F32)],
        compiler_params=_cparams(("parallel", "arbitrary")),
        name="ssd_scan",
    )(xbc, z, krdt, conv_w, conv_b, dtb, alog, dskip_w, g_out)


def _swap_halves(t):
    half = MLA_ROPE // 2
    return pltpu.roll(t, half, 1) + pltpu.roll(t, LANES - half, 1)


def _qkv_kernel(ql_ref, kvl_ref, krdt_ref, cos_ref, sin_ref, gql_ref, wq_ref, gkvl_ref, wkv_ref,
                gq_ref, gk_ref, q_ref, k_ref, v_ref, *, scale):
    hw = 2 * LANES
    cosv = cos_ref[...]
    sinv = sin_ref[...]

    ql = ql_ref[...].astype(F32)
    ms = jnp.mean(ql * ql, axis=-1, keepdims=True)
    qn = (ql * lax.rsqrt(ms + EPS) * gql_ref[...]).astype(BF16)
    qf = jnp.dot(qn, wq_ref[...], preferred_element_type=F32)

    kvl = kvl_ref[...].astype(F32)
    ms = jnp.mean(kvl * kvl, axis=-1, keepdims=True)
    kvn = (kvl * lax.rsqrt(ms + EPS) * gkvl_ref[...]).astype(BF16)
    kvf = jnp.dot(kvn, wkv_ref[...], preferred_element_type=F32)

    lane = lax.broadcasted_iota(I32, (1, LANES), 1)
    kr = jnp.where(lane < MLA_ROPE, krdt_ref[...], 0.0)
    ss_r = jnp.sum(kr * kr, axis=-1, keepdims=True)
    krg = kr * gk_ref[:, LANES:]
    kr_rot = krg * cosv + _swap_halves(krg) * sinv

    for h in range(MLA_HEADS):
        qh = qf[:, h * hw:(h + 1) * hw]
        r = lax.rsqrt(jnp.sum(qh * qh, axis=-1, keepdims=True) * (1.0 / MLA_QK_DIM) + EPS)
        qs = qh * (r * scale) * gq_ref[...]
        q_ref[0, h, :, 0:LANES] = qs[:, :LANES].astype(q_ref.dtype)
        qr = qs[:, LANES:]
        qr = qr * cosv + _swap_halves(qr) * sinv
        q_ref[0, h, :, LANES:MLA_QK_DIM] = qr[:, :MLA_ROPE].astype(q_ref.dtype)

        kn = kvf[:, h * hw:h * hw + LANES]
        rk = lax.rsqrt((jnp.sum(kn * kn, axis=-1, keepdims=True) + ss_r) * (1.0 / MLA_QK_DIM) + EPS)
        k_ref[0, h, :, 0:LANES] = (kn * rk * gk_ref[:, :LANES]).astype(k_ref.dtype)
        k_ref[0, h, :, LANES:MLA_QK_DIM] = (kr_rot * rk)[:, :MLA_ROPE].astype(k_ref.dtype)
        v_ref[0, h] = kvf[:, h * hw + LANES:(h + 1) * hw].T.astype(v_ref.dtype)


def _qkv_prep(ql, kvl, krdt, cos_t, sin_t, g_q_lat, wq_pad, g_kv_lat, wkv, gq_pad, gk_pad, b, s, tm):
    t = b * s
    nst = s // tm
    full = lambda shape: pl.BlockSpec(shape, lambda bi, i: (0,) * len(shape))
    tok = lambda w: pl.BlockSpec((tm, w), lambda bi, i: (bi * nst + i, 0))
    hs = lambda w: pl.BlockSpec((1, MLA_HEADS, tm, w), lambda bi, i: (bi, 0, i, 0))
    return pl.pallas_call(
        functools.partial(_qkv_kernel, scale=MLA_QK_DIM ** -0.5 * math.log2(math.e)),
        grid=(b, nst),
        in_specs=[tok(MLA_Q_RANK), tok(MLA_KV_RANK), tok(LANES),
                  pl.BlockSpec((tm, LANES), lambda bi, i: (i, 0)),
                  pl.BlockSpec((tm, LANES), lambda bi, i: (i, 0)),
                  full((1, MLA_Q_RANK)), full(wq_pad.shape), full((1, MLA_KV_RANK)), full(wkv.shape),
                  full((1, 2 * LANES)), full((1, 2 * LANES))],
        out_specs=(hs(MLA_QK_DIM), hs(MLA_QK_DIM),
                   pl.BlockSpec((1, MLA_HEADS, MLA_V, tm), lambda bi, i: (bi, 0, 0, i))),
        out_shape=(jax.ShapeDtypeStruct((b, MLA_HEADS, s, MLA_QK_DIM), BF16),
                   jax.ShapeDtypeStruct((b, MLA_HEADS, s, MLA_QK_DIM), BF16),
                   jax.ShapeDtypeStruct((b, MLA_HEADS, MLA_V, s), BF16)),
        compiler_params=_cparams(("parallel", "parallel")),
        name="qkv_prep",
    )(ql, kvl, krdt, cos_t, sin_t, g_q_lat, wq_pad, g_kv_lat, wkv, gq_pad, gk_pad)


def _flash_kernel(q_ref, k_ref, vt_ref, o_ref, m_scr, l_scr, acc_scr, s_scr, *, tq):
    i = pl.program_id(2)
    q = q_ref[0, 0]
    m_scr[...] = jnp.full(m_scr.shape, NEG_BIG, F32)
    l_scr[...] = jnp.zeros(l_scr.shape, F32)
    acc_scr[...] = jnp.zeros(acc_scr.shape, F32)

    def scores(j, slot):
        start = pl.multiple_of(j * tq, tq)
        ks = k_ref[0, 0, pl.ds(start, tq), :]
        s_scr[slot] = lax.dot_general(ks, q, (((1,), (1,)), ((), ())), preferred_element_type=F32)

    def softmax_pv(j, slot, masked):
        start = pl.multiple_of(j * tq, tq)
        vt = vt_ref[0, 0, :, pl.ds(start, tq)]
        st = s_scr[slot]
        if masked:
            kc = lax.broadcasted_iota(I32, (tq, tq), 0) // CHUNK
            qc = lax.broadcasted_iota(I32, (tq, tq), 1) // CHUNK
            st = jnp.where(kc <= qc, st, NEG_BIG)
        m_prev = m_scr[...]
        m_new = jnp.maximum(m_prev, jnp.max(st, axis=0, keepdims=True))
        alpha = jnp.exp2(m_prev - m_new)
        pt = jnp.exp2(st - m_new)
        l_scr[...] = alpha * l_scr[...] + jnp.sum(pt, axis=0, keepdims=True)
        acc_scr[...] = alpha * acc_scr[...] + jnp.dot(vt, pt.astype(BF16), preferred_element_type=F32)
        m_scr[...] = m_new

    scores(0, 0)

    def body(jj, carry):
        j = 2 * jj
        scores(j + 1, 1)
        softmax_pv(j, 0, False)
        scores(j + 2, 0)
        softmax_pv(j + 1, 1, False)
        return carry

    lax.fori_loop(0, i // 2, body, 0)

    @pl.when(i % 2 == 0)
    def _():
        softmax_pv(i, 0, True)

    @pl.when(i % 2 == 1)
    def _():
        scores(i, 1)
        softmax_pv(i - 1, 0, False)
        softmax_pv(i, 1, True)

    o_ref[0] = (acc_scr[...] / l_scr[...]).T.astype(o_ref.dtype)


def _flash(q, k, v, tq):
    b, nh, s, _ = q.shape
    return pl.pallas_call(
        functools.partial(_flash_kernel, tq=tq),
        grid=(b, nh, s // tq),
        in_specs=[pl.BlockSpec((1, 1, tq, MLA_QK_DIM), lambda bi, h, i: (bi, h, i, 0)),
                  pl.BlockSpec((1, 1, s, MLA_QK_DIM), lambda bi, h, i: (bi, h, 0, 0)),
                  pl.BlockSpec((1, 1, MLA_V, s), lambda bi, h, i: (bi, h, 0, 0))],
        out_specs=pl.BlockSpec((1, tq, MLA_V), lambda bi, h, i: (bi, i, h)),
        out_shape=jax.ShapeDtypeStruct((b, s, nh * MLA_V), BF16),
        scratch_shapes=[pltpu.VMEM((1, tq), F32), pltpu.VMEM((1, tq), F32), pltpu.VMEM((MLA_V, tq), F32),
                        pltpu.VMEM((2, tq, tq), F32)],
        compiler_params=_cparams(("parallel", "parallel", "arbitrary")),
        name="flash_attn",
    )(q, k, v)


def _merge_kernel(x_ref, ys_ref, om_ref, gate_ref, wss_ref, wml_ref, wo_ref, gffn_ref, wrt_ref, brt_ref,
                  x1_ref, h2_ref, idx_ref, gates_ref, rank_ref, cnt_ref, carry_scr, *, tm):
    d = x_ref.shape[-1]

    @pl.when(pl.program_id(0) == 0)
    def _():
        carry_scr[...] = jnp.zeros_like(carry_scr)

    y1 = jnp.dot(ys_ref[...], wss_ref[...], preferred_element_type=F32)
    y2 = jnp.dot(om_ref[...], wml_ref[...], preferred_element_type=F32)
    g = gate_ref[...].astype(F32)
    merged = (g[:, :d] * y1 + g[:, d:] * y2).astype(BF16)
    x1 = x_ref[...] + jnp.dot(merged, wo_ref[...], preferred_element_type=F32)
    x1_ref[...] = x1
    ms = jnp.mean(x1 * x1, axis=-1, keepdims=True)
    h2 = x1 * lax.rsqrt(ms + EPS) * gffn_ref[...]
    h2_ref[...] = h2

    logits = lax.dot_general(wrt_ref[...], h2, (((1,), (1,)), ((), ())), preferred_element_type=F32,
                             precision=lax.Precision.HIGHEST) + brt_ref[...]
    eid = lax.broadcasted_iota(I32, (N_EXPERTS, tm), 0)
    cur = logits
    onehot = jnp.zeros((N_EXPERTS, tm), F32)
    vals, sels = [], []
    for k in range(TOP_K):
        mx = jnp.max(cur, axis=0, keepdims=True)
        idx = jnp.min(jnp.where(cur == mx, eid, N_EXPERTS), axis=0, keepdims=True)
        sel = eid == idx
        vals.append(mx)
        sels.append(sel)
        idx_ref[k:k + 1, :] = idx
        cur = jnp.where(sel, -jnp.inf, cur)
        onehot = onehot + sel.astype(F32)
    es = [jnp.exp(vk - vals[0]) for vk in vals]
    den = es[0] + es[1] + es[2] + es[3]
    for k in range(TOP_K):
        gates_ref[k:k + 1, :] = es[k] / den

    r = lax.broadcasted_iota(I32, (tm, tm), 0)
    c = lax.broadcasted_iota(I32, (tm, tm), 1)
    before = (r < c).astype(BF16)
    prefix = jnp.dot(onehot.astype(BF16), before, preferred_element_type=F32) + carry_scr[...]
    for k in range(TOP_K):
        rank_ref[k:k + 1, :] = jnp.sum(jnp.where(sels[k], prefix, 0.0), axis=0, keepdims=True).astype(I32)
    carry = carry_scr[...] + jnp.sum(onehot, axis=1, keepdims=True)
    carry_scr[...] = carry
    cnt_ref[...] = jnp.broadcast_to(carry, cnt_ref.shape)


def _merge(x2, y_ssd, o_mla, gate, w_ss, w_ml, w_o, g_ffn, w_rt, b_rt, tm):
    t, d = x2.shape
    full = lambda shape: pl.BlockSpec(shape, lambda i: (0,) * len(shape))
    tok = lambda w: pl.BlockSpec((tm, w), lambda i: (i, 0))
    sel = pl.BlockSpec((TOP_K, tm), lambda i: (0, i))
    return pl.pallas_call(
        functools.partial(_merge_kernel, tm=tm),
        grid=(t // tm,),
        in_specs=[tok(d), tok(d), tok(d), tok(2 * d), full((d, d)), full((d, d)), full((d, d)),
                  full((1, d)), full((N_EXPERTS, d)), full((N_EXPERTS, 1))],
        out_specs=(tok(d), tok(d), sel, sel, sel, full((N_EXPERTS, LANES))),
        out_shape=(jax.ShapeDtypeStruct((t, d), F32), jax.ShapeDtypeStruct((t, d), F32),
                   jax.ShapeDtypeStruct((TOP_K, t), I32), jax.ShapeDtypeStruct((TOP_K, t), F32),
                   jax.ShapeDtypeStruct((TOP_K, t), I32), jax.ShapeDtypeStruct((N_EXPERTS, LANES), F32)),
        scratch_shapes=[pltpu.VMEM((N_EXPERTS, 1), F32)],
        compiler_params=_cparams(("arbitrary",)),
        name="merge_route",
    )(x2, y_ssd, o_mla, gate, w_ss, w_ml, w_o, g_ffn, w_rt, b_rt)


def _tables_kernel(cnt_ref, idx_ref, rank_ref, dest_ref, blk_ref, seg_ref, *, nblk_pad):
    cnt = cnt_ref[...]
    padded = jnp.ceil(cnt * (1.0 / MOE_BLOCK)) * MOE_BLOCK
    r = lax.broadcasted_iota(I32, (N_EXPERTS, N_EXPERTS), 0)
    c = lax.broadcasted_iota(I32, (N_EXPERTS, N_EXPERTS), 1)
    start = jnp.dot((c < r).astype(F32), padded, preferred_element_type=F32,
                    precision=lax.Precision.HIGHEST)
    end = start + padded
    idx = idx_ref[...]
    dest = rank_ref[...]
    for e in range(N_EXPERTS):
        dest = dest + jnp.where(idx == e, start[e:e + 1, 0:1].astype(I32), 0)
    dest_ref[...] = dest
    pos = (lax.broadcasted_iota(I32, (N_EXPERTS, nblk_pad), 1) * MOE_BLOCK).astype(F32)
    nle = jnp.sum((end[:, 0:1] <= pos).astype(I32), axis=0, keepdims=True)
    blk_ref[...] = jnp.minimum(nle, N_EXPERTS - 1)
    lane = lax.broadcasted_iota(I32, (N_EXPERTS, LANES), 1)
    total = jnp.max(end, axis=0, keepdims=True)
    seg = jnp.where(lane == 0, start + cnt, jnp.where(lane == 1, end, total))
    seg_ref[...] = seg.astype(I32)


def _tables(cnt, idx_t, rank_t, nblk_pad):
    t = idx_t.shape[1]
    return pl.pallas_call(
        functools.partial(_tables_kernel, nblk_pad=nblk_pad),
        out_shape=(jax.ShapeDtypeStruct((TOP_K, t), I32),
                   jax.ShapeDtypeStruct((1, nblk_pad), I32),
                   jax.ShapeDtypeStruct((N_EXPERTS, LANES), I32)),
        compiler_params=pltpu.CompilerParams(vmem_limit_bytes=VMEM_LIMIT),
        name="route_tables",
    )(cnt, idx_t, rank_t)


def _row_copy(src_ref, src_row, dst_ref, dst_row, sem):
    return pltpu.make_async_copy(src_ref.at[pl.ds(src_row, 1), :], dst_ref.at[pl.ds(dst_row, 1), :], sem)


def _dispatch_kernel(dest_ref, padlo_ref, padhi_ref, used_ref, h_ref, xs_ref, zero_ref, sem, zsem, *, tt, t):
    i = pl.program_id(0)

    @pl.when(i == 0)
    def _():
        nblk = xs_ref.shape[0] // MOE_BLOCK
        zero_ref[...] = jnp.zeros_like(zero_ref)

        def blk_copy(b):
            start = pl.multiple_of(b * MOE_BLOCK, MOE_BLOCK)
            return pltpu.make_async_copy(zero_ref, xs_ref.at[pl.ds(start, MOE_BLOCK), :], zsem)

        def issue_blk(b, carry):
            blk_copy(b).start()
            return carry

        def drain_blk(b, carry):
            blk_copy(b).wait()
            return carry

        lax.fori_loop(used_ref[0], nblk, issue_blk, 0)
        lax.fori_loop(used_ref[0], nblk, drain_blk, 0)

        for e in range(N_EXPERTS):
            lo = padlo_ref[e]
            hi = padhi_ref[e]

            def issue(r, carry):
                _row_copy(zero_ref, 0, xs_ref, r, zsem).start()
                return carry

            def drain(r, carry):
                _row_copy(zero_ref, 0, xs_ref, r, zsem).wait()
                return carry

            lax.fori_loop(lo, hi, issue, 0)
            lax.fori_loop(lo, hi, drain, 0)

    base = i * tt

    def issue(j, carry):
        for k in range(TOP_K):
            _row_copy(h_ref, j, xs_ref, dest_ref[k * t + base + j], sem).start()
        return carry

    lax.fori_loop(0, tt, issue, 0)
    for k in range(TOP_K):
        pltpu.make_async_copy(h_ref, xs_ref.at[pl.ds(0, tt), :], sem).wait()


def _dispatch(dest_flat, pad_lo, pad_hi, used, h2, rows, tt):
    t, d = h2.shape
    return pl.pallas_call(
        functools.partial(_dispatch_kernel, tt=tt, t=t),
        grid_spec=pltpu.PrefetchScalarGridSpec(
            num_scalar_prefetch=4,
            grid=(t // tt,),
            in_specs=[pl.BlockSpec((tt, d), lambda i, *_: (i, 0))],
            out_specs=pl.BlockSpec(memory_space=pl.ANY),
            scratch_shapes=[pltpu.VMEM((MOE_BLOCK, d), h2.dtype),
                            pltpu.SemaphoreType.DMA(()), pltpu.SemaphoreType.DMA(())]),
        out_shape=jax.ShapeDtypeStruct((rows, d), h2.dtype),
        compiler_params=_cparams(("arbitrary",)),
        name="moe_dispatch",
    )(dest_flat, pad_lo, pad_hi, used, h2)


def _expert_kernel(blk_ref, used_ref, x_ref, wgu_ref, bgu_ref, wd_ref, bd_ref, y_ref, wgu_scr, wd_scr):
    b = pl.program_id(0)
    ff = wd_ref.shape[1]
    prev = blk_ref[jnp.maximum(b - 1, 0)]
    changed = (b == 0) | (blk_ref[b] != prev)

    @pl.when(changed)
    def _():
        wgu_scr[...] = wgu_ref[0].astype(BF16)
        wd_scr[...] = wd_ref[0].astype(BF16)

    @pl.when(b < used_ref[0])
    def _():
        xb = x_ref[...].astype(BF16)
        gu = jnp.dot(xb, wgu_scr[...], preferred_element_type=F32) + bgu_ref[0]
        glu = jnp.minimum(gu[:, :ff], SWIGLU_LIMIT)
        lin = jnp.clip(gu[:, ff:], -SWIGLU_LIMIT, SWIGLU_LIMIT)
        act = glu * _sigmoid(SWIGLU_ALPHA * glu) * (lin + 1.0)
        y_ref[...] = jnp.dot(act.astype(BF16), wd_scr[...], preferred_element_type=F32) + bd_ref[0]

    @pl.when(b >= used_ref[0])
    def _():
        y_ref[...] = jnp.zeros_like(y_ref)


def _experts(blk_e, used, xs, w_gate_up, b_gate_up, w_down, b_down):
    rows, d = xs.shape
    ne, _, ff2 = w_gate_up.shape
    ff = ff2 // 2
    nblk = rows // MOE_BLOCK
    row_map = lambda b, blk, used: (jnp.minimum(b, used[0] - 1), 0)
    return pl.pallas_call(
        _expert_kernel,
        grid_spec=pltpu.PrefetchScalarGridSpec(
            num_scalar_prefetch=2,
            grid=(nblk,),
            in_specs=[pl.BlockSpec((MOE_BLOCK, d), row_map),
                      pl.BlockSpec((1, d, ff2), lambda b, blk, used: (blk[b], 0, 0)),
                      pl.BlockSpec((1, 1, ff2), lambda b, blk, used: (blk[b], 0, 0)),
                      pl.BlockSpec((1, ff, d), lambda b, blk, used: (blk[b], 0, 0)),
                      pl.BlockSpec((1, 1, d), lambda b, blk, used: (blk[b], 0, 0))],
            out_specs=pl.BlockSpec((MOE_BLOCK, d), lambda b, blk, used: (b, 0)),
            scratch_shapes=[pltpu.VMEM((d, ff2), BF16), pltpu.VMEM((ff, d), BF16)]),
        out_shape=jax.ShapeDtypeStruct((rows, d), F32),
        compiler_params=_cparams(("arbitrary",)),
        name="moe_experts",
    )(blk_e, used, xs, w_gate_up, b_gate_up.reshape(ne, 1, ff2), w_down, b_down.reshape(ne, 1, d))


def _combine_kernel(dest_ref, x1_ref, gates_ref, ys_ref, o_ref, buf, sem, *, tt, t):
    base = pl.program_id(0) * tt

    def issue(j, carry):
        tok = base + j
        for k in range(TOP_K):
            pltpu.make_async_copy(ys_ref.at[pl.ds(dest_ref[k * t + tok], 1), :],
                                  buf.at[k, pl.ds(j, 1), :], sem).start()
        return carry

    lax.fori_loop(0, tt, issue, 0)
    for k in range(TOP_K):
        pltpu.make_async_copy(ys_ref.at[pl.ds(0, tt), :], buf.at[k], sem).wait()
    g = gates_ref[...]
    acc = x1_ref[...]
    for k in range(TOP_K):
        acc = acc + g[:, k:k + 1] * buf[k]
    o_ref[...] = acc


def _combine(dest_flat, x1, gates_tk, ys, tt):
    t, d = x1.shape
    return pl.pallas_call(
        functools.partial(_combine_kernel, tt=tt, t=t),
        grid_spec=pltpu.PrefetchScalarGridSpec(
            num_scalar_prefetch=1,
            grid=(t // tt,),
            in_specs=[pl.BlockSpec((tt, d), lambda i, dest: (i, 0)),
                      pl.BlockSpec((tt, TOP_K), lambda i, dest: (i, 0)),
                      pl.BlockSpec(memory_space=pl.ANY)],
            out_specs=pl.BlockSpec((tt, d), lambda i, dest: (i, 0)),
            scratch_shapes=[pltpu.VMEM((TOP_K, tt, d), F32), pltpu.SemaphoreType.DMA(())]),
        out_shape=jax.ShapeDtypeStruct((t, d), F32),
        compiler_params=_cparams(("arbitrary",)),
        name="moe_combine",
    )(dest_flat, x1, gates_tk, ys)


def _rope_tables(s):
    pos = jnp.arange(s, dtype=F32)
    inv = ROPE_BASE ** (-jnp.arange(0, MLA_ROPE, 2, dtype=F32) / MLA_ROPE)
    ang = pos[:, None] * inv[None, :]
    cos, sin = jnp.cos(ang), jnp.sin(ang)
    zeros = jnp.zeros((s, LANES - MLA_ROPE), F32)
    return (jnp.concatenate([cos, cos, zeros], axis=-1),
            jnp.concatenate([-sin, sin, zeros], axis=-1))


def _pad_heads(w, width):
    lead = w.shape[:-1]
    w = w.reshape(lead + (MLA_HEADS, width))
    w = jnp.pad(w, [(0, 0)] * len(lead) + [(0, 0), (0, 2 * LANES - width)])
    return w.reshape(lead + (MLA_HEADS * 2 * LANES,))


def _layer(x, g_mix, w_in, conv_w, conv_b, dt_bias, a_log, d_skip, g_ssd_out, w_ssd_out,
           g_q_lat, w_q_up, g_kv_lat, w_kv_up, g_qk_q, g_qk_k, w_mla_out, w_o,
           g_ffn, w_router, b_router, w_gate_up, b_gate_up, w_down, b_down):
    b, s, d = x.shape
    t = b * s
    x2 = x.reshape(t, d)

    off_xbc = SSD_D_INNER
    off_dt = off_xbc + SSD_CONV_DIM
    off_ql = off_dt + SSD_HEADS
    off_kvl = off_ql + MLA_Q_RANK
    off_kr = off_kvl + MLA_KV_RANK
    off_gate = off_kr + MLA_ROPE
    pad = LANES - MLA_ROPE - SSD_HEADS
    w_cat = jnp.concatenate([
        w_in[:, :off_dt], w_in[:, off_ql:off_kr],
        w_in[:, off_kr:off_gate], w_in[:, off_dt:off_ql], jnp.zeros((d, pad), w_in.dtype),
        w_in[:, off_gate:]], axis=1).astype(BF16)
    widths = (SSD_D_INNER, SSD_CONV_DIM, MLA_Q_RANK, MLA_KV_RANK, LANES, 2 * d)

    tm = min(512, s)
    z, xbc, ql, kvl, krdt, gate = _in_proj(x2, g_mix, w_cat, widths, tm)

    lane_pad = lambda vec: jnp.pad(vec, (MLA_ROPE, LANES - MLA_ROPE - SSD_HEADS)).reshape(1, LANES)
    lt = min(256, s)
    y_ssd = _ssd(xbc.reshape(b, s, SSD_CONV_DIM), z.reshape(b, s, SSD_D_INNER), krdt.reshape(b, s, LANES),
                 conv_w, conv_b.reshape(1, -1), lane_pad(dt_bias), lane_pad(a_log),
                 jnp.repeat(d_skip, SSD_HEAD_DIM).reshape(1, -1), g_ssd_out.reshape(1, -1), lt)

    cos_t, sin_t = _rope_tables(s)
    wq_pad = _pad_heads(w_q_up, MLA_QK_DIM).astype(BF16)
    gq_pad = jnp.pad(g_qk_q, (0, 2 * LANES - MLA_QK_DIM)).reshape(1, -1)
    gk_pad = jnp.pad(g_qk_k, (0, 2 * LANES - MLA_QK_DIM)).reshape(1, -1)
    tq = min(256, s)
    q, k, v = _qkv_prep(ql, kvl, krdt, cos_t, sin_t, g_q_lat.reshape(1, -1), wq_pad,
                        g_kv_lat.reshape(1, -1), w_kv_up.astype(BF16), gq_pad, gk_pad, b, s, tq)
    o_mla = _flash(q, k, v, min(512, s))

    x1, h2, idx_t, gates_t, rank_t, cnt = _merge(
        x2, y_ssd.reshape(t, d), o_mla.reshape(t, d), gate,
        w_ssd_out.astype(BF16), w_mla_out.astype(BF16), w_o.astype(BF16),
        g_ffn.reshape(1, -1), w_router.T, b_router.reshape(-1, 1), min(512, t))

    nblk = t * TOP_K // MOE_BLOCK + N_EXPERTS
    rows = nblk * MOE_BLOCK
    nblk_pad = -(-nblk // LANES) * LANES
    dest_t, blk_e, seg = _tables(cnt, idx_t, rank_t, nblk_pad)
    dest_flat = dest_t.reshape(-1)
    used = (seg[0:1, 2] // MOE_BLOCK).astype(I32)
    xs = _dispatch(dest_flat, seg[:, 0], seg[:, 1], used, h2, rows, min(512, t))
    ys = _experts(blk_e.reshape(-1)[:nblk], used, xs, w_gate_up, b_gate_up, w_down, b_down)
    out = _combine(dest_flat, x1, gates_t.T, ys, min(256, t))
    return out.reshape(b, s, d)


def kernel(x, g_mix, w_in, conv_w, conv_b, dt_bias, a_log, d_skip, g_ssd_out, w_ssd_out, g_q_lat, w_q_up, g_kv_lat, w_kv_up, g_qk_q, g_qk_k, w_mla_out, w_o, g_ffn, w_router, b_router, w_gate_up, b_gate_up, w_down, b_down):
    params = (g_mix, w_in, conv_w, conv_b, dt_bias, a_log, d_skip, g_ssd_out, w_ssd_out, g_q_lat, w_q_up,
              g_kv_lat, w_kv_up, g_qk_q, g_qk_k, w_mla_out, w_o, g_ffn, w_router, b_router,
              w_gate_up, b_gate_up, w_down, b_down)
    for l in range(g_mix.shape[0]):
        x = _layer(x, *(p[l] for p in params))
    return x
```

```python
import functools
import math

import jax
import jax.numpy as jnp
from jax import lax
from jax.experimental import pallas as pl
from jax.experimental.pallas import tpu as pltpu

F32 = jnp.float32
BF16 = jnp.bfloat16
I32 = jnp.int32

EPS = 1e-6
CHUNK = 64

SSD_HEADS = 16
SSD_HEAD_DIM = 64
SSD_GROUPS = 2
SSD_STATE = 128
SSD_CONV = 4
SSD_D_INNER = SSD_HEADS * SSD_HEAD_DIM
SSD_CONV_DIM = SSD_D_INNER + 2 * SSD_GROUPS * SSD_STATE

MLA_HEADS = 8
MLA_Q_RANK = 384
MLA_KV_RANK = 256
MLA_NOPE = 128
MLA_ROPE = 64
MLA_QK_DIM = MLA_NOPE + MLA_ROPE
MLA_V = 128
V_ROWS = MLA_V + 16
ROPE_BASE = 10000.0

N_EXPERTS = 32
TOP_K = 4
SWIGLU_ALPHA = 1.702
SWIGLU_LIMIT = 7.0

LANES = 128
VMEM_LIMIT = 56 * 1024 * 1024
NEG_BIG = -1e30
LOG2E = math.log2(math.e)

MOE_BLOCK = 256


def _cparams(semantics, **kw):
    return pltpu.CompilerParams(dimension_semantics=semantics,
                                vmem_limit_bytes=VMEM_LIMIT, **kw)


def _sigmoid(v):
    return 1.0 / (1.0 + jnp.exp(-v))


def _silu(v):
    return v * _sigmoid(v)


def _bf16_pieces(t, n):
    pieces = []
    for _ in range(n - 1):
        p = t.astype(BF16)
        pieces.append(p)
        t = t - p.astype(F32)
    pieces.append(t.astype(BF16))
    return pieces


def _inproj_kernel(x_ref, g_ref, w_ref, z_ref, xbc_ref, ql_ref, kvl_ref, krdt_ref, gate_ref, *, segs):
    x = x_ref[...]
    ms = jnp.mean(x * x, axis=-1, keepdims=True)
    h = (x * lax.rsqrt(ms + EPS) * g_ref[...]).astype(BF16)
    outs = (z_ref, xbc_ref, ql_ref, kvl_ref, krdt_ref, gate_ref)
    for ref, (lo, hi) in zip(outs, segs):
        p = jnp.dot(h, w_ref[:, lo:hi], preferred_element_type=F32)
        if ref is gate_ref:
            p = _sigmoid(p)
        ref[...] = p.astype(ref.dtype)


def _in_proj(x2, g_mix, w_cat, widths, tm):
    t, d = x2.shape
    offs = [0]
    for w in widths:
        offs.append(offs[-1] + w)
    segs = tuple((offs[i], offs[i + 1]) for i in range(len(widths)))
    dts = (BF16, BF16, BF16, BF16, F32, BF16)
    out_shape = tuple(jax.ShapeDtypeStruct((t, w), dt) for w, dt in zip(widths, dts))
    out_specs = tuple(pl.BlockSpec((tm, w), lambda i: (i, 0)) for w in widths)
    return pl.pallas_call(
        functools.partial(_inproj_kernel, segs=segs),
        grid=(t // tm,),
        in_specs=[pl.BlockSpec((tm, d), lambda i: (i, 0)),
                  pl.BlockSpec((1, d), lambda i: (0, 0)),
                  pl.BlockSpec(w_cat.shape, lambda i: (0, 0))],
        out_specs=out_specs,
        out_shape=out_shape,
        compiler_params=_cparams(("parallel",)),
        name="in_proj",
    )(x2, g_mix.reshape(1, d), w_cat)


def _ssd_kernel(xbc_ref, z_ref, krdt_ref, convw_ref, convb_ref, dtb_ref, alog_ref, dskip_ref, gout_ref,
                y_ref, xext_scr, state_scr, *, lt):
    i = pl.program_id(1)
    halo = 8
    dt_lo = MLA_ROPE
    gw = SSD_D_INNER // SSD_GROUPS
    hpg = SSD_HEADS // SSD_GROUPS

    @pl.when(i == 0)
    def _():
        state_scr[...] = jnp.zeros_like(state_scr)
        xext_scr[0:halo, :] = jnp.zeros((halo, SSD_CONV_DIM), F32)

    xext_scr[halo:halo + lt, :] = xbc_ref[0].astype(F32)
    acc = jnp.broadcast_to(convb_ref[...], (lt, SSD_CONV_DIM))
    for j in range(SSD_CONV):
        acc = acc + convw_ref[j:j + 1, :] * xext_scr[pl.ds(halo - (SSD_CONV - 1) + j, lt), :]
    xext_scr[0:halo, :] = xext_scr[lt:lt + halo, :]
    xbc = _silu(acc)
    xs = xbc[:, :SSD_D_INNER]
    bm = xbc[:, SSD_D_INNER:SSD_D_INNER + SSD_GROUPS * SSD_STATE]
    cm = xbc[:, SSD_D_INNER + SSD_GROUPS * SSD_STATE:]

    lane = lax.broadcasted_iota(I32, (1, LANES), 1)
    head_lane = (lane >= dt_lo) & (lane < dt_lo + SSD_HEADS)
    v = krdt_ref[0] + dtb_ref[...]
    dt = jnp.maximum(v, 0.0) + jnp.log(1.0 + jnp.exp(-jnp.abs(v)))
    dt = jnp.where(head_lane, dt, 0.0)
    a = jnp.where(head_lane, -jnp.exp(alog_ref[...]) * LOG2E, 0.0)
    da = dt * a
    row = lax.broadcasted_iota(I32, (lt, lt), 0)
    col = lax.broadcasted_iota(I32, (lt, lt), 1)
    tri = row >= col
    tri_b = tri.astype(BF16)
    a_cum = jnp.zeros((lt, LANES), F32)
    for piece in _bf16_pieces(da, 3):
        a_cum = a_cum + jnp.dot(tri_b, piece, preferred_element_type=F32)
    a_cum_t = a_cum.T
    a_last = a_cum[lt - 1:lt, :]
    exp_a = jnp.exp2(a_cum)
    dec = jnp.exp2(a_last - a_cum)

    er = lax.broadcasted_iota(I32, (LANES, SSD_D_INNER), 0)
    ec = lax.broadcasted_iota(I32, (LANES, SSD_D_INNER), 1)
    expand = ((er - dt_lo) == (ec // SSD_HEAD_DIM)).astype(BF16)

    def widen(t, pieces):
        out = jnp.zeros((lt, SSD_D_INNER), F32)
        for piece in _bf16_pieces(t, pieces):
            out = out + jnp.dot(piece, expand, preferred_element_type=F32)
        return out

    dt_w = widen(dt, 2)
    exp_a_w = widen(exp_a, 1)
    dec_w = widen(dec, 1)
    xdt = xs * dt_w
    xdt_b = xdt.astype(BF16)
    xdec_b = (xdt * dec_w).astype(BF16)
    lane_pair = lax.broadcasted_iota(I32, (lt, LANES), 1)

    y_parts = []
    for g in range(SSD_GROUPS):
        bg = bm[:, g * SSD_STATE:(g + 1) * SSD_STATE].astype(BF16)
        cg = cm[:, g * SSD_STATE:(g + 1) * SSD_STATE].astype(BF16)
        cb = lax.dot_general(cg, bg, (((1,), (1,)), ((), ())), preferred_element_type=F32)
        for pair in range(hpg // 2):
            h0 = g * hpg + 2 * pair
            rhs = xdt_b[:, h0 * SSD_HEAD_DIM:(h0 + 2) * SSD_HEAD_DIM]
            res = []
            for h in (h0, h0 + 1):
                seg = a_cum[:, dt_lo + h:dt_lo + h + 1] - a_cum_t[dt_lo + h:dt_lo + h + 1, :]
                lmat = jnp.exp2(jnp.where(tri, seg, NEG_BIG))
                res.append(jnp.dot((cb * lmat).astype(BF16), rhs, preferred_element_type=F32))
            y_parts.append(jnp.where(lane_pair < SSD_HEAD_DIM, res[0], res[1]))
        st = state_scr[g]
        y_off = jnp.dot(cg, st.astype(BF16), preferred_element_type=F32)
        y_parts.append(y_off * exp_a_w[:, g * gw:(g + 1) * gw])
        new = lax.dot_general(bg, xdec_b[:, g * gw:(g + 1) * gw], (((0,), (0,)), ((), ())),
                              preferred_element_type=F32)
        state_scr[g] = st * exp_a_w[lt - 1:lt, g * gw:(g + 1) * gw] + new

    zs = _silu(z_ref[0].astype(F32))
    outs = []
    npg = hpg // 2 + 1
    for g in range(SSD_GROUPS):
        parts = y_parts[g * npg:(g + 1) * npg]
        yd = jnp.concatenate(parts[:-1], axis=-1)
        yg = yd + parts[-1] + dskip_ref[:, g * gw:(g + 1) * gw] * xs[:, g * gw:(g + 1) * gw]
        yg = yg * zs[:, g * gw:(g + 1) * gw]
        ms = jnp.mean(yg * yg, axis=-1, keepdims=True)
        outs.append(yg * lax.rsqrt(ms + EPS) * gout_ref[:, g * gw:(g + 1) * gw])
    y_ref[0] = jnp.concatenate(outs, axis=-1).astype(y_ref.dtype)


def _ssd(xbc, z, krdt, conv_w, conv_b, dtb, alog, dskip_w, g_out, lt):
    b, s, _ = xbc.shape
    gw = SSD_D_INNER // SSD_GROUPS
    full = lambda shape: pl.BlockSpec(shape, lambda bi, i: (0,) * len(shape))
    return pl.pallas_call(
        functools.partial(_ssd_kernel, lt=lt),
        grid=(b, s // lt),
        in_specs=[pl.BlockSpec((1, lt, SSD_CONV_DIM), lambda bi, i: (bi, i, 0)),
                  pl.BlockSpec((1, lt, SSD_D_INNER), lambda bi, i: (bi, i, 0)),
                  pl.BlockSpec((1, lt, LANES), lambda bi, i: (bi, i, 0)),
                  full((SSD_CONV, SSD_CONV_DIM)), full((1, SSD_CONV_DIM)),
                  full((1, LANES)), full((1, LANES)),
                  full((1, SSD_D_INNER)), full((1, SSD_D_INNER))],
        out_specs=pl.BlockSpec((1, lt, SSD_D_INNER), lambda bi, i: (bi, i, 0)),
        out_shape=jax.ShapeDtypeStruct((b, s, SSD_D_INNER), BF16),
        scratch_shapes=[pltpu.VMEM((lt + 8, SSD_CONV_DIM), F32),
                        pltpu.VMEM((SSD_GROUPS, SSD_STATE, gw), F32)],
        compiler_params=_cparams(("parallel", "arbitrary")),
        name="ssd_scan",
    )(xbc, z, krdt, conv_w, conv_b, dtb, alog, dskip_w, g_out)


def _swap_halves(t):
    half = MLA_ROPE // 2
    return pltpu.roll(t, half, 1) + pltpu.roll(t, LANES - half, 1)


def _qkv_kernel(ql_ref, kvl_ref, krdt_ref, cos_ref, sin_ref, gql_ref, wq_ref, gkvl_ref, wk_ref, wvt_ref,
                gq_ref, gk_ref, q_ref, k_ref, v_ref, *, scale):
    hw = 2 * LANES
    cosv = cos_ref[...]
    sinv = sin_ref[...]

    ql = ql_ref[...].astype(F32)
    ms = jnp.mean(ql * ql, axis=-1, keepdims=True)
    qn = (ql * lax.rsqrt(ms + EPS) * gql_ref[...]).astype(BF16)
    qf = jnp.dot(qn, wq_ref[...], preferred_element_type=F32)

    kvl = kvl_ref[...].astype(F32)
    ms = jnp.mean(kvl * kvl, axis=-1, keepdims=True)
    kvn = (kvl * lax.rsqrt(ms + EPS) * gkvl_ref[...]).astype(BF16)
    kf = jnp.dot(kvn, wk_ref[...], preferred_element_type=F32)
    vt = lax.dot_general(wvt_ref[...], kvn, (((1,), (1,)), ((), ())), preferred_element_type=F32)
    tm = kvn.shape[0]
    ones_rows = (lax.broadcasted_iota(I32, (V_ROWS - MLA_V, tm), 0) == 0).astype(v_ref.dtype)

    lane = lax.broadcasted_iota(I32, (1, LANES), 1)
    kr = jnp.where(lane < MLA_ROPE, krdt_ref[...], 0.0)
    ss_r = jnp.sum(kr * kr, axis=-1, keepdims=True)
    krg = kr * gk_ref[:, LANES:]
    kr_rot = krg * cosv + _swap_halves(krg) * sinv

    for h in range(MLA_HEADS):
        qh = qf[:, h * hw:(h + 1) * hw]
        r = lax.rsqrt(jnp.sum(qh * qh, axis=-1, keepdims=True) * (1.0 / MLA_QK_DIM) + EPS)
        qs = qh * (r * scale) * gq_ref[...]
        q_ref[0, h, :, 0:LANES] = qs[:, :LANES].astype(q_ref.dtype)
        qr = qs[:, LANES:]
        qr = qr * cosv + _swap_halves(qr) * sinv
        q_ref[0, h, :, LANES:MLA_QK_DIM] = qr[:, :MLA_ROPE].astype(q_ref.dtype)

        kn = kf[:, h * LANES:(h + 1) * LANES]
        rk = lax.rsqrt((jnp.sum(kn * kn, axis=-1, keepdims=True) + ss_r) * (1.0 / MLA_QK_DIM) + EPS)
        k_ref[0, h, :, 0:LANES] = (kn * rk * gk_ref[:, :LANES]).astype(k_ref.dtype)
        k_ref[0, h, :, LANES:MLA_QK_DIM] = (kr_rot * rk)[:, :MLA_ROPE].astype(k_ref.dtype)
        v_ref[0, h, 0:MLA_V, :] = vt[h * MLA_V:(h + 1) * MLA_V, :].astype(v_ref.dtype)
        v_ref[0, h, MLA_V:V_ROWS, :] = ones_rows


def _qkv_prep(ql, kvl, krdt, cos_t, sin_t, g_q_lat, wq_pad, g_kv_lat, wk, wvt, gq_pad, gk_pad, b, s, tm):
    t = b * s
    nst = s // tm
    full = lambda shape: pl.BlockSpec(shape, lambda bi, i: (0,) * len(shape))
    tok = lambda w: pl.BlockSpec((tm, w), lambda bi, i: (bi * nst + i, 0))
    hs = lambda w: pl.BlockSpec((1, MLA_HEADS, tm, w), lambda bi, i: (bi, 0, i, 0))
    return pl.pallas_call(
        functools.partial(_qkv_kernel, scale=MLA_QK_DIM ** -0.5 * math.log2(math.e)),
        grid=(b, nst),
        in_specs=[tok(MLA_Q_RANK), tok(MLA_KV_RANK), tok(LANES),
                  pl.BlockSpec((tm, LANES), lambda bi, i: (i, 0)),
                  pl.BlockSpec((tm, LANES), lambda bi, i: (i, 0)),
                  full((1, MLA_Q_RANK)), full(wq_pad.shape), full((1, MLA_KV_RANK)), full(wk.shape),
                  full(wvt.shape), full((1, 2 * LANES)), full((1, 2 * LANES))],
        out_specs=(hs(MLA_QK_DIM), hs(MLA_QK_DIM),
                   pl.BlockSpec((1, MLA_HEADS, V_ROWS, tm), lambda bi, i: (bi, 0, 0, i))),
        out_shape=(jax.ShapeDtypeStruct((b, MLA_HEADS, s, MLA_QK_DIM), BF16),
                   jax.ShapeDtypeStruct((b, MLA_HEADS, s, MLA_QK_DIM), BF16),
                   jax.ShapeDtypeStruct((b, MLA_HEADS, V_ROWS, s), BF16)),
        compiler_params=_cparams(("parallel", "parallel")),
        name="qkv_prep",
    )(ql, kvl, krdt, cos_t, sin_t, g_q_lat, wq_pad, g_kv_lat, wk, wvt, gq_pad, gk_pad)


def _flash_kernel(q_ref, k_ref, vt_ref, o_ref, m_scr, acc_scr, s_scr, *, tq):
    i = pl.program_id(2)
    q = q_ref[0, 0]
    m_scr[...] = jnp.full(m_scr.shape, NEG_BIG, F32)
    acc_scr[...] = jnp.zeros(acc_scr.shape, F32)

    def scores(j, slot):
        start = pl.multiple_of(j * tq, tq)
        ks = k_ref[0, 0, pl.ds(start, tq), :]
        s_scr[slot] = lax.dot_general(ks, q, (((1,), (1,)), ((), ())), preferred_element_type=F32)

    def softmax_pv(j, slot, masked):
        start = pl.multiple_of(j * tq, tq)
        vt = vt_ref[0, 0, :, pl.ds(start, tq)]
        st = s_scr[slot]
        if masked:
            kc = lax.broadcasted_iota(I32, (tq, tq), 0) // CHUNK
            qc = lax.broadcasted_iota(I32, (tq, tq), 1) // CHUNK
            st = jnp.where(kc <= qc, st, NEG_BIG)
        m_prev = m_scr[...]
        m_new = jnp.maximum(m_prev, jnp.max(st, axis=0, keepdims=True))
        alpha = jnp.exp2(m_prev - m_new)
        pt = jnp.exp2(st - m_new)
        acc_scr[...] = alpha * acc_scr[...] + jnp.dot(vt, pt.astype(BF16), preferred_element_type=F32)
        m_scr[...] = m_new

    scores(0, 0)

    def body(jj, carry):
        j = 2 * jj
        scores(j + 1, 1)
        softmax_pv(j, 0, False)
        scores(j + 2, 0)
        softmax_pv(j + 1, 1, False)
        return carry

    lax.fori_loop(0, i // 2, body, 0)

    @pl.when(i % 2 == 0)
    def _():
        softmax_pv(i, 0, True)

    @pl.when(i % 2 == 1)
    def _():
        scores(i, 1)
        softmax_pv(i - 1, 0, False)
        softmax_pv(i, 1, True)

    o_ref[0] = (acc_scr[0:MLA_V, :] / acc_scr[MLA_V:MLA_V + 1, :]).T.astype(o_ref.dtype)


def _flash(q, k, v, tq):
    b, nh, s, _ = q.shape
    return pl.pallas_call(
        functools.partial(_flash_kernel, tq=tq),
        grid=(b, nh, s // tq),
        in_specs=[pl.BlockSpec((1, 1, tq, MLA_QK_DIM), lambda bi, h, i: (bi, h, i, 0)),
                  pl.BlockSpec((1, 1, s, MLA_QK_DIM), lambda bi, h, i: (bi, h, 0, 0)),
                  pl.BlockSpec((1, 1, V_ROWS, s), lambda bi, h, i: (bi, h, 0, 0))],
        out_specs=pl.BlockSpec((1, tq, MLA_V), lambda bi, h, i: (bi, i, h)),
        out_shape=jax.ShapeDtypeStruct((b, s, nh * MLA_V), BF16),
        scratch_shapes=[pltpu.VMEM((1, tq), F32), pltpu.VMEM((V_ROWS, tq), F32),
                        pltpu.VMEM((2, tq, tq), F32)],
        compiler_params=_cparams(("parallel", "parallel", "arbitrary")),
        name="flash_attn",
    )(q, k, v)


def _merge_kernel(x_ref, ys_ref, om_ref, gate_ref, wss_ref, wml_ref, wo_ref, gffn_ref, wrt_ref, brt_ref,
                  x1_ref, h2_ref, idx_ref, gates_ref, rank_ref, cnt_ref, carry_scr, *, tm):
    d = x_ref.shape[-1]

    @pl.when(pl.program_id(0) == 0)
    def _():
        carry_scr[...] = jnp.zeros_like(carry_scr)

    y1 = jnp.dot(ys_ref[...], wss_ref[...], preferred_element_type=F32)
    y2 = jnp.dot(om_ref[...], wml_ref[...], preferred_element_type=F32)
    g = gate_ref[...].astype(F32)
    merged = (g[:, :d] * y1 + g[:, d:] * y2).astype(BF16)
    x1 = x_ref[...] + jnp.dot(merged, wo_ref[...], preferred_element_type=F32)
    x1_ref[...] = x1
    ms = jnp.mean(x1 * x1, axis=-1, keepdims=True)
    h2 = x1 * lax.rsqrt(ms + EPS) * gffn_ref[...]
    h2_ref[...] = h2

    w_hi, w_lo = _bf16_pieces(wrt_ref[...], 2)
    h_hi, h_lo = _bf16_pieces(h2, 2)
    nt = (((1,), (1,)), ((), ()))
    logits = (lax.dot_general(w_hi, h_hi, nt, preferred_element_type=F32)
              + lax.dot_general(w_hi, h_lo, nt, preferred_element_type=F32)
              + lax.dot_general(w_lo, h_hi, nt, preferred_element_type=F32)) + brt_ref[...]
    eid = lax.broadcasted_iota(I32, (N_EXPERTS, tm), 0)
    cur = logits
    onehot = jnp.zeros((N_EXPERTS, tm), F32)
    vals, sels = [], []
    for k in range(TOP_K):
        mx = jnp.max(cur, axis=0, keepdims=True)
        idx = jnp.min(jnp.where(cur == mx, eid, N_EXPERTS), axis=0, keepdims=True)
        sel = eid == idx
        vals.append(mx)
        sels.append(sel)
        idx_ref[k:k + 1, :] = idx
        cur = jnp.where(sel, -jnp.inf, cur)
        onehot = onehot + sel.astype(F32)
    es = [jnp.exp(vk - vals[0]) for vk in vals]
    den = es[0] + es[1] + es[2] + es[3]
    for k in range(TOP_K):
        gates_ref[k:k + 1, :] = es[k] / den

    r = lax.broadcasted_iota(I32, (tm, tm), 0)
    c = lax.broadcasted_iota(I32, (tm, tm), 1)
    before = (r < c).astype(BF16)
    prefix = jnp.dot(onehot.astype(BF16), before, preferred_element_type=F32) + carry_scr[...]
    for k in range(TOP_K):
        rank_ref[k:k + 1, :] = jnp.sum(jnp.where(sels[k], prefix, 0.0), axis=0, keepdims=True).astype(I32)
    carry = carry_scr[...] + jnp.sum(onehot, axis=1, keepdims=True)
    carry_scr[...] = carry
    cnt_ref[...] = jnp.broadcast_to(carry, cnt_ref.shape)


def _merge(x2, y_ssd, o_mla, gate, w_ss, w_ml, w_o, g_ffn, w_rt, b_rt, tm):
    t, d = x2.shape
    full = lambda shape: pl.BlockSpec(shape, lambda i: (0,) * len(shape))
    tok = lambda w: pl.BlockSpec((tm, w), lambda i: (i, 0))
    sel = pl.BlockSpec((TOP_K, tm), lambda i: (0, i))
    return pl.pallas_call(
        functools.partial(_merge_kernel, tm=tm),
        grid=(t // tm,),
        in_specs=[tok(d), tok(d), tok(d), tok(2 * d), full((d, d)), full((d, d)), full((d, d)),
                  full((1, d)), full((N_EXPERTS, d)), full((N_EXPERTS, 1))],
        out_specs=(tok(d), tok(d), sel, sel, sel, full((N_EXPERTS, LANES))),
        out_shape=(jax.ShapeDtypeStruct((t, d), F32), jax.ShapeDtypeStruct((t, d), F32),
                   jax.ShapeDtypeStruct((TOP_K, t), I32), jax.ShapeDtypeStruct((TOP_K, t), F32),
                   jax.ShapeDtypeStruct((TOP_K, t), I32), jax.ShapeDtypeStruct((N_EXPERTS, LANES), F32)),
        scratch_shapes=[pltpu.VMEM((N_EXPERTS, 1), F32)],
        compiler_params=_cparams(("arbitrary",)),
        name="merge_route",
    )(x2, y_ssd, o_mla, gate, w_ss, w_ml, w_o, g_ffn, w_rt, b_rt)


def _tables_kernel(cnt_ref, idx_ref, rank_ref, dest_ref, blk_ref, seg_ref, *, nblk_pad):
    cnt = cnt_ref[...]
    padded = jnp.ceil(cnt * (1.0 / MOE_BLOCK)) * MOE_BLOCK
    eid = lax.broadcasted_iota(I32, (N_EXPERTS, LANES), 0)
    start = jnp.zeros((N_EXPERTS, LANES), F32)
    for e in range(N_EXPERTS - 1):
        start = start + jnp.where(eid > e, padded[e:e + 1, :], 0.0)
    end = start + padded
    idx = idx_ref[...]
    dest = rank_ref[...]
    for e in range(N_EXPERTS):
        dest = dest + jnp.where(idx == e, start[e:e + 1, 0:1].astype(I32), 0)
    dest_ref[...] = dest
    pos = (lax.broadcasted_iota(I32, (N_EXPERTS, nblk_pad), 1) * MOE_BLOCK).astype(F32)
    nle = jnp.sum((end[:, 0:1] <= pos).astype(I32), axis=0, keepdims=True)
    blk_ref[...] = jnp.minimum(nle, N_EXPERTS - 1)
    lane = lax.broadcasted_iota(I32, (N_EXPERTS, LANES), 1)
    total = jnp.max(end, axis=0, keepdims=True)
    seg = jnp.where(lane == 0, start + cnt, jnp.where(lane == 1, end, total))
    seg_ref[...] = seg.astype(I32)


def _tables(cnt, idx_t, rank_t, nblk_pad):
    t = idx_t.shape[1]
    return pl.pallas_call(
        functools.partial(_tables_kernel, nblk_pad=nblk_pad),
        out_shape=(jax.ShapeDtypeStruct((TOP_K, t), I32),
                   jax.ShapeDtypeStruct((1, nblk_pad), I32),
                   jax.ShapeDtypeStruct((N_EXPERTS, LANES), I32)),
        compiler_params=pltpu.CompilerParams(vmem_limit_bytes=VMEM_LIMIT),
        name="route_tables",
    )(cnt, idx_t, rank_t)


def _row_copy(src_ref, src_row, dst_ref, dst_row, sem):
    return pltpu.make_async_copy(src_ref.at[pl.ds(src_row, 1), :], dst_ref.at[pl.ds(dst_row, 1), :], sem)


def _dispatch_kernel(dest_ref, padlo_ref, padhi_ref, used_ref, h_ref, xs_ref, zero_ref, sem, zsem, *, tt, t):
    i = pl.program_id(0)

    @pl.when(i == 0)
    def _():
        nblk = xs_ref.shape[0] // MOE_BLOCK
        zero_ref[...] = jnp.zeros_like(zero_ref)

        def blk_copy(b):
            start = pl.multiple_of(b * MOE_BLOCK, MOE_BLOCK)
            return pltpu.make_async_copy(zero_ref, xs_ref.at[pl.ds(start, MOE_BLOCK), :], zsem)

        def issue_blk(b, carry):
            blk_copy(b).start()
            return carry

        def drain_blk(b, carry):
            blk_copy(b).wait()
            return carry

        lax.fori_loop(used_ref[0], nblk, issue_blk, 0)
        lax.fori_loop(used_ref[0], nblk, drain_blk, 0)

        for e in range(N_EXPERTS):
            lo = padlo_ref[e]
            hi = padhi_ref[e]

            def issue(r, carry):
                _row_copy(zero_ref, 0, xs_ref, r, zsem).start()
                return carry

            def drain(r, carry):
                _row_copy(zero_ref, 0, xs_ref, r, zsem).wait()
                return carry

            lax.fori_loop(lo, hi, issue, 0)
            lax.fori_loop(lo, hi, drain, 0)

    base = i * tt

    def issue(j, carry):
        for k in range(TOP_K):
            _row_copy(h_ref, j, xs_ref, dest_ref[k * t + base + j], sem).start()
        return carry

    lax.fori_loop(0, tt, issue, 0)
    for k in range(TOP_K):
        pltpu.make_async_copy(h_ref, xs_ref.at[pl.ds(0, tt), :], sem).wait()


def _dispatch(dest_flat, pad_lo, pad_hi, used, h2, rows, tt):
    t, d = h2.shape
    return pl.pallas_call(
        functools.partial(_dispatch_kernel, tt=tt, t=t),
        grid_spec=pltpu.PrefetchScalarGridSpec(
            num_scalar_prefetch=4,
            grid=(t // tt,),
            in_specs=[pl.BlockSpec((tt, d), lambda i, *_: (i, 0))],
            out_specs=pl.BlockSpec(memory_space=pl.ANY),
            scratch_shapes=[pltpu.VMEM((MOE_BLOCK, d), h2.dtype),
                            pltpu.SemaphoreType.DMA(()), pltpu.SemaphoreType.DMA(())]),
        out_shape=jax.ShapeDtypeStruct((rows, d), h2.dtype),
        compiler_params=_cparams(("arbitrary",)),
        name="moe_dispatch",
    )(dest_flat, pad_lo, pad_hi, used, h2)


def _expert_kernel(blk_ref, used_ref, x_ref, wgu_ref, bgu_ref, wd_ref, bd_ref, y_ref, wgu_scr, wd_scr):
    b = pl.program_id(0)
    ff = wd_ref.shape[1]
    prev = blk_ref[jnp.maximum(b - 1, 0)]
    changed = (b == 0) | (blk_ref[b] != prev)

    @pl.when(changed)
    def _():
        wgu_scr[...] = wgu_ref[0].astype(BF16)
        wd_scr[...] = wd_ref[0].astype(BF16)

    @pl.when(b < used_ref[0])
    def _():
        xb = x_ref[...].astype(BF16)
        gu = jnp.dot(xb, wgu_scr[...], preferred_element_type=F32) + bgu_ref[0]
        glu = jnp.minimum(gu[:, :ff], SWIGLU_LIMIT)
        lin = jnp.clip(gu[:, ff:], -SWIGLU_LIMIT, SWIGLU_LIMIT)
        act = glu * _sigmoid(SWIGLU_ALPHA * glu) * (lin + 1.0)
        y_ref[...] = jnp.dot(act.astype(BF16), wd_scr[...], preferred_element_type=F32) + bd_ref[0]

    @pl.when(b >= used_ref[0])
    def _():
        y_ref[...] = jnp.zeros_like(y_ref)


def _experts(blk_e, used, xs, w_gate_up, b_gate_up, w_down, b_down):
    rows, d = xs.shape
    ne, _, ff2 = w_gate_up.shape
    ff = ff2 // 2
    nblk = rows // MOE_BLOCK
    row_map = lambda b, blk, used: (jnp.minimum(b, used[0] - 1), 0)
    return pl.pallas_call(
        _expert_kernel,
        grid_spec=pltpu.PrefetchScalarGridSpec(
            num_scalar_prefetch=2,
            grid=(nblk,),
            in_specs=[pl.BlockSpec((MOE_BLOCK, d), row_map),
                      pl.BlockSpec((1, d, ff2), lambda b, blk, used: (blk[b], 0, 0)),
                      pl.BlockSpec((1, 1, ff2), lambda b, blk, used: (blk[b], 0, 0)),
                      pl.BlockSpec((1, ff, d), lambda b, blk, used: (blk[b], 0, 0)),
                      pl.BlockSpec((1, 1, d), lambda b, blk, used: (blk[b], 0, 0))],
            out_specs=pl.BlockSpec((MOE_BLOCK, d), lambda b, blk, used: (b, 0)),
            scratch_shapes=[pltpu.VMEM((d, ff2), BF16), pltpu.VMEM((ff, d), BF16)]),
        out_shape=jax.ShapeDtypeStruct((rows, d), F32),
        compiler_params=_cparams(("arbitrary",)),
        name="moe_experts",
    )(blk_e, used, xs, w_gate_up, b_gate_up.reshape(ne, 1, ff2), w_down, b_down.reshape(ne, 1, d))


def _combine_kernel(dest_ref, x1_ref, gates_ref, ys_ref, o_ref, buf, sem, *, tt, t):
    base = pl.program_id(0) * tt

    def issue(j, carry):
        tok = base + j
        for k in range(TOP_K):
            pltpu.make_async_copy(ys_ref.at[pl.ds(dest_ref[k * t + tok], 1), :],
                                  buf.at[k, pl.ds(j, 1), :], sem).start()
        return carry

    lax.fori_loop(0, tt, issue, 0)
    for k in range(TOP_K):
        pltpu.make_async_copy(ys_ref.at[pl.ds(0, tt), :], buf.at[k], sem).wait()
    g = gates_ref[...]
    acc = x1_ref[...]
    for k in range(TOP_K):
        acc = acc + g[:, k:k + 1] * buf[k]
    o_ref[...] = acc


def _combine(dest_flat, x1, gates_tk, ys, tt):
    t, d = x1.shape
    return pl.pallas_call(
        functools.partial(_combine_kernel, tt=tt, t=t),
        grid_spec=pltpu.PrefetchScalarGridSpec(
            num_scalar_prefetch=1,
            grid=(t // tt,),
            in_specs=[pl.BlockSpec((tt, d), lambda i, dest: (i, 0)),
                      pl.BlockSpec((tt, TOP_K), lambda i, dest: (i, 0)),
                      pl.BlockSpec(memory_space=pl.ANY)],
            out_specs=pl.BlockSpec((tt, d), lambda i, dest: (i, 0)),
            scratch_shapes=[pltpu.VMEM((TOP_K, tt, d), F32), pltpu.SemaphoreType.DMA(())]),
        out_shape=jax.ShapeDtypeStruct((t, d), F32),
        compiler_params=_cparams(("arbitrary",)),
        name="moe_combine",
    )(dest_flat, x1, gates_tk, ys)


def _rope_tables(s):
    pos = jnp.arange(s, dtype=F32)
    inv = ROPE_BASE ** (-jnp.arange(0, MLA_ROPE, 2, dtype=F32) / MLA_ROPE)
    ang = pos[:, None] * inv[None, :]
    cos, sin = jnp.cos(ang), jnp.sin(ang)
    zeros = jnp.zeros((s, LANES - MLA_ROPE), F32)
    return (jnp.concatenate([cos, cos, zeros], axis=-1),
            jnp.concatenate([-sin, sin, zeros], axis=-1))


def _pad_heads(w, width):
    lead = w.shape[:-1]
    w = w.reshape(lead + (MLA_HEADS, width))
    w = jnp.pad(w, [(0, 0)] * len(lead) + [(0, 0), (0, 2 * LANES - width)])
    return w.reshape(lead + (MLA_HEADS * 2 * LANES,))


def _layer(x, g_mix, w_in, conv_w, conv_b, dt_bias, a_log, d_skip, g_ssd_out, w_ssd_out,
           g_q_lat, w_q_up, g_kv_lat, w_kv_up, g_qk_q, g_qk_k, w_mla_out, w_o,
           g_ffn, w_router, b_router, w_gate_up, b_gate_up, w_down, b_down):
    b, s, d = x.shape
    t = b * s
    x2 = x.reshape(t, d)

    off_xbc = SSD_D_INNER
    off_dt = off_xbc + SSD_CONV_DIM
    off_ql = off_dt + SSD_HEADS
    off_kvl = off_ql + MLA_Q_RANK
    off_kr = off_kvl + MLA_KV_RANK
    off_gate = off_kr + MLA_ROPE
    pad = LANES - MLA_ROPE - SSD_HEADS
    w_cat = jnp.concatenate([
        w_in[:, :off_dt], w_in[:, off_ql:off_kr],
        w_in[:, off_kr:off_gate], w_in[:, off_dt:off_ql], jnp.zeros((d, pad), w_in.dtype),
        w_in[:, off_gate:]], axis=1).astype(BF16)
    widths = (SSD_D_INNER, SSD_CONV_DIM, MLA_Q_RANK, MLA_KV_RANK, LANES, 2 * d)

    tm = min(512, s)
    z, xbc, ql, kvl, krdt, gate = _in_proj(x2, g_mix, w_cat, widths, tm)

    lane_pad = lambda vec: jnp.pad(vec, (MLA_ROPE, LANES - MLA_ROPE - SSD_HEADS)).reshape(1, LANES)
    lt = min(256, s)
    y_ssd = _ssd(xbc.reshape(b, s, SSD_CONV_DIM), z.reshape(b, s, SSD_D_INNER), krdt.reshape(b, s, LANES),
                 conv_w, conv_b.reshape(1, -1), lane_pad(dt_bias), lane_pad(a_log),
                 jnp.repeat(d_skip, SSD_HEAD_DIM).reshape(1, -1), g_ssd_out.reshape(1, -1), lt)

    cos_t, sin_t = _rope_tables(s)
    wq_pad = _pad_heads(w_q_up, MLA_QK_DIM).astype(BF16)
    gq_pad = jnp.pad(g_qk_q, (0, 2 * LANES - MLA_QK_DIM)).reshape(1, -1)
    gk_pad = jnp.pad(g_qk_k, (0, 2 * LANES - MLA_QK_DIM)).reshape(1, -1)
    tq = min(256, s)
    w_kv_h = w_kv_up.reshape(MLA_KV_RANK, MLA_HEADS, MLA_NOPE + MLA_V)
    wk = w_kv_h[:, :, :MLA_NOPE].reshape(MLA_KV_RANK, MLA_HEADS * MLA_NOPE).astype(BF16)
    wvt = w_kv_h[:, :, MLA_NOPE:].reshape(MLA_KV_RANK, MLA_HEADS * MLA_V).T.astype(BF16)
    q, k, v = _qkv_prep(ql, kvl, krdt, cos_t, sin_t, g_q_lat.reshape(1, -1), wq_pad,
                        g_kv_lat.reshape(1, -1), wk, wvt, gq_pad, gk_pad, b, s, tq)
    o_mla = _flash(q, k, v, min(512, s))

    x1, h2, idx_t, gates_t, rank_t, cnt = _merge(
        x2, y_ssd.reshape(t, d), o_mla.reshape(t, d), gate,
        w_ssd_out.astype(BF16), w_mla_out.astype(BF16), w_o.astype(BF16),
        g_ffn.reshape(1, -1), w_router.T, b_router.reshape(-1, 1), min(512, t))

    nblk = t * TOP_K // MOE_BLOCK + N_EXPERTS
    rows = nblk * MOE_BLOCK
    nblk_pad = -(-nblk // LANES) * LANES
    dest_t, blk_e, seg = _tables(cnt, idx_t, rank_t, nblk_pad)
    dest_flat = dest_t.reshape(-1)
    used = (seg[0:1, 2] // MOE_BLOCK).astype(I32)
    xs = _dispatch(dest_flat, seg[:, 0], seg[:, 1], used, h2, rows, min(512, t))
    ys = _experts(blk_e.reshape(-1)[:nblk], used, xs, w_gate_up, b_gate_up, w_down, b_down)
    out = _combine(dest_flat, x1, gates_t.T, ys, min(256, t))
    return out.reshape(b, s, d)


def kernel(x, g_mix, w_in, conv_w, conv_b, dt_bias, a_log, d_skip, g_ssd_out, w_ssd_out, g_q_lat, w_q_up, g_kv_lat, w_kv_up, g_qk_q, g_qk_k, w_mla_out, w_o, g_ffn, w_router, b_router, w_gate_up, b_gate_up, w_down, b_down):
    params = (g_mix, w_in, conv_w, conv_b, dt_bias, a_log, d_skip, g_ssd_out, w_ssd_out, g_q_lat, w_q_up,
              g_kv_lat, w_kv_up, g_qk_q, g_qk_k, w_mla_out, w_o, g_ffn, w_router, b_router,
              w_gate_up, b_gate_up, w_down, b_down)
    for l in range(g_mix.shape[0]):
        x = _layer(x, *(p[l] for p in params))
    return x
```

```python
import functools
import math

import jax
import jax.numpy as jnp
from jax import lax
from jax.experimental import pallas as pl
from jax.experimental.pallas import tpu as pltpu

F32 = jnp.float32
BF16 = jnp.bfloat16
I32 = jnp.int32

EPS = 1e-6
CHUNK = 64

SSD_HEADS = 16
SSD_HEAD_DIM = 64
SSD_GROUPS = 2
SSD_STATE = 128
SSD_CONV = 4
SSD_D_INNER = SSD_HEADS * SSD_HEAD_DIM
SSD_CONV_DIM = SSD_D_INNER + 2 * SSD_GROUPS * SSD_STATE

MLA_HEADS = 8
MLA_Q_RANK = 384
MLA_KV_RANK = 256
MLA_NOPE = 128
MLA_ROPE = 64
MLA_QK_DIM = MLA_NOPE + MLA_ROPE
MLA_V = 128
V_ROWS = MLA_V + 16
ROPE_BASE = 10000.0

N_EXPERTS = 32
TOP_K = 4
SWIGLU_ALPHA = 1.702
SWIGLU_LIMIT = 7.0

LANES = 128
VMEM_LIMIT = 56 * 1024 * 1024
NEG_BIG = -1e30
LOG2E = math.log2(math.e)

MOE_BLOCK = 256
MOE_SUB = 256
SEG_ALIGN = 8
LOCAL_SLOTS = -(-(TOP_K * MOE_SUB + N_EXPERTS * (SEG_ALIGN - 1)) // MOE_BLOCK) * MOE_BLOCK


def _cparams(semantics, **kw):
    return pltpu.CompilerParams(dimension_semantics=semantics,
                                vmem_limit_bytes=VMEM_LIMIT, **kw)


def _sigmoid(v):
    return 1.0 / (1.0 + jnp.exp(-v))


def _silu(v):
    return v * _sigmoid(v)


def _bf16_pieces(t, n):
    pieces = []
    for _ in range(n - 1):
        p = t.astype(BF16)
        pieces.append(p)
        t = t - p.astype(F32)
    pieces.append(t.astype(BF16))
    return pieces


def _inproj_kernel(x_ref, g_ref, w_ref, z_ref, xbc_ref, ql_ref, kvl_ref, krdt_ref, gate_ref, *, segs):
    x = x_ref[...]
    ms = jnp.mean(x * x, axis=-1, keepdims=True)
    h = (x * lax.rsqrt(ms + EPS) * g_ref[...]).astype(BF16)
    outs = (z_ref, xbc_ref, ql_ref, kvl_ref, krdt_ref, gate_ref)
    for ref, (lo, hi) in zip(outs, segs):
        p = jnp.dot(h, w_ref[:, lo:hi], preferred_element_type=F32)
        if ref is gate_ref:
            p = _sigmoid(p)
        ref[...] = p.astype(ref.dtype)


def _in_proj(x2, g_mix, w_cat, widths, tm):
    t, d = x2.shape
    offs = [0]
    for w in widths:
        offs.append(offs[-1] + w)
    segs = tuple((offs[i], offs[i + 1]) for i in range(len(widths)))
    dts = (BF16, BF16, BF16, BF16, F32, BF16)
    out_shape = tuple(jax.ShapeDtypeStruct((t, w), dt) for w, dt in zip(widths, dts))
    out_specs = tuple(pl.BlockSpec((tm, w), lambda i: (i, 0)) for w in widths)
    return pl.pallas_call(
        functools.partial(_inproj_kernel, segs=segs),
        grid=(t // tm,),
        in_specs=[pl.BlockSpec((tm, d), lambda i: (i, 0)),
                  pl.BlockSpec((1, d), lambda i: (0, 0)),
                  pl.BlockSpec(w_cat.shape, lambda i: (0, 0))],
        out_specs=out_specs,
        out_shape=out_shape,
        compiler_params=_cparams(("parallel",)),
        name="in_proj",
    )(x2, g_mix.reshape(1, d), w_cat)


def _ssd_kernel(xbc_ref, z_ref, krdt_ref, convw_ref, convb_ref, dtb_ref, alog_ref, dskip_ref, gout_ref,
                y_ref, xext_scr, state_scr, *, lt):
    i = pl.program_id(1)
    halo = 8
    dt_lo = MLA_ROPE
    gw = SSD_D_INNER // SSD_GROUPS
    hpg = SSD_HEADS // SSD_GROUPS

    @pl.when(i == 0)
    def _():
        state_scr[...] = jnp.zeros_like(state_scr)
        xext_scr[0:halo, :] = jnp.zeros((halo, SSD_CONV_DIM), F32)

    xext_scr[halo:halo + lt, :] = xbc_ref[0].astype(F32)
    acc = jnp.broadcast_to(convb_ref[...], (lt, SSD_CONV_DIM))
    for j in range(SSD_CONV):
        acc = acc + convw_ref[j:j + 1, :] * xext_scr[pl.ds(halo - (SSD_CONV - 1) + j, lt), :]
    xext_scr[0:halo, :] = xext_scr[lt:lt + halo, :]
    xbc = _silu(acc)
    xs = xbc[:, :SSD_D_INNER]
    bm = xbc[:, SSD_D_INNER:SSD_D_INNER + SSD_GROUPS * SSD_STATE]
    cm = xbc[:, SSD_D_INNER + SSD_GROUPS * SSD_STATE:]

    lane = lax.broadcasted_iota(I32, (1, LANES), 1)
    head_lane = (lane >= dt_lo) & (lane < dt_lo + SSD_HEADS)
    v = krdt_ref[0] + dtb_ref[...]
    dt = jnp.maximum(v, 0.0) + jnp.log(1.0 + jnp.exp(-jnp.abs(v)))
    dt = jnp.where(head_lane, dt, 0.0)
    a = jnp.where(head_lane, -jnp.exp(alog_ref[...]) * LOG2E, 0.0)
    da = dt * a
    row = lax.broadcasted_iota(I32, (lt, lt), 0)
    col = lax.broadcasted_iota(I32, (lt, lt), 1)
    tri = row >= col
    tri_b = tri.astype(BF16)
    a_cum = jnp.zeros((lt, LANES), F32)
    for piece in _bf16_pieces(da, 3):
        a_cum = a_cum + jnp.dot(tri_b, piece, preferred_element_type=F32)
    a_cum_t = a_cum.T
    a_last = a_cum[lt - 1:lt, :]
    exp_a = jnp.exp2(a_cum)
    dec = jnp.exp2(a_last - a_cum)

    er = lax.broadcasted_iota(I32, (LANES, SSD_D_INNER), 0)
    ec = lax.broadcasted_iota(I32, (LANES, SSD_D_INNER), 1)
    expand = ((er - dt_lo) == (ec // SSD_HEAD_DIM)).astype(BF16)

    def widen(t, pieces):
        out = jnp.zeros((lt, SSD_D_INNER), F32)
        for piece in _bf16_pieces(t, pieces):
            out = out + jnp.dot(piece, expand, preferred_element_type=F32)
        return out

    dt_w = widen(dt, 2)
    exp_a_w = widen(exp_a, 1)
    dec_w = widen(dec, 1)
    xdt = xs * dt_w
    xdt_b = xdt.astype(BF16)
    xdec_b = (xdt * dec_w).astype(BF16)
    lane_pair = lax.broadcasted_iota(I32, (lt, LANES), 1)

    y_parts = []
    for g in range(SSD_GROUPS):
        bg = bm[:, g * SSD_STATE:(g + 1) * SSD_STATE].astype(BF16)
        cg = cm[:, g * SSD_STATE:(g + 1) * SSD_STATE].astype(BF16)
        cb = lax.dot_general(cg, bg, (((1,), (1,)), ((), ())), preferred_element_type=F32)
        for pair in range(hpg // 2):
            h0 = g * hpg + 2 * pair
            rhs = xdt_b[:, h0 * SSD_HEAD_DIM:(h0 + 2) * SSD_HEAD_DIM]
            res = []
            for h in (h0, h0 + 1):
                seg = a_cum[:, dt_lo + h:dt_lo + h + 1] - a_cum_t[dt_lo + h:dt_lo + h + 1, :]
                lmat = jnp.exp2(jnp.where(tri, seg, NEG_BIG))
                res.append(jnp.dot((cb * lmat).astype(BF16), rhs, preferred_element_type=F32))
            y_parts.append(jnp.where(lane_pair < SSD_HEAD_DIM, res[0], res[1]))
        st = state_scr[g]
        y_off = jnp.dot(cg, st.astype(BF16), preferred_element_type=F32)
        y_parts.append(y_off * exp_a_w[:, g * gw:(g + 1) * gw])
        new = lax.dot_general(bg, xdec_b[:, g * gw:(g + 1) * gw], (((0,), (0,)), ((), ())),
                              preferred_element_type=F32)
        state_scr[g] = st * exp_a_w[lt - 1:lt, g * gw:(g + 1) * gw] + new

    zs = _silu(z_ref[0].astype(F32))
    outs = []
    npg = hpg // 2 + 1
    for g in range(SSD_GROUPS):
        parts = y_parts[g * npg:(g + 1) * npg]
        yd = jnp.concatenate(parts[:-1], axis=-1)
        yg = yd + parts[-1] + dskip_ref[:, g * gw:(g + 1) * gw] * xs[:, g * gw:(g + 1) * gw]
        yg = yg * zs[:, g * gw:(g + 1) * gw]
        ms = jnp.mean(yg * yg, axis=-1, keepdims=True)
        outs.append(yg * lax.rsqrt(ms + EPS) * gout_ref[:, g * gw:(g + 1) * gw])
    y_ref[0] = jnp.concatenate(outs, axis=-1).astype(y_ref.dtype)


def _ssd(xbc, z, krdt, conv_w, conv_b, dtb, alog, dskip_w, g_out, lt):
    b, s, _ = xbc.shape
    gw = SSD_D_INNER // SSD_GROUPS
    full = lambda shape: pl.BlockSpec(shape, lambda bi, i: (0,) * len(shape))
    return pl.pallas_call(
        functools.partial(_ssd_kernel, lt=lt),
        grid=(b, s // lt),
        in_specs=[pl.BlockSpec((1, lt, SSD_CONV_DIM), lambda bi, i: (bi, i, 0)),
                  pl.BlockSpec((1, lt, SSD_D_INNER), lambda bi, i: (bi, i, 0)),
                  pl.BlockSpec((1, lt, LANES), lambda bi, i: (bi, i, 0)),
                  full((SSD_CONV, SSD_CONV_DIM)), full((1, SSD_CONV_DIM)),
                  full((1, LANES)), full((1, LANES)),
                  full((1, SSD_D_INNER)), full((1, SSD_D_INNER))],
        out_specs=pl.BlockSpec((1, lt, SSD_D_INNER), lambda bi, i: (bi, i, 0)),
        out_shape=jax.ShapeDtypeStruct((b, s, SSD_D_INNER), BF16),
        scratch_shapes=[pltpu.VMEM((lt + 8, SSD_CONV_DIM), F32),
                        pltpu.VMEM((SSD_GROUPS, SSD_STATE, gw), F32)],
        compiler_params=_cparams(("parallel", "arbitrary")),
        name="ssd_scan",
    )(xbc, z, krdt, conv_w, conv_b, dtb, alog, dskip_w, g_out)


def _swap_halves(t):
    half = MLA_ROPE // 2
    return pltpu.roll(t, half, 1) + pltpu.roll(t, LANES - half, 1)


def _qkv_kernel(ql_ref, kvl_ref, krdt_ref, cos_ref, sin_ref, gql_ref, wq_ref, gkvl_ref, wk_ref, wvt_ref,
                gq_ref, gk_ref, q_ref, k_ref, v_ref, *, scale):
    hw = 2 * LANES
    cosv = cos_ref[...]
    sinv = sin_ref[...]

    ql = ql_ref[...].astype(F32)
    ms = jnp.mean(ql * ql, axis=-1, keepdims=True)
    qn = (ql * lax.rsqrt(ms + EPS) * gql_ref[...]).astype(BF16)
    qf = jnp.dot(qn, wq_ref[...], preferred_element_type=F32)

    kvl = kvl_ref[...].astype(F32)
    ms = jnp.mean(kvl * kvl, axis=-1, keepdims=True)
    kvn = (kvl * lax.rsqrt(ms + EPS) * gkvl_ref[...]).astype(BF16)
    kf = jnp.dot(kvn, wk_ref[...], preferred_element_type=F32)
    vt = lax.dot_general(wvt_ref[...], kvn, (((1,), (1,)), ((), ())), preferred_element_type=F32)
    tm = kvn.shape[0]
    ones_rows = (lax.broadcasted_iota(I32, (V_ROWS - MLA_V, tm), 0) == 0).astype(v_ref.dtype)

    lane = lax.broadcasted_iota(I32, (1, LANES), 1)
    kr = jnp.where(lane < MLA_ROPE, krdt_ref[...], 0.0)
    ss_r = jnp.sum(kr * kr, axis=-1, keepdims=True)
    krg = kr * gk_ref[:, LANES:]
    kr_rot = krg * cosv + _swap_halves(krg) * sinv

    for h in range(MLA_HEADS):
        qh = qf[:, h * hw:(h + 1) * hw]
        r = lax.rsqrt(jnp.sum(qh * qh, axis=-1, keepdims=True) * (1.0 / MLA_QK_DIM) + EPS)
        qs = qh * (r * scale) * gq_ref[...]
        q_ref[0, h, :, 0:LANES] = qs[:, :LANES].astype(q_ref.dtype)
        qr = qs[:, LANES:]
        qr = qr * cosv + _swap_halves(qr) * sinv
        q_ref[0, h, :, LANES:MLA_QK_DIM] = qr[:, :MLA_ROPE].astype(q_ref.dtype)

        kn = kf[:, h * LANES:(h + 1) * LANES]
        rk = lax.rsqrt((jnp.sum(kn * kn, axis=-1, keepdims=True) + ss_r) * (1.0 / MLA_QK_DIM) + EPS)
        k_ref[0, h, :, 0:LANES] = (kn * rk * gk_ref[:, :LANES]).astype(k_ref.dtype)
        k_ref[0, h, :, LANES:MLA_QK_DIM] = (kr_rot * rk)[:, :MLA_ROPE].astype(k_ref.dtype)
        v_ref[0, h, 0:MLA_V, :] = vt[h * MLA_V:(h + 1) * MLA_V, :].astype(v_ref.dtype)
        v_ref[0, h, MLA_V:V_ROWS, :] = ones_rows


def _qkv_prep(ql, kvl, krdt, cos_t, sin_t, g_q_lat, wq_pad, g_kv_lat, wk, wvt, gq_pad, gk_pad, b, s, tm):
    t = b * s
    nst = s // tm
    full = lambda shape: pl.BlockSpec(shape, lambda bi, i: (0,) * len(shape))
    tok = lambda w: pl.BlockSpec((tm, w), lambda bi, i: (bi * nst + i, 0))
    hs = lambda w: pl.BlockSpec((1, MLA_HEADS, tm, w), lambda bi, i: (bi, 0, i, 0))
    return pl.pallas_call(
        functools.partial(_qkv_kernel, scale=MLA_QK_DIM ** -0.5 * math.log2(math.e)),
        grid=(b, nst),
        in_specs=[tok(MLA_Q_RANK), tok(MLA_KV_RANK), tok(LANES),
                  pl.BlockSpec((tm, LANES), lambda bi, i: (i, 0)),
                  pl.BlockSpec((tm, LANES), lambda bi, i: (i, 0)),
                  full((1, MLA_Q_RANK)), full(wq_pad.shape), full((1, MLA_KV_RANK)), full(wk.shape),
                  full(wvt.shape), full((1, 2 * LANES)), full((1, 2 * LANES))],
        out_specs=(hs(MLA_QK_DIM), hs(MLA_QK_DIM),
                   pl.BlockSpec((1, MLA_HEADS, V_ROWS, tm), lambda bi, i: (bi, 0, 0, i))),
        out_shape=(jax.ShapeDtypeStruct((b, MLA_HEADS, s, MLA_QK_DIM), BF16),
                   jax.ShapeDtypeStruct((b, MLA_HEADS, s, MLA_QK_DIM), BF16),
                   jax.ShapeDtypeStruct((b, MLA_HEADS, V_ROWS, s), BF16)),
        compiler_params=_cparams(("parallel", "parallel")),
        name="qkv_prep",
    )(ql, kvl, krdt, cos_t, sin_t, g_q_lat, wq_pad, g_kv_lat, wk, wvt, gq_pad, gk_pad)


def _flash_kernel(q_ref, k_ref, vt_ref, o_ref, m_scr, acc_scr, s_scr, *, tq):
    i = pl.program_id(2)
    q = q_ref[0, 0]
    m_scr[...] = jnp.full(m_scr.shape, NEG_BIG, F32)
    acc_scr[...] = jnp.zeros(acc_scr.shape, F32)

    def scores(j, slot):
        start = pl.multiple_of(j * tq, tq)
        ks = k_ref[0, 0, pl.ds(start, tq), :]
        s_scr[slot] = lax.dot_general(ks, q, (((1,), (1,)), ((), ())), preferred_element_type=F32)

    def softmax_pv(j, slot, masked):
        start = pl.multiple_of(j * tq, tq)
        vt = vt_ref[0, 0, :, pl.ds(start, tq)]
        st = s_scr[slot]
        if masked:
            kc = lax.broadcasted_iota(I32, (tq, tq), 0) // CHUNK
            qc = lax.broadcasted_iota(I32, (tq, tq), 1) // CHUNK
            st = jnp.where(kc <= qc, st, NEG_BIG)
        m_prev = m_scr[...]
        m_new = jnp.maximum(m_prev, jnp.max(st, axis=0, keepdims=True))
        alpha = jnp.exp2(m_prev - m_new)
        pt = jnp.exp2(st - m_new)
        acc_scr[...] = alpha * acc_scr[...] + jnp.dot(vt, pt.astype(BF16), preferred_element_type=F32)
        m_scr[...] = m_new

    scores(0, 0)

    def body(jj, carry):
        j = 2 * jj
        scores(j + 1, 1)
        softmax_pv(j, 0, False)
        scores(j + 2, 0)
        softmax_pv(j + 1, 1, False)
        return carry

    lax.fori_loop(0, i // 2, body, 0)

    @pl.when(i % 2 == 0)
    def _():
        softmax_pv(i, 0, True)

    @pl.when(i % 2 == 1)
    def _():
        scores(i, 1)
        softmax_pv(i - 1, 0, False)
        softmax_pv(i, 1, True)

    o_ref[0] = (acc_scr[0:MLA_V, :] / acc_scr[MLA_V:MLA_V + 1, :]).T.astype(o_ref.dtype)


def _flash(q, k, v, tq):
    b, nh, s, _ = q.shape
    return pl.pallas_call(
        functools.partial(_flash_kernel, tq=tq),
        grid=(b, nh, s // tq),
        in_specs=[pl.BlockSpec((1, 1, tq, MLA_QK_DIM), lambda bi, h, i: (bi, h, i, 0)),
                  pl.BlockSpec((1, 1, s, MLA_QK_DIM), lambda bi, h, i: (bi, h, 0, 0)),
                  pl.BlockSpec((1, 1, V_ROWS, s), lambda bi, h, i: (bi, h, 0, 0))],
        out_specs=pl.BlockSpec((1, tq, MLA_V), lambda bi, h, i: (bi, i, h)),
        out_shape=jax.ShapeDtypeStruct((b, s, nh * MLA_V), BF16),
        scratch_shapes=[pltpu.VMEM((1, tq), F32), pltpu.VMEM((V_ROWS, tq), F32),
                        pltpu.VMEM((2, tq, tq), F32)],
        compiler_params=_cparams(("parallel", "parallel", "arbitrary")),
        name="flash_attn",
    )(q, k, v)


def _merge_kernel(x_ref, ys_ref, om_ref, gate_ref, wss_ref, wml_ref, wo_ref, gffn_ref, wrt_ref, brt_ref,
                  x1_ref, h2_ref, idx_ref, gates_ref, rank_ref, cnt_ref, *, tm):
    d = x_ref.shape[-1]
    step = pl.program_id(0)

    @pl.when(step == 0)
    def _():
        cnt_ref[...] = jnp.zeros_like(cnt_ref)

    y1 = jnp.dot(ys_ref[...], wss_ref[...], preferred_element_type=F32)
    y2 = jnp.dot(om_ref[...], wml_ref[...], preferred_element_type=F32)
    g = gate_ref[...].astype(F32)
    merged = (g[:, :d] * y1 + g[:, d:] * y2).astype(BF16)
    x1 = x_ref[...] + jnp.dot(merged, wo_ref[...], preferred_element_type=F32)
    x1_ref[...] = x1
    ms = jnp.mean(x1 * x1, axis=-1, keepdims=True)
    h2 = x1 * lax.rsqrt(ms + EPS) * gffn_ref[...]
    h2_ref[...] = h2.astype(h2_ref.dtype)

    w_hi, w_lo = _bf16_pieces(wrt_ref[...], 2)
    h_hi, h_lo = _bf16_pieces(h2, 2)
    nt = (((1,), (1,)), ((), ()))
    logits = (lax.dot_general(w_hi, h_hi, nt, preferred_element_type=F32)
              + lax.dot_general(w_hi, h_lo, nt, preferred_element_type=F32)
              + lax.dot_general(w_lo, h_hi, nt, preferred_element_type=F32)) + brt_ref[...]
    eid = lax.broadcasted_iota(I32, (N_EXPERTS, tm), 0)
    cur = logits
    onehot = jnp.zeros((N_EXPERTS, tm), F32)
    vals, sels = [], []
    for k in range(TOP_K):
        mx = jnp.max(cur, axis=0, keepdims=True)
        idx = jnp.min(jnp.where(cur == mx, eid, N_EXPERTS), axis=0, keepdims=True)
        sel = eid == idx
        vals.append(mx)
        sels.append(sel)
        idx_ref[k:k + 1, :] = idx
        cur = jnp.where(sel, -jnp.inf, cur)
        onehot = onehot + sel.astype(F32)
    es = [jnp.exp(vk - vals[0]) for vk in vals]
    den = es[0] + es[1] + es[2] + es[3]
    for k in range(TOP_K):
        gates_ref[k:k + 1, :] = es[k] / den

    r = lax.broadcasted_iota(I32, (tm, tm), 0)
    c = lax.broadcasted_iota(I32, (tm, tm), 1)
    before = ((r < c) & ((r // MOE_SUB) == (c // MOE_SUB))).astype(BF16)
    prefix = jnp.dot(onehot.astype(BF16), before, preferred_element_type=F32)
    for k in range(TOP_K):
        rank_ref[k:k + 1, :] = jnp.sum(jnp.where(sels[k], prefix, 0.0), axis=0, keepdims=True).astype(I32)
    lane = lax.broadcasted_iota(I32, (N_EXPERTS, LANES), 1)
    cnt = cnt_ref[...]
    for g in range(tm // MOE_SUB):
        c_g = jnp.sum(onehot[:, g * MOE_SUB:(g + 1) * MOE_SUB], axis=1, keepdims=True)
        cnt = jnp.where(lane == step * (tm // MOE_SUB) + g, c_g, cnt)
    cnt_ref[...] = cnt


def _merge(x2, y_ssd, o_mla, gate, w_ss, w_ml, w_o, g_ffn, w_rt, b_rt, tm):
    t, d = x2.shape
    full = lambda shape: pl.BlockSpec(shape, lambda i: (0,) * len(shape))
    tok = lambda w: pl.BlockSpec((tm, w), lambda i: (i, 0))
    sel = pl.BlockSpec((TOP_K, tm), lambda i: (0, i))
    return pl.pallas_call(
        functools.partial(_merge_kernel, tm=tm),
        grid=(t // tm,),
        in_specs=[tok(d), tok(d), tok(d), tok(2 * d), full((d, d)), full((d, d)), full((d, d)),
                  full((1, d)), full((N_EXPERTS, d)), full((N_EXPERTS, 1))],
        out_specs=(tok(d), tok(d), sel, sel, sel, full((N_EXPERTS, LANES))),
        out_shape=(jax.ShapeDtypeStruct((t, d), F32), jax.ShapeDtypeStruct((t, d), BF16),
                   jax.ShapeDtypeStruct((TOP_K, t), I32), jax.ShapeDtypeStruct((TOP_K, t), F32),
                   jax.ShapeDtypeStruct((TOP_K, t), I32), jax.ShapeDtypeStruct((N_EXPERTS, LANES), F32)),
        compiler_params=_cparams(("arbitrary",)),
        name="merge_route",
    )(x2, y_ssd, o_mla, gate, w_ss, w_ml, w_o, g_ffn, w_rt, b_rt)


def _excl_cumsum_rows(v):
    eid = lax.broadcasted_iota(I32, v.shape, 0)
    out = jnp.zeros(v.shape, F32)
    for e in range(N_EXPERTS - 1):
        out = out + jnp.where(eid > e, v[e:e + 1, :], 0.0)
    return out


def _tables_kernel(cnt_ref, idx_ref, rank_ref, slot_ref, seg_ref, loff_ref, gdst_ref, blk_ref, tail_ref, *,
                   nblk_pad):
    t = idx_ref.shape[1]
    cnt = cnt_ref[...]
    seg = jnp.ceil(cnt * (1.0 / SEG_ALIGN)) * SEG_ALIGN
    loff = _excl_cumsum_rows(seg)
    r = lax.broadcasted_iota(I32, (LANES, LANES), 0)
    c = lax.broadcasted_iota(I32, (LANES, LANES), 1)
    run = jnp.dot((seg * (1.0 / SEG_ALIGN)).astype(BF16), (r < c).astype(BF16),
                  preferred_element_type=F32) * SEG_ALIGN
    tot = jnp.sum(seg, axis=1, keepdims=True)
    padded = jnp.broadcast_to(jnp.ceil(tot * (1.0 / MOE_BLOCK)) * MOE_BLOCK, (N_EXPERTS, LANES))
    start = _excl_cumsum_rows(padded)
    end = start + padded
    seg_ref[...] = seg.astype(I32)
    loff_ref[...] = loff.astype(I32)
    gdst_ref[...] = (start + run).astype(I32)

    gr = lax.broadcasted_iota(I32, (LANES, t), 0)
    gc = lax.broadcasted_iota(I32, (LANES, t), 1) // MOE_SUB
    loff_tok = jnp.dot((loff * (1.0 / SEG_ALIGN)).astype(BF16), (gr == gc).astype(BF16),
                       preferred_element_type=F32) * SEG_ALIGN
    idx = idx_ref[...]
    slot = rank_ref[...]
    for e in range(N_EXPERTS):
        slot = slot + jnp.where(idx == e, loff_tok[e:e + 1, :].astype(I32), 0)
    slot_ref[...] = slot

    pos = (lax.broadcasted_iota(I32, (N_EXPERTS, nblk_pad), 1) * MOE_BLOCK).astype(F32)
    nle = jnp.sum((end[:, 0:1] <= pos).astype(I32), axis=0, keepdims=True)
    blk_ref[...] = jnp.minimum(nle, N_EXPERTS - 1)
    lane = lax.broadcasted_iota(I32, (N_EXPERTS, LANES), 1)
    total = jnp.max(end, axis=0, keepdims=True)
    tail = jnp.where(lane == 0, start + tot, jnp.where(lane == 1, end, total))
    tail_ref[...] = tail.astype(I32)


def _tables(cnt, idx_t, rank_t, nblk_pad):
    t = idx_t.shape[1]
    tab = jax.ShapeDtypeStruct((N_EXPERTS, LANES), I32)
    return pl.pallas_call(
        functools.partial(_tables_kernel, nblk_pad=nblk_pad),
        out_shape=(jax.ShapeDtypeStruct((TOP_K, t), I32), tab, tab, tab,
                   jax.ShapeDtypeStruct((1, nblk_pad), I32), tab),
        compiler_params=pltpu.CompilerParams(vmem_limit_bytes=VMEM_LIMIT),
        name="route_tables",
    )(cnt, idx_t, rank_t)


def _pow2_sizes(limit):
    size = SEG_ALIGN
    while size * 2 <= limit:
        size *= 2
    sizes = []
    while size >= SEG_ALIGN:
        sizes.append(size)
        size //= 2
    return tuple(sizes)


def _piece_copies(src_ref, src_off, dst_ref, dst_off, n, limit, sem, wait=False):
    off = 0
    for size in _pow2_sizes(limit):
        take = n & size

        @pl.when(take != 0)
        def _(off=off, size=size):
            cp = pltpu.make_async_copy(
                src_ref.at[pl.ds(pl.multiple_of(src_off + off, SEG_ALIGN), size), :],
                dst_ref.at[pl.ds(pl.multiple_of(dst_off + off, SEG_ALIGN), size), :], sem)
            if wait:
                cp.wait()
            else:
                cp.start()

        off = off + take


def _dispatch_kernel(seg_ref, loff_ref, gdst_ref, taillo_ref, tailhi_ref, used_ref,
                     h_ref, slot_ref, xs_ref, loc_scr, zero_scr, sem, zsem, *, nblk):
    i = pl.program_id(0)
    par = i % 2

    @pl.when(i == 0)
    def _():
        nblk_all = xs_ref.shape[0] // MOE_BLOCK
        zero_scr[...] = jnp.zeros_like(zero_scr)

        def blk_copy(b):
            start = pl.multiple_of(b * MOE_BLOCK, MOE_BLOCK)
            return pltpu.make_async_copy(zero_scr, xs_ref.at[pl.ds(start, MOE_BLOCK), :], zsem)

        def issue_blk(b, carry):
            blk_copy(b).start()
            return carry

        def drain_blk(b, carry):
            blk_copy(b).wait()
            return carry

        lax.fori_loop(used_ref[0], nblk_all, issue_blk, 0)
        lax.fori_loop(used_ref[0], nblk_all, drain_blk, 0)

        def tails(wait):
            def body(e, carry):
                lo = taillo_ref[e]
                _piece_copies(zero_scr, 0, xs_ref, lo, tailhi_ref[e] - lo, MOE_BLOCK - 1, zsem, wait=wait)
                return carry
            return body

        lax.fori_loop(0, N_EXPERTS, tails(False), 0)
        lax.fori_loop(0, N_EXPERTS, tails(True), 0)

    slots = slot_ref[...]
    rid = lax.broadcasted_iota(I32, (LOCAL_SLOTS, MOE_SUB), 0)
    hit = rid == slots[0:1, :]
    for k in range(1, TOP_K):
        hit = hit | (rid == slots[k:k + 1, :])
    loc_scr[par] = jnp.dot(hit.astype(BF16), h_ref[...], preferred_element_type=F32)

    loc = loc_scr.at[par]

    def seg_body(e, total):
        n = seg_ref[e * LANES + i]
        _piece_copies(loc, loff_ref[e * LANES + i], xs_ref, gdst_ref[e * LANES + i], n, MOE_SUB, sem.at[par])
        return total + n

    total = lax.fori_loop(0, N_EXPERTS, seg_body, 0)
    spare = (nblk + par * (LOCAL_SLOTS // MOE_BLOCK)) * MOE_BLOCK
    _piece_copies(loc, total, xs_ref, spare + total, LOCAL_SLOTS - total, LOCAL_SLOTS, sem.at[par])

    def wait_step(p):
        pltpu.make_async_copy(loc_scr.at[p], xs_ref.at[pl.ds(0, LOCAL_SLOTS), :], sem.at[p]).wait()

    @pl.when(i > 0)
    def _():
        wait_step(1 - par)

    @pl.when(i == pl.num_programs(0) - 1)
    def _():
        wait_step(par)


def _dispatch(seg, loff, gdst, tail_lo, tail_hi, used, h2, slot_t, nblk):
    t, d = h2.shape
    rows = (nblk + 2 * (LOCAL_SLOTS // MOE_BLOCK)) * MOE_BLOCK
    return pl.pallas_call(
        functools.partial(_dispatch_kernel, nblk=nblk),
        grid_spec=pltpu.PrefetchScalarGridSpec(
            num_scalar_prefetch=6,
            grid=(t // MOE_SUB,),
            in_specs=[pl.BlockSpec((MOE_SUB, d), lambda i, *_: (i, 0)),
                      pl.BlockSpec((TOP_K, MOE_SUB), lambda i, *_: (0, i))],
            out_specs=pl.BlockSpec(memory_space=pl.ANY),
            scratch_shapes=[pltpu.VMEM((2, LOCAL_SLOTS, d), F32), pltpu.VMEM((MOE_BLOCK, d), F32),
                            pltpu.SemaphoreType.DMA((2,)), pltpu.SemaphoreType.DMA(())]),
        out_shape=jax.ShapeDtypeStruct((rows, d), F32),
        compiler_params=_cparams(("arbitrary",)),
        name="moe_dispatch",
    )(seg, loff, gdst, tail_lo, tail_hi, used, h2, slot_t)


def _expert_kernel(blk_ref, used_ref, x_ref, wgu_ref, bgu_ref, wd_ref, bd_ref, y_ref, wgu_scr, wd_scr):
    b = pl.program_id(0)
    ff = wd_ref.shape[1]
    prev = blk_ref[jnp.maximum(b - 1, 0)]
    changed = (b == 0) | (blk_ref[b] != prev)

    @pl.when(changed)
    def _():
        wgu_scr[...] = wgu_ref[0].astype(BF16)
        wd_scr[...] = wd_ref[0].astype(BF16)

    @pl.when(b < used_ref[0])
    def _():
        xb = x_ref[...].astype(BF16)
        gu = jnp.dot(xb, wgu_scr[...], preferred_element_type=F32) + bgu_ref[0]
        glu = jnp.minimum(gu[:, :ff], SWIGLU_LIMIT)
        lin = jnp.clip(gu[:, ff:], -SWIGLU_LIMIT, SWIGLU_LIMIT)
        act = glu * _sigmoid(SWIGLU_ALPHA * glu) * (lin + 1.0)
        y_ref[...] = jnp.dot(act.astype(BF16), wd_scr[...], preferred_element_type=F32) + bd_ref[0]

    @pl.when(b >= used_ref[0])
    def _():
        y_ref[...] = jnp.zeros_like(y_ref)


def _experts(blk_e, used, xs, w_gate_up, b_gate_up, w_down, b_down):
    rows, d = xs.shape
    ne, _, ff2 = w_gate_up.shape
    ff = ff2 // 2
    nblk = rows // MOE_BLOCK
    row_map = lambda b, blk, used: (jnp.minimum(b, used[0] - 1), 0)
    return pl.pallas_call(
        _expert_kernel,
        grid_spec=pltpu.PrefetchScalarGridSpec(
            num_scalar_prefetch=2,
            grid=(nblk,),
            in_specs=[pl.BlockSpec((MOE_BLOCK, d), row_map),
                      pl.BlockSpec((1, d, ff2), lambda b, blk, used: (blk[b], 0, 0)),
                      pl.BlockSpec((1, 1, ff2), lambda b, blk, used: (blk[b], 0, 0)),
                      pl.BlockSpec((1, ff, d), lambda b, blk, used: (blk[b], 0, 0)),
                      pl.BlockSpec((1, 1, d), lambda b, blk, used: (blk[b], 0, 0))],
            out_specs=pl.BlockSpec((MOE_BLOCK, d), lambda b, blk, used: (b, 0)),
            scratch_shapes=[pltpu.VMEM((d, ff2), BF16), pltpu.VMEM((ff, d), BF16)]),
        out_shape=jax.ShapeDtypeStruct((rows, d), F32),
        compiler_params=_cparams(("arbitrary",)),
        name="moe_experts",
    )(blk_e, used, xs, w_gate_up, b_gate_up.reshape(ne, 1, ff2), w_down, b_down.reshape(ne, 1, d))


def _combine_kernel(seg_ref, loff_ref, gdst_ref, x1_ref, slot_ref, gates_ref, ys_ref, o_ref, loc_scr, sem):
    i = pl.program_id(0)
    par = i % 2

    def fetch(g, p):
        loc = loc_scr.at[p]

        def seg_body(e, total):
            n = seg_ref[e * LANES + g]
            _piece_copies(ys_ref, gdst_ref[e * LANES + g], loc, loff_ref[e * LANES + g], n, MOE_SUB, sem.at[p])
            return total + n

        total = lax.fori_loop(0, N_EXPERTS, seg_body, 0)
        _piece_copies(ys_ref, total, loc, total, LOCAL_SLOTS - total, LOCAL_SLOTS, sem.at[p])

    @pl.when(i == 0)
    def _():
        fetch(0, 0)

    @pl.when(i + 1 < pl.num_programs(0))
    def _():
        fetch(i + 1, 1 - par)

    pltpu.make_async_copy(ys_ref.at[pl.ds(0, LOCAL_SLOTS), :], loc_scr.at[par], sem.at[par]).wait()

    slots = slot_ref[...]
    g = gates_ref[...]
    rid = lax.broadcasted_iota(I32, (MOE_SUB, LOCAL_SLOTS), 1)
    gmat = jnp.where(rid == slots[:, 0:1], g[:, 0:1], 0.0)
    for k in range(1, TOP_K):
        gmat = gmat + jnp.where(rid == slots[:, k:k + 1], g[:, k:k + 1], 0.0)
    o_ref[...] = x1_ref[...] + jnp.dot(gmat.astype(BF16), loc_scr[par].astype(BF16),
                                       preferred_element_type=F32)


def _combine(seg, loff, gdst, x1, slot_tk, gates_tk, ys):
    t, d = x1.shape
    tok = lambda w: pl.BlockSpec((MOE_SUB, w), lambda i, *_: (i, 0))
    return pl.pallas_call(
        _combine_kernel,
        grid_spec=pltpu.PrefetchScalarGridSpec(
            num_scalar_prefetch=3,
            grid=(t // MOE_SUB,),
            in_specs=[tok(d), tok(TOP_K), tok(TOP_K), pl.BlockSpec(memory_space=pl.ANY)],
            out_specs=tok(d),
            scratch_shapes=[pltpu.VMEM((2, LOCAL_SLOTS, d), F32), pltpu.SemaphoreType.DMA((2,))]),
        out_shape=jax.ShapeDtypeStruct((t, d), F32),
        compiler_params=_cparams(("arbitrary",)),
        name="moe_combine",
    )(seg, loff, gdst, x1, slot_tk, gates_tk, ys)


def _rope_tables(s):
    pos = jnp.arange(s, dtype=F32)
    inv = ROPE_BASE ** (-jnp.arange(0, MLA_ROPE, 2, dtype=F32) / MLA_ROPE)
    ang = pos[:, None] * inv[None, :]
    cos, sin = jnp.cos(ang), jnp.sin(ang)
    zeros = jnp.zeros((s, LANES - MLA_ROPE), F32)
    return (jnp.concatenate([cos, cos, zeros], axis=-1),
            jnp.concatenate([-sin, sin, zeros], axis=-1))


def _pad_heads(w, width):
    lead = w.shape[:-1]
    w = w.reshape(lead + (MLA_HEADS, width))
    w = jnp.pad(w, [(0, 0)] * len(lead) + [(0, 0), (0, 2 * LANES - width)])
    return w.reshape(lead + (MLA_HEADS * 2 * LANES,))


def _layer(x, g_mix, w_in, conv_w, conv_b, dt_bias, a_log, d_skip, g_ssd_out, w_ssd_out,
           g_q_lat, w_q_up, g_kv_lat, w_kv_up, g_qk_q, g_qk_k, w_mla_out, w_o,
           g_ffn, w_router, b_router, w_gate_up, b_gate_up, w_down, b_down):
    b, s, d = x.shape
    t = b * s
    x2 = x.reshape(t, d)

    off_xbc = SSD_D_INNER
    off_dt = off_xbc + SSD_CONV_DIM
    off_ql = off_dt + SSD_HEADS
    off_kvl = off_ql + MLA_Q_RANK
    off_kr = off_kvl + MLA_KV_RANK
    off_gate = off_kr + MLA_ROPE
    pad = LANES - MLA_ROPE - SSD_HEADS
    w_cat = jnp.concatenate([
        w_in[:, :off_dt], w_in[:, off_ql:off_kr],
        w_in[:, off_kr:off_gate], w_in[:, off_dt:off_ql], jnp.zeros((d, pad), w_in.dtype),
        w_in[:, off_gate:]], axis=1).astype(BF16)
    widths = (SSD_D_INNER, SSD_CONV_DIM, MLA_Q_RANK, MLA_KV_RANK, LANES, 2 * d)

    tm = min(512, s)
    z, xbc, ql, kvl, krdt, gate = _in_proj(x2, g_mix, w_cat, widths, tm)

    lane_pad = lambda vec: jnp.pad(vec, (MLA_ROPE, LANES - MLA_ROPE - SSD_HEADS)).reshape(1, LANES)
    lt = min(256, s)
    y_ssd = _ssd(xbc.reshape(b, s, SSD_CONV_DIM), z.reshape(b, s, SSD_D_INNER), krdt.reshape(b, s, LANES),
                 conv_w, conv_b.reshape(1, -1), lane_pad(dt_bias), lane_pad(a_log),
                 jnp.repeat(d_skip, SSD_HEAD_DIM).reshape(1, -1), g_ssd_out.reshape(1, -1), lt)

    cos_t, sin_t = _rope_tables(s)
    wq_pad = _pad_heads(w_q_up, MLA_QK_DIM).astype(BF16)
    gq_pad = jnp.pad(g_qk_q, (0, 2 * LANES - MLA_QK_DIM)).reshape(1, -1)
    gk_pad = jnp.pad(g_qk_k, (0, 2 * LANES - MLA_QK_DIM)).reshape(1, -1)
    tq = min(256, s)
    w_kv_h = w_kv_up.reshape(MLA_KV_RANK, MLA_HEADS, MLA_NOPE + MLA_V)
    wk = w_kv_h[:, :, :MLA_NOPE].reshape(MLA_KV_RANK, MLA_HEADS * MLA_NOPE).astype(BF16)
    wvt = w_kv_h[:, :, MLA_NOPE:].reshape(MLA_KV_RANK, MLA_HEADS * MLA_V).T.astype(BF16)
    q, k, v = _qkv_prep(ql, kvl, krdt, cos_t, sin_t, g_q_lat.reshape(1, -1), wq_pad,
                        g_kv_lat.reshape(1, -1), wk, wvt, gq_pad, gk_pad, b, s, tq)
    o_mla = _flash(q, k, v, min(512, s))

    x1, h2, idx_t, gates_t, rank_t, cnt = _merge(
        x2, y_ssd.reshape(t, d), o_mla.reshape(t, d), gate,
        w_ssd_out.astype(BF16), w_mla_out.astype(BF16), w_o.astype(BF16),
        g_ffn.reshape(1, -1), w_router.T, b_router.reshape(-1, 1), min(512, t))

    nsub = t // MOE_SUB
    assert t % MOE_SUB == 0 and nsub <= LANES
    cap = t * TOP_K + nsub * N_EXPERTS * (SEG_ALIGN - 1) + N_EXPERTS * (MOE_BLOCK - 1)
    nblk = -(-cap // MOE_BLOCK)
    nblk_all = nblk + 2 * (LOCAL_SLOTS // MOE_BLOCK)
    nblk_pad = -(-nblk_all // LANES) * LANES
    slot_t, seg, loff, gdst, blk_e, tail = _tables(cnt, idx_t, rank_t, nblk_pad)
    used = (tail[0:1, 2] // MOE_BLOCK).astype(I32)
    seg, loff, gdst = seg.reshape(-1), loff.reshape(-1), gdst.reshape(-1)
    xs = _dispatch(seg, loff, gdst, tail[:, 0], tail[:, 1], used, h2, slot_t, nblk)
    ys = _experts(blk_e.reshape(-1)[:nblk_all], used, xs, w_gate_up, b_gate_up, w_down, b_down)
    out = _combine(seg, loff, gdst, x1, slot_t.T, gates_t.T, ys)
    return out.reshape(b, s, d)


def kernel(x, g_mix, w_in, conv_w, conv_b, dt_bias, a_log, d_skip, g_ssd_out, w_ssd_out, g_q_lat, w_q_up, g_kv_lat, w_kv_up, g_qk_q, g_qk_k, w_mla_out, w_o, g_ffn, w_router, b_router, w_gate_up, b_gate_up, w_down, b_down):
    params = (g_mix, w_in, conv_w, conv_b, dt_bias, a_log, d_skip, g_ssd_out, w_ssd_out, g_q_lat, w_q_up,
              g_kv_lat, w_kv_up, g_qk_q, g_qk_k, w_mla_out, w_o, g_ffn, w_router, b_router,
              w_gate_up, b_gate_up, w_down, b_down)
    for l in range(g_mix.shape[0]):
        x = _layer(x, *(p[l] for p in params))
    return x
```

```python
import functools
import math

import jax
import jax.numpy as jnp
from jax import lax
from jax.experimental import pallas as pl
from jax.experimental.pallas import tpu as pltpu

F32 = jnp.float32
BF16 = jnp.bfloat16
I32 = jnp.int32

EPS = 1e-6
CHUNK = 64

SSD_HEADS = 16
SSD_HEAD_DIM = 64
SSD_GROUPS = 2
SSD_STATE = 128
SSD_CONV = 4
SSD_D_INNER = SSD_HEADS * SSD_HEAD_DIM
SSD_CONV_DIM = SSD_D_INNER + 2 * SSD_GROUPS * SSD_STATE

MLA_HEADS = 8
MLA_Q_RANK = 384
MLA_KV_RANK = 256
MLA_NOPE = 128
MLA_ROPE = 64
MLA_QK_DIM = MLA_NOPE + MLA_ROPE
MLA_V = 128
V_ROWS = MLA_V + 16
ROPE_BASE = 10000.0

N_EXPERTS = 32
TOP_K = 4
SWIGLU_ALPHA = 1.702
SWIGLU_LIMIT = 7.0

LANES = 128
VMEM_LIMIT = 56 * 1024 * 1024
NEG_BIG = -1e30
LOG2E = math.log2(math.e)

MOE_BLOCK = 256
MOE_SUB = 256
SEG_ALIGN = 8
LOCAL_SLOTS = -(-(TOP_K * MOE_SUB + N_EXPERTS * (SEG_ALIGN - 1)) // MOE_BLOCK) * MOE_BLOCK


def _cparams(semantics, **kw):
    return pltpu.CompilerParams(dimension_semantics=semantics,
                                vmem_limit_bytes=VMEM_LIMIT, **kw)


def _sigmoid(v):
    return 1.0 / (1.0 + jnp.exp(-v))


def _silu(v):
    return v * _sigmoid(v)


def _bf16_pieces(t, n):
    pieces = []
    for _ in range(n - 1):
        p = t.astype(BF16)
        pieces.append(p)
        t = t - p.astype(F32)
    pieces.append(t.astype(BF16))
    return pieces


def _inproj_kernel(x_ref, g_ref, w_ref, z_ref, xbc_ref, ql_ref, kvl_ref, krdt_ref, gate_ref, *, segs):
    x = x_ref[...]
    ms = jnp.mean(x * x, axis=-1, keepdims=True)
    h = (x * lax.rsqrt(ms + EPS) * g_ref[...]).astype(BF16)
    outs = (z_ref, xbc_ref, ql_ref, kvl_ref, krdt_ref, gate_ref)
    for ref, (lo, hi) in zip(outs, segs):
        p = jnp.dot(h, w_ref[:, lo:hi], preferred_element_type=F32)
        if ref is gate_ref:
            p = _sigmoid(p)
        ref[...] = p.astype(ref.dtype)


def _in_proj(x2, g_mix, w_cat, widths, tm):
    t, d = x2.shape
    offs = [0]
    for w in widths:
        offs.append(offs[-1] + w)
    segs = tuple((offs[i], offs[i + 1]) for i in range(len(widths)))
    dts = (BF16, BF16, BF16, BF16, F32, BF16)
    out_shape = tuple(jax.ShapeDtypeStruct((t, w), dt) for w, dt in zip(widths, dts))
    out_specs = tuple(pl.BlockSpec((tm, w), lambda i: (i, 0)) for w in widths)
    return pl.pallas_call(
        functools.partial(_inproj_kernel, segs=segs),
        grid=(t // tm,),
        in_specs=[pl.BlockSpec((tm, d), lambda i: (i, 0)),
                  pl.BlockSpec((1, d), lambda i: (0, 0)),
                  pl.BlockSpec(w_cat.shape, lambda i: (0, 0))],
        out_specs=out_specs,
        out_shape=out_shape,
        compiler_params=_cparams(("parallel",)),
        name="in_proj",
    )(x2, g_mix.reshape(1, d), w_cat)


def _ssd_kernel(xbc_ref, z_ref, krdt_ref, convw_ref, convb_ref, dtb_ref, alog_ref, dskip_ref, gout_ref,
                y_ref, xext_scr, state_scr, *, lt):
    i = pl.program_id(1)
    halo = 8
    dt_lo = MLA_ROPE
    gw = SSD_D_INNER // SSD_GROUPS
    hpg = SSD_HEADS // SSD_GROUPS

    @pl.when(i == 0)
    def _():
        state_scr[...] = jnp.zeros_like(state_scr)
        xext_scr[0:halo, :] = jnp.zeros((halo, SSD_CONV_DIM), F32)

    xext_scr[halo:halo + lt, :] = xbc_ref[0].astype(F32)
    acc = jnp.broadcast_to(convb_ref[...], (lt, SSD_CONV_DIM))
    for j in range(SSD_CONV):
        acc = acc + convw_ref[j:j + 1, :] * xext_scr[pl.ds(halo - (SSD_CONV - 1) + j, lt), :]
    xext_scr[0:halo, :] = xext_scr[lt:lt + halo, :]
    xbc = _silu(acc)
    xs = xbc[:, :SSD_D_INNER]
    bm = xbc[:, SSD_D_INNER:SSD_D_INNER + SSD_GROUPS * SSD_STATE]
    cm = xbc[:, SSD_D_INNER + SSD_GROUPS * SSD_STATE:]

    lane = lax.broadcasted_iota(I32, (1, LANES), 1)
    head_lane = (lane >= dt_lo) & (lane < dt_lo + SSD_HEADS)
    v = krdt_ref[0] + dtb_ref[...]
    dt = jnp.maximum(v, 0.0) + jnp.log(1.0 + jnp.exp(-jnp.abs(v)))
    dt = jnp.where(head_lane, dt, 0.0)
    a = jnp.where(head_lane, -jnp.exp(alog_ref[...]) * LOG2E, 0.0)
    da = dt * a
    row = lax.broadcasted_iota(I32, (lt, lt), 0)
    col = lax.broadcasted_iota(I32, (lt, lt), 1)
    tri = row >= col
    tri_b = tri.astype(BF16)
    a_cum = jnp.zeros((lt, LANES), F32)
    for piece in _bf16_pieces(da, 3):
        a_cum = a_cum + jnp.dot(tri_b, piece, preferred_element_type=F32)
    a_cum_t = a_cum.T
    a_last = a_cum[lt - 1:lt, :]
    exp_a = jnp.exp2(a_cum)
    dec = jnp.exp2(a_last - a_cum)

    er = lax.broadcasted_iota(I32, (LANES, SSD_D_INNER), 0)
    ec = lax.broadcasted_iota(I32, (LANES, SSD_D_INNER), 1)
    expand = ((er - dt_lo) == (ec // SSD_HEAD_DIM)).astype(BF16)

    def widen(t, pieces):
        out = jnp.zeros((lt, SSD_D_INNER), F32)
        for piece in _bf16_pieces(t, pieces):
            out = out + jnp.dot(piece, expand, preferred_element_type=F32)
        return out

    dt_w = widen(dt, 2)
    exp_a_w = widen(exp_a, 1)
    dec_w = widen(dec, 1)
    xdt = xs * dt_w
    xdt_b = xdt.astype(BF16)
    xdec_b = (xdt * dec_w).astype(BF16)
    lane_pair = lax.broadcasted_iota(I32, (lt, LANES), 1)

    y_parts = []
    for g in range(SSD_GROUPS):
        bg = bm[:, g * SSD_STATE:(g + 1) * SSD_STATE].astype(BF16)
        cg = cm[:, g * SSD_STATE:(g + 1) * SSD_STATE].astype(BF16)
        cb = lax.dot_general(cg, bg, (((1,), (1,)), ((), ())), preferred_element_type=F32)
        for pair in range(hpg // 2):
            h0 = g * hpg + 2 * pair
            rhs = xdt_b[:, h0 * SSD_HEAD_DIM:(h0 + 2) * SSD_HEAD_DIM]
            res = []
            for h in (h0, h0 + 1):
                seg = a_cum[:, dt_lo + h:dt_lo + h + 1] - a_cum_t[dt_lo + h:dt_lo + h + 1, :]
                lmat = jnp.exp2(jnp.where(tri, seg, NEG_BIG))
                res.append(jnp.dot((cb * lmat).astype(BF16), rhs, preferred_element_type=F32))
            y_parts.append(jnp.where(lane_pair < SSD_HEAD_DIM, res[0], res[1]))
        st = state_scr[g]
        y_off = jnp.dot(cg, st.astype(BF16), preferred_element_type=F32)
        y_parts.append(y_off * exp_a_w[:, g * gw:(g + 1) * gw])
        new = lax.dot_general(bg, xdec_b[:, g * gw:(g + 1) * gw], (((0,), (0,)), ((), ())),
                              preferred_element_type=F32)
        state_scr[g] = st * exp_a_w[lt - 1:lt, g * gw:(g + 1) * gw] + new

    zs = _silu(z_ref[0].astype(F32))
    outs = []
    npg = hpg // 2 + 1
    for g in range(SSD_GROUPS):
        parts = y_parts[g * npg:(g + 1) * npg]
        yd = jnp.concatenate(parts[:-1], axis=-1)
        yg = yd + parts[-1] + dskip_ref[:, g * gw:(g + 1) * gw] * xs[:, g * gw:(g + 1) * gw]
        yg = yg * zs[:, g * gw:(g + 1) * gw]
        ms = jnp.mean(yg * yg, axis=-1, keepdims=True)
        outs.append(yg * lax.rsqrt(ms + EPS) * gout_ref[:, g * gw:(g + 1) * gw])
    y_ref[0] = jnp.concatenate(outs, axis=-1).astype(y_ref.dtype)


def _ssd(xbc, z, krdt, conv_w, conv_b, dtb, alog, dskip_w, g_out, lt):
    b, s, _ = xbc.shape
    gw = SSD_D_INNER // SSD_GROUPS
    full = lambda shape: pl.BlockSpec(shape, lambda bi, i: (0,) * len(shape))
    return pl.pallas_call(
        functools.partial(_ssd_kernel, lt=lt),
        grid=(b, s // lt),
        in_specs=[pl.BlockSpec((1, lt, SSD_CONV_DIM), lambda bi, i: (bi, i, 0)),
                  pl.BlockSpec((1, lt, SSD_D_INNER), lambda bi, i: (bi, i, 0)),
                  pl.BlockSpec((1, lt, LANES), lambda bi, i: (bi, i, 0)),
                  full((SSD_CONV, SSD_CONV_DIM)), full((1, SSD_CONV_DIM)),
                  full((1, LANES)), full((1, LANES)),
                  full((1, SSD_D_INNER)), full((1, SSD_D_INNER))],
        out_specs=pl.BlockSpec((1, lt, SSD_D_INNER), lambda bi, i: (bi, i, 0)),
        out_shape=jax.ShapeDtypeStruct((b, s, SSD_D_INNER), BF16),
        scratch_shapes=[pltpu.VMEM((lt + 8, SSD_CONV_DIM), F32),
                        pltpu.VMEM((SSD_GROUPS, SSD_STATE, gw), F32)],
        compiler_params=_cparams(("parallel", "arbitrary")),
        name="ssd_scan",
    )(xbc, z, krdt, conv_w, conv_b, dtb, alog, dskip_w, g_out)


def _swap_halves(t):
    half = MLA_ROPE // 2
    return pltpu.roll(t, half, 1) + pltpu.roll(t, LANES - half, 1)


def _qkv_kernel(ql_ref, kvl_ref, krdt_ref, cos_ref, sin_ref, gql_ref, wq_ref, gkvl_ref, wk_ref, wvt_ref,
                gq_ref, gk_ref, q_ref, k_ref, v_ref, *, scale):
    hw = 2 * LANES
    cosv = cos_ref[...]
    sinv = sin_ref[...]

    ql = ql_ref[...].astype(F32)
    ms = jnp.mean(ql * ql, axis=-1, keepdims=True)
    qn = (ql * lax.rsqrt(ms + EPS) * gql_ref[...]).astype(BF16)
    qf = jnp.dot(qn, wq_ref[...], preferred_element_type=F32)

    kvl = kvl_ref[...].astype(F32)
    ms = jnp.mean(kvl * kvl, axis=-1, keepdims=True)
    kvn = (kvl * lax.rsqrt(ms + EPS) * gkvl_ref[...]).astype(BF16)
    kf = jnp.dot(kvn, wk_ref[...], preferred_element_type=F32)
    vt = lax.dot_general(wvt_ref[...], kvn, (((1,), (1,)), ((), ())), preferred_element_type=F32)
    tm = kvn.shape[0]
    ones_rows = (lax.broadcasted_iota(I32, (V_ROWS - MLA_V, tm), 0) == 0).astype(v_ref.dtype)

    lane = lax.broadcasted_iota(I32, (1, LANES), 1)
    kr = jnp.where(lane < MLA_ROPE, krdt_ref[...], 0.0)
    ss_r = jnp.sum(kr * kr, axis=-1, keepdims=True)
    krg = kr * gk_ref[:, LANES:]
    kr_rot = krg * cosv + _swap_halves(krg) * sinv

    for h in range(MLA_HEADS):
        qh = qf[:, h * hw:(h + 1) * hw]
        r = lax.rsqrt(jnp.sum(qh * qh, axis=-1, keepdims=True) * (1.0 / MLA_QK_DIM) + EPS)
        qs = qh * (r * scale) * gq_ref[...]
        q_ref[0, h, :, 0:LANES] = qs[:, :LANES].astype(q_ref.dtype)
        qr = qs[:, LANES:]
        qr = qr * cosv + _swap_halves(qr) * sinv
        q_ref[0, h, :, LANES:MLA_QK_DIM] = qr[:, :MLA_ROPE].astype(q_ref.dtype)

        kn = kf[:, h * LANES:(h + 1) * LANES]
        rk = lax.rsqrt((jnp.sum(kn * kn, axis=-1, keepdims=True) + ss_r) * (1.0 / MLA_QK_DIM) + EPS)
        k_ref[0, h, :, 0:LANES] = (kn * rk * gk_ref[:, :LANES]).astype(k_ref.dtype)
        k_ref[0, h, :, LANES:MLA_QK_DIM] = (kr_rot * rk)[:, :MLA_ROPE].astype(k_ref.dtype)
        v_ref[0, h, 0:MLA_V, :] = vt[h * MLA_V:(h + 1) * MLA_V, :].astype(v_ref.dtype)
        v_ref[0, h, MLA_V:V_ROWS, :] = ones_rows


def _qkv_prep(ql, kvl, krdt, cos_t, sin_t, g_q_lat, wq_pad, g_kv_lat, wk, wvt, gq_pad, gk_pad, b, s, tm):
    t = b * s
    nst = s // tm
    full = lambda shape: pl.BlockSpec(shape, lambda bi, i: (0,) * len(shape))
    tok = lambda w: pl.BlockSpec((tm, w), lambda bi, i: (bi * nst + i, 0))
    hs = lambda w: pl.BlockSpec((1, MLA_HEADS, tm, w), lambda bi, i: (bi, 0, i, 0))
    return pl.pallas_call(
        functools.partial(_qkv_kernel, scale=MLA_QK_DIM ** -0.5 * math.log2(math.e)),
        grid=(b, nst),
        in_specs=[tok(MLA_Q_RANK), tok(MLA_KV_RANK), tok(LANES),
                  pl.BlockSpec((tm, LANES), lambda bi, i: (i, 0)),
                  pl.BlockSpec((tm, LANES), lambda bi, i: (i, 0)),
                  full((1, MLA_Q_RANK)), full(wq_pad.shape), full((1, MLA_KV_RANK)), full(wk.shape),
                  full(wvt.shape), full((1, 2 * LANES)), full((1, 2 * LANES))],
        out_specs=(hs(MLA_QK_DIM), hs(MLA_QK_DIM),
                   pl.BlockSpec((1, MLA_HEADS, V_ROWS, tm), lambda bi, i: (bi, 0, 0, i))),
        out_shape=(jax.ShapeDtypeStruct((b, MLA_HEADS, s, MLA_QK_DIM), BF16),
                   jax.ShapeDtypeStruct((b, MLA_HEADS, s, MLA_QK_DIM), BF16),
                   jax.ShapeDtypeStruct((b, MLA_HEADS, V_ROWS, s), BF16)),
        compiler_params=_cparams(("parallel", "parallel")),
        name="qkv_prep",
    )(ql, kvl, krdt, cos_t, sin_t, g_q_lat, wq_pad, g_kv_lat, wk, wvt, gq_pad, gk_pad)


def _flash_kernel(q_ref, k_ref, vt_ref, o_ref, m_scr, acc_scr, s_scr, *, tq):
    i = pl.program_id(2)
    tk = tq // 2
    q = q_ref[0, 0]
    m_scr[...] = jnp.full(m_scr.shape, NEG_BIG, F32)
    acc_scr[...] = jnp.zeros(acc_scr.shape, F32)

    def scores(j, slot, lo=0):
        start = pl.multiple_of(j * tk, tk)
        ks = k_ref[0, 0, pl.ds(start, tk), :]
        s_scr[slot, :, lo:] = lax.dot_general(ks, q[lo:, :], (((1,), (1,)), ((), ())),
                                              preferred_element_type=F32)

    def softmax_pv(j, slot, masked, lo=0):
        start = pl.multiple_of(j * tk, tk)
        vt = vt_ref[0, 0, :, pl.ds(start, tk)]
        st = s_scr[slot, :, lo:]
        if masked:
            kc = lax.broadcasted_iota(I32, st.shape, 0) // CHUNK
            qc = lax.broadcasted_iota(I32, st.shape, 1) // CHUNK
            st = jnp.where(kc <= qc, st, NEG_BIG)
        m_prev = m_scr[:, lo:]
        m_new = jnp.maximum(m_prev, jnp.max(st, axis=0, keepdims=True))
        alpha = jnp.exp2(m_prev - m_new)
        pt = jnp.exp2(st - m_new)
        acc_scr[:, lo:] = alpha * acc_scr[:, lo:] + jnp.dot(vt, pt.astype(BF16), preferred_element_type=F32)
        m_scr[:, lo:] = m_new

    scores(0, 0)

    def body(jj, carry):
        j = 2 * jj
        scores(j + 1, 1)
        softmax_pv(j, 0, False)
        scores(j + 2, 0)
        softmax_pv(j + 1, 1, False)
        return carry

    lax.fori_loop(0, i, body, 0)
    scores(2 * i + 1, 1, lo=tk)
    softmax_pv(2 * i, 0, True)
    softmax_pv(2 * i + 1, 1, True, lo=tk)

    o_ref[0] = (acc_scr[0:MLA_V, :] / acc_scr[MLA_V:MLA_V + 1, :]).T.astype(o_ref.dtype)


def _flash(q, k, v, tq):
    b, nh, s, _ = q.shape
    return pl.pallas_call(
        functools.partial(_flash_kernel, tq=tq),
        grid=(b, nh, s // tq),
        in_specs=[pl.BlockSpec((1, 1, tq, MLA_QK_DIM), lambda bi, h, i: (bi, h, i, 0)),
                  pl.BlockSpec((1, 1, s, MLA_QK_DIM), lambda bi, h, i: (bi, h, 0, 0)),
                  pl.BlockSpec((1, 1, V_ROWS, s), lambda bi, h, i: (bi, h, 0, 0))],
        out_specs=pl.BlockSpec((1, tq, MLA_V), lambda bi, h, i: (bi, i, h)),
        out_shape=jax.ShapeDtypeStruct((b, s, nh * MLA_V), BF16),
        scratch_shapes=[pltpu.VMEM((1, tq), F32), pltpu.VMEM((V_ROWS, tq), F32),
                        pltpu.VMEM((2, tq // 2, tq), F32)],
        compiler_params=_cparams(("parallel", "parallel", "arbitrary")),
        name="flash_attn",
    )(q, k, v)


def _merge_kernel(x_ref, ys_ref, om_ref, gate_ref, wss_ref, wml_ref, wo_ref, gffn_ref, wrt_ref, brt_ref,
                  x1_ref, h2_ref, idx_ref, gates_ref, rank_ref, cnt_ref, *, tm):
    d = x_ref.shape[-1]
    step = pl.program_id(0)

    @pl.when(step == 0)
    def _():
        cnt_ref[...] = jnp.zeros_like(cnt_ref)

    y1 = jnp.dot(ys_ref[...], wss_ref[...], preferred_element_type=F32)
    y2 = jnp.dot(om_ref[...], wml_ref[...], preferred_element_type=F32)
    g = gate_ref[...].astype(F32)
    merged = (g[:, :d] * y1 + g[:, d:] * y2).astype(BF16)
    x1 = x_ref[...] + jnp.dot(merged, wo_ref[...], preferred_element_type=F32)
    x1_ref[...] = x1
    ms = jnp.mean(x1 * x1, axis=-1, keepdims=True)
    h2 = x1 * lax.rsqrt(ms + EPS) * gffn_ref[...]
    h2_ref[...] = h2.astype(h2_ref.dtype)

    w_hi, w_lo = _bf16_pieces(wrt_ref[...], 2)
    h_hi, h_lo = _bf16_pieces(h2, 2)
    nt = (((1,), (1,)), ((), ()))
    logits = (lax.dot_general(w_hi, h_hi, nt, preferred_element_type=F32)
              + lax.dot_general(w_hi, h_lo, nt, preferred_element_type=F32)
              + lax.dot_general(w_lo, h_hi, nt, preferred_element_type=F32)) + brt_ref[...]
    eid = lax.broadcasted_iota(I32, (N_EXPERTS, tm), 0)
    cur = logits
    onehot = jnp.zeros((N_EXPERTS, tm), F32)
    vals, sels = [], []
    for k in range(TOP_K):
        mx = jnp.max(cur, axis=0, keepdims=True)
        idx = jnp.min(jnp.where(cur == mx, eid, N_EXPERTS), axis=0, keepdims=True)
        sel = eid == idx
        vals.append(mx)
        sels.append(sel)
        idx_ref[k:k + 1, :] = idx
        cur = jnp.where(sel, -jnp.inf, cur)
        onehot = onehot + sel.astype(F32)
    es = [jnp.exp(vk - vals[0]) for vk in vals]
    den = es[0] + es[1] + es[2] + es[3]
    for k in range(TOP_K):
        gates_ref[k:k + 1, :] = es[k] / den

    r = lax.broadcasted_iota(I32, (tm, tm), 0)
    c = lax.broadcasted_iota(I32, (tm, tm), 1)
    before = ((r < c) & ((r // MOE_SUB) == (c // MOE_SUB))).astype(BF16)
    prefix = jnp.dot(onehot.astype(BF16), before, preferred_element_type=F32)
    for k in range(TOP_K):
        rank_ref[k:k + 1, :] = jnp.sum(jnp.where(sels[k], prefix, 0.0), axis=0, keepdims=True).astype(I32)
    lane = lax.broadcasted_iota(I32, (N_EXPERTS, LANES), 1)
    cnt = cnt_ref[...]
    for g in range(tm // MOE_SUB):
        c_g = jnp.sum(onehot[:, g * MOE_SUB:(g + 1) * MOE_SUB], axis=1, keepdims=True)
        cnt = jnp.where(lane == step * (tm // MOE_SUB) + g, c_g, cnt)
    cnt_ref[...] = cnt


def _merge(x2, y_ssd, o_mla, gate, w_ss, w_ml, w_o, g_ffn, w_rt, b_rt, tm):
    t, d = x2.shape
    full = lambda shape: pl.BlockSpec(shape, lambda i: (0,) * len(shape))
    tok = lambda w: pl.BlockSpec((tm, w), lambda i: (i, 0))
    sel = pl.BlockSpec((TOP_K, tm), lambda i: (0, i))
    return pl.pallas_call(
        functools.partial(_merge_kernel, tm=tm),
        grid=(t // tm,),
        in_specs=[tok(d), tok(d), tok(d), tok(2 * d), full((d, d)), full((d, d)), full((d, d)),
                  full((1, d)), full((N_EXPERTS, d)), full((N_EXPERTS, 1))],
        out_specs=(tok(d), tok(d), sel, sel, sel, full((N_EXPERTS, LANES))),
        out_shape=(jax.ShapeDtypeStruct((t, d), F32), jax.ShapeDtypeStruct((t, d), BF16),
                   jax.ShapeDtypeStruct((TOP_K, t), I32), jax.ShapeDtypeStruct((TOP_K, t), F32),
                   jax.ShapeDtypeStruct((TOP_K, t), I32), jax.ShapeDtypeStruct((N_EXPERTS, LANES), F32)),
        compiler_params=_cparams(("arbitrary",)),
        name="merge_route",
    )(x2, y_ssd, o_mla, gate, w_ss, w_ml, w_o, g_ffn, w_rt, b_rt)


def _excl_cumsum_rows(v):
    eid = lax.broadcasted_iota(I32, v.shape, 0)
    out = jnp.zeros(v.shape, F32)
    for e in range(N_EXPERTS - 1):
        out = out + jnp.where(eid > e, v[e:e + 1, :], 0.0)
    return out


def _tables_kernel(cnt_ref, idx_ref, rank_ref, slot_ref, seg_ref, loff_ref, gdst_ref, blk_ref, tail_ref, *,
                   nblk_pad):
    t = idx_ref.shape[1]
    cnt = cnt_ref[...]
    seg = jnp.ceil(cnt * (1.0 / SEG_ALIGN)) * SEG_ALIGN
    loff = _excl_cumsum_rows(seg)
    r = lax.broadcasted_iota(I32, (LANES, LANES), 0)
    c = lax.broadcasted_iota(I32, (LANES, LANES), 1)
    run = jnp.dot((seg * (1.0 / SEG_ALIGN)).astype(BF16), (r < c).astype(BF16),
                  preferred_element_type=F32) * SEG_ALIGN
    tot = jnp.sum(seg, axis=1, keepdims=True)
    padded = jnp.broadcast_to(jnp.ceil(tot * (1.0 / MOE_BLOCK)) * MOE_BLOCK, (N_EXPERTS, LANES))
    start = _excl_cumsum_rows(padded)
    end = start + padded
    seg_ref[...] = seg.astype(I32)
    loff_ref[...] = loff.astype(I32)
    gdst_ref[...] = (start + run).astype(I32)

    gr = lax.broadcasted_iota(I32, (LANES, t), 0)
    gc = lax.broadcasted_iota(I32, (LANES, t), 1) // MOE_SUB
    loff_tok = jnp.dot((loff * (1.0 / SEG_ALIGN)).astype(BF16), (gr == gc).astype(BF16),
                       preferred_element_type=F32) * SEG_ALIGN
    idx = idx_ref[...]
    slot = rank_ref[...]
    for e in range(N_EXPERTS):
        slot = slot + jnp.where(idx == e, loff_tok[e:e + 1, :].astype(I32), 0)
    slot_ref[...] = slot

    pos = (lax.broadcasted_iota(I32, (N_EXPERTS, nblk_pad), 1) * MOE_BLOCK).astype(F32)
    nle = jnp.sum((end[:, 0:1] <= pos).astype(I32), axis=0, keepdims=True)
    blk_ref[...] = jnp.minimum(nle, N_EXPERTS - 1)
    lane = lax.broadcasted_iota(I32, (N_EXPERTS, LANES), 1)
    total = jnp.max(end, axis=0, keepdims=True)
    tail = jnp.where(lane == 0, start + tot, jnp.where(lane == 1, end, total))
    tail_ref[...] = tail.astype(I32)


def _tables(cnt, idx_t, rank_t, nblk_pad):
    t = idx_t.shape[1]
    tab = jax.ShapeDtypeStruct((N_EXPERTS, LANES), I32)
    return pl.pallas_call(
        functools.partial(_tables_kernel, nblk_pad=nblk_pad),
        out_shape=(jax.ShapeDtypeStruct((TOP_K, t), I32), tab, tab, tab,
                   jax.ShapeDtypeStruct((1, nblk_pad), I32), tab),
        compiler_params=pltpu.CompilerParams(vmem_limit_bytes=VMEM_LIMIT),
        name="route_tables",
    )(cnt, idx_t, rank_t)


def _pow2_sizes(limit):
    size = SEG_ALIGN
    while size * 2 <= limit:
        size *= 2
    sizes = []
    while size >= SEG_ALIGN:
        sizes.append(size)
        size //= 2
    return tuple(sizes)


def _piece_copies(src_ref, src_off, dst_ref, dst_off, n, limit, sem, wait=False):
    off = 0
    for size in _pow2_sizes(limit):
        take = n & size

        @pl.when(take != 0)
        def _(off=off, size=size):
            cp = pltpu.make_async_copy(
                src_ref.at[pl.ds(pl.multiple_of(src_off + off, SEG_ALIGN), size), :],
                dst_ref.at[pl.ds(pl.multiple_of(dst_off + off, SEG_ALIGN), size), :], sem)
            if wait:
                cp.wait()
            else:
                cp.start()

        off = off + take


def _dispatch_kernel(seg_ref, loff_ref, gdst_ref, taillo_ref, tailhi_ref, used_ref,
                     h_ref, slot_ref, xs_ref, loc_scr, zero_scr, sem, zsem, *, nblk):
    i = pl.program_id(0)
    par = i % 2

    @pl.when(i == 0)
    def _():
        nblk_all = xs_ref.shape[0] // MOE_BLOCK
        zero_scr[...] = jnp.zeros_like(zero_scr)

        def blk_copy(b):
            start = pl.multiple_of(b * MOE_BLOCK, MOE_BLOCK)
            return pltpu.make_async_copy(zero_scr, xs_ref.at[pl.ds(start, MOE_BLOCK), :], zsem)

        def issue_blk(b, carry):
            blk_copy(b).start()
            return carry

        def drain_blk(b, carry):
            blk_copy(b).wait()
            return carry

        lax.fori_loop(used_ref[0], nblk_all, issue_blk, 0)
        lax.fori_loop(used_ref[0], nblk_all, drain_blk, 0)

        def tails(wait):
            def body(e, carry):
                lo = taillo_ref[e]
                _piece_copies(zero_scr, 0, xs_ref, lo, tailhi_ref[e] - lo, MOE_BLOCK - 1, zsem, wait=wait)
                return carry
            return body

        lax.fori_loop(0, N_EXPERTS, tails(False), 0)
        lax.fori_loop(0, N_EXPERTS, tails(True), 0)

    slots = slot_ref[...]
    rid = lax.broadcasted_iota(I32, (LOCAL_SLOTS, MOE_SUB), 0)
    hit = rid == slots[0:1, :]
    for k in range(1, TOP_K):
        hit = hit | (rid == slots[k:k + 1, :])
    loc_scr[par] = jnp.dot(hit.astype(BF16), h_ref[...], preferred_element_type=F32)

    loc = loc_scr.at[par]

    def seg_body(e, total):
        n = seg_ref[e * LANES + i]
        _piece_copies(loc, loff_ref[e * LANES + i], xs_ref, gdst_ref[e * LANES + i], n, MOE_SUB, sem.at[par])
        return total + n

    total = lax.fori_loop(0, N_EXPERTS, seg_body, 0)
    spare = (nblk + par * (LOCAL_SLOTS // MOE_BLOCK)) * MOE_BLOCK
    _piece_copies(loc, total, xs_ref, spare + total, LOCAL_SLOTS - total, LOCAL_SLOTS, sem.at[par])

    def wait_step(p):
        pltpu.make_async_copy(loc_scr.at[p], xs_ref.at[pl.ds(0, LOCAL_SLOTS), :], sem.at[p]).wait()

    @pl.when(i > 0)
    def _():
        wait_step(1 - par)

    @pl.when(i == pl.num_programs(0) - 1)
    def _():
        wait_step(par)


def _dispatch(seg, loff, gdst, tail_lo, tail_hi, used, h2, slot_t, nblk):
    t, d = h2.shape
    rows = (nblk + 2 * (LOCAL_SLOTS // MOE_BLOCK)) * MOE_BLOCK
    return pl.pallas_call(
        functools.partial(_dispatch_kernel, nblk=nblk),
        grid_spec=pltpu.PrefetchScalarGridSpec(
            num_scalar_prefetch=6,
            grid=(t // MOE_SUB,),
            in_specs=[pl.BlockSpec((MOE_SUB, d), lambda i, *_: (i, 0)),
                      pl.BlockSpec((TOP_K, MOE_SUB), lambda i, *_: (0, i))],
            out_specs=pl.BlockSpec(memory_space=pl.ANY),
            scratch_shapes=[pltpu.VMEM((2, LOCAL_SLOTS, d), F32), pltpu.VMEM((MOE_BLOCK, d), F32),
                            pltpu.SemaphoreType.DMA((2,)), pltpu.SemaphoreType.DMA(())]),
        out_shape=jax.ShapeDtypeStruct((rows, d), F32),
        compiler_params=_cparams(("arbitrary",)),
        name="moe_dispatch",
    )(seg, loff, gdst, tail_lo, tail_hi, used, h2, slot_t)


def _expert_kernel(blk_ref, used_ref, x_ref, wgu_ref, bgu_ref, wd_ref, bd_ref, y_ref, wgu_scr, wd_scr):
    b = pl.program_id(0)
    ff = wd_ref.shape[1]
    prev = blk_ref[jnp.maximum(b - 1, 0)]
    changed = (b == 0) | (blk_ref[b] != prev)

    @pl.when(changed)
    def _():
        wgu_scr[...] = wgu_ref[0].astype(BF16)
        wd_scr[...] = wd_ref[0].astype(BF16)

    @pl.when(b < used_ref[0])
    def _():
        xb = x_ref[...].astype(BF16)
        gu = jnp.dot(xb, wgu_scr[...], preferred_element_type=F32) + bgu_ref[0]
        glu = jnp.minimum(gu[:, :ff], SWIGLU_LIMIT)
        lin = jnp.clip(gu[:, ff:], -SWIGLU_LIMIT, SWIGLU_LIMIT)
        act = glu * _sigmoid(SWIGLU_ALPHA * glu) * (lin + 1.0)
        y_ref[...] = jnp.dot(act.astype(BF16), wd_scr[...], preferred_element_type=F32) + bd_ref[0]

    @pl.when(b >= used_ref[0])
    def _():
        y_ref[...] = jnp.zeros_like(y_ref)


def _experts(blk_e, used, xs, w_gate_up, b_gate_up, w_down, b_down):
    rows, d = xs.shape
    ne, _, ff2 = w_gate_up.shape
    ff = ff2 // 2
    nblk = rows // MOE_BLOCK
    row_map = lambda b, blk, used: (jnp.minimum(b, used[0] - 1), 0)
    return pl.pallas_call(
        _expert_kernel,
        grid_spec=pltpu.PrefetchScalarGridSpec(
            num_scalar_prefetch=2,
            grid=(nblk,),
            in_specs=[pl.BlockSpec((MOE_BLOCK, d), row_map),
                      pl.BlockSpec((1, d, ff2), lambda b, blk, used: (blk[b], 0, 0)),
                      pl.BlockSpec((1, 1, ff2), lambda b, blk, used: (blk[b], 0, 0)),
                      pl.BlockSpec((1, ff, d), lambda b, blk, used: (blk[b], 0, 0)),
                      pl.BlockSpec((1, 1, d), lambda b, blk, used: (blk[b], 0, 0))],
            out_specs=pl.BlockSpec((MOE_BLOCK, d), lambda b, blk, used: (b, 0)),
            scratch_shapes=[pltpu.VMEM((d, ff2), BF16), pltpu.VMEM((ff, d), BF16)]),
        out_shape=jax.ShapeDtypeStruct((rows, d), F32),
        compiler_params=_cparams(("arbitrary",)),
        name="moe_experts",
    )(blk_e, used, xs, w_gate_up, b_gate_up.reshape(ne, 1, ff2), w_down, b_down.reshape(ne, 1, d))


def _combine_kernel(seg_ref, loff_ref, gdst_ref, x1_ref, slot_ref, gates_ref, ys_ref, o_ref, loc_scr, sem):
    i = pl.program_id(0)
    par = i % 2

    def fetch(g, p):
        loc = loc_scr.at[p]

        def seg_body(e, total):
            n = seg_ref[e * LANES + g]
            _piece_copies(ys_ref, gdst_ref[e * LANES + g], loc, loff_ref[e * LANES + g], n, MOE_SUB, sem.at[p])
            return total + n

        total = lax.fori_loop(0, N_EXPERTS, seg_body, 0)
        _piece_copies(ys_ref, total, loc, total, LOCAL_SLOTS - total, LOCAL_SLOTS, sem.at[p])

    @pl.when(i == 0)
    def _():
        fetch(0, 0)

    @pl.when(i + 1 < pl.num_programs(0))
    def _():
        fetch(i + 1, 1 - par)

    pltpu.make_async_copy(ys_ref.at[pl.ds(0, LOCAL_SLOTS), :], loc_scr.at[par], sem.at[par]).wait()

    slots = slot_ref[...]
    g = gates_ref[...]
    rid = lax.broadcasted_iota(I32, (MOE_SUB, LOCAL_SLOTS), 1)
    gmat = jnp.where(rid == slots[:, 0:1], g[:, 0:1], 0.0)
    for k in range(1, TOP_K):
        gmat = gmat + jnp.where(rid == slots[:, k:k + 1], g[:, k:k + 1], 0.0)
    o_ref[...] = x1_ref[...] + jnp.dot(gmat.astype(BF16), loc_scr[par].astype(BF16),
                                       preferred_element_type=F32)


def _combine(seg, loff, gdst, x1, slot_tk, gates_tk, ys):
    t, d = x1.shape
    tok = lambda w: pl.BlockSpec((MOE_SUB, w), lambda i, *_: (i, 0))
    return pl.pallas_call(
        _combine_kernel,
        grid_spec=pltpu.PrefetchScalarGridSpec(
            num_scalar_prefetch=3,
            grid=(t // MOE_SUB,),
            in_specs=[tok(d), tok(TOP_K), tok(TOP_K), pl.BlockSpec(memory_space=pl.ANY)],
            out_specs=tok(d),
            scratch_shapes=[pltpu.VMEM((2, LOCAL_SLOTS, d), F32), pltpu.SemaphoreType.DMA((2,))]),
        out_shape=jax.ShapeDtypeStruct((t, d), F32),
        compiler_params=_cparams(("arbitrary",)),
        name="moe_combine",
    )(seg, loff, gdst, x1, slot_tk, gates_tk, ys)


def _rope_tables(s):
    pos = jnp.arange(s, dtype=F32)
    inv = ROPE_BASE ** (-jnp.arange(0, MLA_ROPE, 2, dtype=F32) / MLA_ROPE)
    ang = pos[:, None] * inv[None, :]
    cos, sin = jnp.cos(ang), jnp.sin(ang)
    zeros = jnp.zeros((s, LANES - MLA_ROPE), F32)
    return (jnp.concatenate([cos, cos, zeros], axis=-1),
            jnp.concatenate([-sin, sin, zeros], axis=-1))


def _pad_heads(w, width):
    lead = w.shape[:-1]
    w = w.reshape(lead + (MLA_HEADS, width))
    w = jnp.pad(w, [(0, 0)] * len(lead) + [(0, 0), (0, 2 * LANES - width)])
    return w.reshape(lead + (MLA_HEADS * 2 * LANES,))


def _layer(x, g_mix, w_in, conv_w, conv_b, dt_bias, a_log, d_skip, g_ssd_out, w_ssd_out,
           g_q_lat, w_q_up, g_kv_lat, w_kv_up, g_qk_q, g_qk_k, w_mla_out, w_o,
           g_ffn, w_router, b_router, w_gate_up, b_gate_up, w_down, b_down):
    b, s, d = x.shape
    t = b * s
    x2 = x.reshape(t, d)

    off_xbc = SSD_D_INNER
    off_dt = off_xbc + SSD_CONV_DIM
    off_ql = off_dt + SSD_HEADS
    off_kvl = off_ql + MLA_Q_RANK
    off_kr = off_kvl + MLA_KV_RANK
    off_gate = off_kr + MLA_ROPE
    pad = LANES - MLA_ROPE - SSD_HEADS
    w_cat = jnp.concatenate([
        w_in[:, :off_dt], w_in[:, off_ql:off_kr],
        w_in[:, off_kr:off_gate], w_in[:, off_dt:off_ql], jnp.zeros((d, pad), w_in.dtype),
        w_in[:, off_gate:]], axis=1).astype(BF16)
    widths = (SSD_D_INNER, SSD_CONV_DIM, MLA_Q_RANK, MLA_KV_RANK, LANES, 2 * d)

    tm = min(512, s)
    z, xbc, ql, kvl, krdt, gate = _in_proj(x2, g_mix, w_cat, widths, tm)

    lane_pad = lambda vec: jnp.pad(vec, (MLA_ROPE, LANES - MLA_ROPE - SSD_HEADS)).reshape(1, LANES)
    lt = min(256, s)
    y_ssd = _ssd(xbc.reshape(b, s, SSD_CONV_DIM), z.reshape(b, s, SSD_D_INNER), krdt.reshape(b, s, LANES),
                 conv_w, conv_b.reshape(1, -1), lane_pad(dt_bias), lane_pad(a_log),
                 jnp.repeat(d_skip, SSD_HEAD_DIM).reshape(1, -1), g_ssd_out.reshape(1, -1), lt)

    cos_t, sin_t = _rope_tables(s)
    wq_pad = _pad_heads(w_q_up, MLA_QK_DIM).astype(BF16)
    gq_pad = jnp.pad(g_qk_q, (0, 2 * LANES - MLA_QK_DIM)).reshape(1, -1)
    gk_pad = jnp.pad(g_qk_k, (0, 2 * LANES - MLA_QK_DIM)).reshape(1, -1)
    tq = min(256, s)
    w_kv_h = w_kv_up.reshape(MLA_KV_RANK, MLA_HEADS, MLA_NOPE + MLA_V)
    wk = w_kv_h[:, :, :MLA_NOPE].reshape(MLA_KV_RANK, MLA_HEADS * MLA_NOPE).astype(BF16)
    wvt = w_kv_h[:, :, MLA_NOPE:].reshape(MLA_KV_RANK, MLA_HEADS * MLA_V).T.astype(BF16)
    q, k, v = _qkv_prep(ql, kvl, krdt, cos_t, sin_t, g_q_lat.reshape(1, -1), wq_pad,
                        g_kv_lat.reshape(1, -1), wk, wvt, gq_pad, gk_pad, b, s, tq)
    o_mla = _flash(q, k, v, min(1024, s))

    x1, h2, idx_t, gates_t, rank_t, cnt = _merge(
        x2, y_ssd.reshape(t, d), o_mla.reshape(t, d), gate,
        w_ssd_out.astype(BF16), w_mla_out.astype(BF16), w_o.astype(BF16),
        g_ffn.reshape(1, -1), w_router.T, b_router.reshape(-1, 1), min(512, t))

    nsub = t // MOE_SUB
    assert t % MOE_SUB == 0 and nsub <= LANES
    cap = t * TOP_K + nsub * N_EXPERTS * (SEG_ALIGN - 1) + N_EXPERTS * (MOE_BLOCK - 1)
    nblk = -(-cap // MOE_BLOCK)
    nblk_all = nblk + 2 * (LOCAL_SLOTS // MOE_BLOCK)
    nblk_pad = -(-nblk_all // LANES) * LANES
    slot_t, seg, loff, gdst, blk_e, tail = _tables(cnt, idx_t, rank_t, nblk_pad)
    used = (tail[0:1, 2] // MOE_BLOCK).astype(I32)
    seg, loff, gdst = seg.reshape(-1), loff.reshape(-1), gdst.reshape(-1)
    xs = _dispatch(seg, loff, gdst, tail[:, 0], tail[:, 1], used, h2, slot_t, nblk)
    ys = _experts(blk_e.reshape(-1)[:nblk_all], used, xs, w_gate_up, b_gate_up, w_down, b_down)
    out = _combine(seg, loff, gdst, x1, slot_t.T, gates_t.T, ys)
    return out.reshape(b, s, d)


def kernel(x, g_mix, w_in, conv_w, conv_b, dt_bias, a_log, d_skip, g_ssd_out, w_ssd_out, g_q_lat, w_q_up, g_kv_lat, w_kv_up, g_qk_q, g_qk_k, w_mla_out, w_o, g_ffn, w_router, b_router, w_gate_up, b_gate_up, w_down, b_down):
    params = (g_mix, w_in, conv_w, conv_b, dt_bias, a_log, d_skip, g_ssd_out, w_ssd_out, g_q_lat, w_q_up,
              g_kv_lat, w_kv_up, g_qk_q, g_qk_k, w_mla_out, w_o, g_ffn, w_router, b_router,
              w_gate_up, b_gate_up, w_down, b_down)
    for l in range(g_mix.shape[0]):
        x = _layer(x, *(p[l] for p in params))
    return x
```

```python
import functools
import math

import jax
import jax.numpy as jnp
from jax import lax
from jax.experimental import pallas as pl
from jax.experimental.pallas import tpu as pltpu

F32 = jnp.float32
BF16 = jnp.bfloat16
I32 = jnp.int32

EPS = 1e-6
CHUNK = 64

SSD_HEADS = 16
SSD_HEAD_DIM = 64
SSD_GROUPS = 2
SSD_STATE = 128
SSD_CONV = 4
SSD_D_INNER = SSD_HEADS * SSD_HEAD_DIM
SSD_CONV_DIM = SSD_D_INNER + 2 * SSD_GROUPS * SSD_STATE

MLA_HEADS = 8
MLA_Q_RANK = 384
MLA_KV_RANK = 256
MLA_NOPE = 128
MLA_ROPE = 64
MLA_QK_DIM = MLA_NOPE + MLA_ROPE
MLA_V = 128
V_ROWS = MLA_V + 16
DT_LO = MLA_ROPE // 2
ROPE_BASE = 10000.0

N_EXPERTS = 32
TOP_K = 4
SWIGLU_ALPHA = 1.702
SWIGLU_LIMIT = 7.0

LANES = 128
VMEM_LIMIT = 56 * 1024 * 1024
NEG_BIG = -1e30
LOG2E = math.log2(math.e)

MOE_BLOCK = 256
MOE_SUB = 256
SEG_ALIGN = 8
LOCAL_SLOTS = -(-(TOP_K * MOE_SUB + N_EXPERTS * (SEG_ALIGN - 1)) // MOE_BLOCK) * MOE_BLOCK


def _cparams(semantics, **kw):
    return pltpu.CompilerParams(dimension_semantics=semantics,
                                vmem_limit_bytes=VMEM_LIMIT, **kw)


def _sigmoid(v):
    return 1.0 / (1.0 + jnp.exp(-v))


def _silu(v):
    return v * _sigmoid(v)


def _bf16_pieces(t, n):
    pieces = []
    for _ in range(n - 1):
        p = t.astype(BF16)
        pieces.append(p)
        t = t - p.astype(F32)
    pieces.append(t.astype(BF16))
    return pieces


def _inproj_kernel(x_ref, g_ref, w_ref, z_ref, xbc_ref, ql_ref, kvl_ref, krdt_ref, gate_ref, *, segs):
    x = x_ref[...]
    ms = jnp.mean(x * x, axis=-1, keepdims=True)
    h = (x * lax.rsqrt(ms + EPS) * g_ref[...]).astype(BF16)
    outs = (z_ref, xbc_ref, ql_ref, kvl_ref, krdt_ref, gate_ref)
    for ref, (lo, hi) in zip(outs, segs):
        p = jnp.dot(h, w_ref[:, lo:hi], preferred_element_type=F32)
        if ref is gate_ref:
            p = _sigmoid(p)
        ref[...] = p.astype(ref.dtype)


def _in_proj(x2, g_mix, w_cat, widths, tm):
    t, d = x2.shape
    offs = [0]
    for w in widths:
        offs.append(offs[-1] + w)
    segs = tuple((offs[i], offs[i + 1]) for i in range(len(widths)))
    dts = (BF16, BF16, BF16, BF16, F32, BF16)
    out_shape = tuple(jax.ShapeDtypeStruct((t, w), dt) for w, dt in zip(widths, dts))
    out_specs = tuple(pl.BlockSpec((tm, w), lambda i: (i, 0)) for w in widths)
    return pl.pallas_call(
        functools.partial(_inproj_kernel, segs=segs),
        grid=(t // tm,),
        in_specs=[pl.BlockSpec((tm, d), lambda i: (i, 0)),
                  pl.BlockSpec((1, d), lambda i: (0, 0)),
                  pl.BlockSpec(w_cat.shape, lambda i: (0, 0))],
        out_specs=out_specs,
        out_shape=out_shape,
        compiler_params=_cparams(("parallel",)),
        name="in_proj",
    )(x2, g_mix.reshape(1, d), w_cat)


def _ssd_kernel(xbc_ref, z_ref, krdt_ref, convw_ref, convb_ref, dtb_ref, alog_ref, dskip_ref, gout_ref,
                y_ref, xext_scr, state_scr, *, lt):
    i = pl.program_id(1)
    halo = 8
    dt_lo = DT_LO
    gw = SSD_D_INNER // SSD_GROUPS
    hpg = SSD_HEADS // SSD_GROUPS

    @pl.when(i == 0)
    def _():
        state_scr[...] = jnp.zeros_like(state_scr)
        xext_scr[0:halo, :] = jnp.zeros((halo, SSD_CONV_DIM), F32)

    xext_scr[halo:halo + lt, :] = xbc_ref[0].astype(F32)
    acc = jnp.broadcast_to(convb_ref[...], (lt, SSD_CONV_DIM))
    for j in range(SSD_CONV):
        acc = acc + convw_ref[j:j + 1, :] * xext_scr[pl.ds(halo - (SSD_CONV - 1) + j, lt), :]
    xext_scr[0:halo, :] = xext_scr[lt:lt + halo, :]
    xbc = _silu(acc)
    xs = xbc[:, :SSD_D_INNER]
    bm = xbc[:, SSD_D_INNER:SSD_D_INNER + SSD_GROUPS * SSD_STATE]
    cm = xbc[:, SSD_D_INNER + SSD_GROUPS * SSD_STATE:]

    lane = lax.broadcasted_iota(I32, (1, LANES), 1)
    head_lane = (lane >= dt_lo) & (lane < dt_lo + SSD_HEADS)
    v = krdt_ref[0] + dtb_ref[...]
    dt = jnp.maximum(v, 0.0) + jnp.log(1.0 + jnp.exp(-jnp.abs(v)))
    dt = jnp.where(head_lane, dt, 0.0)
    a = jnp.where(head_lane, -jnp.exp(alog_ref[...]) * LOG2E, 0.0)
    da = dt * a
    row = lax.broadcasted_iota(I32, (lt, lt), 0)
    col = lax.broadcasted_iota(I32, (lt, lt), 1)
    tri = row >= col
    tri_b = tri.astype(BF16)
    a_cum = jnp.zeros((lt, LANES), F32)
    for piece in _bf16_pieces(da, 3):
        a_cum = a_cum + jnp.dot(tri_b, piece, preferred_element_type=F32)
    a_cum_t = a_cum.T
    a_last = a_cum[lt - 1:lt, :]
    exp_a = jnp.exp2(a_cum)
    dec = jnp.exp2(a_last - a_cum)

    er = lax.broadcasted_iota(I32, (LANES, SSD_D_INNER), 0)
    ec = lax.broadcasted_iota(I32, (LANES, SSD_D_INNER), 1)
    expand = ((er - dt_lo) == (ec // SSD_HEAD_DIM)).astype(BF16)

    def widen(t, pieces):
        out = jnp.zeros((lt, SSD_D_INNER), F32)
        for piece in _bf16_pieces(t, pieces):
            out = out + jnp.dot(piece, expand, preferred_element_type=F32)
        return out

    dt_w = widen(dt, 2)
    exp_a_w = widen(exp_a, 1)
    dec_w = widen(dec, 1)
    xdt = xs * dt_w
    xdt_b = xdt.astype(BF16)
    xdec_b = (xdt * dec_w).astype(BF16)
    lane_pair = lax.broadcasted_iota(I32, (lt, LANES), 1)

    y_parts = []
    for g in range(SSD_GROUPS):
        bg = bm[:, g * SSD_STATE:(g + 1) * SSD_STATE].astype(BF16)
        cg = cm[:, g * SSD_STATE:(g + 1) * SSD_STATE].astype(BF16)
        cb = lax.dot_general(cg, bg, (((1,), (1,)), ((), ())), preferred_element_type=F32)
        for pair in range(hpg // 2):
            h0 = g * hpg + 2 * pair
            rhs = xdt_b[:, h0 * SSD_HEAD_DIM:(h0 + 2) * SSD_HEAD_DIM]
            res = []
            for h in (h0, h0 + 1):
                seg = a_cum[:, dt_lo + h:dt_lo + h + 1] - a_cum_t[dt_lo + h:dt_lo + h + 1, :]
                lmat = jnp.exp2(jnp.where(tri, seg, NEG_BIG))
                res.append(jnp.dot((cb * lmat).astype(BF16), rhs, preferred_element_type=F32))
            y_parts.append(jnp.where(lane_pair < SSD_HEAD_DIM, res[0], res[1]))
        st = state_scr[g]
        y_off = jnp.dot(cg, st.astype(BF16), preferred_element_type=F32)
        y_parts.append(y_off * exp_a_w[:, g * gw:(g + 1) * gw])
        new = lax.dot_general(bg, xdec_b[:, g * gw:(g + 1) * gw], (((0,), (0,)), ((), ())),
                              preferred_element_type=F32)
        state_scr[g] = st * exp_a_w[lt - 1:lt, g * gw:(g + 1) * gw] + new

    zs = _silu(z_ref[0].astype(F32))
    outs = []
    npg = hpg // 2 + 1
    for g in range(SSD_GROUPS):
        parts = y_parts[g * npg:(g + 1) * npg]
        yd = jnp.concatenate(parts[:-1], axis=-1)
        yg = yd + parts[-1] + dskip_ref[:, g * gw:(g + 1) * gw] * xs[:, g * gw:(g + 1) * gw]
        yg = yg * zs[:, g * gw:(g + 1) * gw]
        ms = jnp.mean(yg * yg, axis=-1, keepdims=True)
        outs.append(yg * lax.rsqrt(ms + EPS) * gout_ref[:, g * gw:(g + 1) * gw])
    y_ref[0] = jnp.concatenate(outs, axis=-1).astype(y_ref.dtype)


def _ssd(xbc, z, krdt, conv_w, conv_b, dtb, alog, dskip_w, g_out, lt):
    b, s, _ = xbc.shape
    gw = SSD_D_INNER // SSD_GROUPS
    full = lambda shape: pl.BlockSpec(shape, lambda bi, i: (0,) * len(shape))
    return pl.pallas_call(
        functools.partial(_ssd_kernel, lt=lt),
        grid=(b, s // lt),
        in_specs=[pl.BlockSpec((1, lt, SSD_CONV_DIM), lambda bi, i: (bi, i, 0)),
                  pl.BlockSpec((1, lt, SSD_D_INNER), lambda bi, i: (bi, i, 0)),
                  pl.BlockSpec((1, lt, LANES), lambda bi, i: (bi, i, 0)),
                  full((SSD_CONV, SSD_CONV_DIM)), full((1, SSD_CONV_DIM)),
                  full((1, LANES)), full((1, LANES)),
                  full((1, SSD_D_INNER)), full((1, SSD_D_INNER))],
        out_specs=pl.BlockSpec((1, lt, SSD_D_INNER), lambda bi, i: (bi, i, 0)),
        out_shape=jax.ShapeDtypeStruct((b, s, SSD_D_INNER), BF16),
        scratch_shapes=[pltpu.VMEM((lt + 8, SSD_CONV_DIM), F32),
                        pltpu.VMEM((SSD_GROUPS, SSD_STATE, gw), F32)],
        compiler_params=_cparams(("parallel", "arbitrary")),
        name="ssd_scan",
    )(xbc, z, krdt, conv_w, conv_b, dtb, alog, dskip_w, g_out)


def _swap_halves(t):
    return pltpu.roll(t, LANES // 2, 1)


def _lane_sums(sq, width):
    ones = jnp.ones((sq.shape[1], width), BF16)
    return jnp.dot(sq.astype(BF16), ones, preferred_element_type=F32)


def _qkv_kernel(ql_ref, kvl_ref, krdt_ref, cos_ref, sin_ref, gql_ref, wq_ref, gkvl_ref, wk_ref, wvt_ref,
                gq_ref, gk_ref, q_ref, k_ref, v_ref, *, scale):
    hw = 2 * LANES
    cosv = cos_ref[...]
    sinv = sin_ref[...]

    ql = ql_ref[...].astype(F32)
    rq = lax.rsqrt(_lane_sums(ql * ql, LANES) * (1.0 / MLA_Q_RANK) + EPS)
    qn = (ql * jnp.concatenate([rq] * (MLA_Q_RANK // LANES), axis=1) * gql_ref[...]).astype(BF16)

    kvl = kvl_ref[...].astype(F32)
    rkv = lax.rsqrt(_lane_sums(kvl * kvl, LANES) * (1.0 / MLA_KV_RANK) + EPS)
    kvn = (kvl * jnp.concatenate([rkv] * (MLA_KV_RANK // LANES), axis=1) * gkvl_ref[...]).astype(BF16)
    tm = kvn.shape[0]
    ones_rows = (lax.broadcasted_iota(I32, (V_ROWS - MLA_V, tm), 0) == 0).astype(v_ref.dtype)

    lane = lax.broadcasted_iota(I32, (1, LANES), 1)
    kr = jnp.where((lane % (LANES // 2)) < MLA_ROPE // 2, krdt_ref[...], 0.0)
    ss_r = _lane_sums(kr * kr, LANES)
    krg = kr * gk_ref[:, LANES:]
    kr_rot = krg * cosv + _swap_halves(krg) * sinv

    qf = jnp.dot(qn, wq_ref[...], preferred_element_type=F32)
    kf = jnp.dot(kvn, wk_ref[...], preferred_element_type=F32)
    vt = lax.dot_general(wvt_ref[...], kvn, (((1,), (1,)), ((), ())), preferred_element_type=F32)
    for h in range(MLA_HEADS):
        qh = qf[:, h * hw:(h + 1) * hw]
        r = lax.rsqrt(_lane_sums(qh * qh, hw) * (1.0 / MLA_QK_DIM) + EPS)
        qs = qh * (r * scale) * gq_ref[...]
        q_ref[0, h, :, 0:LANES] = qs[:, :LANES].astype(q_ref.dtype)
        qr = qs[:, LANES:]
        q_ref[0, h, :, LANES:hw] = (qr * cosv + _swap_halves(qr) * sinv).astype(q_ref.dtype)

        kn = kf[:, h * LANES:(h + 1) * LANES]
        rk = lax.rsqrt((_lane_sums(kn * kn, LANES) + ss_r) * (1.0 / MLA_QK_DIM) + EPS)
        k_ref[0, h, :, 0:LANES] = (kn * rk * gk_ref[:, :LANES]).astype(k_ref.dtype)
        k_ref[0, h, :, LANES:hw] = (kr_rot * rk).astype(k_ref.dtype)
        v_ref[0, h, 0:MLA_V, :] = vt[h * MLA_V:(h + 1) * MLA_V, :].astype(v_ref.dtype)
        v_ref[0, h, MLA_V:V_ROWS, :] = ones_rows


def _qkv_prep(ql, kvl, krdt, cos_t, sin_t, g_q_lat, wq_pad, g_kv_lat, wk, wvt, gq_pad, gk_pad, b, s, tm):
    t = b * s
    nst = s // tm
    full = lambda shape: pl.BlockSpec(shape, lambda bi, i: (0,) * len(shape))
    tok = lambda w: pl.BlockSpec((tm, w), lambda bi, i: (bi * nst + i, 0))
    hs = lambda w: pl.BlockSpec((1, MLA_HEADS, tm, w), lambda bi, i: (bi, 0, i, 0))
    return pl.pallas_call(
        functools.partial(_qkv_kernel, scale=MLA_QK_DIM ** -0.5 * math.log2(math.e)),
        grid=(b, nst),
        in_specs=[tok(MLA_Q_RANK), tok(MLA_KV_RANK), tok(LANES),
                  pl.BlockSpec((tm, LANES), lambda bi, i: (i, 0)),
                  pl.BlockSpec((tm, LANES), lambda bi, i: (i, 0)),
                  full((1, MLA_Q_RANK)), full(wq_pad.shape), full((1, MLA_KV_RANK)), full(wk.shape),
                  full(wvt.shape), full((1, 2 * LANES)), full((1, 2 * LANES))],
        out_specs=(hs(2 * LANES), hs(2 * LANES),
                   pl.BlockSpec((1, MLA_HEADS, V_ROWS, tm), lambda bi, i: (bi, 0, 0, i))),
        out_shape=(jax.ShapeDtypeStruct((b, MLA_HEADS, s, 2 * LANES), BF16),
                   jax.ShapeDtypeStruct((b, MLA_HEADS, s, 2 * LANES), BF16),
                   jax.ShapeDtypeStruct((b, MLA_HEADS, V_ROWS, s), BF16)),
        compiler_params=_cparams(("parallel", "parallel")),
        name="qkv_prep",
    )(ql, kvl, krdt, cos_t, sin_t, g_q_lat, wq_pad, g_kv_lat, wk, wvt, gq_pad, gk_pad)


def _flash_kernel(q_ref, k_ref, vt_ref, o_ref, m_scr, acc_scr, s_scr, *, tq):
    i = pl.program_id(2)
    tk = tq // 2
    q = q_ref[0, 0]
    m_scr[...] = jnp.full(m_scr.shape, NEG_BIG, F32)
    acc_scr[...] = jnp.zeros(acc_scr.shape, F32)

    def scores(j, slot, lo=0):
        start = pl.multiple_of(j * tk, tk)
        ks = k_ref[0, 0, pl.ds(start, tk), :]
        s_scr[slot, :, lo:] = lax.dot_general(ks, q[lo:, :], (((1,), (1,)), ((), ())),
                                              preferred_element_type=F32)

    def softmax_pv(j, slot, masked, lo=0):
        start = pl.multiple_of(j * tk, tk)
        vt = vt_ref[0, 0, :, pl.ds(start, tk)]
        st = s_scr[slot, :, lo:]
        if masked:
            kc = lax.broadcasted_iota(I32, (tk, tk), 0) // CHUNK
            qc = lax.broadcasted_iota(I32, (tk, tk), 1) // CHUNK
            diag = jnp.where(kc <= qc, st[:, :tk], NEG_BIG)
            st = diag if st.shape[1] == tk else jnp.concatenate([diag, st[:, tk:]], axis=1)
        m_prev = m_scr[:, lo:]
        m_new = jnp.maximum(m_prev, jnp.max(st, axis=0, keepdims=True))
        alpha = jnp.exp2(m_prev - m_new)
        pt = jnp.exp2(st - m_new)
        acc_scr[:, lo:] = alpha * acc_scr[:, lo:] + jnp.dot(vt, pt.astype(BF16), preferred_element_type=F32)
        m_scr[:, lo:] = m_new

    scores(0, 0)

    def body(jj, carry):
        j = 2 * jj
        scores(j + 1, 1)
        softmax_pv(j, 0, False)
        scores(j + 2, 0)
        softmax_pv(j + 1, 1, False)
        return carry

    lax.fori_loop(0, i, body, 0)
    scores(2 * i + 1, 1, lo=tk)
    softmax_pv(2 * i, 0, True)
    softmax_pv(2 * i + 1, 1, True, lo=tk)

    o_ref[0] = (acc_scr[0:MLA_V, :] / acc_scr[MLA_V:MLA_V + 1, :]).T.astype(o_ref.dtype)


def _flash(q, k, v, tq):
    b, nh, s, _ = q.shape
    return pl.pallas_call(
        functools.partial(_flash_kernel, tq=tq),
        grid=(b, nh, s // tq),
        in_specs=[pl.BlockSpec((1, 1, tq, q.shape[-1]), lambda bi, h, i: (bi, h, i, 0)),
                  pl.BlockSpec((1, 1, s, k.shape[-1]), lambda bi, h, i: (bi, h, 0, 0)),
                  pl.BlockSpec((1, 1, V_ROWS, s), lambda bi, h, i: (bi, h, 0, 0))],
        out_specs=pl.BlockSpec((1, tq, MLA_V), lambda bi, h, i: (bi, i, h)),
        out_shape=jax.ShapeDtypeStruct((b, s, nh * MLA_V), BF16),
        scratch_shapes=[pltpu.VMEM((1, tq), F32), pltpu.VMEM((V_ROWS, tq), F32),
                        pltpu.VMEM((2, tq // 2, tq), F32)],
        compiler_params=_cparams(("parallel", "parallel", "arbitrary")),
        name="flash_attn",
    )(q, k, v)


def _merge_kernel(x_ref, ys_ref, om_ref, gate_ref, wss_ref, wml_ref, wo_ref, gffn_ref, wrt_ref, brt_ref,
                  x1_ref, h2_ref, idx_ref, gates_ref, rank_ref, cnt_ref, *, tm):
    d = x_ref.shape[-1]
    step = pl.program_id(0)

    @pl.when(step == 0)
    def _():
        cnt_ref[...] = jnp.zeros_like(cnt_ref)

    y1 = jnp.dot(ys_ref[...], wss_ref[...], preferred_element_type=F32)
    y2 = jnp.dot(om_ref[...], wml_ref[...], preferred_element_type=F32)
    g = gate_ref[...].astype(F32)
    merged = (g[:, :d] * y1 + g[:, d:] * y2).astype(BF16)
    x1 = x_ref[...] + jnp.dot(merged, wo_ref[...], preferred_element_type=F32)
    x1_ref[...] = x1
    ms = jnp.mean(x1 * x1, axis=-1, keepdims=True)
    h2 = x1 * lax.rsqrt(ms + EPS) * gffn_ref[...]
    h2_ref[...] = h2.astype(h2_ref.dtype)

    w_hi, w_lo = _bf16_pieces(wrt_ref[...], 2)
    h_hi, h_lo = _bf16_pieces(h2, 2)
    nt = (((1,), (1,)), ((), ()))
    logits = (lax.dot_general(w_hi, h_hi, nt, preferred_element_type=F32)
              + lax.dot_general(w_hi, h_lo, nt, preferred_element_type=F32)
              + lax.dot_general(w_lo, h_hi, nt, preferred_element_type=F32)) + brt_ref[...]
    eid = lax.broadcasted_iota(I32, (N_EXPERTS, tm), 0)
    cur = logits
    onehot = jnp.zeros((N_EXPERTS, tm), F32)
    vals, sels = [], []
    for k in range(TOP_K):
        mx = jnp.max(cur, axis=0, keepdims=True)
        idx = jnp.min(jnp.where(cur == mx, eid, N_EXPERTS), axis=0, keepdims=True)
        sel = eid == idx
        vals.append(mx)
        sels.append(sel)
        idx_ref[k:k + 1, :] = idx
        cur = jnp.where(sel, -jnp.inf, cur)
        onehot = onehot + sel.astype(F32)
    es = [jnp.exp(vk - vals[0]) for vk in vals]
    den = es[0] + es[1] + es[2] + es[3]
    for k in range(TOP_K):
        gates_ref[k:k + 1, :] = es[k] / den

    r = lax.broadcasted_iota(I32, (tm, tm), 0)
    c = lax.broadcasted_iota(I32, (tm, tm), 1)
    before = ((r < c) & ((r // MOE_SUB) == (c // MOE_SUB))).astype(BF16)
    prefix = jnp.dot(onehot.astype(BF16), before, preferred_element_type=F32)
    for k in range(TOP_K):
        rank_ref[k:k + 1, :] = jnp.sum(jnp.where(sels[k], prefix, 0.0), axis=0, keepdims=True).astype(I32)
    lane = lax.broadcasted_iota(I32, (N_EXPERTS, LANES), 1)
    cnt = cnt_ref[...]
    for g in range(tm // MOE_SUB):
        c_g = jnp.sum(onehot[:, g * MOE_SUB:(g + 1) * MOE_SUB], axis=1, keepdims=True)
        cnt = jnp.where(lane == step * (tm // MOE_SUB) + g, c_g, cnt)
    cnt_ref[...] = cnt


def _merge(x2, y_ssd, o_mla, gate, w_ss, w_ml, w_o, g_ffn, w_rt, b_rt, tm):
    t, d = x2.shape
    full = lambda shape: pl.BlockSpec(shape, lambda i: (0,) * len(shape))
    tok = lambda w: pl.BlockSpec((tm, w), lambda i: (i, 0))
    sel = pl.BlockSpec((TOP_K, tm), lambda i: (0, i))
    return pl.pallas_call(
        functools.partial(_merge_kernel, tm=tm),
        grid=(t // tm,),
        in_specs=[tok(d), tok(d), tok(d), tok(2 * d), full((d, d)), full((d, d)), full((d, d)),
                  full((1, d)), full((N_EXPERTS, d)), full((N_EXPERTS, 1))],
        out_specs=(tok(d), tok(d), sel, sel, sel, full((N_EXPERTS, LANES))),
        out_shape=(jax.ShapeDtypeStruct((t, d), F32), jax.ShapeDtypeStruct((t, d), BF16),
                   jax.ShapeDtypeStruct((TOP_K, t), I32), jax.ShapeDtypeStruct((TOP_K, t), F32),
                   jax.ShapeDtypeStruct((TOP_K, t), I32), jax.ShapeDtypeStruct((N_EXPERTS, LANES), F32)),
        compiler_params=_cparams(("arbitrary",)),
        name="merge_route",
    )(x2, y_ssd, o_mla, gate, w_ss, w_ml, w_o, g_ffn, w_rt, b_rt)


def _excl_cumsum_rows(v):
    eid = lax.broadcasted_iota(I32, v.shape, 0)
    out = jnp.zeros(v.shape, F32)
    for e in range(N_EXPERTS - 1):
        out = out + jnp.where(eid > e, v[e:e + 1, :], 0.0)
    return out


def _tables_kernel(cnt_ref, idx_ref, rank_ref, slot_ref, seg_ref, loff_ref, gdst_ref, tail_ref):
    t = idx_ref.shape[1]
    cnt = cnt_ref[...]
    seg = jnp.ceil(cnt * (1.0 / SEG_ALIGN)) * SEG_ALIGN
    loff = _excl_cumsum_rows(seg)
    r = lax.broadcasted_iota(I32, (LANES, LANES), 0)
    c = lax.broadcasted_iota(I32, (LANES, LANES), 1)
    run = jnp.dot((seg * (1.0 / SEG_ALIGN)).astype(BF16), (r < c).astype(BF16),
                  preferred_element_type=F32) * SEG_ALIGN
    tot = jnp.sum(seg, axis=1, keepdims=True)
    padded = jnp.broadcast_to(jnp.ceil(tot * (1.0 / MOE_BLOCK)) * MOE_BLOCK, (N_EXPERTS, LANES))
    start = _excl_cumsum_rows(padded)
    end = start + padded
    seg_ref[...] = seg.astype(I32)
    loff_ref[...] = loff.astype(I32)
    gdst_ref[...] = (start + run).astype(I32)

    gr = lax.broadcasted_iota(I32, (LANES, t), 0)
    gc = lax.broadcasted_iota(I32, (LANES, t), 1) // MOE_SUB
    loff_tok = jnp.dot((loff * (1.0 / SEG_ALIGN)).astype(BF16), (gr == gc).astype(BF16),
                       preferred_element_type=F32) * SEG_ALIGN
    idx = idx_ref[...]
    slot = rank_ref[...]
    for e in range(N_EXPERTS):
        slot = slot + jnp.where(idx == e, loff_tok[e:e + 1, :].astype(I32), 0)
    slot_ref[...] = slot

    lane = lax.broadcasted_iota(I32, (N_EXPERTS, LANES), 1)
    total = jnp.max(end, axis=0, keepdims=True)
    tail = jnp.where(lane == 0, start + tot, jnp.where(lane == 1, end, jnp.where(lane == 3, start, total)))
    tail_ref[...] = tail.astype(I32)


def _tables(cnt, idx_t, rank_t):
    t = idx_t.shape[1]
    tab = jax.ShapeDtypeStruct((N_EXPERTS, LANES), I32)
    return pl.pallas_call(
        _tables_kernel,
        out_shape=(jax.ShapeDtypeStruct((TOP_K, t), I32), tab, tab, tab, tab),
        compiler_params=pltpu.CompilerParams(vmem_limit_bytes=VMEM_LIMIT),
        name="route_tables",
    )(cnt, idx_t, rank_t)


def _pow2_sizes(limit):
    size = SEG_ALIGN
    while size * 2 <= limit:
        size *= 2
    sizes = []
    while size >= SEG_ALIGN:
        sizes.append(size)
        size //= 2
    return tuple(sizes)


def _piece_copies(src_ref, src_off, dst_ref, dst_off, n, limit, sem, wait=False):
    off = 0
    for size in _pow2_sizes(limit):
        take = n & size

        @pl.when(take != 0)
        def _(off=off, size=size):
            cp = pltpu.make_async_copy(
                src_ref.at[pl.ds(pl.multiple_of(src_off + off, SEG_ALIGN), size), :],
                dst_ref.at[pl.ds(pl.multiple_of(dst_off + off, SEG_ALIGN), size), :], sem)
            if wait:
                cp.wait()
            else:
                cp.start()

        off = off + take


def _dispatch_kernel(seg_ref, loff_ref, gdst_ref, taillo_ref, tailhi_ref, used_ref,
                     h_ref, slot_ref, xs_ref, loc_scr, zero_scr, sem, zsem, *, nblk):
    i = pl.program_id(0)
    par = i % 2

    @pl.when(i == 0)
    def _():
        nblk_all = xs_ref.shape[0] // MOE_BLOCK
        zero_scr[...] = jnp.zeros_like(zero_scr)

        def blk_copy(b):
            start = pl.multiple_of(b * MOE_BLOCK, MOE_BLOCK)
            return pltpu.make_async_copy(zero_scr, xs_ref.at[pl.ds(start, MOE_BLOCK), :], zsem)

        def issue_blk(b, carry):
            blk_copy(b).start()
            return carry

        def drain_blk(b, carry):
            blk_copy(b).wait()
            return carry

        lax.fori_loop(used_ref[0], nblk_all, issue_blk, 0)
        lax.fori_loop(used_ref[0], nblk_all, drain_blk, 0)

        def tails(wait):
            def body(e, carry):
                lo = taillo_ref[e]
                _piece_copies(zero_scr, 0, xs_ref, lo, tailhi_ref[e] - lo, MOE_BLOCK - 1, zsem, wait=wait)
                return carry
            return body

        lax.fori_loop(0, N_EXPERTS, tails(False), 0)
        lax.fori_loop(0, N_EXPERTS, tails(True), 0)

    slots = slot_ref[...]
    rid = lax.broadcasted_iota(I32, (LOCAL_SLOTS, MOE_SUB), 0)
    hit = rid == slots[0:1, :]
    for k in range(1, TOP_K):
        hit = hit | (rid == slots[k:k + 1, :])
    loc_scr[par] = jnp.dot(hit.astype(BF16), h_ref[...], preferred_element_type=F32)

    loc = loc_scr.at[par]

    def seg_body(e, total):
        n = seg_ref[e * LANES + i]
        _piece_copies(loc, loff_ref[e * LANES + i], xs_ref, gdst_ref[e * LANES + i], n, MOE_SUB, sem.at[par])
        return total + n

    total = lax.fori_loop(0, N_EXPERTS, seg_body, 0)
    spare = (nblk + par * (LOCAL_SLOTS // MOE_BLOCK)) * MOE_BLOCK
    _piece_copies(loc, total, xs_ref, spare + total, LOCAL_SLOTS - total, LOCAL_SLOTS, sem.at[par])

    def wait_step(p):
        pltpu.make_async_copy(loc_scr.at[p], xs_ref.at[pl.ds(0, LOCAL_SLOTS), :], sem.at[p]).wait()

    @pl.when(i > 0)
    def _():
        wait_step(1 - par)

    @pl.when(i == pl.num_programs(0) - 1)
    def _():
        wait_step(par)


def _dispatch(seg, loff, gdst, tail_lo, tail_hi, used, h2, slot_t, nblk):
    t, d = h2.shape
    rows = (nblk + 2 * (LOCAL_SLOTS // MOE_BLOCK)) * MOE_BLOCK
    return pl.pallas_call(
        functools.partial(_dispatch_kernel, nblk=nblk),
        grid_spec=pltpu.PrefetchScalarGridSpec(
            num_scalar_prefetch=6,
            grid=(t // MOE_SUB,),
            in_specs=[pl.BlockSpec((MOE_SUB, d), lambda i, *_: (i, 0)),
                      pl.BlockSpec((TOP_K, MOE_SUB), lambda i, *_: (0, i))],
            out_specs=pl.BlockSpec(memory_space=pl.ANY),
            scratch_shapes=[pltpu.VMEM((2, LOCAL_SLOTS, d), F32), pltpu.VMEM((MOE_BLOCK, d), F32),
                            pltpu.SemaphoreType.DMA((2,)), pltpu.SemaphoreType.DMA(())]),
        out_shape=jax.ShapeDtypeStruct((rows, d), F32),
        compiler_params=_cparams(("arbitrary",)),
        name="moe_dispatch",
    )(seg, loff, gdst, tail_lo, tail_hi, used, h2, slot_t)


def _expert_kernel(lo_ref, hi_ref, used_ref, x_ref, wgu_ref, bgu_ref, wd_ref, bd_ref, y_ref,
                   wgu_scr, wd_scr, xbuf, ybuf, xsem, ysem):
    e = pl.program_id(0)
    ff = wd_ref.shape[1]
    first = lo_ref[e] // MOE_BLOCK
    n = (hi_ref[e] - lo_ref[e]) // MOE_BLOCK

    def x_copy(j, slot):
        start = pl.multiple_of((first + j) * MOE_BLOCK, MOE_BLOCK)
        return pltpu.make_async_copy(x_ref.at[pl.ds(start, MOE_BLOCK), :], xbuf.at[slot], xsem.at[slot])

    def y_copy(j, slot):
        start = pl.multiple_of((first + j) * MOE_BLOCK, MOE_BLOCK)
        return pltpu.make_async_copy(ybuf.at[slot], y_ref.at[pl.ds(start, MOE_BLOCK), :], ysem.at[slot])

    @pl.when(n > 0)
    def _():
        x_copy(0, 0).start()
        wgu_scr[...] = wgu_ref[0].astype(BF16)
        wd_scr[...] = wd_ref[0].astype(BF16)

    def body(j, carry):
        slot = j % 2
        x_copy(j, slot).wait()

        @pl.when(j + 1 < n)
        def _():
            x_copy(j + 1, 1 - slot).start()

        @pl.when(j >= 2)
        def _():
            y_copy(j - 2, slot).wait()

        xb = xbuf[slot].astype(BF16)
        gu = jnp.dot(xb, wgu_scr[...], preferred_element_type=F32) + bgu_ref[0]
        glu = jnp.minimum(gu[:, :ff], SWIGLU_LIMIT)
        lin = jnp.clip(gu[:, ff:], -SWIGLU_LIMIT, SWIGLU_LIMIT)
        act = glu * _sigmoid(SWIGLU_ALPHA * glu) * (lin + 1.0)
        ybuf[slot] = jnp.dot(act.astype(BF16), wd_scr[...], preferred_element_type=F32) + bd_ref[0]
        y_copy(j, slot).start()
        return carry

    lax.fori_loop(0, n, body, 0)

    @pl.when(n >= 2)
    def _():
        y_copy(n - 2, n % 2).wait()

    @pl.when(n >= 1)
    def _():
        y_copy(n - 1, (n - 1) % 2).wait()

    @pl.when(e == pl.num_programs(0) - 1)
    def _():
        nblk_all = y_ref.shape[0] // MOE_BLOCK
        ybuf[0] = jnp.zeros(ybuf.shape[1:], ybuf.dtype)

        def zero_copy(b):
            start = pl.multiple_of(b * MOE_BLOCK, MOE_BLOCK)
            return pltpu.make_async_copy(ybuf.at[0], y_ref.at[pl.ds(start, MOE_BLOCK), :], ysem.at[0])

        def issue(b, carry):
            zero_copy(b).start()
            return carry

        def drain(b, carry):
            zero_copy(b).wait()
            return carry

        lax.fori_loop(used_ref[0], nblk_all, issue, 0)
        lax.fori_loop(used_ref[0], nblk_all, drain, 0)


def _experts(row_lo, row_hi, used, xs, w_gate_up, b_gate_up, w_down, b_down):
    rows, d = xs.shape
    ne, _, ff2 = w_gate_up.shape
    ff = ff2 // 2
    return pl.pallas_call(
        _expert_kernel,
        grid_spec=pltpu.PrefetchScalarGridSpec(
            num_scalar_prefetch=3,
            grid=(ne,),
            in_specs=[pl.BlockSpec(memory_space=pl.ANY),
                      pl.BlockSpec((1, d, ff2), lambda e, *_: (e, 0, 0)),
                      pl.BlockSpec((1, 1, ff2), lambda e, *_: (e, 0, 0)),
                      pl.BlockSpec((1, ff, d), lambda e, *_: (e, 0, 0)),
                      pl.BlockSpec((1, 1, d), lambda e, *_: (e, 0, 0))],
            out_specs=pl.BlockSpec(memory_space=pl.ANY),
            scratch_shapes=[pltpu.VMEM((d, ff2), BF16), pltpu.VMEM((ff, d), BF16),
                            pltpu.VMEM((2, MOE_BLOCK, d), F32), pltpu.VMEM((2, MOE_BLOCK, d), F32),
                            pltpu.SemaphoreType.DMA((2,)), pltpu.SemaphoreType.DMA((2,))]),
        out_shape=jax.ShapeDtypeStruct((rows, d), F32),
        compiler_params=_cparams(("arbitrary",)),
        name="moe_experts",
    )(row_lo, row_hi, used, xs, w_gate_up, b_gate_up.reshape(ne, 1, ff2), w_down, b_down.reshape(ne, 1, d))


def _combine_kernel(seg_ref, loff_ref, gdst_ref, x1_ref, slot_ref, gates_ref, ys_ref, o_ref, loc_scr, sem):
    i = pl.program_id(0)
    par = i % 2

    def fetch(g, p):
        loc = loc_scr.at[p]

        def seg_body(e, total):
            n = seg_ref[e * LANES + g]
            _piece_copies(ys_ref, gdst_ref[e * LANES + g], loc, loff_ref[e * LANES + g], n, MOE_SUB, sem.at[p])
            return total + n

        total = lax.fori_loop(0, N_EXPERTS, seg_body, 0)
        _piece_copies(ys_ref, total, loc, total, LOCAL_SLOTS - total, LOCAL_SLOTS, sem.at[p])

    @pl.when(i == 0)
    def _():
        fetch(0, 0)

    @pl.when(i + 1 < pl.num_programs(0))
    def _():
        fetch(i + 1, 1 - par)

    pltpu.make_async_copy(ys_ref.at[pl.ds(0, LOCAL_SLOTS), :], loc_scr.at[par], sem.at[par]).wait()

    slots = slot_ref[...]
    g = gates_ref[...]
    rid = lax.broadcasted_iota(I32, (MOE_SUB, LOCAL_SLOTS), 1)
    gmat = jnp.where(rid == slots[:, 0:1], g[:, 0:1], 0.0)
    for k in range(1, TOP_K):
        gmat = gmat + jnp.where(rid == slots[:, k:k + 1], g[:, k:k + 1], 0.0)
    o_ref[...] = x1_ref[...] + jnp.dot(gmat.astype(BF16), loc_scr[par].astype(BF16),
                                       preferred_element_type=F32)


def _combine(seg, loff, gdst, x1, slot_tk, gates_tk, ys):
    t, d = x1.shape
    tok = lambda w: pl.BlockSpec((MOE_SUB, w), lambda i, *_: (i, 0))
    return pl.pallas_call(
        _combine_kernel,
        grid_spec=pltpu.PrefetchScalarGridSpec(
            num_scalar_prefetch=3,
            grid=(t // MOE_SUB,),
            in_specs=[tok(d), tok(TOP_K), tok(TOP_K), pl.BlockSpec(memory_space=pl.ANY)],
            out_specs=tok(d),
            scratch_shapes=[pltpu.VMEM((2, LOCAL_SLOTS, d), F32), pltpu.SemaphoreType.DMA((2,))]),
        out_shape=jax.ShapeDtypeStruct((t, d), F32),
        compiler_params=_cparams(("arbitrary",)),
        name="moe_combine",
    )(seg, loff, gdst, x1, slot_tk, gates_tk, ys)


def _rope_tables(s):
    pos = jnp.arange(s, dtype=F32)
    inv = ROPE_BASE ** (-jnp.arange(0, MLA_ROPE, 2, dtype=F32) / MLA_ROPE)
    ang = pos[:, None] * inv[None, :]
    cos, sin = jnp.cos(ang), jnp.sin(ang)
    return _spread_rope(jnp.concatenate([cos, cos], axis=-1)), _spread_rope(jnp.concatenate([-sin, sin], axis=-1))


def _spread_rope(w, gap=None):
    half = MLA_ROPE // 2
    zeros = jnp.zeros(w.shape[:-1] + (half,), w.dtype)
    return jnp.concatenate([w[..., :half], zeros if gap is None else gap, w[..., half:], zeros], axis=-1)


def _pad_heads(w):
    lead = w.shape[:-1]
    w = w.reshape(lead + (MLA_HEADS, MLA_QK_DIM))
    w = jnp.concatenate([w[..., :MLA_NOPE], _spread_rope(w[..., MLA_NOPE:])], axis=-1)
    return w.reshape(lead + (MLA_HEADS * 2 * LANES,))


def _layer(x, g_mix, w_in, conv_w, conv_b, dt_bias, a_log, d_skip, g_ssd_out, w_ssd_out,
           g_q_lat, w_q_up, g_kv_lat, w_kv_up, g_qk_q, g_qk_k, w_mla_out, w_o,
           g_ffn, w_router, b_router, w_gate_up, b_gate_up, w_down, b_down):
    b, s, d = x.shape
    t = b * s
    x2 = x.reshape(t, d)

    off_xbc = SSD_D_INNER
    off_dt = off_xbc + SSD_CONV_DIM
    off_ql = off_dt + SSD_HEADS
    off_kvl = off_ql + MLA_Q_RANK
    off_kr = off_kvl + MLA_KV_RANK
    off_gate = off_kr + MLA_ROPE
    dt_gap = jnp.pad(w_in[:, off_dt:off_ql], ((0, 0), (0, MLA_ROPE // 2 - SSD_HEADS)))
    w_cat = jnp.concatenate([
        w_in[:, :off_dt], w_in[:, off_ql:off_kr],
        _spread_rope(w_in[:, off_kr:off_gate], gap=dt_gap),
        w_in[:, off_gate:]], axis=1).astype(BF16)
    widths = (SSD_D_INNER, SSD_CONV_DIM, MLA_Q_RANK, MLA_KV_RANK, LANES, 2 * d)

    tm = min(512, s)
    z, xbc, ql, kvl, krdt, gate = _in_proj(x2, g_mix, w_cat, widths, tm)

    lane_pad = lambda vec: jnp.pad(vec, (DT_LO, LANES - DT_LO - SSD_HEADS)).reshape(1, LANES)
    lt = min(256, s)
    y_ssd = _ssd(xbc.reshape(b, s, SSD_CONV_DIM), z.reshape(b, s, SSD_D_INNER), krdt.reshape(b, s, LANES),
                 conv_w, conv_b.reshape(1, -1), lane_pad(dt_bias), lane_pad(a_log),
                 jnp.repeat(d_skip, SSD_HEAD_DIM).reshape(1, -1), g_ssd_out.reshape(1, -1), lt)

    cos_t, sin_t = _rope_tables(s)
    wq_pad = _pad_heads(w_q_up).astype(BF16)
    spread_gain = lambda g: jnp.concatenate([g[:MLA_NOPE], _spread_rope(g[MLA_NOPE:])]).reshape(1, -1)
    gq_pad = spread_gain(g_qk_q)
    gk_pad = spread_gain(g_qk_k)
    tq = min(512, s)
    w_kv_h = w_kv_up.reshape(MLA_KV_RANK, MLA_HEADS, MLA_NOPE + MLA_V)
    wk = w_kv_h[:, :, :MLA_NOPE].reshape(MLA_KV_RANK, MLA_HEADS * MLA_NOPE).astype(BF16)
    wvt = w_kv_h[:, :, MLA_NOPE:].reshape(MLA_KV_RANK, MLA_HEADS * MLA_V).T.astype(BF16)
    q, k, v = _qkv_prep(ql, kvl, krdt, cos_t, sin_t, g_q_lat.reshape(1, -1), wq_pad,
                        g_kv_lat.reshape(1, -1), wk, wvt, gq_pad, gk_pad, b, s, tq)
    o_mla = _flash(q, k, v, min(1024, s))

    x1, h2, idx_t, gates_t, rank_t, cnt = _merge(
        x2, y_ssd.reshape(t, d), o_mla.reshape(t, d), gate,
        w_ssd_out.astype(BF16), w_mla_out.astype(BF16), w_o.astype(BF16),
        g_ffn.reshape(1, -1), w_router.T, b_router.reshape(-1, 1), min(512, t))

    nsub = t // MOE_SUB
    assert t % MOE_SUB == 0 and nsub <= LANES
    cap = t * TOP_K + nsub * N_EXPERTS * (SEG_ALIGN - 1) + N_EXPERTS * (MOE_BLOCK - 1)
    nblk = -(-cap // MOE_BLOCK)
    slot_t, seg, loff, gdst, tail = _tables(cnt, idx_t, rank_t)
    used = (tail[0:1, 2] // MOE_BLOCK).astype(I32)
    seg, loff, gdst = seg.reshape(-1), loff.reshape(-1), gdst.reshape(-1)
    xs = _dispatch(seg, loff, gdst, tail[:, 0], tail[:, 1], used, h2, slot_t, nblk)
    ys = _experts(tail[:, 3], tail[:, 1], used, xs, w_gate_up, b_gate_up, w_down, b_down)
    out = _combine(seg, loff, gdst, x1, slot_t.T, gates_t.T, ys)
    return out.reshape(b, s, d)


def kernel(x, g_mix, w_in, conv_w, conv_b, dt_bias, a_log, d_skip, g_ssd_out, w_ssd_out, g_q_lat, w_q_up, g_kv_lat, w_kv_up, g_qk_q, g_qk_k, w_mla_out, w_o, g_ffn, w_router, b_router, w_gate_up, b_gate_up, w_down, b_down):
    params = (g_mix, w_in, conv_w, conv_b, dt_bias, a_log, d_skip, g_ssd_out, w_ssd_out, g_q_lat, w_q_up,
              g_kv_lat, w_kv_up, g_qk_q, g_qk_k, w_mla_out, w_o, g_ffn, w_router, b_router,
              w_gate_up, b_gate_up, w_down, b_down)
    for l in range(g_mix.shape[0]):
        x = _layer(x, *(p[l] for p in params))
    return x
```

```python
import functools
import math

import jax
import jax.numpy as jnp
from jax import lax
from jax.experimental import pallas as pl
from jax.experimental.pallas import tpu as pltpu

F32 = jnp.float32
BF16 = jnp.bfloat16
I32 = jnp.int32

EPS = 1e-6
CHUNK = 64

SSD_HEADS = 16
SSD_HEAD_DIM = 64
SSD_GROUPS = 2
SSD_STATE = 128
SSD_CONV = 4
SSD_D_INNER = SSD_HEADS * SSD_HEAD_DIM
SSD_CONV_DIM = SSD_D_INNER + 2 * SSD_GROUPS * SSD_STATE

MLA_HEADS = 8
MLA_Q_RANK = 384
MLA_KV_RANK = 256
MLA_NOPE = 128
MLA_ROPE = 64
MLA_QK_DIM = MLA_NOPE + MLA_ROPE
MLA_V = 128
V_ROWS = MLA_V + 16
DT_LO = MLA_ROPE // 2
ROPE_BASE = 10000.0

N_EXPERTS = 32
TOP_K = 4
SWIGLU_ALPHA = 1.702
SWIGLU_LIMIT = 7.0

LANES = 128
VMEM_LIMIT = 56 * 1024 * 1024
NEG_BIG = -1e30
LOG2E = math.log2(math.e)

MOE_BLOCK = 256
MOE_SUB = 256
SEG_ALIGN = 8
LOCAL_SLOTS = -(-(TOP_K * MOE_SUB + N_EXPERTS * (SEG_ALIGN - 1)) // MOE_BLOCK) * MOE_BLOCK


def _cparams(semantics, **kw):
    return pltpu.CompilerParams(dimension_semantics=semantics,
                                vmem_limit_bytes=VMEM_LIMIT, **kw)


def _sigmoid(v):
    return 1.0 / (1.0 + jnp.exp(-v))


def _silu(v):
    return v * _sigmoid(v)


def _bf16_pieces(t, n):
    pieces = []
    for _ in range(n - 1):
        p = t.astype(BF16)
        pieces.append(p)
        t = t - p.astype(F32)
    pieces.append(t.astype(BF16))
    return pieces


def _inproj_kernel(x_ref, g_ref, w_ref, z_ref, xbc_ref, ql_ref, kvl_ref, krdt_ref, gate_ref, *, segs):
    x = x_ref[...]
    ms = jnp.mean(x * x, axis=-1, keepdims=True)
    h = (x * lax.rsqrt(ms + EPS) * g_ref[...]).astype(BF16)
    outs = (z_ref, xbc_ref, ql_ref, kvl_ref, krdt_ref, gate_ref)
    for ref, (lo, hi) in zip(outs, segs):
        p = jnp.dot(h, w_ref[:, lo:hi], preferred_element_type=F32)
        if ref is gate_ref:
            p = _sigmoid(p)
        ref[...] = p.astype(ref.dtype)


def _in_proj(x2, g_mix, w_cat, widths, tm):
    t, d = x2.shape
    offs = [0]
    for w in widths:
        offs.append(offs[-1] + w)
    segs = tuple((offs[i], offs[i + 1]) for i in range(len(widths)))
    dts = (BF16, BF16, BF16, BF16, F32, BF16)
    out_shape = tuple(jax.ShapeDtypeStruct((t, w), dt) for w, dt in zip(widths, dts))
    out_specs = tuple(pl.BlockSpec((tm, w), lambda i: (i, 0)) for w in widths)
    return pl.pallas_call(
        functools.partial(_inproj_kernel, segs=segs),
        grid=(t // tm,),
        in_specs=[pl.BlockSpec((tm, d), lambda i: (i, 0)),
                  pl.BlockSpec((1, d), lambda i: (0, 0)),
                  pl.BlockSpec(w_cat.shape, lambda i: (0, 0))],
        out_specs=out_specs,
        out_shape=out_shape,
        compiler_params=_cparams(("parallel",)),
        name="in_proj",
    )(x2, g_mix.reshape(1, d), w_cat)


def _ssd_kernel(xbc_ref, z_ref, krdt_ref, convw_ref, convb_ref, dtb_ref, alog_ref, dskip_ref, gout_ref,
                y_ref, xext_scr, state_scr, *, lt):
    i = pl.program_id(1)
    halo = 8
    dt_lo = DT_LO
    gw = SSD_D_INNER // SSD_GROUPS
    hpg = SSD_HEADS // SSD_GROUPS

    @pl.when(i == 0)
    def _():
        state_scr[...] = jnp.zeros_like(state_scr)
        xext_scr[0:halo, :] = jnp.zeros((halo, SSD_CONV_DIM), F32)

    xext_scr[halo:halo + lt, :] = xbc_ref[0].astype(F32)
    acc = jnp.broadcast_to(convb_ref[...], (lt, SSD_CONV_DIM))
    for j in range(SSD_CONV):
        acc = acc + convw_ref[j:j + 1, :] * xext_scr[pl.ds(halo - (SSD_CONV - 1) + j, lt), :]
    xext_scr[0:halo, :] = xext_scr[lt:lt + halo, :]
    xbc = _silu(acc)
    xs = xbc[:, :SSD_D_INNER]
    bm = xbc[:, SSD_D_INNER:SSD_D_INNER + SSD_GROUPS * SSD_STATE]
    cm = xbc[:, SSD_D_INNER + SSD_GROUPS * SSD_STATE:]

    lane = lax.broadcasted_iota(I32, (1, LANES), 1)
    head_lane = (lane >= dt_lo) & (lane < dt_lo + SSD_HEADS)
    v = krdt_ref[0] + dtb_ref[...]
    dt = jnp.maximum(v, 0.0) + jnp.log(1.0 + jnp.exp(-jnp.abs(v)))
    dt = jnp.where(head_lane, dt, 0.0)
    a = jnp.where(head_lane, -jnp.exp(alog_ref[...]) * LOG2E, 0.0)
    da = dt * a
    row = lax.broadcasted_iota(I32, (lt, lt), 0)
    col = lax.broadcasted_iota(I32, (lt, lt), 1)
    tri = row >= col
    tri_b = tri.astype(BF16)
    a_cum = jnp.zeros((lt, LANES), F32)
    for piece in _bf16_pieces(da, 3):
        a_cum = a_cum + jnp.dot(tri_b, piece, preferred_element_type=F32)
    a_cum_t = a_cum.T
    a_last = a_cum[lt - 1:lt, :]
    exp_a = jnp.exp2(a_cum)
    dec = jnp.exp2(a_last - a_cum)

    er = lax.broadcasted_iota(I32, (LANES, SSD_D_INNER), 0)
    ec = lax.broadcasted_iota(I32, (LANES, SSD_D_INNER), 1)
    expand = ((er - dt_lo) == (ec // SSD_HEAD_DIM)).astype(BF16)

    def widen(t, pieces):
        out = jnp.zeros((lt, SSD_D_INNER), F32)
        for piece in _bf16_pieces(t, pieces):
            out = out + jnp.dot(piece, expand, preferred_element_type=F32)
        return out

    dt_w = widen(dt, 2)
    exp_a_w = widen(exp_a, 1)
    dec_w = widen(dec, 1)
    xdt = xs * dt_w
    xdt_b = xdt.astype(BF16)
    xdec_b = (xdt * dec_w).astype(BF16)
    lane_pair = lax.broadcasted_iota(I32, (lt, LANES), 1)

    y_parts = []
    for g in range(SSD_GROUPS):
        bg = bm[:, g * SSD_STATE:(g + 1) * SSD_STATE].astype(BF16)
        cg = cm[:, g * SSD_STATE:(g + 1) * SSD_STATE].astype(BF16)
        cb = lax.dot_general(cg, bg, (((1,), (1,)), ((), ())), preferred_element_type=F32)
        for pair in range(hpg // 2):
            h0 = g * hpg + 2 * pair
            rhs = xdt_b[:, h0 * SSD_HEAD_DIM:(h0 + 2) * SSD_HEAD_DIM]
            res = []
            for h in (h0, h0 + 1):
                seg = a_cum[:, dt_lo + h:dt_lo + h + 1] - a_cum_t[dt_lo + h:dt_lo + h + 1, :]
                lmat = jnp.exp2(jnp.where(tri, seg, NEG_BIG))
                res.append(jnp.dot((cb * lmat).astype(BF16), rhs, preferred_element_type=F32))
            y_parts.append(jnp.where(lane_pair < SSD_HEAD_DIM, res[0], res[1]))
        st = state_scr[g]
        y_off = jnp.dot(cg, st.astype(BF16), preferred_element_type=F32)
        y_parts.append(y_off * exp_a_w[:, g * gw:(g + 1) * gw])
        new = lax.dot_general(bg, xdec_b[:, g * gw:(g + 1) * gw], (((0,), (0,)), ((), ())),
                              preferred_element_type=F32)
        state_scr[g] = st * exp_a_w[lt - 1:lt, g * gw:(g + 1) * gw] + new

    zs = _silu(z_ref[0].astype(F32))
    outs = []
    npg = hpg // 2 + 1
    for g in range(SSD_GROUPS):
        parts = y_parts[g * npg:(g + 1) * npg]
        yd = jnp.concatenate(parts[:-1], axis=-1)
        yg = yd + parts[-1] + dskip_ref[:, g * gw:(g + 1) * gw] * xs[:, g * gw:(g + 1) * gw]
        yg = yg * zs[:, g * gw:(g + 1) * gw]
        ms = jnp.mean(yg * yg, axis=-1, keepdims=True)
        outs.append(yg * lax.rsqrt(ms + EPS) * gout_ref[:, g * gw:(g + 1) * gw])
    y_ref[0] = jnp.concatenate(outs, axis=-1).astype(y_ref.dtype)


def _ssd(xbc, z, krdt, conv_w, conv_b, dtb, alog, dskip_w, g_out, lt):
    b, s, _ = xbc.shape
    gw = SSD_D_INNER // SSD_GROUPS
    full = lambda shape: pl.BlockSpec(shape, lambda bi, i: (0,) * len(shape))
    return pl.pallas_call(
        functools.partial(_ssd_kernel, lt=lt),
        grid=(b, s // lt),
        in_specs=[pl.BlockSpec((1, lt, SSD_CONV_DIM), lambda bi, i: (bi, i, 0)),
                  pl.BlockSpec((1, lt, SSD_D_INNER), lambda bi, i: (bi, i, 0)),
                  pl.BlockSpec((1, lt, LANES), lambda bi, i: (bi, i, 0)),
                  full((SSD_CONV, SSD_CONV_DIM)), full((1, SSD_CONV_DIM)),
                  full((1, LANES)), full((1, LANES)),
                  full((1, SSD_D_INNER)), full((1, SSD_D_INNER))],
        out_specs=pl.BlockSpec((1, lt, SSD_D_INNER), lambda bi, i: (bi, i, 0)),
        out_shape=jax.ShapeDtypeStruct((b, s, SSD_D_INNER), BF16),
        scratch_shapes=[pltpu.VMEM((lt + 8, SSD_CONV_DIM), F32),
                        pltpu.VMEM((SSD_GROUPS, SSD_STATE, gw), F32)],
        compiler_params=_cparams(("parallel", "arbitrary")),
        name="ssd_scan",
    )(xbc, z, krdt, conv_w, conv_b, dtb, alog, dskip_w, g_out)


def _swap_halves(t):
    return pltpu.roll(t, LANES // 2, 1)


def _lane_sums(sq, width):
    ones = jnp.ones((sq.shape[1], width), BF16)
    return jnp.dot(sq.astype(BF16), ones, preferred_element_type=F32)


def _qkv_kernel(ql_ref, kvl_ref, krdt_ref, cos_ref, sin_ref, gql_ref, wq_ref, gkvl_ref, wk_ref, wvt_ref,
                gq_ref, gk_ref, q_ref, k_ref, v_ref, *, scale):
    hw = 2 * LANES
    cosv = cos_ref[...]
    sinv = sin_ref[...]

    ql = ql_ref[...].astype(F32)
    rq = lax.rsqrt(_lane_sums(ql * ql, LANES) * (1.0 / MLA_Q_RANK) + EPS)
    qn = (ql * jnp.concatenate([rq] * (MLA_Q_RANK // LANES), axis=1) * gql_ref[...]).astype(BF16)

    kvl = kvl_ref[...].astype(F32)
    rkv = lax.rsqrt(_lane_sums(kvl * kvl, LANES) * (1.0 / MLA_KV_RANK) + EPS)
    kvn = (kvl * jnp.concatenate([rkv] * (MLA_KV_RANK // LANES), axis=1) * gkvl_ref[...]).astype(BF16)
    tm = kvn.shape[0]
    ones_rows = (lax.broadcasted_iota(I32, (V_ROWS - MLA_V, tm), 0) == 0).astype(v_ref.dtype)

    lane = lax.broadcasted_iota(I32, (1, LANES), 1)
    kr = jnp.where((lane % (LANES // 2)) < MLA_ROPE // 2, krdt_ref[...], 0.0)
    ss_r = _lane_sums(kr * kr, LANES)
    krg = kr * gk_ref[:, LANES:]
    kr_rot = krg * cosv + _swap_halves(krg) * sinv

    qf = jnp.dot(qn, wq_ref[...], preferred_element_type=F32)
    kf = jnp.dot(kvn, wk_ref[...], preferred_element_type=F32)
    vt = lax.dot_general(wvt_ref[...], kvn, (((1,), (1,)), ((), ())), preferred_element_type=F32)
    for h in range(MLA_HEADS):
        qh = qf[:, h * hw:(h + 1) * hw]
        r = lax.rsqrt(_lane_sums(qh * qh, hw) * (1.0 / MLA_QK_DIM) + EPS)
        qs = qh * (r * scale) * gq_ref[...]
        q_ref[0, h, :, 0:LANES] = qs[:, :LANES].astype(q_ref.dtype)
        qr = qs[:, LANES:]
        q_ref[0, h, :, LANES:hw] = (qr * cosv + _swap_halves(qr) * sinv).astype(q_ref.dtype)

        kn = kf[:, h * LANES:(h + 1) * LANES]
        rk = lax.rsqrt((_lane_sums(kn * kn, LANES) + ss_r) * (1.0 / MLA_QK_DIM) + EPS)
        k_ref[0, h, :, 0:LANES] = (kn * rk * gk_ref[:, :LANES]).astype(k_ref.dtype)
        k_ref[0, h, :, LANES:hw] = (kr_rot * rk).astype(k_ref.dtype)
        v_ref[0, h, 0:MLA_V, :] = vt[h * MLA_V:(h + 1) * MLA_V, :].astype(v_ref.dtype)
        v_ref[0, h, MLA_V:V_ROWS, :] = ones_rows


def _qkv_prep(ql, kvl, krdt, cos_t, sin_t, g_q_lat, wq_pad, g_kv_lat, wk, wvt, gq_pad, gk_pad, b, s, tm):
    t = b * s
    nst = s // tm
    full = lambda shape: pl.BlockSpec(shape, lambda bi, i: (0,) * len(shape))
    tok = lambda w: pl.BlockSpec((tm, w), lambda bi, i: (bi * nst + i, 0))
    hs = lambda w: pl.BlockSpec((1, MLA_HEADS, tm, w), lambda bi, i: (bi, 0, i, 0))
    return pl.pallas_call(
        functools.partial(_qkv_kernel, scale=MLA_QK_DIM ** -0.5 * math.log2(math.e)),
        grid=(b, nst),
        in_specs=[tok(MLA_Q_RANK), tok(MLA_KV_RANK), tok(LANES),
                  pl.BlockSpec((tm, LANES), lambda bi, i: (i, 0)),
                  pl.BlockSpec((tm, LANES), lambda bi, i: (i, 0)),
                  full((1, MLA_Q_RANK)), full(wq_pad.shape), full((1, MLA_KV_RANK)), full(wk.shape),
                  full(wvt.shape), full((1, 2 * LANES)), full((1, 2 * LANES))],
        out_specs=(hs(2 * LANES), hs(2 * LANES),
                   pl.BlockSpec((1, MLA_HEADS, V_ROWS, tm), lambda bi, i: (bi, 0, 0, i))),
        out_shape=(jax.ShapeDtypeStruct((b, MLA_HEADS, s, 2 * LANES), BF16),
                   jax.ShapeDtypeStruct((b, MLA_HEADS, s, 2 * LANES), BF16),
                   jax.ShapeDtypeStruct((b, MLA_HEADS, V_ROWS, s), BF16)),
        compiler_params=_cparams(("parallel", "parallel")),
        name="qkv_prep",
    )(ql, kvl, krdt, cos_t, sin_t, g_q_lat, wq_pad, g_kv_lat, wk, wvt, gq_pad, gk_pad)


def _flash_kernel(q_ref, k_ref, vt_ref, o_ref, m_scr, acc_scr, s_scr, *, tq):
    i = pl.program_id(2)
    tk = tq // 2
    q = q_ref[0, 0]
    m_scr[...] = jnp.full(m_scr.shape, NEG_BIG, F32)
    acc_scr[...] = jnp.zeros(acc_scr.shape, F32)

    def scores(j, slot, lo=0):
        start = pl.multiple_of(j * tk, tk)
        ks = k_ref[0, 0, pl.ds(start, tk), :]
        s_scr[slot, :, lo:] = lax.dot_general(ks, q[lo:, :], (((1,), (1,)), ((), ())),
                                              preferred_element_type=F32)

    def softmax_pv(j, slot, masked, lo=0):
        start = pl.multiple_of(j * tk, tk)
        vt = vt_ref[0, 0, :, pl.ds(start, tk)]
        st = s_scr[slot, :, lo:]
        if masked:
            kc = lax.broadcasted_iota(I32, (tk, tk), 0) // CHUNK
            qc = lax.broadcasted_iota(I32, (tk, tk), 1) // CHUNK
            diag = jnp.where(kc <= qc, st[:, :tk], NEG_BIG)
            st = diag if st.shape[1] == tk else jnp.concatenate([diag, st[:, tk:]], axis=1)
        m_prev = m_scr[:, lo:]
        m_new = jnp.maximum(m_prev, jnp.max(st, axis=0, keepdims=True))
        alpha = jnp.exp2(m_prev - m_new)
        pt = jnp.exp2(st - m_new)
        acc_scr[:, lo:] = alpha * acc_scr[:, lo:] + jnp.dot(vt, pt.astype(BF16), preferred_element_type=F32)
        m_scr[:, lo:] = m_new

    scores(0, 0)

    def body(jj, carry):
        j = 2 * jj
        scores(j + 1, 1)
        softmax_pv(j, 0, False)
        scores(j + 2, 0)
        softmax_pv(j + 1, 1, False)
        return carry

    lax.fori_loop(0, i, body, 0)
    scores(2 * i + 1, 1, lo=tk)
    softmax_pv(2 * i, 0, True)
    softmax_pv(2 * i + 1, 1, True, lo=tk)

    o_ref[0] = (acc_scr[0:MLA_V, :] / acc_scr[MLA_V:MLA_V + 1, :]).T.astype(o_ref.dtype)


def _flash(q, k, v, tq):
    b, nh, s, _ = q.shape
    return pl.pallas_call(
        functools.partial(_flash_kernel, tq=tq),
        grid=(b, nh, s // tq),
        in_specs=[pl.BlockSpec((1, 1, tq, q.shape[-1]), lambda bi, h, i: (bi, h, i, 0)),
                  pl.BlockSpec((1, 1, s, k.shape[-1]), lambda bi, h, i: (bi, h, 0, 0)),
                  pl.BlockSpec((1, 1, V_ROWS, s), lambda bi, h, i: (bi, h, 0, 0))],
        out_specs=pl.BlockSpec((1, tq, MLA_V), lambda bi, h, i: (bi, i, h)),
        out_shape=jax.ShapeDtypeStruct((b, s, nh * MLA_V), BF16),
        scratch_shapes=[pltpu.VMEM((1, tq), F32), pltpu.VMEM((V_ROWS, tq), F32),
                        pltpu.VMEM((2, tq // 2, tq), F32)],
        compiler_params=_cparams(("parallel", "parallel", "arbitrary")),
        name="flash_attn",
    )(q, k, v)


def _merge_kernel(x_ref, ys_ref, om_ref, gate_ref, wss_ref, wml_ref, wo_ref, gffn_ref, wrt_ref, brt_ref,
                  x1_ref, h2_ref, idx_ref, gates_ref, rank_ref, cnt_ref, *, tm):
    d = x_ref.shape[-1]
    step = pl.program_id(0)

    @pl.when(step == 0)
    def _():
        cnt_ref[...] = jnp.zeros_like(cnt_ref)

    y1 = jnp.dot(ys_ref[...], wss_ref[...], preferred_element_type=F32)
    y2 = jnp.dot(om_ref[...], wml_ref[...], preferred_element_type=F32)
    g = gate_ref[...].astype(F32)
    merged = (g[:, :d] * y1 + g[:, d:] * y2).astype(BF16)
    x1 = x_ref[...] + jnp.dot(merged, wo_ref[...], preferred_element_type=F32)
    x1_ref[...] = x1
    ms = jnp.mean(x1 * x1, axis=-1, keepdims=True)
    h2 = x1 * lax.rsqrt(ms + EPS) * gffn_ref[...]
    h2_ref[...] = h2.astype(h2_ref.dtype)

    w_hi, w_lo = _bf16_pieces(wrt_ref[...], 2)
    h_hi, h_lo = _bf16_pieces(h2, 2)
    nt = (((1,), (1,)), ((), ()))
    logits = (lax.dot_general(w_hi, h_hi, nt, preferred_element_type=F32)
              + lax.dot_general(w_hi, h_lo, nt, preferred_element_type=F32)
              + lax.dot_general(w_lo, h_hi, nt, preferred_element_type=F32)) + brt_ref[...]
    eid = lax.broadcasted_iota(I32, (N_EXPERTS, tm), 0)
    cur = logits
    onehot = jnp.zeros((N_EXPERTS, tm), F32)
    vals, sels = [], []
    for k in range(TOP_K):
        mx = jnp.max(cur, axis=0, keepdims=True)
        idx = jnp.min(jnp.where(cur == mx, eid, N_EXPERTS), axis=0, keepdims=True)
        sel = eid == idx
        vals.append(mx)
        sels.append(sel)
        idx_ref[k:k + 1, :] = idx
        cur = jnp.where(sel, -jnp.inf, cur)
        onehot = onehot + sel.astype(F32)
    es = [jnp.exp(vk - vals[0]) for vk in vals]
    den = es[0] + es[1] + es[2] + es[3]
    for k in range(TOP_K):
        gates_ref[k:k + 1, :] = es[k] / den

    r = lax.broadcasted_iota(I32, (tm, tm), 0)
    c = lax.broadcasted_iota(I32, (tm, tm), 1)
    before = ((r < c) & ((r // MOE_SUB) == (c // MOE_SUB))).astype(BF16)
    prefix = jnp.dot(onehot.astype(BF16), before, preferred_element_type=F32)
    for k in range(TOP_K):
        rank_ref[k:k + 1, :] = jnp.sum(jnp.where(sels[k], prefix, 0.0), axis=0, keepdims=True).astype(I32)
    lane = lax.broadcasted_iota(I32, (N_EXPERTS, LANES), 1)
    cnt = cnt_ref[...]
    for g in range(tm // MOE_SUB):
        c_g = jnp.sum(onehot[:, g * MOE_SUB:(g + 1) * MOE_SUB], axis=1, keepdims=True)
        cnt = jnp.where(lane == step * (tm // MOE_SUB) + g, c_g, cnt)
    cnt_ref[...] = cnt


def _merge(x2, y_ssd, o_mla, gate, w_ss, w_ml, w_o, g_ffn, w_rt, b_rt, tm):
    t, d = x2.shape
    full = lambda shape: pl.BlockSpec(shape, lambda i: (0,) * len(shape))
    tok = lambda w: pl.BlockSpec((tm, w), lambda i: (i, 0))
    sel = pl.BlockSpec((TOP_K, tm), lambda i: (0, i))
    return pl.pallas_call(
        functools.partial(_merge_kernel, tm=tm),
        grid=(t // tm,),
        in_specs=[tok(d), tok(d), tok(d), tok(2 * d), full((d, d)), full((d, d)), full((d, d)),
                  full((1, d)), full((N_EXPERTS, d)), full((N_EXPERTS, 1))],
        out_specs=(tok(d), tok(d), sel, sel, sel, full((N_EXPERTS, LANES))),
        out_shape=(jax.ShapeDtypeStruct((t, d), F32), jax.ShapeDtypeStruct((t, d), BF16),
                   jax.ShapeDtypeStruct((TOP_K, t), I32), jax.ShapeDtypeStruct((TOP_K, t), F32),
                   jax.ShapeDtypeStruct((TOP_K, t), I32), jax.ShapeDtypeStruct((N_EXPERTS, LANES), F32)),
        compiler_params=_cparams(("arbitrary",)),
        name="merge_route",
    )(x2, y_ssd, o_mla, gate, w_ss, w_ml, w_o, g_ffn, w_rt, b_rt)


def _excl_cumsum_rows(v):
    eid = lax.broadcasted_iota(I32, v.shape, 0)
    out = jnp.zeros(v.shape, F32)
    for e in range(N_EXPERTS - 1):
        out = out + jnp.where(eid > e, v[e:e + 1, :], 0.0)
    return out


def _tables_kernel(cnt_ref, idx_ref, rank_ref, slot_ref, seg_ref, loff_ref, gdst_ref, tail_ref):
    t = idx_ref.shape[1]
    cnt = cnt_ref[...]
    seg = jnp.ceil(cnt * (1.0 / SEG_ALIGN)) * SEG_ALIGN
    loff = _excl_cumsum_rows(seg)
    r = lax.broadcasted_iota(I32, (LANES, LANES), 0)
    c = lax.broadcasted_iota(I32, (LANES, LANES), 1)
    run = jnp.dot((seg * (1.0 / SEG_ALIGN)).astype(BF16), (r < c).astype(BF16),
                  preferred_element_type=F32) * SEG_ALIGN
    tot = jnp.sum(seg, axis=1, keepdims=True)
    padded = jnp.broadcast_to(jnp.ceil(tot * (1.0 / MOE_BLOCK)) * MOE_BLOCK, (N_EXPERTS, LANES))
    start = _excl_cumsum_rows(padded)
    end = start + padded
    seg_ref[...] = seg.astype(I32)
    loff_ref[...] = loff.astype(I32)
    gdst_ref[...] = (start + run).astype(I32)

    gr = lax.broadcasted_iota(I32, (LANES, t), 0)
    gc = lax.broadcasted_iota(I32, (LANES, t), 1) // MOE_SUB
    loff_tok = jnp.dot((loff * (1.0 / SEG_ALIGN)).astype(BF16), (gr == gc).astype(BF16),
                       preferred_element_type=F32) * SEG_ALIGN
    idx = idx_ref[...]
    slot = rank_ref[...]
    for e in range(N_EXPERTS):
        slot = slot + jnp.where(idx == e, loff_tok[e:e + 1, :].astype(I32), 0)
    slot_ref[...] = slot

    lane = lax.broadcasted_iota(I32, (N_EXPERTS, LANES), 1)
    total = jnp.max(end, axis=0, keepdims=True)
    tail = jnp.where(lane == 0, start + tot, jnp.where(lane == 1, end, jnp.where(lane == 3, start, total)))
    tail_ref[...] = tail.astype(I32)


def _tables(cnt, idx_t, rank_t):
    t = idx_t.shape[1]
    tab = jax.ShapeDtypeStruct((N_EXPERTS, LANES), I32)
    return pl.pallas_call(
        _tables_kernel,
        out_shape=(jax.ShapeDtypeStruct((TOP_K, t), I32), tab, tab, tab, tab),
        compiler_params=pltpu.CompilerParams(vmem_limit_bytes=VMEM_LIMIT),
        name="route_tables",
    )(cnt, idx_t, rank_t)


def _pow2_sizes(limit):
    size = SEG_ALIGN
    while size * 2 <= limit:
        size *= 2
    sizes = []
    while size >= SEG_ALIGN:
        sizes.append(size)
        size //= 2
    return tuple(sizes)


def _piece_copies(src_ref, src_off, dst_ref, dst_off, n, limit, sem, wait=False):
    off = 0
    for size in _pow2_sizes(limit):
        take = n & size

        @pl.when(take != 0)
        def _(off=off, size=size):
            cp = pltpu.make_async_copy(
                src_ref.at[pl.ds(pl.multiple_of(src_off + off, SEG_ALIGN), size), :],
                dst_ref.at[pl.ds(pl.multiple_of(dst_off + off, SEG_ALIGN), size), :], sem)
            if wait:
                cp.wait()
            else:
                cp.start()

        off = off + take


def _dispatch_kernel(seg_ref, loff_ref, gdst_ref, taillo_ref, tailhi_ref, used_ref,
                     h_ref, slot_ref, xs_ref, loc_scr, zero_scr, sem, zsem, *, nblk):
    i = pl.program_id(0)
    par = i % 2

    @pl.when(i == 0)
    def _():
        nblk_all = xs_ref.shape[0] // MOE_BLOCK
        zero_scr[...] = jnp.zeros_like(zero_scr)

        def blk_copy(b):
            start = pl.multiple_of(b * MOE_BLOCK, MOE_BLOCK)
            return pltpu.make_async_copy(zero_scr, xs_ref.at[pl.ds(start, MOE_BLOCK), :], zsem)

        def issue_blk(b, carry):
            blk_copy(b).start()
            return carry

        def drain_blk(b, carry):
            blk_copy(b).wait()
            return carry

        lax.fori_loop(used_ref[0], nblk_all, issue_blk, 0)
        lax.fori_loop(used_ref[0], nblk_all, drain_blk, 0)

        def tails(wait):
            def body(e, carry):
                lo = taillo_ref[e]
                _piece_copies(zero_scr, 0, xs_ref, lo, tailhi_ref[e] - lo, MOE_BLOCK - 1, zsem, wait=wait)
                return carry
            return body

        lax.fori_loop(0, N_EXPERTS, tails(False), 0)
        lax.fori_loop(0, N_EXPERTS, tails(True), 0)

    slots = slot_ref[...]
    rid = lax.broadcasted_iota(I32, (LOCAL_SLOTS, MOE_SUB), 0)
    hit = rid == slots[0:1, :]
    for k in range(1, TOP_K):
        hit = hit | (rid == slots[k:k + 1, :])
    loc_scr[par] = jnp.dot(hit.astype(BF16), h_ref[...], preferred_element_type=F32)

    loc = loc_scr.at[par]

    def seg_body(e, total):
        n = seg_ref[e * LANES + i]
        _piece_copies(loc, loff_ref[e * LANES + i], xs_ref, gdst_ref[e * LANES + i], n, MOE_SUB, sem.at[par])
        return total + n

    total = lax.fori_loop(0, N_EXPERTS, seg_body, 0)
    spare = (nblk + par * (LOCAL_SLOTS // MOE_BLOCK)) * MOE_BLOCK
    _piece_copies(loc, total, xs_ref, spare + total, LOCAL_SLOTS - total, LOCAL_SLOTS, sem.at[par])

    def wait_step(p):
        pltpu.make_async_copy(loc_scr.at[p], xs_ref.at[pl.ds(0, LOCAL_SLOTS), :], sem.at[p]).wait()

    @pl.when(i > 0)
    def _():
        wait_step(1 - par)

    @pl.when(i == pl.num_programs(0) - 1)
    def _():
        wait_step(par)


def _dispatch(seg, loff, gdst, tail_lo, tail_hi, used, h2, slot_t, nblk):
    t, d = h2.shape
    rows = (nblk + 2 * (LOCAL_SLOTS // MOE_BLOCK)) * MOE_BLOCK
    return pl.pallas_call(
        functools.partial(_dispatch_kernel, nblk=nblk),
        grid_spec=pltpu.PrefetchScalarGridSpec(
            num_scalar_prefetch=6,
            grid=(t // MOE_SUB,),
            in_specs=[pl.BlockSpec((MOE_SUB, d), lambda i, *_: (i, 0)),
                      pl.BlockSpec((TOP_K, MOE_SUB), lambda i, *_: (0, i))],
            out_specs=pl.BlockSpec(memory_space=pl.ANY),
            scratch_shapes=[pltpu.VMEM((2, LOCAL_SLOTS, d), F32), pltpu.VMEM((MOE_BLOCK, d), F32),
                            pltpu.SemaphoreType.DMA((2,)), pltpu.SemaphoreType.DMA(())]),
        out_shape=jax.ShapeDtypeStruct((rows, d), F32),
        compiler_params=_cparams(("arbitrary",)),
        name="moe_dispatch",
    )(seg, loff, gdst, tail_lo, tail_hi, used, h2, slot_t)


def _expert_kernel(lo_ref, hi_ref, used_ref, x_ref, wgu_ref, bgu_ref, wd_ref, bd_ref, y_ref,
                   wgu_scr, wd_scr, xbuf, ybuf, xsem, ysem):
    e = pl.program_id(0)
    ff = wd_ref.shape[1]
    first = lo_ref[e] // MOE_BLOCK
    n = (hi_ref[e] - lo_ref[e]) // MOE_BLOCK

    big = 2 * MOE_BLOCK
    npair = n // 2
    items = npair + n % 2

    def x_copy(k, rows):
        start = pl.multiple_of((first + 2 * k) * MOE_BLOCK, MOE_BLOCK)
        return pltpu.make_async_copy(x_ref.at[pl.ds(start, rows), :], xbuf.at[k % 2, pl.ds(0, rows), :],
                                     xsem.at[k % 2])

    def y_copy(k, rows):
        start = pl.multiple_of((first + 2 * k) * MOE_BLOCK, MOE_BLOCK)
        return pltpu.make_async_copy(ybuf.at[k % 2, pl.ds(0, rows), :], y_ref.at[pl.ds(start, rows), :],
                                     ysem.at[k % 2])

    def start_x(k):
        @pl.when(k < npair)
        def _():
            x_copy(k, big).start()

        @pl.when((k == npair) & (k < items))
        def _():
            x_copy(k, MOE_BLOCK).start()

    def compute(k, rows):
        slot = k % 2
        xb = xbuf[slot, 0:rows, :].astype(BF16)
        gu = jnp.dot(xb, wgu_scr[...], preferred_element_type=F32) + bgu_ref[0]
        glu = jnp.minimum(gu[:, :ff], SWIGLU_LIMIT)
        lin = jnp.clip(gu[:, ff:], -SWIGLU_LIMIT, SWIGLU_LIMIT)
        act = glu * _sigmoid(SWIGLU_ALPHA * glu) * (lin + 1.0)
        ybuf[slot, 0:rows, :] = jnp.dot(act.astype(BF16), wd_scr[...], preferred_element_type=F32) + bd_ref[0]

    def item(k, rows):
        x_copy(k, rows).wait()
        start_x(k + 1)

        @pl.when(k >= 2)
        def _():
            y_copy(k - 2, big).wait()

        compute(k, rows)
        y_copy(k, rows).start()

    @pl.when(n > 0)
    def _():
        start_x(0)
        wgu_scr[...] = wgu_ref[0].astype(BF16)
        wd_scr[...] = wd_ref[0].astype(BF16)

    def body(k, carry):
        item(k, big)
        return carry

    lax.fori_loop(0, npair, body, 0)

    @pl.when(items > npair)
    def _():
        item(npair, MOE_BLOCK)

    @pl.when(items >= 2)
    def _():
        y_copy(items - 2, big).wait()

    @pl.when(items > npair)
    def _():
        y_copy(items - 1, MOE_BLOCK).wait()

    @pl.when((items == npair) & (items >= 1))
    def _():
        y_copy(items - 1, big).wait()

    @pl.when(e == pl.num_programs(0) - 1)
    def _():
        nblk_all = y_ref.shape[0] // MOE_BLOCK
        ybuf[0] = jnp.zeros(ybuf.shape[1:], ybuf.dtype)

        def zero_copy(b):
            start = pl.multiple_of(b * MOE_BLOCK, MOE_BLOCK)
            return pltpu.make_async_copy(ybuf.at[0, pl.ds(0, MOE_BLOCK), :],
                                         y_ref.at[pl.ds(start, MOE_BLOCK), :], ysem.at[0])

        def issue(b, carry):
            zero_copy(b).start()
            return carry

        def drain(b, carry):
            zero_copy(b).wait()
            return carry

        lax.fori_loop(used_ref[0], nblk_all, issue, 0)
        lax.fori_loop(used_ref[0], nblk_all, drain, 0)


def _experts(row_lo, row_hi, used, xs, w_gate_up, b_gate_up, w_down, b_down):
    rows, d = xs.shape
    ne, _, ff2 = w_gate_up.shape
    ff = ff2 // 2
    return pl.pallas_call(
        _expert_kernel,
        grid_spec=pltpu.PrefetchScalarGridSpec(
            num_scalar_prefetch=3,
            grid=(ne,),
            in_specs=[pl.BlockSpec(memory_space=pl.ANY),
                      pl.BlockSpec((1, d, ff2), lambda e, *_: (e, 0, 0)),
                      pl.BlockSpec((1, 1, ff2), lambda e, *_: (e, 0, 0)),
                      pl.BlockSpec((1, ff, d), lambda e, *_: (e, 0, 0)),
                      pl.BlockSpec((1, 1, d), lambda e, *_: (e, 0, 0))],
            out_specs=pl.BlockSpec(memory_space=pl.ANY),
            scratch_shapes=[pltpu.VMEM((d, ff2), BF16), pltpu.VMEM((ff, d), BF16),
                            pltpu.VMEM((2, 2 * MOE_BLOCK, d), F32), pltpu.VMEM((2, 2 * MOE_BLOCK, d), F32),
                            pltpu.SemaphoreType.DMA((2,)), pltpu.SemaphoreType.DMA((2,))]),
        out_shape=jax.ShapeDtypeStruct((rows, d), F32),
        compiler_params=_cparams(("arbitrary",)),
        name="moe_experts",
    )(row_lo, row_hi, used, xs, w_gate_up, b_gate_up.reshape(ne, 1, ff2), w_down, b_down.reshape(ne, 1, d))


def _combine_kernel(seg_ref, loff_ref, gdst_ref, x1_ref, slot_ref, gates_ref, ys_ref, o_ref, loc_scr, sem):
    i = pl.program_id(0)
    par = i % 2

    def fetch(g, p):
        loc = loc_scr.at[p]

        def seg_body(e, total):
            n = seg_ref[e * LANES + g]
            _piece_copies(ys_ref, gdst_ref[e * LANES + g], loc, loff_ref[e * LANES + g], n, MOE_SUB, sem.at[p])
            return total + n

        total = lax.fori_loop(0, N_EXPERTS, seg_body, 0)
        _piece_copies(ys_ref, total, loc, total, LOCAL_SLOTS - total, LOCAL_SLOTS, sem.at[p])

    @pl.when(i == 0)
    def _():
        fetch(0, 0)

    @pl.when(i + 1 < pl.num_programs(0))
    def _():
        fetch(i + 1, 1 - par)

    pltpu.make_async_copy(ys_ref.at[pl.ds(0, LOCAL_SLOTS), :], loc_scr.at[par], sem.at[par]).wait()

    slots = slot_ref[...]
    g = gates_ref[...]
    rid = lax.broadcasted_iota(I32, (MOE_SUB, LOCAL_SLOTS), 1)
    gmat = jnp.where(rid == slots[:, 0:1], g[:, 0:1], 0.0)
    for k in range(1, TOP_K):
        gmat = gmat + jnp.where(rid == slots[:, k:k + 1], g[:, k:k + 1], 0.0)
    o_ref[...] = x1_ref[...] + jnp.dot(gmat.astype(BF16), loc_scr[par].astype(BF16),
                                       preferred_element_type=F32)


def _combine(seg, loff, gdst, x1, slot_tk, gates_tk, ys):
    t, d = x1.shape
    tok = lambda w: pl.BlockSpec((MOE_SUB, w), lambda i, *_: (i, 0))
    return pl.pallas_call(
        _combine_kernel,
        grid_spec=pltpu.PrefetchScalarGridSpec(
            num_scalar_prefetch=3,
            grid=(t // MOE_SUB,),
            in_specs=[tok(d), tok(TOP_K), tok(TOP_K), pl.BlockSpec(memory_space=pl.ANY)],
            out_specs=tok(d),
            scratch_shapes=[pltpu.VMEM((2, LOCAL_SLOTS, d), F32), pltpu.SemaphoreType.DMA((2,))]),
        out_shape=jax.ShapeDtypeStruct((t, d), F32),
        compiler_params=_cparams(("arbitrary",)),
        name="moe_combine",
    )(seg, loff, gdst, x1, slot_tk, gates_tk, ys)


def _rope_tables(s):
    pos = jnp.arange(s, dtype=F32)
    inv = ROPE_BASE ** (-jnp.arange(0, MLA_ROPE, 2, dtype=F32) / MLA_ROPE)
    ang = pos[:, None] * inv[None, :]
    cos, sin = jnp.cos(ang), jnp.sin(ang)
    return _spread_rope(jnp.concatenate([cos, cos], axis=-1)), _spread_rope(jnp.concatenate([-sin, sin], axis=-1))


def _spread_rope(w, gap=None):
    half = MLA_ROPE // 2
    zeros = jnp.zeros(w.shape[:-1] + (half,), w.dtype)
    return jnp.concatenate([w[..., :half], zeros if gap is None else gap, w[..., half:], zeros], axis=-1)


def _pad_heads(w):
    lead = w.shape[:-1]
    w = w.reshape(lead + (MLA_HEADS, MLA_QK_DIM))
    w = jnp.concatenate([w[..., :MLA_NOPE], _spread_rope(w[..., MLA_NOPE:])], axis=-1)
    return w.reshape(lead + (MLA_HEADS * 2 * LANES,))


def _layer(x, g_mix, w_in, conv_w, conv_b, dt_bias, a_log, d_skip, g_ssd_out, w_ssd_out,
           g_q_lat, w_q_up, g_kv_lat, w_kv_up, g_qk_q, g_qk_k, w_mla_out, w_o,
           g_ffn, w_router, b_router, w_gate_up, b_gate_up, w_down, b_down):
    b, s, d = x.shape
    t = b * s
    x2 = x.reshape(t, d)

    off_xbc = SSD_D_INNER
    off_dt = off_xbc + SSD_CONV_DIM
    off_ql = off_dt + SSD_HEADS
    off_kvl = off_ql + MLA_Q_RANK
    off_kr = off_kvl + MLA_KV_RANK
    off_gate = off_kr + MLA_ROPE
    dt_gap = jnp.pad(w_in[:, off_dt:off_ql], ((0, 0), (0, MLA_ROPE // 2 - SSD_HEADS)))
    w_cat = jnp.concatenate([
        w_in[:, :off_dt], w_in[:, off_ql:off_kr],
        _spread_rope(w_in[:, off_kr:off_gate], gap=dt_gap),
        w_in[:, off_gate:]], axis=1).astype(BF16)
    widths = (SSD_D_INNER, SSD_CONV_DIM, MLA_Q_RANK, MLA_KV_RANK, LANES, 2 * d)

    tm = min(512, s)
    z, xbc, ql, kvl, krdt, gate = _in_proj(x2, g_mix, w_cat, widths, tm)

    lane_pad = lambda vec: jnp.pad(vec, (DT_LO, LANES - DT_LO - SSD_HEADS)).reshape(1, LANES)
    lt = min(256, s)
    y_ssd = _ssd(xbc.reshape(b, s, SSD_CONV_DIM), z.reshape(b, s, SSD_D_INNER), krdt.reshape(b, s, LANES),
                 conv_w, conv_b.reshape(1, -1), lane_pad(dt_bias), lane_pad(a_log),
                 jnp.repeat(d_skip, SSD_HEAD_DIM).reshape(1, -1), g_ssd_out.reshape(1, -1), lt)

    cos_t, sin_t = _rope_tables(s)
    wq_pad = _pad_heads(w_q_up).astype(BF16)
    spread_gain = lambda g: jnp.concatenate([g[:MLA_NOPE], _spread_rope(g[MLA_NOPE:])]).reshape(1, -1)
    gq_pad = spread_gain(g_qk_q)
    gk_pad = spread_gain(g_qk_k)
    tq = min(512, s)
    w_kv_h = w_kv_up.reshape(MLA_KV_RANK, MLA_HEADS, MLA_NOPE + MLA_V)
    wk = w_kv_h[:, :, :MLA_NOPE].reshape(MLA_KV_RANK, MLA_HEADS * MLA_NOPE).astype(BF16)
    wvt = w_kv_h[:, :, MLA_NOPE:].reshape(MLA_KV_RANK, MLA_HEADS * MLA_V).T.astype(BF16)
    q, k, v = _qkv_prep(ql, kvl, krdt, cos_t, sin_t, g_q_lat.reshape(1, -1), wq_pad,
                        g_kv_lat.reshape(1, -1), wk, wvt, gq_pad, gk_pad, b, s, tq)
    o_mla = _flash(q, k, v, min(1024, s))

    x1, h2, idx_t, gates_t, rank_t, cnt = _merge(
        x2, y_ssd.reshape(t, d), o_mla.reshape(t, d), gate,
        w_ssd_out.astype(BF16), w_mla_out.astype(BF16), w_o.astype(BF16),
        g_ffn.reshape(1, -1), w_router.T, b_router.reshape(-1, 1), min(512, t))

    nsub = t // MOE_SUB
    assert t % MOE_SUB == 0 and nsub <= LANES
    cap = t * TOP_K + nsub * N_EXPERTS * (SEG_ALIGN - 1) + N_EXPERTS * (MOE_BLOCK - 1)
    nblk = -(-cap // MOE_BLOCK)
    slot_t, seg, loff, gdst, tail = _tables(cnt, idx_t, rank_t)
    used = (tail[0:1, 2] // MOE_BLOCK).astype(I32)
    seg, loff, gdst = seg.reshape(-1), loff.reshape(-1), gdst.reshape(-1)
    xs = _dispatch(seg, loff, gdst, tail[:, 0], tail[:, 1], used, h2, slot_t, nblk)
    ys = _experts(tail[:, 3], tail[:, 1], used, xs, w_gate_up, b_gate_up, w_down, b_down)
    out = _combine(seg, loff, gdst, x1, slot_t.T, gates_t.T, ys)
    return out.reshape(b, s, d)


def kernel(x, g_mix, w_in, conv_w, conv_b, dt_bias, a_log, d_skip, g_ssd_out, w_ssd_out, g_q_lat, w_q_up, g_kv_lat, w_kv_up, g_qk_q, g_qk_k, w_mla_out, w_o, g_ffn, w_router, b_router, w_gate_up, b_gate_up, w_down, b_down):
    params = (g_mix, w_in, conv_w, conv_b, dt_bias, a_log, d_skip, g_ssd_out, w_ssd_out, g_q_lat, w_q_up,
              g_kv_lat, w_kv_up, g_qk_q, g_qk_k, w_mla_out, w_o, g_ffn, w_router, b_router,
              w_gate_up, b_gate_up, w_down, b_down)
    for l in range(g_mix.shape[0]):
        x = _layer(x, *(p[l] for p in params))
    return x
```

```python
import functools
import math

import jax
import jax.numpy as jnp
from jax import lax
from jax.experimental import pallas as pl
from jax.experimental.pallas import tpu as pltpu

F32 = jnp.float32
BF16 = jnp.bfloat16
I32 = jnp.int32

EPS = 1e-6
CHUNK = 64

SSD_HEADS = 16
SSD_HEAD_DIM = 64
SSD_GROUPS = 2
SSD_STATE = 128
SSD_CONV = 4
SSD_D_INNER = SSD_HEADS * SSD_HEAD_DIM
SSD_CONV_DIM = SSD_D_INNER + 2 * SSD_GROUPS * SSD_STATE

MLA_HEADS = 8
MLA_Q_RANK = 384
MLA_KV_RANK = 256
MLA_NOPE = 128
MLA_ROPE = 64
MLA_QK_DIM = MLA_NOPE + MLA_ROPE
MLA_V = 128
V_ROWS = MLA_V + 16
DT_LO = MLA_ROPE // 2
ROPE_BASE = 10000.0

N_EXPERTS = 32
TOP_K = 4
SWIGLU_ALPHA = 1.702
SWIGLU_LIMIT = 7.0

LANES = 128
VMEM_LIMIT = 56 * 1024 * 1024
NEG_BIG = -1e30
LOG2E = math.log2(math.e)

MOE_BLOCK = 256
MOE_SUB = 256
SEG_ALIGN = 16
LOCAL_SLOTS = -(-(TOP_K * MOE_SUB + N_EXPERTS * (SEG_ALIGN - 1)) // MOE_BLOCK) * MOE_BLOCK


def _cparams(semantics, **kw):
    return pltpu.CompilerParams(dimension_semantics=semantics,
                                vmem_limit_bytes=VMEM_LIMIT, **kw)


def _sigmoid(v):
    return 1.0 / (1.0 + jnp.exp(-v))


def _silu(v):
    return v * _sigmoid(v)


def _bf16_pieces(t, n):
    pieces = []
    for _ in range(n - 1):
        p = t.astype(BF16)
        pieces.append(p)
        t = t - p.astype(F32)
    pieces.append(t.astype(BF16))
    return pieces


def _inproj_kernel(x_ref, g_ref, w_ref, z_ref, xbc_ref, ql_ref, kvl_ref, krdt_ref, gate_ref, *, segs):
    x = x_ref[...]
    ms = jnp.mean(x * x, axis=-1, keepdims=True)
    h = (x * lax.rsqrt(ms + EPS) * g_ref[...]).astype(BF16)
    outs = (z_ref, xbc_ref, ql_ref, kvl_ref, krdt_ref, gate_ref)
    for ref, (lo, hi) in zip(outs, segs):
        p = jnp.dot(h, w_ref[:, lo:hi], preferred_element_type=F32)
        if ref is gate_ref:
            p = _sigmoid(p)
        ref[...] = p.astype(ref.dtype)


def _in_proj(x2, g_mix, w_cat, widths, tm):
    t, d = x2.shape
    offs = [0]
    for w in widths:
        offs.append(offs[-1] + w)
    segs = tuple((offs[i], offs[i + 1]) for i in range(len(widths)))
    dts = (BF16, BF16, BF16, BF16, F32, BF16)
    out_shape = tuple(jax.ShapeDtypeStruct((t, w), dt) for w, dt in zip(widths, dts))
    out_specs = tuple(pl.BlockSpec((tm, w), lambda i: (i, 0)) for w in widths)
    return pl.pallas_call(
        functools.partial(_inproj_kernel, segs=segs),
        grid=(t // tm,),
        in_specs=[pl.BlockSpec((tm, d), lambda i: (i, 0)),
                  pl.BlockSpec((1, d), lambda i: (0, 0)),
                  pl.BlockSpec(w_cat.shape, lambda i: (0, 0))],
        out_specs=out_specs,
        out_shape=out_shape,
        compiler_params=_cparams(("parallel",)),
        name="in_proj",
    )(x2, g_mix.reshape(1, d), w_cat)


def _ssd_kernel(xbc_ref, z_ref, krdt_ref, convw_ref, convb_ref, dtb_ref, alog_ref, dskip_ref, gout_ref,
                y_ref, xext_scr, state_scr, *, lt):
    i = pl.program_id(1)
    halo = 8
    dt_lo = DT_LO
    gw = SSD_D_INNER // SSD_GROUPS
    hpg = SSD_HEADS // SSD_GROUPS

    @pl.when(i == 0)
    def _():
        state_scr[...] = jnp.zeros_like(state_scr)
        xext_scr[0:halo, :] = jnp.zeros((halo, SSD_CONV_DIM), F32)

    xext_scr[halo:halo + lt, :] = xbc_ref[0].astype(F32)
    acc = jnp.broadcast_to(convb_ref[...], (lt, SSD_CONV_DIM))
    for j in range(SSD_CONV):
        acc = acc + convw_ref[j:j + 1, :] * xext_scr[pl.ds(halo - (SSD_CONV - 1) + j, lt), :]
    xext_scr[0:halo, :] = xext_scr[lt:lt + halo, :]
    xbc = _silu(acc)
    xs = xbc[:, :SSD_D_INNER]
    bm = xbc[:, SSD_D_INNER:SSD_D_INNER + SSD_GROUPS * SSD_STATE]
    cm = xbc[:, SSD_D_INNER + SSD_GROUPS * SSD_STATE:]

    lane = lax.broadcasted_iota(I32, (1, LANES), 1)
    head_lane = (lane >= dt_lo) & (lane < dt_lo + SSD_HEADS)
    v = krdt_ref[0] + dtb_ref[...]
    dt = jnp.maximum(v, 0.0) + jnp.log(1.0 + jnp.exp(-jnp.abs(v)))
    dt = jnp.where(head_lane, dt, 0.0)
    a = jnp.where(head_lane, -jnp.exp(alog_ref[...]) * LOG2E, 0.0)
    da = dt * a
    row = lax.broadcasted_iota(I32, (lt, lt), 0)
    col = lax.broadcasted_iota(I32, (lt, lt), 1)
    tri = row >= col
    tri_b = tri.astype(BF16)
    a_cum = jnp.zeros((lt, LANES), F32)
    for piece in _bf16_pieces(da, 3):
        a_cum = a_cum + jnp.dot(tri_b, piece, preferred_element_type=F32)
    a_cum_t = a_cum.T
    a_last = a_cum[lt - 1:lt, :]
    exp_a = jnp.exp2(a_cum)
    dec = jnp.exp2(a_last - a_cum)

    er = lax.broadcasted_iota(I32, (LANES, SSD_D_INNER), 0)
    ec = lax.broadcasted_iota(I32, (LANES, SSD_D_INNER), 1)
    expand = ((er - dt_lo) == (ec // SSD_HEAD_DIM)).astype(BF16)

    def widen(t, pieces):
        out = jnp.zeros((lt, SSD_D_INNER), F32)
        for piece in _bf16_pieces(t, pieces):
            out = out + jnp.dot(piece, expand, preferred_element_type=F32)
        return out

    dt_w = widen(dt, 2)
    exp_a_w = widen(exp_a, 1)
    dec_w = widen(dec, 1)
    xdt = xs * dt_w
    xdt_b = xdt.astype(BF16)
    xdec_b = (xdt * dec_w).astype(BF16)
    lane_pair = lax.broadcasted_iota(I32, (lt, LANES), 1)

    y_parts = []
    for g in range(SSD_GROUPS):
        bg = bm[:, g * SSD_STATE:(g + 1) * SSD_STATE].astype(BF16)
        cg = cm[:, g * SSD_STATE:(g + 1) * SSD_STATE].astype(BF16)
        cb = lax.dot_general(cg, bg, (((1,), (1,)), ((), ())), preferred_element_type=F32)
        for pair in range(hpg // 2):
            h0 = g * hpg + 2 * pair
            rhs = xdt_b[:, h0 * SSD_HEAD_DIM:(h0 + 2) * SSD_HEAD_DIM]
            res = []
            for h in (h0, h0 + 1):
                seg = a_cum[:, dt_lo + h:dt_lo + h + 1] - a_cum_t[dt_lo + h:dt_lo + h + 1, :]
                lmat = jnp.exp2(jnp.where(tri, seg, NEG_BIG))
                res.append(jnp.dot((cb * lmat).astype(BF16), rhs, preferred_element_type=F32))
            y_parts.append(jnp.where(lane_pair < SSD_HEAD_DIM, res[0], res[1]))
        st = state_scr[g]
        y_off = jnp.dot(cg, st.astype(BF16), preferred_element_type=F32)
        y_parts.append(y_off * exp_a_w[:, g * gw:(g + 1) * gw])
        new = lax.dot_general(bg, xdec_b[:, g * gw:(g + 1) * gw], (((0,), (0,)), ((), ())),
                              preferred_element_type=F32)
        state_scr[g] = st * exp_a_w[lt - 1:lt, g * gw:(g + 1) * gw] + new

    zs = _silu(z_ref[0].astype(F32))
    outs = []
    npg = hpg // 2 + 1
    for g in range(SSD_GROUPS):
        parts = y_parts[g * npg:(g + 1) * npg]
        yd = jnp.concatenate(parts[:-1], axis=-1)
        yg = yd + parts[-1] + dskip_ref[:, g * gw:(g + 1) * gw] * xs[:, g * gw:(g + 1) * gw]
        yg = yg * zs[:, g * gw:(g + 1) * gw]
        ms = jnp.mean(yg * yg, axis=-1, keepdims=True)
        outs.append(yg * lax.rsqrt(ms + EPS) * gout_ref[:, g * gw:(g + 1) * gw])
    y_ref[0] = jnp.concatenate(outs, axis=-1).astype(y_ref.dtype)


def _ssd(xbc, z, krdt, conv_w, conv_b, dtb, alog, dskip_w, g_out, lt):
    b, s, _ = xbc.shape
    gw = SSD_D_INNER // SSD_GROUPS
    full = lambda shape: pl.BlockSpec(shape, lambda bi, i: (0,) * len(shape))
    return pl.pallas_call(
        functools.partial(_ssd_kernel, lt=lt),
        grid=(b, s // lt),
        in_specs=[pl.BlockSpec((1, lt, SSD_CONV_DIM), lambda bi, i: (bi, i, 0)),
                  pl.BlockSpec((1, lt, SSD_D_INNER), lambda bi, i: (bi, i, 0)),
                  pl.BlockSpec((1, lt, LANES), lambda bi, i: (bi, i, 0)),
                  full((SSD_CONV, SSD_CONV_DIM)), full((1, SSD_CONV_DIM)),
                  full((1, LANES)), full((1, LANES)),
                  full((1, SSD_D_INNER)), full((1, SSD_D_INNER))],
        out_specs=pl.BlockSpec((1, lt, SSD_D_INNER), lambda bi, i: (bi, i, 0)),
        out_shape=jax.ShapeDtypeStruct((b, s, SSD_D_INNER), BF16),
        scratch_shapes=[pltpu.VMEM((lt + 8, SSD_CONV_DIM), F32),
                        pltpu.VMEM((SSD_GROUPS, SSD_STATE, gw), F32)],
        compiler_params=_cparams(("parallel", "arbitrary")),
        name="ssd_scan",
    )(xbc, z, krdt, conv_w, conv_b, dtb, alog, dskip_w, g_out)


def _swap_halves(t):
    return pltpu.roll(t, LANES // 2, 1)


def _lane_sums(sq, width):
    ones = jnp.ones((sq.shape[1], width), BF16)
    return jnp.dot(sq.astype(BF16), ones, preferred_element_type=F32)


def _qkv_kernel(ql_ref, kvl_ref, krdt_ref, cos_ref, sin_ref, gql_ref, wq_ref, gkvl_ref, wk_ref, wvt_ref,
                gq_ref, gk_ref, q_ref, k_ref, v_ref, *, scale):
    hw = 2 * LANES
    cosv = cos_ref[...]
    sinv = sin_ref[...]

    ql = ql_ref[...].astype(F32)
    rq = lax.rsqrt(_lane_sums(ql * ql, LANES) * (1.0 / MLA_Q_RANK) + EPS)
    qn = (ql * jnp.concatenate([rq] * (MLA_Q_RANK // LANES), axis=1) * gql_ref[...]).astype(BF16)

    kvl = kvl_ref[...].astype(F32)
    rkv = lax.rsqrt(_lane_sums(kvl * kvl, LANES) * (1.0 / MLA_KV_RANK) + EPS)
    kvn = (kvl * jnp.concatenate([rkv] * (MLA_KV_RANK // LANES), axis=1) * gkvl_ref[...]).astype(BF16)
    tm = kvn.shape[0]
    ones_rows = (lax.broadcasted_iota(I32, (V_ROWS - MLA_V, tm), 0) == 0).astype(v_ref.dtype)

    lane = lax.broadcasted_iota(I32, (1, LANES), 1)
    kr = jnp.where((lane % (LANES // 2)) < MLA_ROPE // 2, krdt_ref[...], 0.0)
    ss_r = _lane_sums(kr * kr, LANES)
    krg = kr * gk_ref[:, LANES:]
    kr_rot = krg * cosv + _swap_halves(krg) * sinv

    qf = jnp.dot(qn, wq_ref[...], preferred_element_type=F32)
    kf = jnp.dot(kvn, wk_ref[...], preferred_element_type=F32)
    vt = lax.dot_general(wvt_ref[...], kvn, (((1,), (1,)), ((), ())), preferred_element_type=F32)
    for h in range(MLA_HEADS):
        qh = qf[:, h * hw:(h + 1) * hw]
        r = lax.rsqrt(_lane_sums(qh * qh, hw) * (1.0 / MLA_QK_DIM) + EPS)
        qs = qh * (r * scale) * gq_ref[...]
        q_ref[0, h, :, 0:LANES] = qs[:, :LANES].astype(q_ref.dtype)
        qr = qs[:, LANES:]
        q_ref[0, h, :, LANES:hw] = (qr * cosv + _swap_halves(qr) * sinv).astype(q_ref.dtype)

        kn = kf[:, h * LANES:(h + 1) * LANES]
        rk = lax.rsqrt((_lane_sums(kn * kn, LANES) + ss_r) * (1.0 / MLA_QK_DIM) + EPS)
        k_ref[0, h, :, 0:LANES] = (kn * rk * gk_ref[:, :LANES]).astype(k_ref.dtype)
        k_ref[0, h, :, LANES:hw] = (kr_rot * rk).astype(k_ref.dtype)
        v_ref[0, h, 0:MLA_V, :] = vt[h * MLA_V:(h + 1) * MLA_V, :].astype(v_ref.dtype)
        v_ref[0, h, MLA_V:V_ROWS, :] = ones_rows


def _qkv_prep(ql, kvl, krdt, cos_t, sin_t, g_q_lat, wq_pad, g_kv_lat, wk, wvt, gq_pad, gk_pad, b, s, tm):
    t = b * s
    nst = s // tm
    full = lambda shape: pl.BlockSpec(shape, lambda bi, i: (0,) * len(shape))
    tok = lambda w: pl.BlockSpec((tm, w), lambda bi, i: (bi * nst + i, 0))
    hs = lambda w: pl.BlockSpec((1, MLA_HEADS, tm, w), lambda bi, i: (bi, 0, i, 0))
    return pl.pallas_call(
        functools.partial(_qkv_kernel, scale=MLA_QK_DIM ** -0.5 * math.log2(math.e)),
        grid=(b, nst),
        in_specs=[tok(MLA_Q_RANK), tok(MLA_KV_RANK), tok(LANES),
                  pl.BlockSpec((tm, LANES), lambda bi, i: (i, 0)),
                  pl.BlockSpec((tm, LANES), lambda bi, i: (i, 0)),
                  full((1, MLA_Q_RANK)), full(wq_pad.shape), full((1, MLA_KV_RANK)), full(wk.shape),
                  full(wvt.shape), full((1, 2 * LANES)), full((1, 2 * LANES))],
        out_specs=(hs(2 * LANES), hs(2 * LANES),
                   pl.BlockSpec((1, MLA_HEADS, V_ROWS, tm), lambda bi, i: (bi, 0, 0, i))),
        out_shape=(jax.ShapeDtypeStruct((b, MLA_HEADS, s, 2 * LANES), BF16),
                   jax.ShapeDtypeStruct((b, MLA_HEADS, s, 2 * LANES), BF16),
                   jax.ShapeDtypeStruct((b, MLA_HEADS, V_ROWS, s), BF16)),
        compiler_params=_cparams(("parallel", "parallel")),
        name="qkv_prep",
    )(ql, kvl, krdt, cos_t, sin_t, g_q_lat, wq_pad, g_kv_lat, wk, wvt, gq_pad, gk_pad)


def _flash_kernel(q_ref, k_ref, vt_ref, o_ref, m_scr, acc_scr, s_scr, *, tq):
    i = pl.program_id(2)
    tk = tq // 2
    q = q_ref[0, 0]
    m_scr[...] = jnp.full(m_scr.shape, NEG_BIG, F32)
    acc_scr[...] = jnp.zeros(acc_scr.shape, F32)

    def scores(j, slot, lo=0):
        start = pl.multiple_of(j * tk, tk)
        ks = k_ref[0, 0, pl.ds(start, tk), :]
        s_scr[slot, :, lo:] = lax.dot_general(ks, q[lo:, :], (((1,), (1,)), ((), ())),
                                              preferred_element_type=F32)

    def softmax_pv(j, slot, masked, lo=0):
        start = pl.multiple_of(j * tk, tk)
        vt = vt_ref[0, 0, :, pl.ds(start, tk)]
        st = s_scr[slot, :, lo:]
        if masked:
            kc = lax.broadcasted_iota(I32, (tk, tk), 0) // CHUNK
            qc = lax.broadcasted_iota(I32, (tk, tk), 1) // CHUNK
            diag = jnp.where(kc <= qc, st[:, :tk], NEG_BIG)
            st = diag if st.shape[1] == tk else jnp.concatenate([diag, st[:, tk:]], axis=1)
        m_prev = m_scr[:, lo:]
        m_new = jnp.maximum(m_prev, jnp.max(st, axis=0, keepdims=True))
        alpha = jnp.exp2(m_prev - m_new)
        pt = jnp.exp2(st - m_new)
        acc_scr[:, lo:] = alpha * acc_scr[:, lo:] + jnp.dot(vt, pt.astype(BF16), preferred_element_type=F32)
        m_scr[:, lo:] = m_new

    scores(0, 0)

    def body(jj, carry):
        j = 2 * jj
        scores(j + 1, 1)
        softmax_pv(j, 0, False)
        scores(j + 2, 0)
        softmax_pv(j + 1, 1, False)
        return carry

    lax.fori_loop(0, i, body, 0)
    scores(2 * i + 1, 1, lo=tk)
    softmax_pv(2 * i, 0, True)
    softmax_pv(2 * i + 1, 1, True, lo=tk)

    o_ref[0] = (acc_scr[0:MLA_V, :] / acc_scr[MLA_V:MLA_V + 1, :]).T.astype(o_ref.dtype)


def _flash(q, k, v, tq):
    b, nh, s, _ = q.shape
    return pl.pallas_call(
        functools.partial(_flash_kernel, tq=tq),
        grid=(b, nh, s // tq),
        in_specs=[pl.BlockSpec((1, 1, tq, q.shape[-1]), lambda bi, h, i: (bi, h, i, 0)),
                  pl.BlockSpec((1, 1, s, k.shape[-1]), lambda bi, h, i: (bi, h, 0, 0)),
                  pl.BlockSpec((1, 1, V_ROWS, s), lambda bi, h, i: (bi, h, 0, 0))],
        out_specs=pl.BlockSpec((1, tq, MLA_V), lambda bi, h, i: (bi, i, h)),
        out_shape=jax.ShapeDtypeStruct((b, s, nh * MLA_V), BF16),
        scratch_shapes=[pltpu.VMEM((1, tq), F32), pltpu.VMEM((V_ROWS, tq), F32),
                        pltpu.VMEM((2, tq // 2, tq), F32)],
        compiler_params=_cparams(("parallel", "parallel", "arbitrary")),
        name="flash_attn",
    )(q, k, v)


def _merge_kernel(x_ref, ys_ref, om_ref, gate_ref, wss_ref, wml_ref, wo_ref, gffn_ref, wrt_ref, brt_ref,
                  x1_ref, h2_ref, idx_ref, gates_ref, rank_ref, cnt_ref, *, tm):
    d = x_ref.shape[-1]
    step = pl.program_id(0)

    @pl.when(step == 0)
    def _():
        cnt_ref[...] = jnp.zeros_like(cnt_ref)

    y1 = jnp.dot(ys_ref[...], wss_ref[...], preferred_element_type=F32)
    y2 = jnp.dot(om_ref[...], wml_ref[...], preferred_element_type=F32)
    g = gate_ref[...].astype(F32)
    merged = (g[:, :d] * y1 + g[:, d:] * y2).astype(BF16)
    x1 = x_ref[...] + jnp.dot(merged, wo_ref[...], preferred_element_type=F32)
    x1_ref[...] = x1
    ms = jnp.mean(x1 * x1, axis=-1, keepdims=True)
    h2 = x1 * lax.rsqrt(ms + EPS) * gffn_ref[...]
    h2_ref[...] = h2.astype(h2_ref.dtype)

    w_hi, w_lo = _bf16_pieces(wrt_ref[...], 2)
    h_hi, h_lo = _bf16_pieces(h2, 2)
    nt = (((1,), (1,)), ((), ()))
    logits = (lax.dot_general(w_hi, h_hi, nt, preferred_element_type=F32)
              + lax.dot_general(w_hi, h_lo, nt, preferred_element_type=F32)
              + lax.dot_general(w_lo, h_hi, nt, preferred_element_type=F32)) + brt_ref[...]
    eid = lax.broadcasted_iota(I32, (N_EXPERTS, tm), 0)
    cur = logits
    onehot = jnp.zeros((N_EXPERTS, tm), F32)
    vals, sels = [], []
    for k in range(TOP_K):
        mx = jnp.max(cur, axis=0, keepdims=True)
        idx = jnp.min(jnp.where(cur == mx, eid, N_EXPERTS), axis=0, keepdims=True)
        sel = eid == idx
        vals.append(mx)
        sels.append(sel)
        idx_ref[k:k + 1, :] = idx
        cur = jnp.where(sel, -jnp.inf, cur)
        onehot = onehot + sel.astype(F32)
    es = [jnp.exp(vk - vals[0]) for vk in vals]
    den = es[0] + es[1] + es[2] + es[3]
    for k in range(TOP_K):
        gates_ref[k:k + 1, :] = es[k] / den

    r = lax.broadcasted_iota(I32, (tm, tm), 0)
    c = lax.broadcasted_iota(I32, (tm, tm), 1)
    before = ((r < c) & ((r // MOE_SUB) == (c // MOE_SUB))).astype(BF16)
    prefix = jnp.dot(onehot.astype(BF16), before, preferred_element_type=F32)
    for k in range(TOP_K):
        rank_ref[k:k + 1, :] = jnp.sum(jnp.where(sels[k], prefix, 0.0), axis=0, keepdims=True).astype(I32)
    lane = lax.broadcasted_iota(I32, (N_EXPERTS, LANES), 1)
    cnt = cnt_ref[...]
    for g in range(tm // MOE_SUB):
        c_g = jnp.sum(onehot[:, g * MOE_SUB:(g + 1) * MOE_SUB], axis=1, keepdims=True)
        cnt = jnp.where(lane == step * (tm // MOE_SUB) + g, c_g, cnt)
    cnt_ref[...] = cnt


def _merge(x2, y_ssd, o_mla, gate, w_ss, w_ml, w_o, g_ffn, w_rt, b_rt, tm):
    t, d = x2.shape
    full = lambda shape: pl.BlockSpec(shape, lambda i: (0,) * len(shape))
    tok = lambda w: pl.BlockSpec((tm, w), lambda i: (i, 0))
    sel = pl.BlockSpec((TOP_K, tm), lambda i: (0, i))
    return pl.pallas_call(
        functools.partial(_merge_kernel, tm=tm),
        grid=(t // tm,),
        in_specs=[tok(d), tok(d), tok(d), tok(2 * d), full((d, d)), full((d, d)), full((d, d)),
                  full((1, d)), full((N_EXPERTS, d)), full((N_EXPERTS, 1))],
        out_specs=(tok(d), tok(d), sel, sel, sel, full((N_EXPERTS, LANES))),
        out_shape=(jax.ShapeDtypeStruct((t, d), F32), jax.ShapeDtypeStruct((t, d), BF16),
                   jax.ShapeDtypeStruct((TOP_K, t), I32), jax.ShapeDtypeStruct((TOP_K, t), F32),
                   jax.ShapeDtypeStruct((TOP_K, t), I32), jax.ShapeDtypeStruct((N_EXPERTS, LANES), F32)),
        compiler_params=_cparams(("arbitrary",)),
        name="merge_route",
    )(x2, y_ssd, o_mla, gate, w_ss, w_ml, w_o, g_ffn, w_rt, b_rt)


def _excl_cumsum_rows(v):
    eid = lax.broadcasted_iota(I32, v.shape, 0)
    out = jnp.zeros(v.shape, F32)
    for e in range(N_EXPERTS - 1):
        out = out + jnp.where(eid > e, v[e:e + 1, :], 0.0)
    return out


def _tables_kernel(cnt_ref, idx_ref, rank_ref, slot_ref, seg_ref, loff_ref, gdst_ref, tail_ref):
    t = idx_ref.shape[1]
    cnt = cnt_ref[...]
    seg = jnp.ceil(cnt * (1.0 / SEG_ALIGN)) * SEG_ALIGN
    loff = _excl_cumsum_rows(seg)
    r = lax.broadcasted_iota(I32, (LANES, LANES), 0)
    c = lax.broadcasted_iota(I32, (LANES, LANES), 1)
    run = jnp.dot((seg * (1.0 / SEG_ALIGN)).astype(BF16), (r < c).astype(BF16),
                  preferred_element_type=F32) * SEG_ALIGN
    tot = jnp.sum(seg, axis=1, keepdims=True)
    padded = jnp.broadcast_to(jnp.ceil(tot * (1.0 / MOE_BLOCK)) * MOE_BLOCK, (N_EXPERTS, LANES))
    start = _excl_cumsum_rows(padded)
    end = start + padded
    seg_ref[...] = seg.astype(I32)
    loff_ref[...] = loff.astype(I32)
    gdst_ref[...] = (start + run).astype(I32)

    gr = lax.broadcasted_iota(I32, (LANES, t), 0)
    gc = lax.broadcasted_iota(I32, (LANES, t), 1) // MOE_SUB
    loff_tok = jnp.dot((loff * (1.0 / SEG_ALIGN)).astype(BF16), (gr == gc).astype(BF16),
                       preferred_element_type=F32) * SEG_ALIGN
    idx = idx_ref[...]
    slot = rank_ref[...]
    for e in range(N_EXPERTS):
        slot = slot + jnp.where(idx == e, loff_tok[e:e + 1, :].astype(I32), 0)
    slot_ref[...] = slot

    lane = lax.broadcasted_iota(I32, (N_EXPERTS, LANES), 1)
    total = jnp.max(end, axis=0, keepdims=True)
    tail = jnp.where(lane == 0, start + tot, jnp.where(lane == 1, end, jnp.where(lane == 3, start, total)))
    tail_ref[...] = tail.astype(I32)


def _tables(cnt, idx_t, rank_t):
    t = idx_t.shape[1]
    tab = jax.ShapeDtypeStruct((N_EXPERTS, LANES), I32)
    return pl.pallas_call(
        _tables_kernel,
        out_shape=(jax.ShapeDtypeStruct((TOP_K, t), I32), tab, tab, tab, tab),
        compiler_params=pltpu.CompilerParams(vmem_limit_bytes=VMEM_LIMIT),
        name="route_tables",
    )(cnt, idx_t, rank_t)


def _pow2_sizes(limit):
    size = SEG_ALIGN
    while size * 2 <= limit:
        size *= 2
    sizes = []
    while size >= SEG_ALIGN:
        sizes.append(size)
        size //= 2
    return tuple(sizes)


def _piece_copies(src_ref, src_off, dst_ref, dst_off, n, limit, sem, wait=False):
    off = 0
    for size in _pow2_sizes(limit):
        take = n & size

        @pl.when(take != 0)
        def _(off=off, size=size):
            cp = pltpu.make_async_copy(
                src_ref.at[pl.ds(pl.multiple_of(src_off + off, SEG_ALIGN), size), :],
                dst_ref.at[pl.ds(pl.multiple_of(dst_off + off, SEG_ALIGN), size), :], sem)
            if wait:
                cp.wait()
            else:
                cp.start()

        off = off + take


def _dispatch_kernel(seg_ref, loff_ref, gdst_ref, taillo_ref, tailhi_ref, used_ref,
                     h_ref, slot_ref, xs_ref, loc_scr, zero_scr, sem, zsem, *, nblk):
    i = pl.program_id(0)
    par = i % 2

    @pl.when(i == 0)
    def _():
        nblk_all = xs_ref.shape[0] // MOE_BLOCK
        zero_scr[...] = jnp.zeros_like(zero_scr)

        def blk_copy(b):
            start = pl.multiple_of(b * MOE_BLOCK, MOE_BLOCK)
            return pltpu.make_async_copy(zero_scr, xs_ref.at[pl.ds(start, MOE_BLOCK), :], zsem)

        def issue_blk(b, carry):
            blk_copy(b).start()
            return carry

        def drain_blk(b, carry):
            blk_copy(b).wait()
            return carry

        lax.fori_loop(used_ref[0], nblk_all, issue_blk, 0)
        lax.fori_loop(used_ref[0], nblk_all, drain_blk, 0)

        def tails(wait):
            def body(e, carry):
                lo = taillo_ref[e]
                _piece_copies(zero_scr, 0, xs_ref, lo, tailhi_ref[e] - lo, MOE_BLOCK - 1, zsem, wait=wait)
                return carry
            return body

        lax.fori_loop(0, N_EXPERTS, tails(False), 0)
        lax.fori_loop(0, N_EXPERTS, tails(True), 0)

    slots = slot_ref[...]
    rid = lax.broadcasted_iota(I32, (LOCAL_SLOTS, MOE_SUB), 0)
    hit = rid == slots[0:1, :]
    for k in range(1, TOP_K):
        hit = hit | (rid == slots[k:k + 1, :])
    loc_scr[par] = jnp.dot(hit.astype(BF16), h_ref[...], preferred_element_type=F32).astype(BF16)

    loc = loc_scr.at[par]

    def seg_body(e, total):
        n = seg_ref[e * LANES + i]
        _piece_copies(loc, loff_ref[e * LANES + i], xs_ref, gdst_ref[e * LANES + i], n, MOE_SUB, sem.at[par])
        return total + n

    total = lax.fori_loop(0, N_EXPERTS, seg_body, 0)
    spare = (nblk + par * (LOCAL_SLOTS // MOE_BLOCK)) * MOE_BLOCK
    _piece_copies(loc, total, xs_ref, spare + total, LOCAL_SLOTS - total, LOCAL_SLOTS, sem.at[par])

    def wait_step(p):
        pltpu.make_async_copy(loc_scr.at[p], xs_ref.at[pl.ds(0, LOCAL_SLOTS), :], sem.at[p]).wait()

    @pl.when(i > 0)
    def _():
        wait_step(1 - par)

    @pl.when(i == pl.num_programs(0) - 1)
    def _():
        wait_step(par)


def _dispatch(seg, loff, gdst, tail_lo, tail_hi, used, h2, slot_t, nblk):
    t, d = h2.shape
    rows = (nblk + 2 * (LOCAL_SLOTS // MOE_BLOCK)) * MOE_BLOCK
    return pl.pallas_call(
        functools.partial(_dispatch_kernel, nblk=nblk),
        grid_spec=pltpu.PrefetchScalarGridSpec(
            num_scalar_prefetch=6,
            grid=(t // MOE_SUB,),
            in_specs=[pl.BlockSpec((MOE_SUB, d), lambda i, *_: (i, 0)),
                      pl.BlockSpec((TOP_K, MOE_SUB), lambda i, *_: (0, i))],
            out_specs=pl.BlockSpec(memory_space=pl.ANY),
            scratch_shapes=[pltpu.VMEM((2, LOCAL_SLOTS, d), BF16), pltpu.VMEM((MOE_BLOCK, d), BF16),
                            pltpu.SemaphoreType.DMA((2,)), pltpu.SemaphoreType.DMA(())]),
        out_shape=jax.ShapeDtypeStruct((rows, d), BF16),
        compiler_params=_cparams(("arbitrary",)),
        name="moe_dispatch",
    )(seg, loff, gdst, tail_lo, tail_hi, used, h2, slot_t)


def _expert_kernel(lo_ref, hi_ref, used_ref, x_ref, wgu_ref, bgu_ref, wd_ref, bd_ref, y_ref,
                   wgu_scr, wd_scr, xbuf, ybuf, xsem, ysem):
    e = pl.program_id(0)
    ff = wd_ref.shape[1]
    first = lo_ref[e] // MOE_BLOCK
    n = (hi_ref[e] - lo_ref[e]) // MOE_BLOCK

    big = 2 * MOE_BLOCK
    npair = n // 2
    items = npair + n % 2

    def x_copy(k, rows):
        start = pl.multiple_of((first + 2 * k) * MOE_BLOCK, MOE_BLOCK)
        return pltpu.make_async_copy(x_ref.at[pl.ds(start, rows), :], xbuf.at[k % 2, pl.ds(0, rows), :],
                                     xsem.at[k % 2])

    def y_copy(k, rows):
        start = pl.multiple_of((first + 2 * k) * MOE_BLOCK, MOE_BLOCK)
        return pltpu.make_async_copy(ybuf.at[k % 2, pl.ds(0, rows), :], y_ref.at[pl.ds(start, rows), :],
                                     ysem.at[k % 2])

    def start_x(k):
        @pl.when(k < npair)
        def _():
            x_copy(k, big).start()

        @pl.when((k == npair) & (k < items))
        def _():
            x_copy(k, MOE_BLOCK).start()

    def compute(k, rows):
        slot = k % 2
        xb = xbuf[slot, 0:rows, :]
        gu = jnp.dot(xb, wgu_scr[...], preferred_element_type=F32) + bgu_ref[0]
        glu = jnp.minimum(gu[:, :ff], SWIGLU_LIMIT)
        lin = jnp.clip(gu[:, ff:], -SWIGLU_LIMIT, SWIGLU_LIMIT)
        act = glu * _sigmoid(SWIGLU_ALPHA * glu) * (lin + 1.0)
        y = jnp.dot(act.astype(BF16), wd_scr[...], preferred_element_type=F32) + bd_ref[0]
        ybuf[slot, 0:rows, :] = y.astype(ybuf.dtype)

    def item(k, rows):
        x_copy(k, rows).wait()
        start_x(k + 1)

        @pl.when(k >= 2)
        def _():
            y_copy(k - 2, big).wait()

        compute(k, rows)
        y_copy(k, rows).start()

    @pl.when(n > 0)
    def _():
        start_x(0)
        wgu_scr[...] = wgu_ref[0].astype(BF16)
        wd_scr[...] = wd_ref[0].astype(BF16)

    def body(k, carry):
        item(k, big)
        return carry

    lax.fori_loop(0, npair, body, 0)

    @pl.when(items > npair)
    def _():
        item(npair, MOE_BLOCK)

    @pl.when(items >= 2)
    def _():
        y_copy(items - 2, big).wait()

    @pl.when(items > npair)
    def _():
        y_copy(items - 1, MOE_BLOCK).wait()

    @pl.when((items == npair) & (items >= 1))
    def _():
        y_copy(items - 1, big).wait()

    @pl.when(e == pl.num_programs(0) - 1)
    def _():
        nblk_all = y_ref.shape[0] // MOE_BLOCK
        ybuf[0] = jnp.zeros(ybuf.shape[1:], ybuf.dtype)

        def zero_copy(b):
            start = pl.multiple_of(b * MOE_BLOCK, MOE_BLOCK)
            return pltpu.make_async_copy(ybuf.at[0, pl.ds(0, MOE_BLOCK), :],
                                         y_ref.at[pl.ds(start, MOE_BLOCK), :], ysem.at[0])

        def issue(b, carry):
            zero_copy(b).start()
            return carry

        def drain(b, carry):
            zero_copy(b).wait()
            return carry

        lax.fori_loop(used_ref[0], nblk_all, issue, 0)
        lax.fori_loop(used_ref[0], nblk_all, drain, 0)


def _experts(row_lo, row_hi, used, xs, w_gate_up, b_gate_up, w_down, b_down):
    rows, d = xs.shape
    ne, _, ff2 = w_gate_up.shape
    ff = ff2 // 2
    return pl.pallas_call(
        _expert_kernel,
        grid_spec=pltpu.PrefetchScalarGridSpec(
            num_scalar_prefetch=3,
            grid=(ne,),
            in_specs=[pl.BlockSpec(memory_space=pl.ANY),
                      pl.BlockSpec((1, d, ff2), lambda e, *_: (e, 0, 0)),
                      pl.BlockSpec((1, 1, ff2), lambda e, *_: (e, 0, 0)),
                      pl.BlockSpec((1, ff, d), lambda e, *_: (e, 0, 0)),
                      pl.BlockSpec((1, 1, d), lambda e, *_: (e, 0, 0))],
            out_specs=pl.BlockSpec(memory_space=pl.ANY),
            scratch_shapes=[pltpu.VMEM((d, ff2), BF16), pltpu.VMEM((ff, d), BF16),
                            pltpu.VMEM((2, 2 * MOE_BLOCK, d), BF16), pltpu.VMEM((2, 2 * MOE_BLOCK, d), BF16),
                            pltpu.SemaphoreType.DMA((2,)), pltpu.SemaphoreType.DMA((2,))]),
        out_shape=jax.ShapeDtypeStruct((rows, d), BF16),
        compiler_params=_cparams(("arbitrary",)),
        name="moe_experts",
    )(row_lo, row_hi, used, xs, w_gate_up, b_gate_up.reshape(ne, 1, ff2), w_down, b_down.reshape(ne, 1, d))


def _combine_kernel(seg_ref, loff_ref, gdst_ref, x1_ref, slot_ref, gates_ref, ys_ref, o_ref, loc_scr, sem):
    i = pl.program_id(0)
    par = i % 2

    def fetch(g, p):
        loc = loc_scr.at[p]

        def seg_body(e, total):
            n = seg_ref[e * LANES + g]
            _piece_copies(ys_ref, gdst_ref[e * LANES + g], loc, loff_ref[e * LANES + g], n, MOE_SUB, sem.at[p])
            return total + n

        total = lax.fori_loop(0, N_EXPERTS, seg_body, 0)
        _piece_copies(ys_ref, total, loc, total, LOCAL_SLOTS - total, LOCAL_SLOTS, sem.at[p])

    @pl.when(i == 0)
    def _():
        fetch(0, 0)

    @pl.when(i + 1 < pl.num_programs(0))
    def _():
        fetch(i + 1, 1 - par)

    pltpu.make_async_copy(ys_ref.at[pl.ds(0, LOCAL_SLOTS), :], loc_scr.at[par], sem.at[par]).wait()

    slots = slot_ref[...]
    g = gates_ref[...]
    rid = lax.broadcasted_iota(I32, (MOE_SUB, LOCAL_SLOTS), 1)
    gmat = jnp.where(rid == slots[:, 0:1], g[:, 0:1], 0.0)
    for k in range(1, TOP_K):
        gmat = gmat + jnp.where(rid == slots[:, k:k + 1], g[:, k:k + 1], 0.0)
    o_ref[...] = x1_ref[...] + jnp.dot(gmat.astype(BF16), loc_scr[par], preferred_element_type=F32)


def _combine(seg, loff, gdst, x1, slot_tk, gates_tk, ys):
    t, d = x1.shape
    tok = lambda w: pl.BlockSpec((MOE_SUB, w), lambda i, *_: (i, 0))
    return pl.pallas_call(
        _combine_kernel,
        grid_spec=pltpu.PrefetchScalarGridSpec(
            num_scalar_prefetch=3,
            grid=(t // MOE_SUB,),
            in_specs=[tok(d), tok(TOP_K), tok(TOP_K), pl.BlockSpec(memory_space=pl.ANY)],
            out_specs=tok(d),
            scratch_shapes=[pltpu.VMEM((2, LOCAL_SLOTS, d), BF16), pltpu.SemaphoreType.DMA((2,))]),
        out_shape=jax.ShapeDtypeStruct((t, d), F32),
        compiler_params=_cparams(("arbitrary",)),
        name="moe_combine",
    )(seg, loff, gdst, x1, slot_tk, gates_tk, ys)


def _rope_tables(s):
    pos = jnp.arange(s, dtype=F32)
    inv = ROPE_BASE ** (-jnp.arange(0, MLA_ROPE, 2, dtype=F32) / MLA_ROPE)
    ang = pos[:, None] * inv[None, :]
    cos, sin = jnp.cos(ang), jnp.sin(ang)
    return _spread_rope(jnp.concatenate([cos, cos], axis=-1)), _spread_rope(jnp.concatenate([-sin, sin], axis=-1))


def _spread_rope(w, gap=None):
    half = MLA_ROPE // 2
    zeros = jnp.zeros(w.shape[:-1] + (half,), w.dtype)
    return jnp.concatenate([w[..., :half], zeros if gap is None else gap, w[..., half:], zeros], axis=-1)


def _pad_heads(w):
    lead = w.shape[:-1]
    w = w.reshape(lead + (MLA_HEADS, MLA_QK_DIM))
    w = jnp.concatenate([w[..., :MLA_NOPE], _spread_rope(w[..., MLA_NOPE:])], axis=-1)
    return w.reshape(lead + (MLA_HEADS * 2 * LANES,))


def _layer(x, g_mix, w_in, conv_w, conv_b, dt_bias, a_log, d_skip, g_ssd_out, w_ssd_out,
           g_q_lat, w_q_up, g_kv_lat, w_kv_up, g_qk_q, g_qk_k, w_mla_out, w_o,
           g_ffn, w_router, b_router, w_gate_up, b_gate_up, w_down, b_down):
    b, s, d = x.shape
    t = b * s
    x2 = x.reshape(t, d)

    off_xbc = SSD_D_INNER
    off_dt = off_xbc + SSD_CONV_DIM
    off_ql = off_dt + SSD_HEADS
    off_kvl = off_ql + MLA_Q_RANK
    off_kr = off_kvl + MLA_KV_RANK
    off_gate = off_kr + MLA_ROPE
    dt_gap = jnp.pad(w_in[:, off_dt:off_ql], ((0, 0), (0, MLA_ROPE // 2 - SSD_HEADS)))
    w_cat = jnp.concatenate([
        w_in[:, :off_dt], w_in[:, off_ql:off_kr],
        _spread_rope(w_in[:, off_kr:off_gate], gap=dt_gap),
        w_in[:, off_gate:]], axis=1).astype(BF16)
    widths = (SSD_D_INNER, SSD_CONV_DIM, MLA_Q_RANK, MLA_KV_RANK, LANES, 2 * d)

    tm = min(512, s)
    z, xbc, ql, kvl, krdt, gate = _in_proj(x2, g_mix, w_cat, widths, tm)

    lane_pad = lambda vec: jnp.pad(vec, (DT_LO, LANES - DT_LO - SSD_HEADS)).reshape(1, LANES)
    lt = min(256, s)
    y_ssd = _ssd(xbc.reshape(b, s, SSD_CONV_DIM), z.reshape(b, s, SSD_D_INNER), krdt.reshape(b, s, LANES),
                 conv_w, conv_b.reshape(1, -1), lane_pad(dt_bias), lane_pad(a_log),
                 jnp.repeat(d_skip, SSD_HEAD_DIM).reshape(1, -1), g_ssd_out.reshape(1, -1), lt)

    cos_t, sin_t = _rope_tables(s)
    wq_pad = _pad_heads(w_q_up).astype(BF16)
    spread_gain = lambda g: jnp.concatenate([g[:MLA_NOPE], _spread_rope(g[MLA_NOPE:])]).reshape(1, -1)
    gq_pad = spread_gain(g_qk_q)
    gk_pad = spread_gain(g_qk_k)
    tq = min(512, s)
    w_kv_h = w_kv_up.reshape(MLA_KV_RANK, MLA_HEADS, MLA_NOPE + MLA_V)
    wk = w_kv_h[:, :, :MLA_NOPE].reshape(MLA_KV_RANK, MLA_HEADS * MLA_NOPE).astype(BF16)
    wvt = w_kv_h[:, :, MLA_NOPE:].reshape(MLA_KV_RANK, MLA_HEADS * MLA_V).T.astype(BF16)
    q, k, v = _qkv_prep(ql, kvl, krdt, cos_t, sin_t, g_q_lat.reshape(1, -1), wq_pad,
                        g_kv_lat.reshape(1, -1), wk, wvt, gq_pad, gk_pad, b, s, tq)
    o_mla = _flash(q, k, v, min(1024, s))

    x1, h2, idx_t, gates_t, rank_t, cnt = _merge(
        x2, y_ssd.reshape(t, d), o_mla.reshape(t, d), gate,
        w_ssd_out.astype(BF16), w_mla_out.astype(BF16), w_o.astype(BF16),
        g_ffn.reshape(1, -1), w_router.T, b_router.reshape(-1, 1), min(512, t))

    nsub = t // MOE_SUB
    assert t % MOE_SUB == 0 and nsub <= LANES
    cap = t * TOP_K + nsub * N_EXPERTS * (SEG_ALIGN - 1) + N_EXPERTS * (MOE_BLOCK - 1)
    nblk = -(-cap // MOE_BLOCK)
    slot_t, seg, loff, gdst, tail = _tables(cnt, idx_t, rank_t)
    used = (tail[0:1, 2] // MOE_BLOCK).astype(I32)
    seg, loff, gdst = seg.reshape(-1), loff.reshape(-1), gdst.reshape(-1)
    xs = _dispatch(seg, loff, gdst, tail[:, 0], tail[:, 1], used, h2, slot_t, nblk)
    ys = _experts(tail[:, 3], tail[:, 1], used, xs, w_gate_up, b_gate_up, w_down, b_down)
    out = _combine(seg, loff, gdst, x1, slot_t.T, gates_t.T, ys)
    return out.reshape(b, s, d)


def kernel(x, g_mix, w_in, conv_w, conv_b, dt_bias, a_log, d_skip, g_ssd_out, w_ssd_out, g_q_lat, w_q_up, g_kv_lat, w_kv_up, g_qk_q, g_qk_k, w_mla_out, w_o, g_ffn, w_router, b_router, w_gate_up, b_gate_up, w_down, b_down):
    params = (g_mix, w_in, conv_w, conv_b, dt_bias, a_log, d_skip, g_ssd_out, w_ssd_out, g_q_lat, w_q_up,
              g_kv_lat, w_kv_up, g_qk_q, g_qk_k, w_mla_out, w_o, g_ffn, w_router, b_router,
              w_gate_up, b_gate_up, w_down, b_down)
    for l in range(g_mix.shape[0]):
        x = _layer(x, *(p[l] for p in params))
    return x
```

```python
import functools
import math

import jax
import jax.numpy as jnp
from jax import lax
from jax.experimental import pallas as pl
from jax.experimental.pallas import tpu as pltpu

F32 = jnp.float32
BF16 = jnp.bfloat16
I32 = jnp.int32

EPS = 1e-6
CHUNK = 64

SSD_HEADS = 16
SSD_HEAD_DIM = 64
SSD_GROUPS = 2
SSD_STATE = 128
SSD_CONV = 4
SSD_D_INNER = SSD_HEADS * SSD_HEAD_DIM
SSD_CONV_DIM = SSD_D_INNER + 2 * SSD_GROUPS * SSD_STATE

MLA_HEADS = 8
MLA_Q_RANK = 384
MLA_KV_RANK = 256
MLA_NOPE = 128
MLA_ROPE = 64
MLA_QK_DIM = MLA_NOPE + MLA_ROPE
MLA_V = 128
V_ROWS = MLA_V + 16
DT_LO = MLA_ROPE // 2
ROPE_BASE = 10000.0

N_EXPERTS = 32
TOP_K = 4
SWIGLU_ALPHA = 1.702
SWIGLU_LIMIT = 7.0

LANES = 128
VMEM_LIMIT = 56 * 1024 * 1024
NEG_BIG = -1e30
LOG2E = math.log2(math.e)

MOE_BLOCK = 256
MOE_SUB = 256
ROW_DTYPE = F32
SEG_ALIGN = 8 * 4 // jnp.dtype(ROW_DTYPE).itemsize
LOCAL_SLOTS = -(-(TOP_K * MOE_SUB + N_EXPERTS * (SEG_ALIGN - 1)) // MOE_BLOCK) * MOE_BLOCK


def _cparams(semantics, **kw):
    return pltpu.CompilerParams(dimension_semantics=semantics,
                                vmem_limit_bytes=VMEM_LIMIT, **kw)


def _sigmoid(v):
    return 1.0 / (1.0 + jnp.exp(-v))


def _silu(v):
    return v * _sigmoid(v)


def _bf16_pieces(t, n):
    pieces = []
    for _ in range(n - 1):
        p = t.astype(BF16)
        pieces.append(p)
        t = t - p.astype(F32)
    pieces.append(t.astype(BF16))
    return pieces


def _inproj_kernel(x_ref, g_ref, w_ref, z_ref, xbc_ref, ql_ref, kvl_ref, krdt_ref, gate_ref, *, segs):
    x = x_ref[...]
    ms = jnp.mean(x * x, axis=-1, keepdims=True)
    h = (x * lax.rsqrt(ms + EPS) * g_ref[...]).astype(BF16)
    outs = (z_ref, xbc_ref, ql_ref, kvl_ref, krdt_ref, gate_ref)
    for ref, (lo, hi) in zip(outs, segs):
        p = jnp.dot(h, w_ref[:, lo:hi], preferred_element_type=F32)
        if ref is gate_ref:
            p = _sigmoid(p)
        ref[...] = p.astype(ref.dtype)


def _in_proj(x2, g_mix, w_cat, widths, tm):
    t, d = x2.shape
    offs = [0]
    for w in widths:
        offs.append(offs[-1] + w)
    segs = tuple((offs[i], offs[i + 1]) for i in range(len(widths)))
    dts = (BF16, BF16, BF16, BF16, F32, BF16)
    out_shape = tuple(jax.ShapeDtypeStruct((t, w), dt) for w, dt in zip(widths, dts))
    out_specs = tuple(pl.BlockSpec((tm, w), lambda i: (i, 0)) for w in widths)
    return pl.pallas_call(
        functools.partial(_inproj_kernel, segs=segs),
        grid=(t // tm,),
        in_specs=[pl.BlockSpec((tm, d), lambda i: (i, 0)),
                  pl.BlockSpec((1, d), lambda i: (0, 0)),
                  pl.BlockSpec(w_cat.shape, lambda i: (0, 0))],
        out_specs=out_specs,
        out_shape=out_shape,
        compiler_params=_cparams(("parallel",)),
        name="in_proj",
    )(x2, g_mix.reshape(1, d), w_cat)


def _ssd_kernel(xbc_ref, z_ref, krdt_ref, convw_ref, convb_ref, dtb_ref, alog_ref, dskip_ref, gout_ref,
                y_ref, xext_scr, state_scr, *, lt):
    i = pl.program_id(1)
    halo = 8
    dt_lo = DT_LO
    gw = SSD_D_INNER // SSD_GROUPS
    hpg = SSD_HEADS // SSD_GROUPS

    @pl.when(i == 0)
    def _():
        state_scr[...] = jnp.zeros_like(state_scr)
        xext_scr[0:halo, :] = jnp.zeros((halo, SSD_CONV_DIM), F32)

    xext_scr[halo:halo + lt, :] = xbc_ref[0].astype(F32)
    acc = jnp.broadcast_to(convb_ref[...], (lt, SSD_CONV_DIM))
    for j in range(SSD_CONV):
        acc = acc + convw_ref[j:j + 1, :] * xext_scr[pl.ds(halo - (SSD_CONV - 1) + j, lt), :]
    xext_scr[0:halo, :] = xext_scr[lt:lt + halo, :]
    xbc = _silu(acc)
    xs = xbc[:, :SSD_D_INNER]
    bm = xbc[:, SSD_D_INNER:SSD_D_INNER + SSD_GROUPS * SSD_STATE]
    cm = xbc[:, SSD_D_INNER + SSD_GROUPS * SSD_STATE:]

    lane = lax.broadcasted_iota(I32, (1, LANES), 1)
    head_lane = (lane >= dt_lo) & (lane < dt_lo + SSD_HEADS)
    v = krdt_ref[0] + dtb_ref[...]
    dt = jnp.maximum(v, 0.0) + jnp.log(1.0 + jnp.exp(-jnp.abs(v)))
    dt = jnp.where(head_lane, dt, 0.0)
    a = jnp.where(head_lane, -jnp.exp(alog_ref[...]) * LOG2E, 0.0)
    da = dt * a
    row = lax.broadcasted_iota(I32, (lt, lt), 0)
    col = lax.broadcasted_iota(I32, (lt, lt), 1)
    tri = row >= col
    tri_b = tri.astype(BF16)
    a_cum = jnp.zeros((lt, LANES), F32)
    for piece in _bf16_pieces(da, 3):
        a_cum = a_cum + jnp.dot(tri_b, piece, preferred_element_type=F32)
    a_cum_t = a_cum.T
    a_last = a_cum[lt - 1:lt, :]
    exp_a = jnp.exp2(a_cum)
    dec = jnp.exp2(a_last - a_cum)

    er = lax.broadcasted_iota(I32, (LANES, SSD_D_INNER), 0)
    ec = lax.broadcasted_iota(I32, (LANES, SSD_D_INNER), 1)
    expand = ((er - dt_lo) == (ec // SSD_HEAD_DIM)).astype(BF16)

    def widen(t, pieces):
        out = jnp.zeros((lt, SSD_D_INNER), F32)
        for piece in _bf16_pieces(t, pieces):
            out = out + jnp.dot(piece, expand, preferred_element_type=F32)
        return out

    dt_w = widen(dt, 2)
    exp_a_w = widen(exp_a, 1)
    dec_w = widen(dec, 1)
    xdt = xs * dt_w
    xdt_b = xdt.astype(BF16)
    xdec_b = (xdt * dec_w).astype(BF16)
    lane_pair = lax.broadcasted_iota(I32, (lt, LANES), 1)

    y_parts = []
    for g in range(SSD_GROUPS):
        bg = bm[:, g * SSD_STATE:(g + 1) * SSD_STATE].astype(BF16)
        cg = cm[:, g * SSD_STATE:(g + 1) * SSD_STATE].astype(BF16)
        cb = lax.dot_general(cg, bg, (((1,), (1,)), ((), ())), preferred_element_type=F32)
        for pair in range(hpg // 2):
            h0 = g * hpg + 2 * pair
            rhs = xdt_b[:, h0 * SSD_HEAD_DIM:(h0 + 2) * SSD_HEAD_DIM]
            res = []
            for h in (h0, h0 + 1):
                seg = a_cum[:, dt_lo + h:dt_lo + h + 1] - a_cum_t[dt_lo + h:dt_lo + h + 1, :]
                lmat = jnp.exp2(jnp.where(tri, seg, NEG_BIG))
                res.append(jnp.dot((cb * lmat).astype(BF16), rhs, preferred_element_type=F32))
            y_parts.append(jnp.where(lane_pair < SSD_HEAD_DIM, res[0], res[1]))
        st = state_scr[g]
        y_off = jnp.dot(cg, st.astype(BF16), preferred_element_type=F32)
        y_parts.append(y_off * exp_a_w[:, g * gw:(g + 1) * gw])
        new = lax.dot_general(bg, xdec_b[:, g * gw:(g + 1) * gw], (((0,), (0,)), ((), ())),
                              preferred_element_type=F32)
        state_scr[g] = st * exp_a_w[lt - 1:lt, g * gw:(g + 1) * gw] + new

    zs = _silu(z_ref[0].astype(F32))
    outs = []
    npg = hpg // 2 + 1
    for g in range(SSD_GROUPS):
        parts = y_parts[g * npg:(g + 1) * npg]
        yd = jnp.concatenate(parts[:-1], axis=-1)
        yg = yd + parts[-1] + dskip_ref[:, g * gw:(g + 1) * gw] * xs[:, g * gw:(g + 1) * gw]
        yg = yg * zs[:, g * gw:(g + 1) * gw]
        ms = jnp.mean(yg * yg, axis=-1, keepdims=True)
        outs.append(yg * lax.rsqrt(ms + EPS) * gout_ref[:, g * gw:(g + 1) * gw])
    y_ref[0] = jnp.concatenate(outs, axis=-1).astype(y_ref.dtype)


def _ssd(xbc, z, krdt, conv_w, conv_b, dtb, alog, dskip_w, g_out, lt):
    b, s, _ = xbc.shape
    gw = SSD_D_INNER // SSD_GROUPS
    full = lambda shape: pl.BlockSpec(shape, lambda bi, i: (0,) * len(shape))
    return pl.pallas_call(
        functools.partial(_ssd_kernel, lt=lt),
        grid=(b, s // lt),
        in_specs=[pl.BlockSpec((1, lt, SSD_CONV_DIM), lambda bi, i: (bi, i, 0)),
                  pl.BlockSpec((1, lt, SSD_D_INNER), lambda bi, i: (bi, i, 0)),
                  pl.BlockSpec((1, lt, LANES), lambda bi, i: (bi, i, 0)),
                  full((SSD_CONV, SSD_CONV_DIM)), full((1, SSD_CONV_DIM)),
                  full((1, LANES)), full((1, LANES)),
                  full((1, SSD_D_INNER)), full((1, SSD_D_INNER))],
        out_specs=pl.BlockSpec((1, lt, SSD_D_INNER), lambda bi, i: (bi, i, 0)),
        out_shape=jax.ShapeDtypeStruct((b, s, SSD_D_INNER), BF16),
        scratch_shapes=[pltpu.VMEM((lt + 8, SSD_CONV_DIM), F32),
                        pltpu.VMEM((SSD_GROUPS, SSD_STATE, gw), F32)],
        compiler_params=_cparams(("parallel", "arbitrary")),
        name="ssd_scan",
    )(xbc, z, krdt, conv_w, conv_b, dtb, alog, dskip_w, g_out)


def _swap_halves(t):
    return pltpu.roll(t, LANES // 2, 1)


def _lane_sums(sq, width):
    ones = jnp.ones((sq.shape[1], width), BF16)
    return jnp.dot(sq.astype(BF16), ones, preferred_element_type=F32)


def _qkv_kernel(ql_ref, kvl_ref, krdt_ref, cos_ref, sin_ref, gql_ref, wq_ref, gkvl_ref, wk_ref, wvt_ref,
                gq_ref, gk_ref, q_ref, k_ref, v_ref, *, scale):
    hw = 2 * LANES
    cosv = cos_ref[...]
    sinv = sin_ref[...]

    ql = ql_ref[...].astype(F32)
    rq = lax.rsqrt(_lane_sums(ql * ql, LANES) * (1.0 / MLA_Q_RANK) + EPS)
    qn = (ql * jnp.concatenate([rq] * (MLA_Q_RANK // LANES), axis=1) * gql_ref[...]).astype(BF16)

    kvl = kvl_ref[...].astype(F32)
    rkv = lax.rsqrt(_lane_sums(kvl * kvl, LANES) * (1.0 / MLA_KV_RANK) + EPS)
    kvn = (kvl * jnp.concatenate([rkv] * (MLA_KV_RANK // LANES), axis=1) * gkvl_ref[...]).astype(BF16)
    tm = kvn.shape[0]
    ones_rows = (lax.broadcasted_iota(I32, (V_ROWS - MLA_V, tm), 0) == 0).astype(v_ref.dtype)

    lane = lax.broadcasted_iota(I32, (1, LANES), 1)
    kr = jnp.where((lane % (LANES // 2)) < MLA_ROPE // 2, krdt_ref[...], 0.0)
    ss_r = _lane_sums(kr * kr, LANES)
    krg = kr * gk_ref[:, LANES:]
    kr_rot = krg * cosv + _swap_halves(krg) * sinv

    qf = jnp.dot(qn, wq_ref[...], preferred_element_type=F32)
    kf = jnp.dot(kvn, wk_ref[...], preferred_element_type=F32)
    vt = lax.dot_general(wvt_ref[...], kvn, (((1,), (1,)), ((), ())), preferred_element_type=F32)
    for h in range(MLA_HEADS):
        qh = qf[:, h * hw:(h + 1) * hw]
        r = lax.rsqrt(_lane_sums(qh * qh, hw) * (1.0 / MLA_QK_DIM) + EPS)
        qs = qh * (r * scale) * gq_ref[...]
        q_ref[0, h, :, 0:LANES] = qs[:, :LANES].astype(q_ref.dtype)
        qr = qs[:, LANES:]
        q_ref[0, h, :, LANES:hw] = (qr * cosv + _swap_halves(qr) * sinv).astype(q_ref.dtype)

        kn = kf[:, h * LANES:(h + 1) * LANES]
        rk = lax.rsqrt((_lane_sums(kn * kn, LANES) + ss_r) * (1.0 / MLA_QK_DIM) + EPS)
        k_ref[0, h, :, 0:LANES] = (kn * rk * gk_ref[:, :LANES]).astype(k_ref.dtype)
        k_ref[0, h, :, LANES:hw] = (kr_rot * rk).astype(k_ref.dtype)
        v_ref[0, h, 0:MLA_V, :] = vt[h * MLA_V:(h + 1) * MLA_V, :].astype(v_ref.dtype)
        v_ref[0, h, MLA_V:V_ROWS, :] = ones_rows


def _qkv_prep(ql, kvl, krdt, cos_t, sin_t, g_q_lat, wq_pad, g_kv_lat, wk, wvt, gq_pad, gk_pad, b, s, tm):
    t = b * s
    nst = s // tm
    full = lambda shape: pl.BlockSpec(shape, lambda bi, i: (0,) * len(shape))
    tok = lambda w: pl.BlockSpec((tm, w), lambda bi, i: (bi * nst + i, 0))
    hs = lambda w: pl.BlockSpec((1, MLA_HEADS, tm, w), lambda bi, i: (bi, 0, i, 0))
    return pl.pallas_call(
        functools.partial(_qkv_kernel, scale=MLA_QK_DIM ** -0.5 * math.log2(math.e)),
        grid=(b, nst),
        in_specs=[tok(MLA_Q_RANK), tok(MLA_KV_RANK), tok(LANES),
                  pl.BlockSpec((tm, LANES), lambda bi, i: (i, 0)),
                  pl.BlockSpec((tm, LANES), lambda bi, i: (i, 0)),
                  full((1, MLA_Q_RANK)), full(wq_pad.shape), full((1, MLA_KV_RANK)), full(wk.shape),
                  full(wvt.shape), full((1, 2 * LANES)), full((1, 2 * LANES))],
        out_specs=(hs(2 * LANES), hs(2 * LANES),
                   pl.BlockSpec((1, MLA_HEADS, V_ROWS, tm), lambda bi, i: (bi, 0, 0, i))),
        out_shape=(jax.ShapeDtypeStruct((b, MLA_HEADS, s, 2 * LANES), BF16),
                   jax.ShapeDtypeStruct((b, MLA_HEADS, s, 2 * LANES), BF16),
                   jax.ShapeDtypeStruct((b, MLA_HEADS, V_ROWS, s), BF16)),
        compiler_params=_cparams(("parallel", "parallel")),
        name="qkv_prep",
    )(ql, kvl, krdt, cos_t, sin_t, g_q_lat, wq_pad, g_kv_lat, wk, wvt, gq_pad, gk_pad)


def _flash_kernel(q_ref, k_ref, vt_ref, o_ref, m_scr, acc_scr, s_scr, *, tq):
    i = pl.program_id(2)
    tk = tq // 2
    q = q_ref[0, 0]
    m_scr[...] = jnp.full(m_scr.shape, NEG_BIG, F32)
    acc_scr[...] = jnp.zeros(acc_scr.shape, F32)

    def scores(j, slot, lo=0):
        start = pl.multiple_of(j * tk, tk)
        ks = k_ref[0, 0, pl.ds(start, tk), :]
        s_scr[slot, :, lo:] = lax.dot_general(ks, q[lo:, :], (((1,), (1,)), ((), ())),
                                              preferred_element_type=F32)

    def softmax_pv(j, slot, masked, lo=0):
        start = pl.multiple_of(j * tk, tk)
        vt = vt_ref[0, 0, :, pl.ds(start, tk)]
        st = s_scr[slot, :, lo:]
        if masked:
            kc = lax.broadcasted_iota(I32, (tk, tk), 0) // CHUNK
            qc = lax.broadcasted_iota(I32, (tk, tk), 1) // CHUNK
            diag = jnp.where(kc <= qc, st[:, :tk], NEG_BIG)
            st = diag if st.shape[1] == tk else jnp.concatenate([diag, st[:, tk:]], axis=1)
        m_prev = m_scr[:, lo:]
        m_new = jnp.maximum(m_prev, jnp.max(st, axis=0, keepdims=True))
        alpha = jnp.exp2(m_prev - m_new)
        pt = jnp.exp2(st - m_new)
        acc_scr[:, lo:] = alpha * acc_scr[:, lo:] + jnp.dot(vt, pt.astype(BF16), preferred_element_type=F32)
        m_scr[:, lo:] = m_new

    scores(0, 0)

    def body(jj, carry):
        j = 2 * jj
        scores(j + 1, 1)
        softmax_pv(j, 0, False)
        scores(j + 2, 0)
        softmax_pv(j + 1, 1, False)
        return carry

    lax.fori_loop(0, i, body, 0)
    scores(2 * i + 1, 1, lo=tk)
    softmax_pv(2 * i, 0, True)
    softmax_pv(2 * i + 1, 1, True, lo=tk)

    o_ref[0] = (acc_scr[0:MLA_V, :] / acc_scr[MLA_V:MLA_V + 1, :]).T.astype(o_ref.dtype)


def _flash(q, k, v, tq):
    b, nh, s, _ = q.shape
    return pl.pallas_call(
        functools.partial(_flash_kernel, tq=tq),
        grid=(b, nh, s // tq),
        in_specs=[pl.BlockSpec((1, 1, tq, q.shape[-1]), lambda bi, h, i: (bi, h, i, 0)),
                  pl.BlockSpec((1, 1, s, k.shape[-1]), lambda bi, h, i: (bi, h, 0, 0)),
                  pl.BlockSpec((1, 1, V_ROWS, s), lambda bi, h, i: (bi, h, 0, 0))],
        out_specs=pl.BlockSpec((1, tq, MLA_V), lambda bi, h, i: (bi, i, h)),
        out_shape=jax.ShapeDtypeStruct((b, s, nh * MLA_V), BF16),
        scratch_shapes=[pltpu.VMEM((1, tq), F32), pltpu.VMEM((V_ROWS, tq), F32),
                        pltpu.VMEM((2, tq // 2, tq), F32)],
        compiler_params=_cparams(("parallel", "parallel", "arbitrary")),
        name="flash_attn",
    )(q, k, v)


def _merge_kernel(x_ref, ys_ref, om_ref, gate_ref, wss_ref, wml_ref, wo_ref, gffn_ref, wrt_ref, brt_ref,
                  x1_ref, h2_ref, idx_ref, gates_ref, rank_ref, cnt_ref, *, tm):
    d = x_ref.shape[-1]
    step = pl.program_id(0)

    @pl.when(step == 0)
    def _():
        cnt_ref[...] = jnp.zeros_like(cnt_ref)

    y1 = jnp.dot(ys_ref[...], wss_ref[...], preferred_element_type=F32)
    y2 = jnp.dot(om_ref[...], wml_ref[...], preferred_element_type=F32)
    g = gate_ref[...].astype(F32)
    merged = (g[:, :d] * y1 + g[:, d:] * y2).astype(BF16)
    x1 = x_ref[...] + jnp.dot(merged, wo_ref[...], preferred_element_type=F32)
    x1_ref[...] = x1
    ms = jnp.mean(x1 * x1, axis=-1, keepdims=True)
    h2 = x1 * lax.rsqrt(ms + EPS) * gffn_ref[...]
    h2_ref[...] = h2.astype(h2_ref.dtype)

    w_hi, w_lo = _bf16_pieces(wrt_ref[...], 2)
    h_hi, h_lo = _bf16_pieces(h2, 2)
    nt = (((1,), (1,)), ((), ()))
    logits = (lax.dot_general(w_hi, h_hi, nt, preferred_element_type=F32)
              + lax.dot_general(w_hi, h_lo, nt, preferred_element_type=F32)
              + lax.dot_general(w_lo, h_hi, nt, preferred_element_type=F32)) + brt_ref[...]
    eid = lax.broadcasted_iota(I32, (N_EXPERTS, tm), 0)
    cur = logits
    onehot = jnp.zeros((N_EXPERTS, tm), F32)
    vals, sels = [], []
    for k in range(TOP_K):
        mx = jnp.max(cur, axis=0, keepdims=True)
        idx = jnp.min(jnp.where(cur == mx, eid, N_EXPERTS), axis=0, keepdims=True)
        sel = eid == idx
        vals.append(mx)
        sels.append(sel)
        idx_ref[k:k + 1, :] = idx
        cur = jnp.where(sel, -jnp.inf, cur)
        onehot = onehot + sel.astype(F32)
    es = [jnp.exp(vk - vals[0]) for vk in vals]
    den = es[0] + es[1] + es[2] + es[3]
    for k in range(TOP_K):
        gates_ref[k:k + 1, :] = es[k] / den

    r = lax.broadcasted_iota(I32, (tm, tm), 0)
    c = lax.broadcasted_iota(I32, (tm, tm), 1)
    before = ((r < c) & ((r // MOE_SUB) == (c // MOE_SUB))).astype(BF16)
    prefix = jnp.dot(onehot.astype(BF16), before, preferred_element_type=F32)
    for k in range(TOP_K):
        rank_ref[k:k + 1, :] = jnp.sum(jnp.where(sels[k], prefix, 0.0), axis=0, keepdims=True).astype(I32)
    lane = lax.broadcasted_iota(I32, (N_EXPERTS, LANES), 1)
    cnt = cnt_ref[...]
    for g in range(tm // MOE_SUB):
        c_g = jnp.sum(onehot[:, g * MOE_SUB:(g + 1) * MOE_SUB], axis=1, keepdims=True)
        cnt = jnp.where(lane == step * (tm // MOE_SUB) + g, c_g, cnt)
    cnt_ref[...] = cnt


def _merge(x2, y_ssd, o_mla, gate, w_ss, w_ml, w_o, g_ffn, w_rt, b_rt, tm):
    t, d = x2.shape
    full = lambda shape: pl.BlockSpec(shape, lambda i: (0,) * len(shape))
    tok = lambda w: pl.BlockSpec((tm, w), lambda i: (i, 0))
    sel = pl.BlockSpec((TOP_K, tm), lambda i: (0, i))
    return pl.pallas_call(
        functools.partial(_merge_kernel, tm=tm),
        grid=(t // tm,),
        in_specs=[tok(d), tok(d), tok(d), tok(2 * d), full((d, d)), full((d, d)), full((d, d)),
                  full((1, d)), full((N_EXPERTS, d)), full((N_EXPERTS, 1))],
        out_specs=(tok(d), tok(d), sel, sel, sel, full((N_EXPERTS, LANES))),
        out_shape=(jax.ShapeDtypeStruct((t, d), F32), jax.ShapeDtypeStruct((t, d), BF16),
                   jax.ShapeDtypeStruct((TOP_K, t), I32), jax.ShapeDtypeStruct((TOP_K, t), F32),
                   jax.ShapeDtypeStruct((TOP_K, t), I32), jax.ShapeDtypeStruct((N_EXPERTS, LANES), F32)),
        compiler_params=_cparams(("arbitrary",)),
        name="merge_route",
    )(x2, y_ssd, o_mla, gate, w_ss, w_ml, w_o, g_ffn, w_rt, b_rt)


def _excl_cumsum_rows(v):
    eid = lax.broadcasted_iota(I32, v.shape, 0)
    out = jnp.zeros(v.shape, F32)
    for e in range(N_EXPERTS - 1):
        out = out + jnp.where(eid > e, v[e:e + 1, :], 0.0)
    return out


def _tables_kernel(cnt_ref, idx_ref, rank_ref, slot_ref, seg_ref, loff_ref, gdst_ref, tail_ref):
    t = idx_ref.shape[1]
    cnt = cnt_ref[...]
    seg = jnp.ceil(cnt * (1.0 / SEG_ALIGN)) * SEG_ALIGN
    loff = _excl_cumsum_rows(seg)
    r = lax.broadcasted_iota(I32, (LANES, LANES), 0)
    c = lax.broadcasted_iota(I32, (LANES, LANES), 1)
    run = jnp.dot((seg * (1.0 / SEG_ALIGN)).astype(BF16), (r < c).astype(BF16),
                  preferred_element_type=F32) * SEG_ALIGN
    tot = jnp.sum(seg, axis=1, keepdims=True)
    padded = jnp.broadcast_to(jnp.ceil(tot * (1.0 / MOE_BLOCK)) * MOE_BLOCK, (N_EXPERTS, LANES))
    start = _excl_cumsum_rows(padded)
    end = start + padded
    seg_ref[...] = seg.astype(I32)
    loff_ref[...] = loff.astype(I32)
    gdst_ref[...] = (start + run).astype(I32)

    gr = lax.broadcasted_iota(I32, (LANES, t), 0)
    gc = lax.broadcasted_iota(I32, (LANES, t), 1) // MOE_SUB
    loff_tok = jnp.dot((loff * (1.0 / SEG_ALIGN)).astype(BF16), (gr == gc).astype(BF16),
                       preferred_element_type=F32) * SEG_ALIGN
    idx = idx_ref[...]
    slot = rank_ref[...]
    for e in range(N_EXPERTS):
        slot = slot + jnp.where(idx == e, loff_tok[e:e + 1, :].astype(I32), 0)
    slot_ref[...] = slot

    lane = lax.broadcasted_iota(I32, (N_EXPERTS, LANES), 1)
    total = jnp.max(end, axis=0, keepdims=True)
    tail = jnp.where(lane == 0, start + tot, jnp.where(lane == 1, end, jnp.where(lane == 3, start, total)))
    tail_ref[...] = tail.astype(I32)


def _tables(cnt, idx_t, rank_t):
    t = idx_t.shape[1]
    tab = jax.ShapeDtypeStruct((N_EXPERTS, LANES), I32)
    return pl.pallas_call(
        _tables_kernel,
        out_shape=(jax.ShapeDtypeStruct((TOP_K, t), I32), tab, tab, tab, tab),
        compiler_params=pltpu.CompilerParams(vmem_limit_bytes=VMEM_LIMIT),
        name="route_tables",
    )(cnt, idx_t, rank_t)


def _pow2_sizes(limit):
    size = SEG_ALIGN
    while size * 2 <= limit:
        size *= 2
    sizes = []
    while size >= SEG_ALIGN:
        sizes.append(size)
        size //= 2
    return tuple(sizes)


def _piece_copies(src_ref, src_off, dst_ref, dst_off, n, limit, sem, wait=False):
    off = 0
    for size in _pow2_sizes(limit):
        take = n & size

        @pl.when(take != 0)
        def _(off=off, size=size):
            cp = pltpu.make_async_copy(
                src_ref.at[pl.ds(pl.multiple_of(src_off + off, SEG_ALIGN), size), :],
                dst_ref.at[pl.ds(pl.multiple_of(dst_off + off, SEG_ALIGN), size), :], sem)
            if wait:
                cp.wait()
            else:
                cp.start()

        off = off + take


def _dispatch_kernel(seg_ref, loff_ref, gdst_ref, taillo_ref, tailhi_ref, used_ref,
                     h_ref, slot_ref, xs_ref, loc_scr, zero_scr, sem, zsem, *, nblk):
    i = pl.program_id(0)
    par = i % 2

    @pl.when(i == 0)
    def _():
        nblk_all = xs_ref.shape[0] // MOE_BLOCK
        zero_scr[...] = jnp.zeros_like(zero_scr)

        def blk_copy(b):
            start = pl.multiple_of(b * MOE_BLOCK, MOE_BLOCK)
            return pltpu.make_async_copy(zero_scr, xs_ref.at[pl.ds(start, MOE_BLOCK), :], zsem)

        def issue_blk(b, carry):
            blk_copy(b).start()
            return carry

        def drain_blk(b, carry):
            blk_copy(b).wait()
            return carry

        lax.fori_loop(used_ref[0], nblk_all, issue_blk, 0)
        lax.fori_loop(used_ref[0], nblk_all, drain_blk, 0)

        def tails(wait):
            def body(e, carry):
                lo = taillo_ref[e]
                _piece_copies(zero_scr, 0, xs_ref, lo, tailhi_ref[e] - lo, MOE_BLOCK - 1, zsem, wait=wait)
                return carry
            return body

        lax.fori_loop(0, N_EXPERTS, tails(False), 0)
        lax.fori_loop(0, N_EXPERTS, tails(True), 0)

    slots = slot_ref[...]
    rid = lax.broadcasted_iota(I32, (LOCAL_SLOTS, MOE_SUB), 0)
    hit = rid == slots[0:1, :]
    for k in range(1, TOP_K):
        hit = hit | (rid == slots[k:k + 1, :])
    loc_scr[par] = jnp.dot(hit.astype(BF16), h_ref[...], preferred_element_type=F32).astype(ROW_DTYPE)

    loc = loc_scr.at[par]

    def seg_body(e, total):
        n = seg_ref[e * LANES + i]
        _piece_copies(loc, loff_ref[e * LANES + i], xs_ref, gdst_ref[e * LANES + i], n, MOE_SUB, sem.at[par])
        return total + n

    total = lax.fori_loop(0, N_EXPERTS, seg_body, 0)
    spare = (nblk + par * (LOCAL_SLOTS // MOE_BLOCK)) * MOE_BLOCK
    _piece_copies(loc, total, xs_ref, spare + total, LOCAL_SLOTS - total, LOCAL_SLOTS, sem.at[par])

    def wait_step(p):
        pltpu.make_async_copy(loc_scr.at[p], xs_ref.at[pl.ds(0, LOCAL_SLOTS), :], sem.at[p]).wait()

    @pl.when(i > 0)
    def _():
        wait_step(1 - par)

    @pl.when(i == pl.num_programs(0) - 1)
    def _():
        wait_step(par)


def _dispatch(seg, loff, gdst, tail_lo, tail_hi, used, h2, slot_t, nblk):
    t, d = h2.shape
    rows = (nblk + 2 * (LOCAL_SLOTS // MOE_BLOCK)) * MOE_BLOCK
    return pl.pallas_call(
        functools.partial(_dispatch_kernel, nblk=nblk),
        grid_spec=pltpu.PrefetchScalarGridSpec(
            num_scalar_prefetch=6,
            grid=(t // MOE_SUB,),
            in_specs=[pl.BlockSpec((MOE_SUB, d), lambda i, *_: (i, 0)),
                      pl.BlockSpec((TOP_K, MOE_SUB), lambda i, *_: (0, i))],
            out_specs=pl.BlockSpec(memory_space=pl.ANY),
            scratch_shapes=[pltpu.VMEM((2, LOCAL_SLOTS, d), ROW_DTYPE), pltpu.VMEM((MOE_BLOCK, d), ROW_DTYPE),
                            pltpu.SemaphoreType.DMA((2,)), pltpu.SemaphoreType.DMA(())]),
        out_shape=jax.ShapeDtypeStruct((rows, d), ROW_DTYPE),
        compiler_params=_cparams(("arbitrary",)),
        name="moe_dispatch",
    )(seg, loff, gdst, tail_lo, tail_hi, used, h2, slot_t)


def _expert_kernel(lo_ref, hi_ref, used_ref, x_ref, wgu_ref, bgu_ref, wd_ref, bd_ref, y_ref,
                   wgu_scr, wd_scr, xbuf, ybuf, xsem, ysem):
    e = pl.program_id(0)
    ne = pl.num_programs(0)
    ff = wd_ref.shape[1]
    big = 2 * MOE_BLOCK

    def layout(ex):
        first = lo_ref[ex] // MOE_BLOCK
        n = (hi_ref[ex] - lo_ref[ex]) // MOE_BLOCK
        return first, n // 2, n // 2 + n % 2

    cur = layout(e)
    first, npair, items = cur

    def x_copy(lay, k, rows):
        start = pl.multiple_of((lay[0] + 2 * k) * MOE_BLOCK, MOE_BLOCK)
        return pltpu.make_async_copy(x_ref.at[pl.ds(start, rows), :], xbuf.at[k % 2, pl.ds(0, rows), :],
                                     xsem.at[k % 2])

    def y_copy(k, rows):
        start = pl.multiple_of((first + 2 * k) * MOE_BLOCK, MOE_BLOCK)
        return pltpu.make_async_copy(ybuf.at[k % 2, pl.ds(0, rows), :], y_ref.at[pl.ds(start, rows), :],
                                     ysem.at[k % 2])

    def start_x(lay, k):
        @pl.when(k < lay[1])
        def _():
            x_copy(lay, k, big).start()

        @pl.when((k == lay[1]) & (k < lay[2]))
        def _():
            x_copy(lay, k, MOE_BLOCK).start()

    def compute(k, rows):
        slot = k % 2
        xb = xbuf[slot, 0:rows, :].astype(BF16)
        gu = jnp.dot(xb, wgu_scr[...], preferred_element_type=F32) + bgu_ref[0]
        glu = jnp.minimum(gu[:, :ff], SWIGLU_LIMIT)
        lin = jnp.clip(gu[:, ff:], -SWIGLU_LIMIT, SWIGLU_LIMIT)
        act = glu * _sigmoid(SWIGLU_ALPHA * glu) * (lin + 1.0)
        y = jnp.dot(act.astype(BF16), wd_scr[...], preferred_element_type=F32) + bd_ref[0]
        ybuf[slot, 0:rows, :] = y.astype(ybuf.dtype)

    def item(k, rows):
        x_copy(cur, k, rows).wait()

        @pl.when(k >= 1)
        def _():
            start_x(cur, k + 1)

        @pl.when(k >= 2)
        def _():
            y_copy(k - 2, big).wait()

        compute(k, rows)
        y_copy(k, rows).start()

    @pl.when(e == 0)
    def _():
        start_x(cur, 0)
        start_x(cur, 1)

    @pl.when(items > 0)
    def _():
        wgu_scr[...] = wgu_ref[0].astype(BF16)
        wd_scr[...] = wd_ref[0].astype(BF16)

    def body(k, carry):
        item(k, big)
        return carry

    lax.fori_loop(0, npair, body, 0)

    @pl.when(items > npair)
    def _():
        item(npair, MOE_BLOCK)

    @pl.when(items >= 2)
    def _():
        y_copy(items - 2, big).wait()

    @pl.when(items > npair)
    def _():
        y_copy(items - 1, MOE_BLOCK).wait()

    @pl.when((items == npair) & (items >= 1))
    def _():
        y_copy(items - 1, big).wait()

    @pl.when(e + 1 < ne)
    def _():
        nxt = layout(jnp.minimum(e + 1, ne - 1))
        start_x(nxt, 0)
        start_x(nxt, 1)

    @pl.when(e == pl.num_programs(0) - 1)
    def _():
        nblk_all = y_ref.shape[0] // MOE_BLOCK
        ybuf[0] = jnp.zeros(ybuf.shape[1:], ybuf.dtype)

        def zero_copy(b):
            start = pl.multiple_of(b * MOE_BLOCK, MOE_BLOCK)
            return pltpu.make_async_copy(ybuf.at[0, pl.ds(0, MOE_BLOCK), :],
                                         y_ref.at[pl.ds(start, MOE_BLOCK), :], ysem.at[0])

        def issue(b, carry):
            zero_copy(b).start()
            return carry

        def drain(b, carry):
            zero_copy(b).wait()
            return carry

        lax.fori_loop(used_ref[0], nblk_all, issue, 0)
        lax.fori_loop(used_ref[0], nblk_all, drain, 0)


def _experts(row_lo, row_hi, used, xs, w_gate_up, b_gate_up, w_down, b_down):
    rows, d = xs.shape
    ne, _, ff2 = w_gate_up.shape
    ff = ff2 // 2
    return pl.pallas_call(
        _expert_kernel,
        grid_spec=pltpu.PrefetchScalarGridSpec(
            num_scalar_prefetch=3,
            grid=(ne,),
            in_specs=[pl.BlockSpec(memory_space=pl.ANY),
                      pl.BlockSpec((1, d, ff2), lambda e, *_: (e, 0, 0)),
                      pl.BlockSpec((1, 1, ff2), lambda e, *_: (e, 0, 0)),
                      pl.BlockSpec((1, ff, d), lambda e, *_: (e, 0, 0)),
                      pl.BlockSpec((1, 1, d), lambda e, *_: (e, 0, 0))],
            out_specs=pl.BlockSpec(memory_space=pl.ANY),
            scratch_shapes=[pltpu.VMEM((d, ff2), BF16), pltpu.VMEM((ff, d), BF16),
                            pltpu.VMEM((2, 2 * MOE_BLOCK, d), ROW_DTYPE),
                            pltpu.VMEM((2, 2 * MOE_BLOCK, d), ROW_DTYPE),
                            pltpu.SemaphoreType.DMA((2,)), pltpu.SemaphoreType.DMA((2,))]),
        out_shape=jax.ShapeDtypeStruct((rows, d), ROW_DTYPE),
        compiler_params=_cparams(("arbitrary",)),
        name="moe_experts",
    )(row_lo, row_hi, used, xs, w_gate_up, b_gate_up.reshape(ne, 1, ff2), w_down, b_down.reshape(ne, 1, d))


def _combine_kernel(seg_ref, loff_ref, gdst_ref, x1_ref, slot_ref, gates_ref, ys_ref, o_ref, loc_scr, sem):
    i = pl.program_id(0)
    par = i % 2

    def fetch(g, p):
        loc = loc_scr.at[p]

        def seg_body(e, total):
            n = seg_ref[e * LANES + g]
            _piece_copies(ys_ref, gdst_ref[e * LANES + g], loc, loff_ref[e * LANES + g], n, MOE_SUB, sem.at[p])
            return total + n

        total = lax.fori_loop(0, N_EXPERTS, seg_body, 0)
        _piece_copies(ys_ref, total, loc, total, LOCAL_SLOTS - total, LOCAL_SLOTS, sem.at[p])

    @pl.when(i == 0)
    def _():
        fetch(0, 0)

    @pl.when(i + 1 < pl.num_programs(0))
    def _():
        fetch(i + 1, 1 - par)

    pltpu.make_async_copy(ys_ref.at[pl.ds(0, LOCAL_SLOTS), :], loc_scr.at[par], sem.at[par]).wait()

    slots = slot_ref[...]
    g = gates_ref[...]
    rid = lax.broadcasted_iota(I32, (MOE_SUB, LOCAL_SLOTS), 1)
    gmat = jnp.where(rid == slots[:, 0:1], g[:, 0:1], 0.0)
    for k in range(1, TOP_K):
        gmat = gmat + jnp.where(rid == slots[:, k:k + 1], g[:, k:k + 1], 0.0)
    o_ref[...] = x1_ref[...] + jnp.dot(gmat.astype(BF16), loc_scr[par].astype(BF16),
                                       preferred_element_type=F32)


def _combine(seg, loff, gdst, x1, slot_tk, gates_tk, ys):
    t, d = x1.shape
    tok = lambda w: pl.BlockSpec((MOE_SUB, w), lambda i, *_: (i, 0))
    return pl.pallas_call(
        _combine_kernel,
        grid_spec=pltpu.PrefetchScalarGridSpec(
            num_scalar_prefetch=3,
            grid=(t // MOE_SUB,),
            in_specs=[tok(d), tok(TOP_K), tok(TOP_K), pl.BlockSpec(memory_space=pl.ANY)],
            out_specs=tok(d),
            scratch_shapes=[pltpu.VMEM((2, LOCAL_SLOTS, d), ROW_DTYPE), pltpu.SemaphoreType.DMA((2,))]),
        out_shape=jax.ShapeDtypeStruct((t, d), F32),
        compiler_params=_cparams(("arbitrary",)),
        name="moe_combine",
    )(seg, loff, gdst, x1, slot_tk, gates_tk, ys)


def _rope_tables(s):
    pos = jnp.arange(s, dtype=F32)
    inv = ROPE_BASE ** (-jnp.arange(0, MLA_ROPE, 2, dtype=F32) / MLA_ROPE)
    ang = pos[:, None] * inv[None, :]
    cos, sin = jnp.cos(ang), jnp.sin(ang)
    return _spread_rope(jnp.concatenate([cos, cos], axis=-1)), _spread_rope(jnp.concatenate([-sin, sin], axis=-1))


def _spread_rope(w, gap=None):
    half = MLA_ROPE // 2
    zeros = jnp.zeros(w.shape[:-1] + (half,), w.dtype)
    return jnp.concatenate([w[..., :half], zeros if gap is None else gap, w[..., half:], zeros], axis=-1)


def _pad_heads(w):
    lead = w.shape[:-1]
    w = w.reshape(lead + (MLA_HEADS, MLA_QK_DIM))
    w = jnp.concatenate([w[..., :MLA_NOPE], _spread_rope(w[..., MLA_NOPE:])], axis=-1)
    return w.reshape(lead + (MLA_HEADS * 2 * LANES,))


def _layer(x, g_mix, w_in, conv_w, conv_b, dt_bias, a_log, d_skip, g_ssd_out, w_ssd_out,
           g_q_lat, w_q_up, g_kv_lat, w_kv_up, g_qk_q, g_qk_k, w_mla_out, w_o,
           g_ffn, w_router, b_router, w_gate_up, b_gate_up, w_down, b_down):
    b, s, d = x.shape
    t = b * s
    x2 = x.reshape(t, d)

    off_xbc = SSD_D_INNER
    off_dt = off_xbc + SSD_CONV_DIM
    off_ql = off_dt + SSD_HEADS
    off_kvl = off_ql + MLA_Q_RANK
    off_kr = off_kvl + MLA_KV_RANK
    off_gate = off_kr + MLA_ROPE
    dt_gap = jnp.pad(w_in[:, off_dt:off_ql], ((0, 0), (0, MLA_ROPE // 2 - SSD_HEADS)))
    w_cat = jnp.concatenate([
        w_in[:, :off_dt], w_in[:, off_ql:off_kr],
        _spread_rope(w_in[:, off_kr:off_gate], gap=dt_gap),
        w_in[:, off_gate:]], axis=1).astype(BF16)
    widths = (SSD_D_INNER, SSD_CONV_DIM, MLA_Q_RANK, MLA_KV_RANK, LANES, 2 * d)

    tm = min(512, s)
    z, xbc, ql, kvl, krdt, gate = _in_proj(x2, g_mix, w_cat, widths, tm)

    lane_pad = lambda vec: jnp.pad(vec, (DT_LO, LANES - DT_LO - SSD_HEADS)).reshape(1, LANES)
    lt = min(256, s)
    y_ssd = _ssd(xbc.reshape(b, s, SSD_CONV_DIM), z.reshape(b, s, SSD_D_INNER), krdt.reshape(b, s, LANES),
                 conv_w, conv_b.reshape(1, -1), lane_pad(dt_bias), lane_pad(a_log),
                 jnp.repeat(d_skip, SSD_HEAD_DIM).reshape(1, -1), g_ssd_out.reshape(1, -1), lt)

    cos_t, sin_t = _rope_tables(s)
    wq_pad = _pad_heads(w_q_up).astype(BF16)
    spread_gain = lambda g: jnp.concatenate([g[:MLA_NOPE], _spread_rope(g[MLA_NOPE:])]).reshape(1, -1)
    gq_pad = spread_gain(g_qk_q)
    gk_pad = spread_gain(g_qk_k)
    tq = min(512, s)
    w_kv_h = w_kv_up.reshape(MLA_KV_RANK, MLA_HEADS, MLA_NOPE + MLA_V)
    wk = w_kv_h[:, :, :MLA_NOPE].reshape(MLA_KV_RANK, MLA_HEADS * MLA_NOPE).astype(BF16)
    wvt = w_kv_h[:, :, MLA_NOPE:].reshape(MLA_KV_RANK, MLA_HEADS * MLA_V).T.astype(BF16)
    q, k, v = _qkv_prep(ql, kvl, krdt, cos_t, sin_t, g_q_lat.reshape(1, -1), wq_pad,
                        g_kv_lat.reshape(1, -1), wk, wvt, gq_pad, gk_pad, b, s, tq)
    o_mla = _flash(q, k, v, min(1024, s))

    x1, h2, idx_t, gates_t, rank_t, cnt = _merge(
        x2, y_ssd.reshape(t, d), o_mla.reshape(t, d), gate,
        w_ssd_out.astype(BF16), w_mla_out.astype(BF16), w_o.astype(BF16),
        g_ffn.reshape(1, -1), w_router.T, b_router.reshape(-1, 1), min(512, t))

    nsub = t // MOE_SUB
    assert t % MOE_SUB == 0 and nsub <= LANES
    cap = t * TOP_K + nsub * N_EXPERTS * (SEG_ALIGN - 1) + N_EXPERTS * (MOE_BLOCK - 1)
    nblk = -(-cap // MOE_BLOCK)
    slot_t, seg, loff, gdst, tail = _tables(cnt, idx_t, rank_t)
    used = (tail[0:1, 2] // MOE_BLOCK).astype(I32)
    seg, loff, gdst = seg.reshape(-1), loff.reshape(-1), gdst.reshape(-1)
    xs = _dispatch(seg, loff, gdst, tail[:, 0], tail[:, 1], used, h2, slot_t, nblk)
    ys = _experts(tail[:, 3], tail[:, 1], used, xs, w_gate_up, b_gate_up, w_down, b_down)
    out = _combine(seg, loff, gdst, x1, slot_t.T, gates_t.T, ys)
    return out.reshape(b, s, d)


def kernel(x, g_mix, w_in, conv_w, conv_b, dt_bias, a_log, d_skip, g_ssd_out, w_ssd_out, g_q_lat, w_q_up, g_kv_lat, w_kv_up, g_qk_q, g_qk_k, w_mla_out, w_o, g_ffn, w_router, b_router, w_gate_up, b_gate_up, w_down, b_down):
    params = (g_mix, w_in, conv_w, conv_b, dt_bias, a_log, d_skip, g_ssd_out, w_ssd_out, g_q_lat, w_q_up,
              g_kv_lat, w_kv_up, g_qk_q, g_qk_k, w_mla_out, w_o, g_ffn, w_router, b_router,
              w_gate_up, b_gate_up, w_down, b_down)
    for l in range(g_mix.shape[0]):
        x = _layer(x, *(p[l] for p in params))
    return x
```

```python
import functools
import math

import jax
import jax.numpy as jnp
import numpy as np
from jax import lax
from jax.experimental import pallas as pl
from jax.experimental.pallas import tpu as pltpu

F32 = jnp.float32
BF16 = jnp.bfloat16
I32 = jnp.int32

EPS = 1e-6
CHUNK = 64

SSD_HEADS = 16
SSD_HEAD_DIM = 64
SSD_GROUPS = 2
SSD_STATE = 128
SSD_CONV = 4
SSD_D_INNER = SSD_HEADS * SSD_HEAD_DIM
SSD_CONV_DIM = SSD_D_INNER + 2 * SSD_GROUPS * SSD_STATE

MLA_HEADS = 8
MLA_Q_RANK = 384
MLA_KV_RANK = 256
MLA_NOPE = 128
MLA_ROPE = 64
MLA_QK_DIM = MLA_NOPE + MLA_ROPE
MLA_V = 128
V_ROWS = MLA_V + 16
DT_LO = MLA_ROPE // 2
SSD_SCAN_CHUNK = 128
ROPE_BASE = 10000.0

N_EXPERTS = 32
TOP_K = 4
SWIGLU_ALPHA = 1.702
SWIGLU_LIMIT = 7.0

LANES = 128
VMEM_LIMIT = 56 * 1024 * 1024
NEG_BIG = -1e30
LOG2E = math.log2(math.e)

MOE_BLOCK = 256
MOE_SUB = 256
ROW_DTYPE = F32
SEG_ALIGN = 8 * 4 // jnp.dtype(ROW_DTYPE).itemsize
LOCAL_SLOTS = -(-(TOP_K * MOE_SUB + N_EXPERTS * (SEG_ALIGN - 1)) // MOE_BLOCK) * MOE_BLOCK


def _cparams(semantics, **kw):
    return pltpu.CompilerParams(dimension_semantics=semantics,
                                vmem_limit_bytes=VMEM_LIMIT, **kw)


def _sigmoid(v):
    return 1.0 / (1.0 + jnp.exp(-v))


def _silu(v):
    return v * _sigmoid(v)


def _bf16_pieces(t, n):
    pieces = []
    for _ in range(n - 1):
        p = t.astype(BF16)
        pieces.append(p)
        t = t - p.astype(F32)
    pieces.append(t.astype(BF16))
    return pieces


def _inproj_kernel(x_ref, g_ref, w_ref, z_ref, xbc_ref, ql_ref, kvl_ref, krdt_ref, gate_ref, *, segs):
    x = x_ref[...]
    ms = jnp.mean(x * x, axis=-1, keepdims=True)
    h = (x * lax.rsqrt(ms + EPS) * g_ref[...]).astype(BF16)
    outs = (z_ref, xbc_ref, ql_ref, kvl_ref, krdt_ref, gate_ref)
    for ref, (lo, hi) in zip(outs, segs):
        p = jnp.dot(h, w_ref[:, lo:hi], preferred_element_type=F32)
        if ref is gate_ref:
            p = _sigmoid(p)
        ref[...] = p.astype(ref.dtype)


def _in_proj(x2, g_mix, w_cat, widths, tm):
    t, d = x2.shape
    offs = [0]
    for w in widths:
        offs.append(offs[-1] + w)
    segs = tuple((offs[i], offs[i + 1]) for i in range(len(widths)))
    dts = (BF16, BF16, BF16, BF16, F32, BF16)
    out_shape = tuple(jax.ShapeDtypeStruct((t, w), dt) for w, dt in zip(widths, dts))
    out_specs = tuple(pl.BlockSpec((tm, w), lambda i: (i, 0)) for w in widths)
    return pl.pallas_call(
        functools.partial(_inproj_kernel, segs=segs),
        grid=(t // tm,),
        in_specs=[pl.BlockSpec((tm, d), lambda i: (i, 0)),
                  pl.BlockSpec((1, d), lambda i: (0, 0)),
                  pl.BlockSpec(w_cat.shape, lambda i: (0, 0))],
        out_specs=out_specs,
        out_shape=out_shape,
        compiler_params=_cparams(("parallel",)),
        name="in_proj",
    )(x2, g_mix.reshape(1, d), w_cat)


def _ssd_kernel(xbc_ref, z_ref, krdt_ref, convw_ref, convb_ref, dtb_ref, alog_ref, dskip_ref, gout_ref,
                y_ref, xext_scr, state_scr, *, lt):
    i = pl.program_id(1)
    halo = 8
    dt_lo = DT_LO
    gw = SSD_D_INNER // SSD_GROUPS
    hpg = SSD_HEADS // SSD_GROUPS

    @pl.when(i == 0)
    def _():
        state_scr[...] = jnp.zeros_like(state_scr)
        xext_scr[0:halo, :] = jnp.zeros((halo, SSD_CONV_DIM), F32)

    xin = xbc_ref[0]
    xext_scr[halo:2 * halo, :] = xin[0:halo, :].astype(F32)
    srow = lax.broadcasted_iota(I32, (lt, lt), 0)
    scol = lax.broadcasted_iota(I32, (lt, lt), 1)
    acc = convb_ref[...] + convw_ref[SSD_CONV - 1:SSD_CONV, :] * xin.astype(F32)
    head = jnp.broadcast_to(convb_ref[...], (halo, SSD_CONV_DIM))
    for j in range(SSD_CONV):
        shift = SSD_CONV - 1 - j
        head = head + convw_ref[j:j + 1, :] * xext_scr[pl.ds(halo - shift, halo), :]
        if shift:
            shifted = jnp.dot((srow == scol + shift).astype(BF16), xin, preferred_element_type=F32)
            acc = acc + convw_ref[j:j + 1, :] * shifted
    acc = jnp.concatenate([head, acc[halo:, :]], axis=0)
    xext_scr[0:halo, :] = xin[lt - halo:lt, :].astype(F32)
    xbc = _silu(acc)
    xs = xbc[:, :SSD_D_INNER]
    bm = xbc[:, SSD_D_INNER:SSD_D_INNER + SSD_GROUPS * SSD_STATE]
    cm = xbc[:, SSD_D_INNER + SSD_GROUPS * SSD_STATE:]

    lane = lax.broadcasted_iota(I32, (1, LANES), 1)
    head_lane = (lane >= dt_lo) & (lane < dt_lo + SSD_HEADS)
    v = krdt_ref[0] + dtb_ref[...]
    dt = jnp.maximum(v, 0.0) + jnp.log(1.0 + jnp.exp(-jnp.abs(v)))
    dt = jnp.where(head_lane, dt, 0.0)
    a = jnp.where(head_lane, -jnp.exp(alog_ref[...]) * LOG2E, 0.0)
    da = dt * a
    lc = min(SSD_SCAN_CHUNK, lt)
    nchunk = lt // lc
    row = lax.broadcasted_iota(I32, (lt, lt), 0)
    col = lax.broadcasted_iota(I32, (lt, lt), 1)
    tri_b = ((row >= col) & ((row // lc) == (col // lc))).astype(BF16)
    a_cum = jnp.zeros((lt, LANES), F32)
    for piece in _bf16_pieces(da, 3):
        a_cum = a_cum + jnp.dot(tri_b, piece, preferred_element_type=F32)
    a_cum_t = a_cum.T
    a_last = jnp.concatenate([jnp.broadcast_to(a_cum[(c + 1) * lc - 1:(c + 1) * lc, :], (lc, LANES))
                              for c in range(nchunk)], axis=0)
    exp_a = jnp.exp2(a_cum)
    dec = jnp.exp2(a_last - a_cum)
    tri = lax.broadcasted_iota(I32, (lc, lc), 0) >= lax.broadcasted_iota(I32, (lc, lc), 1)

    er = lax.broadcasted_iota(I32, (LANES, SSD_D_INNER), 0)
    ec = lax.broadcasted_iota(I32, (LANES, SSD_D_INNER), 1)
    expand = ((er - dt_lo) == (ec // SSD_HEAD_DIM)).astype(BF16)

    def widen(t, pieces):
        out = jnp.zeros((lt, SSD_D_INNER), F32)
        for piece in _bf16_pieces(t, pieces):
            out = out + jnp.dot(piece, expand, preferred_element_type=F32)
        return out

    dt_w = widen(dt, 2)
    exp_a_w = widen(exp_a, 1)
    dec_w = widen(dec, 1)
    xdt = xs * dt_w
    xdt_b = xdt.astype(BF16)
    xdec_b = (xdt * dec_w).astype(BF16)
    lane_pair = lax.broadcasted_iota(I32, (lc, LANES), 1)

    states = [state_scr[g] for g in range(SSD_GROUPS)]
    y_rows = [[] for _ in range(SSD_GROUPS)]
    for c in range(nchunk):
        rows = slice(c * lc, (c + 1) * lc)
        for g in range(SSD_GROUPS):
            cols = slice(g * gw, (g + 1) * gw)
            bg = bm[rows, g * SSD_STATE:(g + 1) * SSD_STATE].astype(BF16)
            cg = cm[rows, g * SSD_STATE:(g + 1) * SSD_STATE].astype(BF16)
            cb = lax.dot_general(cg, bg, (((1,), (1,)), ((), ())), preferred_element_type=F32)
            parts = []
            for pair in range(hpg // 2):
                h0 = g * hpg + 2 * pair
                rhs = xdt_b[rows, h0 * SSD_HEAD_DIM:(h0 + 2) * SSD_HEAD_DIM]
                res = []
                for h in (h0, h0 + 1):
                    seg = a_cum[rows, dt_lo + h:dt_lo + h + 1] - a_cum_t[dt_lo + h:dt_lo + h + 1, rows]
                    lmat = jnp.exp2(jnp.where(tri, seg, NEG_BIG))
                    res.append(jnp.dot((cb * lmat).astype(BF16), rhs, preferred_element_type=F32))
                parts.append(jnp.where(lane_pair < SSD_HEAD_DIM, res[0], res[1]))
            st = states[g]
            y_off = jnp.dot(cg, st.astype(BF16), preferred_element_type=F32)
            y_rows[g].append(jnp.concatenate(parts, axis=-1) + y_off * exp_a_w[rows, cols])
            new = lax.dot_general(bg, xdec_b[rows, cols], (((0,), (0,)), ((), ())),
                                  preferred_element_type=F32)
            states[g] = st * exp_a_w[(c + 1) * lc - 1:(c + 1) * lc, cols] + new
    for g in range(SSD_GROUPS):
        state_scr[g] = states[g]

    zs = _silu(z_ref[0].astype(F32))
    outs = []
    for g in range(SSD_GROUPS):
        yg = (jnp.concatenate(y_rows[g], axis=0)
              + dskip_ref[:, g * gw:(g + 1) * gw] * xs[:, g * gw:(g + 1) * gw])
        yg = yg * zs[:, g * gw:(g + 1) * gw]
        ms = jnp.mean(yg * yg, axis=-1, keepdims=True)
        outs.append(yg * lax.rsqrt(ms + EPS) * gout_ref[:, g * gw:(g + 1) * gw])
    y_ref[0] = jnp.concatenate(outs, axis=-1).astype(y_ref.dtype)


def _ssd(xbc, z, krdt, conv_w, conv_b, dtb, alog, dskip_w, g_out, lt):
    b, s, _ = xbc.shape
    gw = SSD_D_INNER // SSD_GROUPS
    full = lambda shape: pl.BlockSpec(shape, lambda bi, i: (0,) * len(shape))
    return pl.pallas_call(
        functools.partial(_ssd_kernel, lt=lt),
        grid=(b, s // lt),
        in_specs=[pl.BlockSpec((1, lt, SSD_CONV_DIM), lambda bi, i: (bi, i, 0)),
                  pl.BlockSpec((1, lt, SSD_D_INNER), lambda bi, i: (bi, i, 0)),
                  pl.BlockSpec((1, lt, LANES), lambda bi, i: (bi, i, 0)),
                  full((SSD_CONV, SSD_CONV_DIM)), full((1, SSD_CONV_DIM)),
                  full((1, LANES)), full((1, LANES)),
                  full((1, SSD_D_INNER)), full((1, SSD_D_INNER))],
        out_specs=pl.BlockSpec((1, lt, SSD_D_INNER), lambda bi, i: (bi, i, 0)),
        out_shape=jax.ShapeDtypeStruct((b, s, SSD_D_INNER), BF16),
        scratch_shapes=[pltpu.VMEM((16, SSD_CONV_DIM), F32),
                        pltpu.VMEM((SSD_GROUPS, SSD_STATE, gw), F32)],
        compiler_params=_cparams(("parallel", "arbitrary")),
        name="ssd_scan",
    )(xbc, z, krdt, conv_w, conv_b, dtb, alog, dskip_w, g_out)


def _swap_halves(t):
    return pltpu.roll(t, LANES // 2, 1)


def _lane_sums(sq, width):
    ones = jnp.ones((sq.shape[1], width), BF16)
    return jnp.dot(sq.astype(BF16), ones, preferred_element_type=F32)


def _qkv_kernel(ql_ref, kvl_ref, krdt_ref, cos_ref, sin_ref, gql_ref, wq_ref, gkvl_ref, wk_ref, wvt_ref,
                gq_ref, gk_ref, q_ref, k_ref, v_ref, *, scale):
    hw = 2 * LANES
    cosv = cos_ref[...]
    sinv = sin_ref[...]

    ql = ql_ref[...].astype(F32)
    rq = lax.rsqrt(_lane_sums(ql * ql, LANES) * (1.0 / MLA_Q_RANK) + EPS)
    qn = (ql * jnp.concatenate([rq] * (MLA_Q_RANK // LANES), axis=1) * gql_ref[...]).astype(BF16)

    kvl = kvl_ref[...].astype(F32)
    rkv = lax.rsqrt(_lane_sums(kvl * kvl, LANES) * (1.0 / MLA_KV_RANK) + EPS)
    kvn = (kvl * jnp.concatenate([rkv] * (MLA_KV_RANK // LANES), axis=1) * gkvl_ref[...]).astype(BF16)
    tm = kvn.shape[0]
    ones_rows = (lax.broadcasted_iota(I32, (V_ROWS - MLA_V, tm), 0) == 0).astype(v_ref.dtype)

    lane = lax.broadcasted_iota(I32, (1, LANES), 1)
    kr = jnp.where((lane % (LANES // 2)) < MLA_ROPE // 2, krdt_ref[...], 0.0)
    ss_r = _lane_sums(kr * kr, LANES)
    krg = kr * gk_ref[:, LANES:]
    kr_rot = krg * cosv + _swap_halves(krg) * sinv

    qf = jnp.dot(qn, wq_ref[...], preferred_element_type=F32)
    kf = jnp.dot(kvn, wk_ref[...], preferred_element_type=F32)
    vt = lax.dot_general(wvt_ref[...], kvn, (((1,), (1,)), ((), ())), preferred_element_type=F32)
    for h in range(MLA_HEADS):
        qh = qf[:, h * hw:(h + 1) * hw]
        r = lax.rsqrt(_lane_sums(qh * qh, hw) * (1.0 / MLA_QK_DIM) + EPS)
        qs = qh * (r * scale) * gq_ref[...]
        q_ref[0, h, :, 0:LANES] = qs[:, :LANES].astype(q_ref.dtype)
        qr = qs[:, LANES:]
        q_ref[0, h, :, LANES:hw] = (qr * cosv + _swap_halves(qr) * sinv).astype(q_ref.dtype)

        kn = kf[:, h * LANES:(h + 1) * LANES]
        rk = lax.rsqrt((_lane_sums(kn * kn, LANES) + ss_r) * (1.0 / MLA_QK_DIM) + EPS)
        k_ref[0, h, :, 0:LANES] = (kn * rk * gk_ref[:, :LANES]).astype(k_ref.dtype)
        k_ref[0, h, :, LANES:hw] = (kr_rot * rk).astype(k_ref.dtype)
        v_ref[0, h, 0:MLA_V, :] = vt[h * MLA_V:(h + 1) * MLA_V, :].astype(v_ref.dtype)
        v_ref[0, h, MLA_V:V_ROWS, :] = ones_rows


def _qkv_prep(ql, kvl, krdt, cos_t, sin_t, g_q_lat, wq_pad, g_kv_lat, wk, wvt, gq_pad, gk_pad, b, s, tm):
    t = b * s
    nst = s // tm
    full = lambda shape: pl.BlockSpec(shape, lambda bi, i: (0,) * len(shape))
    tok = lambda w: pl.BlockSpec((tm, w), lambda bi, i: (bi * nst + i, 0))
    hs = lambda w: pl.BlockSpec((1, MLA_HEADS, tm, w), lambda bi, i: (bi, 0, i, 0))
    return pl.pallas_call(
        functools.partial(_qkv_kernel, scale=MLA_QK_DIM ** -0.5 * math.log2(math.e)),
        grid=(b, nst),
        in_specs=[tok(MLA_Q_RANK), tok(MLA_KV_RANK), tok(LANES),
                  pl.BlockSpec((tm, LANES), lambda bi, i: (i, 0)),
                  pl.BlockSpec((tm, LANES), lambda bi, i: (i, 0)),
                  full((1, MLA_Q_RANK)), full(wq_pad.shape), full((1, MLA_KV_RANK)), full(wk.shape),
                  full(wvt.shape), full((1, 2 * LANES)), full((1, 2 * LANES))],
        out_specs=(hs(2 * LANES), hs(2 * LANES),
                   pl.BlockSpec((1, MLA_HEADS, V_ROWS, tm), lambda bi, i: (bi, 0, 0, i))),
        out_shape=(jax.ShapeDtypeStruct((b, MLA_HEADS, s, 2 * LANES), BF16),
                   jax.ShapeDtypeStruct((b, MLA_HEADS, s, 2 * LANES), BF16),
                   jax.ShapeDtypeStruct((b, MLA_HEADS, V_ROWS, s), BF16)),
        compiler_params=_cparams(("parallel", "parallel")),
        name="qkv_prep",
    )(ql, kvl, krdt, cos_t, sin_t, g_q_lat, wq_pad, g_kv_lat, wk, wvt, gq_pad, gk_pad)


def _flash_kernel(q_ref, k_ref, vt_ref, o_ref, m_scr, acc_scr, s_scr, *, tq):
    i = pl.program_id(2)
    tk = tq // 2
    q = q_ref[0, 0]
    m_scr[...] = jnp.full(m_scr.shape, NEG_BIG, F32)
    acc_scr[...] = jnp.zeros(acc_scr.shape, F32)

    def scores(j, slot, lo=0):
        start = pl.multiple_of(j * tk, tk)
        ks = k_ref[0, 0, pl.ds(start, tk), :]
        s_scr[slot, :, lo:] = lax.dot_general(ks, q[lo:, :], (((1,), (1,)), ((), ())),
                                              preferred_element_type=F32)

    def softmax_pv(j, slot, masked, lo=0):
        start = pl.multiple_of(j * tk, tk)
        vt = vt_ref[0, 0, :, pl.ds(start, tk)]
        st = s_scr[slot, :, lo:]
        if masked:
            kc = lax.broadcasted_iota(I32, (tk, tk), 0) // CHUNK
            qc = lax.broadcasted_iota(I32, (tk, tk), 1) // CHUNK
            diag = jnp.where(kc <= qc, st[:, :tk], NEG_BIG)
            st = diag if st.shape[1] == tk else jnp.concatenate([diag, st[:, tk:]], axis=1)
        m_prev = m_scr[:, lo:]
        m_new = jnp.maximum(m_prev, jnp.max(st, axis=0, keepdims=True))
        alpha = jnp.exp2(m_prev - m_new)
        pt = jnp.exp2(st - m_new)
        acc_scr[:, lo:] = alpha * acc_scr[:, lo:] + jnp.dot(vt, pt.astype(BF16), preferred_element_type=F32)
        m_scr[:, lo:] = m_new

    scores(0, 0)

    def body(jj, carry):
        j = 2 * jj
        scores(j + 1, 1)
        softmax_pv(j, 0, False)
        scores(j + 2, 0)
        softmax_pv(j + 1, 1, False)
        return carry

    lax.fori_loop(0, i, body, 0)
    scores(2 * i + 1, 1, lo=tk)
    softmax_pv(2 * i, 0, True)
    softmax_pv(2 * i + 1, 1, True, lo=tk)

    o_ref[0] = (acc_scr[0:MLA_V, :] / acc_scr[MLA_V:MLA_V + 1, :]).T.astype(o_ref.dtype)


def _flash(q, k, v, tq):
    b, nh, s, _ = q.shape
    return pl.pallas_call(
        functools.partial(_flash_kernel, tq=tq),
        grid=(b, nh, s // tq),
        in_specs=[pl.BlockSpec((1, 1, tq, q.shape[-1]), lambda bi, h, i: (bi, h, i, 0)),
                  pl.BlockSpec((1, 1, s, k.shape[-1]), lambda bi, h, i: (bi, h, 0, 0)),
                  pl.BlockSpec((1, 1, V_ROWS, s), lambda bi, h, i: (bi, h, 0, 0))],
        out_specs=pl.BlockSpec((1, tq, MLA_V), lambda bi, h, i: (bi, i, h)),
        out_shape=jax.ShapeDtypeStruct((b, s, nh * MLA_V), BF16),
        scratch_shapes=[pltpu.VMEM((1, tq), F32), pltpu.VMEM((V_ROWS, tq), F32),
                        pltpu.VMEM((2, tq // 2, tq), F32)],
        compiler_params=_cparams(("parallel", "parallel", "arbitrary")),
        name="flash_attn",
    )(q, k, v)


def _merge_kernel(x_ref, ys_ref, om_ref, gate_ref, wss_ref, wml_ref, wo_ref, gffn_ref, wrt_ref, brt_ref,
                  x1_ref, h2_ref, idx_ref, gates_ref, rank_ref, cnt_ref, *, tm):
    d = x_ref.shape[-1]
    step = pl.program_id(0)

    @pl.when(step == 0)
    def _():
        cnt_ref[...] = jnp.zeros_like(cnt_ref)

    y1 = jnp.dot(ys_ref[...], wss_ref[...], preferred_element_type=F32)
    y2 = jnp.dot(om_ref[...], wml_ref[...], preferred_element_type=F32)
    g = gate_ref[...].astype(F32)
    merged = (g[:, :d] * y1 + g[:, d:] * y2).astype(BF16)
    x1 = x_ref[...] + jnp.dot(merged, wo_ref[...], preferred_element_type=F32)
    x1_ref[...] = x1
    ms = jnp.mean(x1 * x1, axis=-1, keepdims=True)
    h2 = x1 * lax.rsqrt(ms + EPS) * gffn_ref[...]
    h2_ref[...] = h2.astype(h2_ref.dtype)

    w_hi, w_lo = _bf16_pieces(wrt_ref[...], 2)
    h_hi, h_lo = _bf16_pieces(h2, 2)
    nt = (((1,), (1,)), ((), ()))
    logits = (lax.dot_general(w_hi, h_hi, nt, preferred_element_type=F32)
              + lax.dot_general(w_hi, h_lo, nt, preferred_element_type=F32)
              + lax.dot_general(w_lo, h_hi, nt, preferred_element_type=F32)) + brt_ref[...]
    eid = lax.broadcasted_iota(I32, (N_EXPERTS, tm), 0)
    cur = logits
    onehot = jnp.zeros((N_EXPERTS, tm), F32)
    vals, sels = [], []
    for k in range(TOP_K):
        mx = jnp.max(cur, axis=0, keepdims=True)
        idx = jnp.min(jnp.where(cur == mx, eid, N_EXPERTS), axis=0, keepdims=True)
        sel = eid == idx
        vals.append(mx)
        sels.append(sel)
        idx_ref[k:k + 1, :] = idx
        cur = jnp.where(sel, -jnp.inf, cur)
        onehot = onehot + sel.astype(F32)
    es = [jnp.exp(vk - vals[0]) for vk in vals]
    den = es[0] + es[1] + es[2] + es[3]
    for k in range(TOP_K):
        gates_ref[k:k + 1, :] = es[k] / den

    r = lax.broadcasted_iota(I32, (tm, tm), 0)
    c = lax.broadcasted_iota(I32, (tm, tm), 1)
    before = ((r < c) & ((r // MOE_SUB) == (c // MOE_SUB))).astype(BF16)
    prefix = jnp.dot(onehot.astype(BF16), before, preferred_element_type=F32)
    for k in range(TOP_K):
        rank_ref[k:k + 1, :] = jnp.sum(jnp.where(sels[k], prefix, 0.0), axis=0, keepdims=True).astype(I32)
    lane = lax.broadcasted_iota(I32, (N_EXPERTS, LANES), 1)
    cnt = cnt_ref[...]
    for g in range(tm // MOE_SUB):
        c_g = jnp.sum(onehot[:, g * MOE_SUB:(g + 1) * MOE_SUB], axis=1, keepdims=True)
        cnt = jnp.where(lane == step * (tm // MOE_SUB) + g, c_g, cnt)
    cnt_ref[...] = cnt


def _merge(x2, y_ssd, o_mla, gate, w_ss, w_ml, w_o, g_ffn, w_rt, b_rt, tm):
    t, d = x2.shape
    full = lambda shape: pl.BlockSpec(shape, lambda i: (0,) * len(shape))
    tok = lambda w: pl.BlockSpec((tm, w), lambda i: (i, 0))
    sel = pl.BlockSpec((TOP_K, tm), lambda i: (0, i))
    return pl.pallas_call(
        functools.partial(_merge_kernel, tm=tm),
        grid=(t // tm,),
        in_specs=[tok(d), tok(d), tok(d), tok(2 * d), full((d, d)), full((d, d)), full((d, d)),
                  full((1, d)), full((N_EXPERTS, d)), full((N_EXPERTS, 1))],
        out_specs=(tok(d), tok(d), sel, sel, sel, full((N_EXPERTS, LANES))),
        out_shape=(jax.ShapeDtypeStruct((t, d), F32), jax.ShapeDtypeStruct((t, d), BF16),
                   jax.ShapeDtypeStruct((TOP_K, t), I32), jax.ShapeDtypeStruct((TOP_K, t), F32),
                   jax.ShapeDtypeStruct((TOP_K, t), I32), jax.ShapeDtypeStruct((N_EXPERTS, LANES), F32)),
        compiler_params=_cparams(("arbitrary",)),
        name="merge_route",
    )(x2, y_ssd, o_mla, gate, w_ss, w_ml, w_o, g_ffn, w_rt, b_rt)


def _excl_cumsum_rows(v):
    eid = lax.broadcasted_iota(I32, v.shape, 0)
    out = jnp.zeros(v.shape, F32)
    for e in range(N_EXPERTS - 1):
        out = out + jnp.where(eid > e, v[e:e + 1, :], 0.0)
    return out


def _tables_kernel(cnt_ref, idx_ref, rank_ref, slot_ref, seg_ref, loff_ref, gdst_ref, tail_ref):
    t = idx_ref.shape[1]
    cnt = cnt_ref[...]
    seg = jnp.ceil(cnt * (1.0 / SEG_ALIGN)) * SEG_ALIGN
    loff = _excl_cumsum_rows(seg)
    r = lax.broadcasted_iota(I32, (LANES, LANES), 0)
    c = lax.broadcasted_iota(I32, (LANES, LANES), 1)
    run = jnp.dot((seg * (1.0 / SEG_ALIGN)).astype(BF16), (r < c).astype(BF16),
                  preferred_element_type=F32) * SEG_ALIGN
    tot = jnp.sum(seg, axis=1, keepdims=True)
    padded = jnp.broadcast_to(jnp.ceil(tot * (1.0 / MOE_BLOCK)) * MOE_BLOCK, (N_EXPERTS, LANES))
    start = _excl_cumsum_rows(padded)
    end = start + padded
    seg_ref[...] = seg.astype(I32)
    loff_ref[...] = loff.astype(I32)
    gdst_ref[...] = (start + run).astype(I32)

    gr = lax.broadcasted_iota(I32, (LANES, t), 0)
    gc = lax.broadcasted_iota(I32, (LANES, t), 1) // MOE_SUB
    loff_tok = jnp.dot((loff * (1.0 / SEG_ALIGN)).astype(BF16), (gr == gc).astype(BF16),
                       preferred_element_type=F32) * SEG_ALIGN
    idx = idx_ref[...]
    slot = rank_ref[...]
    for e in range(N_EXPERTS):
        slot = slot + jnp.where(idx == e, loff_tok[e:e + 1, :].astype(I32), 0)
    slot_ref[...] = slot

    lane = lax.broadcasted_iota(I32, (N_EXPERTS, LANES), 1)
    total = jnp.max(end, axis=0, keepdims=True)
    tail = jnp.where(lane == 0, start + tot, jnp.where(lane == 1, end, jnp.where(lane == 3, start, total)))
    tail_ref[...] = tail.astype(I32)


def _tables(cnt, idx_t, rank_t):
    t = idx_t.shape[1]
    tab = jax.ShapeDtypeStruct((N_EXPERTS, LANES), I32)
    return pl.pallas_call(
        _tables_kernel,
        out_shape=(jax.ShapeDtypeStruct((TOP_K, t), I32), tab, tab, tab, tab),
        compiler_params=pltpu.CompilerParams(vmem_limit_bytes=VMEM_LIMIT),
        name="route_tables",
    )(cnt, idx_t, rank_t)


def _pow2_sizes(limit):
    size = SEG_ALIGN
    while size * 2 <= limit:
        size *= 2
    sizes = []
    while size >= SEG_ALIGN:
        sizes.append(size)
        size //= 2
    return tuple(sizes)


def _piece_copies(src_ref, src_off, dst_ref, dst_off, n, limit, sem, wait=False):
    off = 0
    for size in _pow2_sizes(limit):
        take = n & size

        @pl.when(take != 0)
        def _(off=off, size=size):
            cp = pltpu.make_async_copy(
                src_ref.at[pl.ds(pl.multiple_of(src_off + off, SEG_ALIGN), size), :],
                dst_ref.at[pl.ds(pl.multiple_of(dst_off + off, SEG_ALIGN), size), :], sem)
            if wait:
                cp.wait()
            else:
                cp.start()

        off = off + take


def _dispatch_kernel(seg_ref, loff_ref, gdst_ref, taillo_ref, tailhi_ref, used_ref,
                     h_ref, slot_ref, xs_ref, loc_scr, zero_scr, sem, zsem, *, nblk):
    i = pl.program_id(0)
    par = i % 2

    @pl.when(i == 0)
    def _():
        nblk_all = xs_ref.shape[0] // MOE_BLOCK
        zero_scr[...] = jnp.zeros_like(zero_scr)

        def blk_copy(b):
            start = pl.multiple_of(b * MOE_BLOCK, MOE_BLOCK)
            return pltpu.make_async_copy(zero_scr, xs_ref.at[pl.ds(start, MOE_BLOCK), :], zsem)

        def issue_blk(b, carry):
            blk_copy(b).start()
            return carry

        def drain_blk(b, carry):
            blk_copy(b).wait()
            return carry

        lax.fori_loop(used_ref[0], nblk_all, issue_blk, 0)
        lax.fori_loop(used_ref[0], nblk_all, drain_blk, 0)

        def tails(wait):
            def body(e, carry):
                lo = taillo_ref[e]
                _piece_copies(zero_scr, 0, xs_ref, lo, tailhi_ref[e] - lo, MOE_BLOCK - 1, zsem, wait=wait)
                return carry
            return body

        lax.fori_loop(0, N_EXPERTS, tails(False), 0)
        lax.fori_loop(0, N_EXPERTS, tails(True), 0)

    slots = slot_ref[...]
    rid = lax.broadcasted_iota(I32, (LOCAL_SLOTS, MOE_SUB), 0)
    hit = rid == slots[0:1, :]
    for k in range(1, TOP_K):
        hit = hit | (rid == slots[k:k + 1, :])
    loc_scr[par] = jnp.dot(hit.astype(BF16), h_ref[...], preferred_element_type=F32).astype(ROW_DTYPE)

    loc = loc_scr.at[par]

    def seg_body(e, total):
        n = seg_ref[e * LANES + i]
        _piece_copies(loc, loff_ref[e * LANES + i], xs_ref, gdst_ref[e * LANES + i], n, MOE_SUB, sem.at[par])
        return total + n

    total = lax.fori_loop(0, N_EXPERTS, seg_body, 0)
    spare = (nblk + par * (LOCAL_SLOTS // MOE_BLOCK)) * MOE_BLOCK
    _piece_copies(loc, total, xs_ref, spare + total, LOCAL_SLOTS - total, LOCAL_SLOTS, sem.at[par])

    def wait_step(p):
        pltpu.make_async_copy(loc_scr.at[p], xs_ref.at[pl.ds(0, LOCAL_SLOTS), :], sem.at[p]).wait()

    @pl.when(i > 0)
    def _():
        wait_step(1 - par)

    @pl.when(i == pl.num_programs(0) - 1)
    def _():
        wait_step(par)


def _dispatch(seg, loff, gdst, tail_lo, tail_hi, used, h2, slot_t, nblk):
    t, d = h2.shape
    rows = (nblk + 2 * (LOCAL_SLOTS // MOE_BLOCK)) * MOE_BLOCK
    return pl.pallas_call(
        functools.partial(_dispatch_kernel, nblk=nblk),
        grid_spec=pltpu.PrefetchScalarGridSpec(
            num_scalar_prefetch=6,
            grid=(t // MOE_SUB,),
            in_specs=[pl.BlockSpec((MOE_SUB, d), lambda i, *_: (i, 0)),
                      pl.BlockSpec((TOP_K, MOE_SUB), lambda i, *_: (0, i))],
            out_specs=pl.BlockSpec(memory_space=pl.ANY),
            scratch_shapes=[pltpu.VMEM((2, LOCAL_SLOTS, d), ROW_DTYPE), pltpu.VMEM((MOE_BLOCK, d), ROW_DTYPE),
                            pltpu.SemaphoreType.DMA((2,)), pltpu.SemaphoreType.DMA(())]),
        out_shape=jax.ShapeDtypeStruct((rows, d), ROW_DTYPE),
        compiler_params=_cparams(("arbitrary",)),
        name="moe_dispatch",
    )(seg, loff, gdst, tail_lo, tail_hi, used, h2, slot_t)


def _expert_kernel(lo_ref, hi_ref, used_ref, x_ref, wgu_ref, bgu_ref, wd_ref, bd_ref, y_ref,
                   wgu_scr, wd_scr, xbuf, ybuf, xsem, ysem):
    e = pl.program_id(0)
    ne = pl.num_programs(0)
    ff = wd_ref.shape[1]
    big = 2 * MOE_BLOCK

    def layout(ex):
        first = lo_ref[ex] // MOE_BLOCK
        n = (hi_ref[ex] - lo_ref[ex]) // MOE_BLOCK
        return first, n // 2, n // 2 + n % 2

    cur = layout(e)
    first, npair, items = cur

    def x_copy(lay, k, rows):
        start = pl.multiple_of((lay[0] + 2 * k) * MOE_BLOCK, MOE_BLOCK)
        return pltpu.make_async_copy(x_ref.at[pl.ds(start, rows), :], xbuf.at[k % 2, pl.ds(0, rows), :],
                                     xsem.at[k % 2])

    def y_copy(k, rows):
        start = pl.multiple_of((first + 2 * k) * MOE_BLOCK, MOE_BLOCK)
        return pltpu.make_async_copy(ybuf.at[k % 2, pl.ds(0, rows), :], y_ref.at[pl.ds(start, rows), :],
                                     ysem.at[k % 2])

    def start_x(lay, k):
        @pl.when(k < lay[1])
        def _():
            x_copy(lay, k, big).start()

        @pl.when((k == lay[1]) & (k < lay[2]))
        def _():
            x_copy(lay, k, MOE_BLOCK).start()

    def compute(k, rows):
        slot = k % 2
        xb = xbuf[slot, 0:rows, :].astype(BF16)
        gu = jnp.dot(xb, wgu_scr[...], preferred_element_type=F32) + bgu_ref[0]
        glu = jnp.minimum(gu[:, :ff], SWIGLU_LIMIT)
        lin = jnp.clip(gu[:, ff:], -SWIGLU_LIMIT, SWIGLU_LIMIT)
        act = glu * _sigmoid(SWIGLU_ALPHA * glu) * (lin + 1.0)
        y = jnp.dot(act.astype(BF16), wd_scr[...], preferred_element_type=F32) + bd_ref[0]
        ybuf[slot, 0:rows, :] = y.astype(ybuf.dtype)

    def item(k, rows):
        x_copy(cur, k, rows).wait()

        @pl.when(k >= 1)
        def _():
            start_x(cur, k + 1)

        @pl.when(k >= 2)
        def _():
            y_copy(k - 2, big).wait()

        compute(k, rows)
        y_copy(k, rows).start()

    @pl.when(e == 0)
    def _():
        start_x(cur, 0)
        start_x(cur, 1)

    @pl.when(items > 0)
    def _():
        wgu_scr[...] = wgu_ref[0].astype(BF16)
        wd_scr[...] = wd_ref[0].astype(BF16)

    def body(k, carry):
        item(k, big)
        return carry

    lax.fori_loop(0, npair, body, 0)

    @pl.when(items > npair)
    def _():
        item(npair, MOE_BLOCK)

    @pl.when(items >= 2)
    def _():
        y_copy(items - 2, big).wait()

    @pl.when(items > npair)
    def _():
        y_copy(items - 1, MOE_BLOCK).wait()

    @pl.when((items == npair) & (items >= 1))
    def _():
        y_copy(items - 1, big).wait()

    @pl.when(e + 1 < ne)
    def _():
        nxt = layout(jnp.minimum(e + 1, ne - 1))
        start_x(nxt, 0)
        start_x(nxt, 1)

    @pl.when(e == pl.num_programs(0) - 1)
    def _():
        nblk_all = y_ref.shape[0] // MOE_BLOCK
        ybuf[0] = jnp.zeros(ybuf.shape[1:], ybuf.dtype)

        def zero_copy(b):
            start = pl.multiple_of(b * MOE_BLOCK, MOE_BLOCK)
            return pltpu.make_async_copy(ybuf.at[0, pl.ds(0, MOE_BLOCK), :],
                                         y_ref.at[pl.ds(start, MOE_BLOCK), :], ysem.at[0])

        def issue(b, carry):
            zero_copy(b).start()
            return carry

        def drain(b, carry):
            zero_copy(b).wait()
            return carry

        lax.fori_loop(used_ref[0], nblk_all, issue, 0)
        lax.fori_loop(used_ref[0], nblk_all, drain, 0)


def _experts(row_lo, row_hi, used, xs, w_gate_up, b_gate_up, w_down, b_down):
    rows, d = xs.shape
    ne, _, ff2 = w_gate_up.shape
    ff = ff2 // 2
    return pl.pallas_call(
        _expert_kernel,
        grid_spec=pltpu.PrefetchScalarGridSpec(
            num_scalar_prefetch=3,
            grid=(ne,),
            in_specs=[pl.BlockSpec(memory_space=pl.ANY),
                      pl.BlockSpec((1, d, ff2), lambda e, *_: (e, 0, 0)),
                      pl.BlockSpec((1, 1, ff2), lambda e, *_: (e, 0, 0)),
                      pl.BlockSpec((1, ff, d), lambda e, *_: (e, 0, 0)),
                      pl.BlockSpec((1, 1, d), lambda e, *_: (e, 0, 0))],
            out_specs=pl.BlockSpec(memory_space=pl.ANY),
            scratch_shapes=[pltpu.VMEM((d, ff2), BF16), pltpu.VMEM((ff, d), BF16),
                            pltpu.VMEM((2, 2 * MOE_BLOCK, d), ROW_DTYPE),
                            pltpu.VMEM((2, 2 * MOE_BLOCK, d), ROW_DTYPE),
                            pltpu.SemaphoreType.DMA((2,)), pltpu.SemaphoreType.DMA((2,))]),
        out_shape=jax.ShapeDtypeStruct((rows, d), ROW_DTYPE),
        compiler_params=_cparams(("arbitrary",)),
        name="moe_experts",
    )(row_lo, row_hi, used, xs, w_gate_up, b_gate_up.reshape(ne, 1, ff2), w_down, b_down.reshape(ne, 1, d))


def _combine_kernel(seg_ref, loff_ref, gdst_ref, x1_ref, slot_ref, gates_ref, ys_ref, o_ref, loc_scr, sem):
    i = pl.program_id(0)
    par = i % 2

    def fetch(g, p):
        loc = loc_scr.at[p]

        def seg_body(e, total):
            n = seg_ref[e * LANES + g]
            _piece_copies(ys_ref, gdst_ref[e * LANES + g], loc, loff_ref[e * LANES + g], n, MOE_SUB, sem.at[p])
            return total + n

        total = lax.fori_loop(0, N_EXPERTS, seg_body, 0)
        _piece_copies(ys_ref, total, loc, total, LOCAL_SLOTS - total, LOCAL_SLOTS, sem.at[p])

    @pl.when(i == 0)
    def _():
        fetch(0, 0)

    @pl.when(i + 1 < pl.num_programs(0))
    def _():
        fetch(i + 1, 1 - par)

    pltpu.make_async_copy(ys_ref.at[pl.ds(0, LOCAL_SLOTS), :], loc_scr.at[par], sem.at[par]).wait()

    slots = slot_ref[...]
    g = gates_ref[...]
    rid = lax.broadcasted_iota(I32, (MOE_SUB, LOCAL_SLOTS), 1)
    gmat = jnp.where(rid == slots[:, 0:1], g[:, 0:1], 0.0)
    for k in range(1, TOP_K):
        gmat = gmat + jnp.where(rid == slots[:, k:k + 1], g[:, k:k + 1], 0.0)
    o_ref[...] = x1_ref[...] + jnp.dot(gmat.astype(BF16), loc_scr[par].astype(BF16),
                                       preferred_element_type=F32)


def _combine(seg, loff, gdst, x1, slot_tk, gates_tk, ys):
    t, d = x1.shape
    tok = lambda w: pl.BlockSpec((MOE_SUB, w), lambda i, *_: (i, 0))
    return pl.pallas_call(
        _combine_kernel,
        grid_spec=pltpu.PrefetchScalarGridSpec(
            num_scalar_prefetch=3,
            grid=(t // MOE_SUB,),
            in_specs=[tok(d), tok(TOP_K), tok(TOP_K), pl.BlockSpec(memory_space=pl.ANY)],
            out_specs=tok(d),
            scratch_shapes=[pltpu.VMEM((2, LOCAL_SLOTS, d), ROW_DTYPE), pltpu.SemaphoreType.DMA((2,))]),
        out_shape=jax.ShapeDtypeStruct((t, d), F32),
        compiler_params=_cparams(("arbitrary",)),
        name="moe_combine",
    )(seg, loff, gdst, x1, slot_tk, gates_tk, ys)


def _rope_tables(s):
    pos = np.arange(s, dtype=np.float64)
    inv = ROPE_BASE ** (-np.arange(0, MLA_ROPE, 2, dtype=np.float64) / MLA_ROPE)
    ang = pos[:, None] * inv[None, :]
    cos, sin = jnp.asarray(np.cos(ang), F32), jnp.asarray(np.sin(ang), F32)
    return _spread_rope(jnp.concatenate([cos, cos], axis=-1)), _spread_rope(jnp.concatenate([-sin, sin], axis=-1))


def _spread_rope(w, gap=None):
    half = MLA_ROPE // 2
    zeros = jnp.zeros(w.shape[:-1] + (half,), w.dtype)
    return jnp.concatenate([w[..., :half], zeros if gap is None else gap, w[..., half:], zeros], axis=-1)


def _pad_heads(w):
    lead = w.shape[:-1]
    w = w.reshape(lead + (MLA_HEADS, MLA_QK_DIM))
    w = jnp.concatenate([w[..., :MLA_NOPE], _spread_rope(w[..., MLA_NOPE:])], axis=-1)
    return w.reshape(lead + (MLA_HEADS * 2 * LANES,))


def _layer(x, g_mix, w_in, conv_w, conv_b, dt_bias, a_log, d_skip, g_ssd_out, w_ssd_out,
           g_q_lat, w_q_up, g_kv_lat, w_kv_up, g_qk_q, g_qk_k, w_mla_out, w_o,
           g_ffn, w_router, b_router, w_gate_up, b_gate_up, w_down, b_down):
    b, s, d = x.shape
    t = b * s
    x2 = x.reshape(t, d)

    off_xbc = SSD_D_INNER
    off_dt = off_xbc + SSD_CONV_DIM
    off_ql = off_dt + SSD_HEADS
    off_kvl = off_ql + MLA_Q_RANK
    off_kr = off_kvl + MLA_KV_RANK
    off_gate = off_kr + MLA_ROPE
    dt_gap = jnp.pad(w_in[:, off_dt:off_ql], ((0, 0), (0, MLA_ROPE // 2 - SSD_HEADS)))
    w_cat = jnp.concatenate([
        w_in[:, :off_dt], w_in[:, off_ql:off_kr],
        _spread_rope(w_in[:, off_kr:off_gate], gap=dt_gap),
        w_in[:, off_gate:]], axis=1).astype(BF16)
    widths = (SSD_D_INNER, SSD_CONV_DIM, MLA_Q_RANK, MLA_KV_RANK, LANES, 2 * d)

    tm = min(512, s)
    z, xbc, ql, kvl, krdt, gate = _in_proj(x2, g_mix, w_cat, widths, tm)

    lane_pad = lambda vec: jnp.pad(vec, (DT_LO, LANES - DT_LO - SSD_HEADS)).reshape(1, LANES)
    lt = min(256, s)
    y_ssd = _ssd(xbc.reshape(b, s, SSD_CONV_DIM), z.reshape(b, s, SSD_D_INNER), krdt.reshape(b, s, LANES),
                 conv_w, conv_b.reshape(1, -1), lane_pad(dt_bias), lane_pad(a_log),
                 jnp.repeat(d_skip, SSD_HEAD_DIM).reshape(1, -1), g_ssd_out.reshape(1, -1), lt)

    cos_t, sin_t = _rope_tables(s)
    wq_pad = _pad_heads(w_q_up).astype(BF16)
    spread_gain = lambda g: jnp.concatenate([g[:MLA_NOPE], _spread_rope(g[MLA_NOPE:])]).reshape(1, -1)
    gq_pad = spread_gain(g_qk_q)
    gk_pad = spread_gain(g_qk_k)
    tq = min(512, s)
    w_kv_h = w_kv_up.reshape(MLA_KV_RANK, MLA_HEADS, MLA_NOPE + MLA_V)
    wk = w_kv_h[:, :, :MLA_NOPE].reshape(MLA_KV_RANK, MLA_HEADS * MLA_NOPE).astype(BF16)
    wvt = w_kv_h[:, :, MLA_NOPE:].reshape(MLA_KV_RANK, MLA_HEADS * MLA_V).T.astype(BF16)
    q, k, v = _qkv_prep(ql, kvl, krdt, cos_t, sin_t, g_q_lat.reshape(1, -1), wq_pad,
                        g_kv_lat.reshape(1, -1), wk, wvt, gq_pad, gk_pad, b, s, tq)
    o_mla = _flash(q, k, v, min(1024, s))

    x1, h2, idx_t, gates_t, rank_t, cnt = _merge(
        x2, y_ssd.reshape(t, d), o_mla.reshape(t, d), gate,
        w_ssd_out.astype(BF16), w_mla_out.astype(BF16), w_o.astype(BF16),
        g_ffn.reshape(1, -1), w_router.T, b_router.reshape(-1, 1), min(512, t))

    nsub = t // MOE_SUB
    assert t % MOE_SUB == 0 and nsub <= LANES
    cap = t * TOP_K + nsub * N_EXPERTS * (SEG_ALIGN - 1) + N_EXPERTS * (MOE_BLOCK - 1)
    nblk = -(-cap // MOE_BLOCK)
    slot_t, seg, loff, gdst, tail = _tables(cnt, idx_t, rank_t)
    used = (tail[0:1, 2] // MOE_BLOCK).astype(I32)
    seg, loff, gdst = seg.reshape(-1), loff.reshape(-1), gdst.reshape(-1)
    xs = _dispatch(seg, loff, gdst, tail[:, 0], tail[:, 1], used, h2, slot_t, nblk)
    ys = _experts(tail[:, 3], tail[:, 1], used, xs, w_gate_up, b_gate_up, w_down, b_down)
    out = _combine(seg, loff, gdst, x1, slot_t.T, gates_t.T, ys)
    return out.reshape(b, s, d)


def kernel(x, g_mix, w_in, conv_w, conv_b, dt_bias, a_log, d_skip, g_ssd_out, w_ssd_out, g_q_lat, w_q_up, g_kv_lat, w_kv_up, g_qk_q, g_qk_k, w_mla_out, w_o, g_ffn, w_router, b_router, w_gate_up, b_gate_up, w_down, b_down):
    params = (g_mix, w_in, conv_w, conv_b, dt_bias, a_log, d_skip, g_ssd_out, w_ssd_out, g_q_lat, w_q_up,
              g_kv_lat, w_kv_up, g_qk_q, g_qk_k, w_mla_out, w_o, g_ffn, w_router, b_router,
              w_gate_up, b_gate_up, w_down, b_down)
    for l in range(g_mix.shape[0]):
        x = _layer(x, *(p[l] for p in params))
    return x
```

```python
import functools
import math
from typing import NamedTuple

import jax
import jax.numpy as jnp
import numpy as np
from jax import lax
from jax.experimental import pallas as pl
from jax.experimental.pallas import tpu as pltpu

F32 = jnp.float32
BF16 = jnp.bfloat16
I32 = jnp.int32

EPS = 1e-6
CHUNK = 64

SSD_HEADS = 16
SSD_HEAD_DIM = 64
SSD_GROUPS = 2
SSD_STATE = 128
SSD_CONV = 4
SSD_D_INNER = SSD_HEADS * SSD_HEAD_DIM
SSD_CONV_DIM = SSD_D_INNER + 2 * SSD_GROUPS * SSD_STATE

MLA_HEADS = 8
MLA_Q_RANK = 384
MLA_KV_RANK = 256
MLA_NOPE = 128
MLA_ROPE = 64
MLA_QK_DIM = MLA_NOPE + MLA_ROPE
MLA_V = 128
V_ROWS = MLA_V + 16
DT_LO = MLA_ROPE // 2
SSD_SCAN_CHUNK = 128
ROPE_BASE = 10000.0

N_EXPERTS = 32
TOP_K = 4
SWIGLU_ALPHA = 1.702
SWIGLU_LIMIT = 7.0

LANES = 128
VMEM_LIMIT = 56 * 1024 * 1024
NEG_BIG = -1e30
LOG2E = math.log2(math.e)

MOE_BLOCK = 256
MOE_SUB = 256
ROW_DTYPE = F32
SEG_ALIGN = 8 * 4 // jnp.dtype(ROW_DTYPE).itemsize
LOCAL_SLOTS = -(-(TOP_K * MOE_SUB + N_EXPERTS * (SEG_ALIGN - 1)) // MOE_BLOCK) * MOE_BLOCK


class _Tiles(NamedTuple):
    tokens: int
    ssd: int
    queries: int


def _tile_sizes(seq):
    return _Tiles(tokens=min(512, seq), ssd=min(256, seq), queries=min(1024, seq))


def _cparams(semantics, **kw):
    return pltpu.CompilerParams(dimension_semantics=semantics,
                                vmem_limit_bytes=VMEM_LIMIT, **kw)


def _sigmoid(v):
    return 1.0 / (1.0 + jnp.exp(-v))


def _silu(v):
    return v * _sigmoid(v)


def _bf16_pieces(t, n):
    pieces = []
    for _ in range(n - 1):
        p = t.astype(BF16)
        pieces.append(p)
        t = t - p.astype(F32)
    pieces.append(t.astype(BF16))
    return pieces


def _inproj_kernel(x_ref, g_ref, wz_ref, wxbc_ref, wql_ref, wkvl_ref, wkrdt_ref, wgate_ref,
                   z_ref, xbc_ref, ql_ref, kvl_ref, krdt_ref, gate_ref):
    x = x_ref[...]
    ms = jnp.mean(x * x, axis=-1, keepdims=True)
    h = (x * lax.rsqrt(ms + EPS) * g_ref[...]).astype(BF16)
    streams = ((wz_ref, z_ref), (wxbc_ref, xbc_ref), (wql_ref, ql_ref), (wkvl_ref, kvl_ref),
               (wkrdt_ref, krdt_ref), (wgate_ref, gate_ref))
    for w_ref, ref in streams:
        p = jnp.dot(h, w_ref[...], preferred_element_type=F32)
        if ref is gate_ref:
            p = _sigmoid(p)
        ref[...] = p.astype(ref.dtype)


def _in_proj(x2, g_mix, weights, tm):
    t, d = x2.shape
    widths = tuple(w.shape[1] for w in weights)
    dts = (BF16, BF16, BF16, BF16, F32, BF16)
    out_shape = tuple(jax.ShapeDtypeStruct((t, w), dt) for w, dt in zip(widths, dts))
    out_specs = tuple(pl.BlockSpec((tm, w), lambda i: (i, 0)) for w in widths)
    return pl.pallas_call(
        _inproj_kernel,
        grid=(t // tm,),
        in_specs=[pl.BlockSpec((tm, d), lambda i: (i, 0)),
                  pl.BlockSpec((1, d), lambda i: (0, 0))]
                 + [pl.BlockSpec((d, w), lambda i: (0, 0)) for w in widths],
        out_specs=out_specs,
        out_shape=out_shape,
        compiler_params=_cparams(("parallel",)),
        name="in_proj",
    )(x2, g_mix.reshape(1, d), *weights)


def _ssd_kernel(xbc_ref, z_ref, krdt_ref, convw_ref, convb_ref, dtb_ref, alog_ref, dskip_ref, gout_ref,
                y_ref, xext_scr, state_scr, *, lt):
    i = pl.program_id(1)
    halo = 8
    dt_lo = DT_LO
    gw = SSD_D_INNER // SSD_GROUPS
    hpg = SSD_HEADS // SSD_GROUPS

    @pl.when(i == 0)
    def _():
        state_scr[...] = jnp.zeros_like(state_scr)
        xext_scr[0:halo, :] = jnp.zeros((halo, SSD_CONV_DIM), F32)

    xin = xbc_ref[0]
    xext_scr[halo:2 * halo, :] = xin[0:halo, :].astype(F32)
    srow = lax.broadcasted_iota(I32, (lt, lt), 0)
    scol = lax.broadcasted_iota(I32, (lt, lt), 1)
    acc = convb_ref[...] + convw_ref[SSD_CONV - 1:SSD_CONV, :] * xin.astype(F32)
    head = jnp.broadcast_to(convb_ref[...], (halo, SSD_CONV_DIM))
    for j in range(SSD_CONV):
        shift = SSD_CONV - 1 - j
        head = head + convw_ref[j:j + 1, :] * xext_scr[pl.ds(halo - shift, halo), :]
        if shift:
            shifted = jnp.dot((srow == scol + shift).astype(BF16), xin, preferred_element_type=F32)
            acc = acc + convw_ref[j:j + 1, :] * shifted
    acc = jnp.concatenate([head, acc[halo:, :]], axis=0)
    xext_scr[0:halo, :] = xin[lt - halo:lt, :].astype(F32)
    xbc = _silu(acc)
    xs = xbc[:, :SSD_D_INNER]
    bm = xbc[:, SSD_D_INNER:SSD_D_INNER + SSD_GROUPS * SSD_STATE]
    cm = xbc[:, SSD_D_INNER + SSD_GROUPS * SSD_STATE:]

    lane = lax.broadcasted_iota(I32, (1, LANES), 1)
    head_lane = (lane >= dt_lo) & (lane < dt_lo + SSD_HEADS)
    v = krdt_ref[0] + dtb_ref[...]
    dt = jnp.maximum(v, 0.0) + jnp.log(1.0 + jnp.exp(-jnp.abs(v)))
    dt = jnp.where(head_lane, dt, 0.0)
    a = jnp.where(head_lane, -jnp.exp(alog_ref[...]) * LOG2E, 0.0)
    da = dt * a
    lc = min(SSD_SCAN_CHUNK, lt)
    nchunk = lt // lc
    row = lax.broadcasted_iota(I32, (lt, lt), 0)
    col = lax.broadcasted_iota(I32, (lt, lt), 1)
    tri_b = ((row >= col) & ((row // lc) == (col // lc))).astype(BF16)
    a_cum = jnp.zeros((lt, LANES), F32)
    for piece in _bf16_pieces(da, 3):
        a_cum = a_cum + jnp.dot(tri_b, piece, preferred_element_type=F32)
    a_cum_t = a_cum.T
    a_last = jnp.concatenate([jnp.broadcast_to(a_cum[(c + 1) * lc - 1:(c + 1) * lc, :], (lc, LANES))
                              for c in range(nchunk)], axis=0)
    exp_a = jnp.exp2(a_cum)
    dec = jnp.exp2(a_last - a_cum)
    tri = lax.broadcasted_iota(I32, (lc, lc), 0) >= lax.broadcasted_iota(I32, (lc, lc), 1)

    er = lax.broadcasted_iota(I32, (LANES, SSD_D_INNER), 0)
    ec = lax.broadcasted_iota(I32, (LANES, SSD_D_INNER), 1)
    expand = ((er - dt_lo) == (ec // SSD_HEAD_DIM)).astype(BF16)

    def widen(t, pieces):
        out = jnp.zeros((lt, SSD_D_INNER), F32)
        for piece in _bf16_pieces(t, pieces):
            out = out + jnp.dot(piece, expand, preferred_element_type=F32)
        return out

    dt_w = widen(dt, 2)
    exp_a_w = widen(exp_a, 1)
    dec_w = widen(dec, 1)
    xdt = xs * dt_w
    xdt_b = xdt.astype(BF16)
    xdec_b = (xdt * dec_w).astype(BF16)
    lane_pair = lax.broadcasted_iota(I32, (lc, LANES), 1)

    states = [state_scr[g] for g in range(SSD_GROUPS)]
    y_rows = [[] for _ in range(SSD_GROUPS)]
    for c in range(nchunk):
        rows = slice(c * lc, (c + 1) * lc)
        for g in range(SSD_GROUPS):
            cols = slice(g * gw, (g + 1) * gw)
            bg = bm[rows, g * SSD_STATE:(g + 1) * SSD_STATE].astype(BF16)
            cg = cm[rows, g * SSD_STATE:(g + 1) * SSD_STATE].astype(BF16)
            cb = lax.dot_general(cg, bg, (((1,), (1,)), ((), ())), preferred_element_type=F32)
            parts = []
            for pair in range(hpg // 2):
                h0 = g * hpg + 2 * pair
                rhs = xdt_b[rows, h0 * SSD_HEAD_DIM:(h0 + 2) * SSD_HEAD_DIM]
                res = []
                for h in (h0, h0 + 1):
                    seg = a_cum[rows, dt_lo + h:dt_lo + h + 1] - a_cum_t[dt_lo + h:dt_lo + h + 1, rows]
                    lmat = jnp.exp2(jnp.where(tri, seg, NEG_BIG))
                    res.append(jnp.dot((cb * lmat).astype(BF16), rhs, preferred_element_type=F32))
                parts.append(jnp.where(lane_pair < SSD_HEAD_DIM, res[0], res[1]))
            st = states[g]
            y_off = jnp.dot(cg, st.astype(BF16), preferred_element_type=F32)
            y_rows[g].append(jnp.concatenate(parts, axis=-1) + y_off * exp_a_w[rows, cols])
            new = lax.dot_general(bg, xdec_b[rows, cols], (((0,), (0,)), ((), ())),
                                  preferred_element_type=F32)
            states[g] = st * exp_a_w[(c + 1) * lc - 1:(c + 1) * lc, cols] + new
    for g in range(SSD_GROUPS):
        state_scr[g] = states[g]

    zs = _silu(z_ref[0].astype(F32))
    outs = []
    for g in range(SSD_GROUPS):
        yg = (jnp.concatenate(y_rows[g], axis=0)
              + dskip_ref[:, g * gw:(g + 1) * gw] * xs[:, g * gw:(g + 1) * gw])
        yg = yg * zs[:, g * gw:(g + 1) * gw]
        ms = jnp.mean(yg * yg, axis=-1, keepdims=True)
        outs.append(yg * lax.rsqrt(ms + EPS) * gout_ref[:, g * gw:(g + 1) * gw])
    y_ref[0] = jnp.concatenate(outs, axis=-1).astype(y_ref.dtype)


def _ssd(xbc, z, krdt, conv_w, conv_b, dtb, alog, dskip_w, g_out, lt):
    b, s, _ = xbc.shape
    gw = SSD_D_INNER // SSD_GROUPS
    full = lambda shape: pl.BlockSpec(shape, lambda bi, i: (0,) * len(shape))
    return pl.pallas_call(
        functools.partial(_ssd_kernel, lt=lt),
        grid=(b, s // lt),
        in_specs=[pl.BlockSpec((1, lt, SSD_CONV_DIM), lambda bi, i: (bi, i, 0)),
                  pl.BlockSpec((1, lt, SSD_D_INNER), lambda bi, i: (bi, i, 0)),
                  pl.BlockSpec((1, lt, LANES), lambda bi, i: (bi, i, 0)),
                  full((SSD_CONV, SSD_CONV_DIM)), full((1, SSD_CONV_DIM)),
                  full((1, LANES)), full((1, LANES)),
                  full((1, SSD_D_INNER)), full((1, SSD_D_INNER))],
        out_specs=pl.BlockSpec((1, lt, SSD_D_INNER), lambda bi, i: (bi, i, 0)),
        out_shape=jax.ShapeDtypeStruct((b, s, SSD_D_INNER), BF16),
        scratch_shapes=[pltpu.VMEM((16, SSD_CONV_DIM), F32),
                        pltpu.VMEM((SSD_GROUPS, SSD_STATE, gw), F32)],
        compiler_params=_cparams(("parallel", "arbitrary")),
        name="ssd_scan",
    )(xbc, z, krdt, conv_w, conv_b, dtb, alog, dskip_w, g_out)


def _swap_halves(t):
    return pltpu.roll(t, LANES // 2, 1)


def _lane_sums(sq, width):
    ones = jnp.ones((sq.shape[1], width), BF16)
    return jnp.dot(sq.astype(BF16), ones, preferred_element_type=F32)


def _qkv_kernel(ql_ref, kvl_ref, krdt_ref, cos_ref, sin_ref, gql_ref, wq_ref, gkvl_ref, wk_ref, wvt_ref,
                gq_ref, gk_ref, q_ref, k_ref, v_ref, *, scale):
    hw = 2 * LANES
    cosv = cos_ref[...]
    sinv = sin_ref[...]

    ql = ql_ref[...].astype(F32)
    rq = lax.rsqrt(_lane_sums(ql * ql, LANES) * (1.0 / MLA_Q_RANK) + EPS)
    qn = (ql * jnp.concatenate([rq] * (MLA_Q_RANK // LANES), axis=1) * gql_ref[...]).astype(BF16)

    kvl = kvl_ref[...].astype(F32)
    rkv = lax.rsqrt(_lane_sums(kvl * kvl, LANES) * (1.0 / MLA_KV_RANK) + EPS)
    kvn = (kvl * jnp.concatenate([rkv] * (MLA_KV_RANK // LANES), axis=1) * gkvl_ref[...]).astype(BF16)
    tm = kvn.shape[0]
    ones_rows = (lax.broadcasted_iota(I32, (V_ROWS - MLA_V, tm), 0) == 0).astype(v_ref.dtype)

    lane = lax.broadcasted_iota(I32, (1, LANES), 1)
    kr = jnp.where((lane % (LANES // 2)) < MLA_ROPE // 2, krdt_ref[...], 0.0)
    ss_r = _lane_sums(kr * kr, LANES)
    krg = kr * gk_ref[:, LANES:]
    kr_rot = krg * cosv + _swap_halves(krg) * sinv

    qf = jnp.dot(qn, wq_ref[...], preferred_element_type=F32)
    kf = jnp.dot(kvn, wk_ref[...], preferred_element_type=F32)
    vt = lax.dot_general(wvt_ref[...], kvn, (((1,), (1,)), ((), ())), preferred_element_type=F32)
    for h in range(MLA_HEADS):
        qh = qf[:, h * hw:(h + 1) * hw]
        r = lax.rsqrt(_lane_sums(qh * qh, hw) * (1.0 / MLA_QK_DIM) + EPS)
        qs = qh * (r * scale) * gq_ref[...]
        q_ref[0, h, :, 0:LANES] = qs[:, :LANES].astype(q_ref.dtype)
        qr = qs[:, LANES:]
        q_ref[0, h, :, LANES:hw] = (qr * cosv + _swap_halves(qr) * sinv).astype(q_ref.dtype)

        kn = kf[:, h * LANES:(h + 1) * LANES]
        rk = lax.rsqrt((_lane_sums(kn * kn, LANES) + ss_r) * (1.0 / MLA_QK_DIM) + EPS)
        k_ref[0, h, :, 0:LANES] = (kn * rk * gk_ref[:, :LANES]).astype(k_ref.dtype)
        k_ref[0, h, :, LANES:hw] = (kr_rot * rk).astype(k_ref.dtype)
        v_ref[0, h, 0:MLA_V, :] = vt[h * MLA_V:(h + 1) * MLA_V, :].astype(v_ref.dtype)
        v_ref[0, h, MLA_V:V_ROWS, :] = ones_rows


def _qkv_prep(ql, kvl, krdt, cos_t, sin_t, g_q_lat, wq_pad, g_kv_lat, wk, wvt, gq_pad, gk_pad, b, s, tm):
    t = b * s
    nst = s // tm
    full = lambda shape: pl.BlockSpec(shape, lambda bi, i: (0,) * len(shape))
    tok = lambda w: pl.BlockSpec((tm, w), lambda bi, i: (bi * nst + i, 0))
    hs = lambda w: pl.BlockSpec((1, MLA_HEADS, tm, w), lambda bi, i: (bi, 0, i, 0))
    return pl.pallas_call(
        functools.partial(_qkv_kernel, scale=MLA_QK_DIM ** -0.5 * math.log2(math.e)),
        grid=(b, nst),
        in_specs=[tok(MLA_Q_RANK), tok(MLA_KV_RANK), tok(LANES),
                  pl.BlockSpec((tm, LANES), lambda bi, i: (i, 0)),
                  pl.BlockSpec((tm, LANES), lambda bi, i: (i, 0)),
                  full((1, MLA_Q_RANK)), full(wq_pad.shape), full((1, MLA_KV_RANK)), full(wk.shape),
                  full(wvt.shape), full((1, 2 * LANES)), full((1, 2 * LANES))],
        out_specs=(hs(2 * LANES), hs(2 * LANES),
                   pl.BlockSpec((1, MLA_HEADS, V_ROWS, tm), lambda bi, i: (bi, 0, 0, i))),
        out_shape=(jax.ShapeDtypeStruct((b, MLA_HEADS, s, 2 * LANES), BF16),
                   jax.ShapeDtypeStruct((b, MLA_HEADS, s, 2 * LANES), BF16),
                   jax.ShapeDtypeStruct((b, MLA_HEADS, V_ROWS, s), BF16)),
        compiler_params=_cparams(("parallel", "parallel")),
        name="qkv_prep",
    )(ql, kvl, krdt, cos_t, sin_t, g_q_lat, wq_pad, g_kv_lat, wk, wvt, gq_pad, gk_pad)


def _flash_kernel(q_ref, k_ref, vt_ref, o_ref, m_scr, acc_scr, s_scr, *, tq):
    i = pl.program_id(2)
    tk = tq // 2
    q = q_ref[0, 0]
    m_scr[...] = jnp.full(m_scr.shape, NEG_BIG, F32)
    acc_scr[...] = jnp.zeros(acc_scr.shape, F32)

    def scores(j, slot, lo=0):
        start = pl.multiple_of(j * tk, tk)
        ks = k_ref[0, 0, pl.ds(start, tk), :]
        s_scr[slot, :, lo:] = lax.dot_general(ks, q[lo:, :], (((1,), (1,)), ((), ())),
                                              preferred_element_type=F32)

    def softmax_pv(j, slot, masked, lo=0):
        start = pl.multiple_of(j * tk, tk)
        vt = vt_ref[0, 0, :, pl.ds(start, tk)]
        st = s_scr[slot, :, lo:]
        if masked:
            kc = lax.broadcasted_iota(I32, (tk, tk), 0) // CHUNK
            qc = lax.broadcasted_iota(I32, (tk, tk), 1) // CHUNK
            diag = jnp.where(kc <= qc, st[:, :tk], NEG_BIG)
            st = diag if st.shape[1] == tk else jnp.concatenate([diag, st[:, tk:]], axis=1)
        m_prev = m_scr[:, lo:]
        m_new = jnp.maximum(m_prev, jnp.max(st, axis=0, keepdims=True))
        alpha = jnp.exp2(m_prev - m_new)
        pt = jnp.exp2(st - m_new)
        acc_scr[:, lo:] = alpha * acc_scr[:, lo:] + jnp.dot(vt, pt.astype(BF16), preferred_element_type=F32)
        m_scr[:, lo:] = m_new

    scores(0, 0)

    def body(jj, carry):
        j = 2 * jj
        scores(j + 1, 1)
        softmax_pv(j, 0, False)
        scores(j + 2, 0)
        softmax_pv(j + 1, 1, False)
        return carry

    lax.fori_loop(0, i, body, 0)
    scores(2 * i + 1, 1, lo=tk)
    softmax_pv(2 * i, 0, True)
    softmax_pv(2 * i + 1, 1, True, lo=tk)

    o_ref[0] = (acc_scr[0:MLA_V, :] / acc_scr[MLA_V:MLA_V + 1, :]).T.astype(o_ref.dtype)


def _flash(q, k, v, tq):
    b, nh, s, _ = q.shape
    return pl.pallas_call(
        functools.partial(_flash_kernel, tq=tq),
        grid=(b, nh, s // tq),
        in_specs=[pl.BlockSpec((1, 1, tq, q.shape[-1]), lambda bi, h, i: (bi, h, i, 0)),
                  pl.BlockSpec((1, 1, s, k.shape[-1]), lambda bi, h, i: (bi, h, 0, 0)),
                  pl.BlockSpec((1, 1, V_ROWS, s), lambda bi, h, i: (bi, h, 0, 0))],
        out_specs=pl.BlockSpec((1, tq, MLA_V), lambda bi, h, i: (bi, i, h)),
        out_shape=jax.ShapeDtypeStruct((b, s, nh * MLA_V), BF16),
        scratch_shapes=[pltpu.VMEM((1, tq), F32), pltpu.VMEM((V_ROWS, tq), F32),
                        pltpu.VMEM((2, tq // 2, tq), F32)],
        compiler_params=_cparams(("parallel", "parallel", "arbitrary")),
        name="flash_attn",
    )(q, k, v)


def _merge_kernel(x_ref, ys_ref, om_ref, gate_ref, wss_ref, wml_ref, wo_ref, gffn_ref, wrt_ref, brt_ref,
                  x1_ref, h2_ref, idx_ref, gates_ref, rank_ref, cnt_ref, *, tm):
    d = x_ref.shape[-1]
    step = pl.program_id(0)

    @pl.when(step == 0)
    def _():
        cnt_ref[...] = jnp.zeros_like(cnt_ref)

    y1 = jnp.dot(ys_ref[...], wss_ref[...], preferred_element_type=F32)
    y2 = jnp.dot(om_ref[...], wml_ref[...], preferred_element_type=F32)
    g = gate_ref[...].astype(F32)
    merged = (g[:, :d] * y1 + g[:, d:] * y2).astype(BF16)
    x1 = x_ref[...] + jnp.dot(merged, wo_ref[...], preferred_element_type=F32)
    x1_ref[...] = x1
    ms = jnp.mean(x1 * x1, axis=-1, keepdims=True)
    h2 = x1 * lax.rsqrt(ms + EPS) * gffn_ref[...]
    h2_ref[...] = h2.astype(h2_ref.dtype)

    w_hi, w_lo = _bf16_pieces(wrt_ref[...], 2)
    h_hi, h_lo = _bf16_pieces(h2, 2)
    nt = (((1,), (1,)), ((), ()))
    logits = (lax.dot_general(w_hi, h_hi, nt, preferred_element_type=F32)
              + lax.dot_general(w_hi, h_lo, nt, preferred_element_type=F32)
              + lax.dot_general(w_lo, h_hi, nt, preferred_element_type=F32)) + brt_ref[...]
    eid = lax.broadcasted_iota(I32, (N_EXPERTS, tm), 0)
    cur = logits
    onehot = jnp.zeros((N_EXPERTS, tm), F32)
    vals, sels = [], []
    for k in range(TOP_K):
        mx = jnp.max(cur, axis=0, keepdims=True)
        idx = jnp.min(jnp.where(cur == mx, eid, N_EXPERTS), axis=0, keepdims=True)
        sel = eid == idx
        vals.append(mx)
        sels.append(sel)
        idx_ref[k:k + 1, :] = idx
        cur = jnp.where(sel, -jnp.inf, cur)
        onehot = onehot + sel.astype(F32)
    es = [jnp.exp(vk - vals[0]) for vk in vals]
    den = es[0] + es[1] + es[2] + es[3]
    for k in range(TOP_K):
        gates_ref[k:k + 1, :] = es[k] / den

    r = lax.broadcasted_iota(I32, (tm, tm), 0)
    c = lax.broadcasted_iota(I32, (tm, tm), 1)
    before = ((r < c) & ((r // MOE_SUB) == (c // MOE_SUB))).astype(BF16)
    prefix = jnp.dot(onehot.astype(BF16), before, preferred_element_type=F32)
    for k in range(TOP_K):
        rank_ref[k:k + 1, :] = jnp.sum(jnp.where(sels[k], prefix, 0.0), axis=0, keepdims=True).astype(I32)
    lane = lax.broadcasted_iota(I32, (N_EXPERTS, LANES), 1)
    cnt = cnt_ref[...]
    for g in range(tm // MOE_SUB):
        c_g = jnp.sum(onehot[:, g * MOE_SUB:(g + 1) * MOE_SUB], axis=1, keepdims=True)
        cnt = jnp.where(lane == step * (tm // MOE_SUB) + g, c_g, cnt)
    cnt_ref[...] = cnt


def _merge(x2, y_ssd, o_mla, gate, w_ss, w_ml, w_o, g_ffn, w_rt, b_rt, tm):
    t, d = x2.shape
    full = lambda shape: pl.BlockSpec(shape, lambda i: (0,) * len(shape))
    tok = lambda w: pl.BlockSpec((tm, w), lambda i: (i, 0))
    sel = pl.BlockSpec((TOP_K, tm), lambda i: (0, i))
    return pl.pallas_call(
        functools.partial(_merge_kernel, tm=tm),
        grid=(t // tm,),
        in_specs=[tok(d), tok(d), tok(d), tok(2 * d), full((d, d)), full((d, d)), full((d, d)),
                  full((1, d)), full((N_EXPERTS, d)), full((N_EXPERTS, 1))],
        out_specs=(tok(d), tok(d), sel, sel, sel, full((N_EXPERTS, LANES))),
        out_shape=(jax.ShapeDtypeStruct((t, d), F32), jax.ShapeDtypeStruct((t, d), BF16),
                   jax.ShapeDtypeStruct((TOP_K, t), I32), jax.ShapeDtypeStruct((TOP_K, t), F32),
                   jax.ShapeDtypeStruct((TOP_K, t), I32), jax.ShapeDtypeStruct((N_EXPERTS, LANES), F32)),
        compiler_params=_cparams(("arbitrary",)),
        name="merge_route",
    )(x2, y_ssd, o_mla, gate, w_ss, w_ml, w_o, g_ffn, w_rt, b_rt)


def _excl_cumsum_rows(v):
    eid = lax.broadcasted_iota(I32, v.shape, 0)
    out = jnp.zeros(v.shape, F32)
    for e in range(N_EXPERTS - 1):
        out = out + jnp.where(eid > e, v[e:e + 1, :], 0.0)
    return out


def _tables_kernel(cnt_ref, idx_ref, rank_ref, slot_ref, seg_ref, loff_ref, gdst_ref, tail_ref):
    t = idx_ref.shape[1]
    cnt = cnt_ref[...]
    seg = jnp.ceil(cnt * (1.0 / SEG_ALIGN)) * SEG_ALIGN
    loff = _excl_cumsum_rows(seg)
    r = lax.broadcasted_iota(I32, (LANES, LANES), 0)
    c = lax.broadcasted_iota(I32, (LANES, LANES), 1)
    run = jnp.dot((seg * (1.0 / SEG_ALIGN)).astype(BF16), (r < c).astype(BF16),
                  preferred_element_type=F32) * SEG_ALIGN
    tot = jnp.sum(seg, axis=1, keepdims=True)
    padded = jnp.broadcast_to(jnp.ceil(tot * (1.0 / MOE_BLOCK)) * MOE_BLOCK, (N_EXPERTS, LANES))
    start = _excl_cumsum_rows(padded)
    end = start + padded
    seg_ref[...] = seg.astype(I32)
    loff_ref[...] = loff.astype(I32)
    gdst_ref[...] = (start + run).astype(I32)

    gr = lax.broadcasted_iota(I32, (LANES, t), 0)
    gc = lax.broadcasted_iota(I32, (LANES, t), 1) // MOE_SUB
    loff_tok = jnp.dot((loff * (1.0 / SEG_ALIGN)).astype(BF16), (gr == gc).astype(BF16),
                       preferred_element_type=F32) * SEG_ALIGN
    idx = idx_ref[...]
    slot = rank_ref[...]
    for e in range(N_EXPERTS):
        slot = slot + jnp.where(idx == e, loff_tok[e:e + 1, :].astype(I32), 0)
    slot_ref[...] = slot

    lane = lax.broadcasted_iota(I32, (N_EXPERTS, LANES), 1)
    total = jnp.max(end, axis=0, keepdims=True)
    tail = jnp.where(lane == 0, start + tot, jnp.where(lane == 1, end, jnp.where(lane == 3, start, total)))
    tail_ref[...] = tail.astype(I32)


def _tables(cnt, idx_t, rank_t):
    t = idx_t.shape[1]
    tab = jax.ShapeDtypeStruct((N_EXPERTS, LANES), I32)
    return pl.pallas_call(
        _tables_kernel,
        out_shape=(jax.ShapeDtypeStruct((TOP_K, t), I32), tab, tab, tab, tab),
        compiler_params=pltpu.CompilerParams(vmem_limit_bytes=VMEM_LIMIT),
        name="route_tables",
    )(cnt, idx_t, rank_t)


def _pow2_sizes(limit):
    size = SEG_ALIGN
    while size * 2 <= limit:
        size *= 2
    sizes = []
    while size >= SEG_ALIGN:
        sizes.append(size)
        size //= 2
    return tuple(sizes)


def _piece_copies(src_ref, src_off, dst_ref, dst_off, n, limit, sem, wait=False):
    off = 0
    for size in _pow2_sizes(limit):
        take = n & size

        @pl.when(take != 0)
        def _(off=off, size=size):
            cp = pltpu.make_async_copy(
                src_ref.at[pl.ds(pl.multiple_of(src_off + off, SEG_ALIGN), size), :],
                dst_ref.at[pl.ds(pl.multiple_of(dst_off + off, SEG_ALIGN), size), :], sem)
            if wait:
                cp.wait()
            else:
                cp.start()

        off = off + take


def _dispatch_kernel(seg_ref, loff_ref, gdst_ref, taillo_ref, tailhi_ref, used_ref,
                     h_ref, slot_ref, xs_ref, loc_scr, zero_scr, sem, zsem, *, nblk):
    i = pl.program_id(0)
    par = i % 2

    @pl.when(i == 0)
    def _():
        nblk_all = xs_ref.shape[0] // MOE_BLOCK
        zero_scr[...] = jnp.zeros_like(zero_scr)

        def blk_copy(b):
            start = pl.multiple_of(b * MOE_BLOCK, MOE_BLOCK)
            return pltpu.make_async_copy(zero_scr, xs_ref.at[pl.ds(start, MOE_BLOCK), :], zsem)

        def issue_blk(b, carry):
            blk_copy(b).start()
            return carry

        def drain_blk(b, carry):
            blk_copy(b).wait()
            return carry

        lax.fori_loop(used_ref[0], nblk_all, issue_blk, 0)
        lax.fori_loop(used_ref[0], nblk_all, drain_blk, 0)

        def tails(wait):
            def body(e, carry):
                lo = taillo_ref[e]
                _piece_copies(zero_scr, 0, xs_ref, lo, tailhi_ref[e] - lo, MOE_BLOCK - 1, zsem, wait=wait)
                return carry
            return body

        lax.fori_loop(0, N_EXPERTS, tails(False), 0)
        lax.fori_loop(0, N_EXPERTS, tails(True), 0)

    slots = slot_ref[...]
    rid = lax.broadcasted_iota(I32, (LOCAL_SLOTS, MOE_SUB), 0)
    hit = rid == slots[0:1, :]
    for k in range(1, TOP_K):
        hit = hit | (rid == slots[k:k + 1, :])
    loc_scr[par] = jnp.dot(hit.astype(BF16), h_ref[...], preferred_element_type=F32).astype(ROW_DTYPE)

    loc = loc_scr.at[par]

    def seg_body(e, total):
        n = seg_ref[e * LANES + i]
        _piece_copies(loc, loff_ref[e * LANES + i], xs_ref, gdst_ref[e * LANES + i], n, MOE_SUB, sem.at[par])
        return total + n

    total = lax.fori_loop(0, N_EXPERTS, seg_body, 0)
    spare = (nblk + par * (LOCAL_SLOTS // MOE_BLOCK)) * MOE_BLOCK
    _piece_copies(loc, total, xs_ref, spare + total, LOCAL_SLOTS - total, LOCAL_SLOTS, sem.at[par])

    def wait_step(p):
        pltpu.make_async_copy(loc_scr.at[p], xs_ref.at[pl.ds(0, LOCAL_SLOTS), :], sem.at[p]).wait()

    @pl.when(i > 0)
    def _():
        wait_step(1 - par)

    @pl.when(i == pl.num_programs(0) - 1)
    def _():
        wait_step(par)


def _dispatch(seg, loff, gdst, tail_lo, tail_hi, used, h2, slot_t, nblk):
    t, d = h2.shape
    rows = (nblk + 2 * (LOCAL_SLOTS // MOE_BLOCK)) * MOE_BLOCK
    return pl.pallas_call(
        functools.partial(_dispatch_kernel, nblk=nblk),
        grid_spec=pltpu.PrefetchScalarGridSpec(
            num_scalar_prefetch=6,
            grid=(t // MOE_SUB,),
            in_specs=[pl.BlockSpec((MOE_SUB, d), lambda i, *_: (i, 0)),
                      pl.BlockSpec((TOP_K, MOE_SUB), lambda i, *_: (0, i))],
            out_specs=pl.BlockSpec(memory_space=pl.ANY),
            scratch_shapes=[pltpu.VMEM((2, LOCAL_SLOTS, d), ROW_DTYPE), pltpu.VMEM((MOE_BLOCK, d), ROW_DTYPE),
                            pltpu.SemaphoreType.DMA((2,)), pltpu.SemaphoreType.DMA(())]),
        out_shape=jax.ShapeDtypeStruct((rows, d), ROW_DTYPE),
        compiler_params=_cparams(("arbitrary",)),
        name="moe_dispatch",
    )(seg, loff, gdst, tail_lo, tail_hi, used, h2, slot_t)


def _expert_kernel(lo_ref, hi_ref, used_ref, x_ref, wgu_ref, bgu_ref, wd_ref, bd_ref, y_ref,
                   wgu_scr, wd_scr, xbuf, ybuf, xsem, ysem):
    e = pl.program_id(0)
    ne = pl.num_programs(0)
    ff = wd_ref.shape[1]
    big = 2 * MOE_BLOCK

    def layout(ex):
        first = lo_ref[ex] // MOE_BLOCK
        n = (hi_ref[ex] - lo_ref[ex]) // MOE_BLOCK
        return first, n // 2, n // 2 + n % 2

    cur = layout(e)
    first, npair, items = cur

    def x_copy(lay, k, rows):
        start = pl.multiple_of((lay[0] + 2 * k) * MOE_BLOCK, MOE_BLOCK)
        return pltpu.make_async_copy(x_ref.at[pl.ds(start, rows), :], xbuf.at[k % 2, pl.ds(0, rows), :],
                                     xsem.at[k % 2])

    def y_copy(k, rows):
        start = pl.multiple_of((first + 2 * k) * MOE_BLOCK, MOE_BLOCK)
        return pltpu.make_async_copy(ybuf.at[k % 2, pl.ds(0, rows), :], y_ref.at[pl.ds(start, rows), :],
                                     ysem.at[k % 2])

    def start_x(lay, k):
        @pl.when(k < lay[1])
        def _():
            x_copy(lay, k, big).start()

        @pl.when((k == lay[1]) & (k < lay[2]))
        def _():
            x_copy(lay, k, MOE_BLOCK).start()

    def compute(k, rows):
        slot = k % 2
        xb = xbuf[slot, 0:rows, :].astype(BF16)
        gu = jnp.dot(xb, wgu_scr[...], preferred_element_type=F32) + bgu_ref[0]
        glu = jnp.minimum(gu[:, :ff], SWIGLU_LIMIT)
        lin = jnp.clip(gu[:, ff:], -SWIGLU_LIMIT, SWIGLU_LIMIT)
        act = glu * _sigmoid(SWIGLU_ALPHA * glu) * (lin + 1.0)
        y = jnp.dot(act.astype(BF16), wd_scr[...], preferred_element_type=F32) + bd_ref[0]
        ybuf[slot, 0:rows, :] = y.astype(ybuf.dtype)

    def item(k, rows):
        x_copy(cur, k, rows).wait()

        @pl.when(k >= 1)
        def _():
            start_x(cur, k + 1)

        @pl.when(k >= 2)
        def _():
            y_copy(k - 2, big).wait()

        compute(k, rows)
        y_copy(k, rows).start()

    @pl.when(e == 0)
    def _():
        start_x(cur, 0)
        start_x(cur, 1)

    @pl.when(items > 0)
    def _():
        wgu_scr[...] = wgu_ref[0].astype(BF16)
        wd_scr[...] = wd_ref[0].astype(BF16)

    def body(k, carry):
        item(k, big)
        return carry

    lax.fori_loop(0, npair, body, 0)

    @pl.when(items > npair)
    def _():
        item(npair, MOE_BLOCK)

    @pl.when(items >= 2)
    def _():
        y_copy(items - 2, big).wait()

    @pl.when(items > npair)
    def _():
        y_copy(items - 1, MOE_BLOCK).wait()

    @pl.when((items == npair) & (items >= 1))
    def _():
        y_copy(items - 1, big).wait()

    @pl.when(e + 1 < ne)
    def _():
        nxt = layout(jnp.minimum(e + 1, ne - 1))
        start_x(nxt, 0)
        start_x(nxt, 1)

    @pl.when(e == pl.num_programs(0) - 1)
    def _():
        nblk_all = y_ref.shape[0] // MOE_BLOCK
        ybuf[0] = jnp.zeros(ybuf.shape[1:], ybuf.dtype)

        def zero_copy(b):
            start = pl.multiple_of(b * MOE_BLOCK, MOE_BLOCK)
            return pltpu.make_async_copy(ybuf.at[0, pl.ds(0, MOE_BLOCK), :],
                                         y_ref.at[pl.ds(start, MOE_BLOCK), :], ysem.at[0])

        def issue(b, carry):
            zero_copy(b).start()
            return carry

        def drain(b, carry):
            zero_copy(b).wait()
            return carry

        lax.fori_loop(used_ref[0], nblk_all, issue, 0)
        lax.fori_loop(used_ref[0], nblk_all, drain, 0)


def _experts(row_lo, row_hi, used, xs, w_gate_up, b_gate_up, w_down, b_down):
    rows, d = xs.shape
    ne, _, ff2 = w_gate_up.shape
    ff = ff2 // 2
    return pl.pallas_call(
        _expert_kernel,
        grid_spec=pltpu.PrefetchScalarGridSpec(
            num_scalar_prefetch=3,
            grid=(ne,),
            in_specs=[pl.BlockSpec(memory_space=pl.ANY),
                      pl.BlockSpec((1, d, ff2), lambda e, *_: (e, 0, 0)),
                      pl.BlockSpec((1, 1, ff2), lambda e, *_: (e, 0, 0)),
                      pl.BlockSpec((1, ff, d), lambda e, *_: (e, 0, 0)),
                      pl.BlockSpec((1, 1, d), lambda e, *_: (e, 0, 0))],
            out_specs=pl.BlockSpec(memory_space=pl.ANY),
            scratch_shapes=[pltpu.VMEM((d, ff2), BF16), pltpu.VMEM((ff, d), BF16),
                            pltpu.VMEM((2, 2 * MOE_BLOCK, d), ROW_DTYPE),
                            pltpu.VMEM((2, 2 * MOE_BLOCK, d), ROW_DTYPE),
                            pltpu.SemaphoreType.DMA((2,)), pltpu.SemaphoreType.DMA((2,))]),
        out_shape=jax.ShapeDtypeStruct((rows, d), ROW_DTYPE),
        compiler_params=_cparams(("arbitrary",)),
        name="moe_experts",
    )(row_lo, row_hi, used, xs, w_gate_up, b_gate_up.reshape(ne, 1, ff2), w_down, b_down.reshape(ne, 1, d))


def _combine_kernel(seg_ref, loff_ref, gdst_ref, x1_ref, slot_ref, gates_ref, ys_ref, o_ref, loc_scr, sem):
    i = pl.program_id(0)
    par = i % 2

    def fetch(g, p):
        loc = loc_scr.at[p]

        def seg_body(e, total):
            n = seg_ref[e * LANES + g]
            _piece_copies(ys_ref, gdst_ref[e * LANES + g], loc, loff_ref[e * LANES + g], n, MOE_SUB, sem.at[p])
            return total + n

        total = lax.fori_loop(0, N_EXPERTS, seg_body, 0)
        _piece_copies(ys_ref, total, loc, total, LOCAL_SLOTS - total, LOCAL_SLOTS, sem.at[p])

    @pl.when(i == 0)
    def _():
        fetch(0, 0)

    @pl.when(i + 1 < pl.num_programs(0))
    def _():
        fetch(i + 1, 1 - par)

    pltpu.make_async_copy(ys_ref.at[pl.ds(0, LOCAL_SLOTS), :], loc_scr.at[par], sem.at[par]).wait()

    slots = slot_ref[...]
    g = gates_ref[...]
    rid = lax.broadcasted_iota(I32, (MOE_SUB, LOCAL_SLOTS), 1)
    gmat = jnp.where(rid == slots[:, 0:1], g[:, 0:1], 0.0)
    for k in range(1, TOP_K):
        gmat = gmat + jnp.where(rid == slots[:, k:k + 1], g[:, k:k + 1], 0.0)
    o_ref[...] = x1_ref[...] + jnp.dot(gmat.astype(BF16), loc_scr[par].astype(BF16),
                                       preferred_element_type=F32)


def _combine(seg, loff, gdst, x1, slot_tk, gates_tk, ys):
    t, d = x1.shape
    tok = lambda w: pl.BlockSpec((MOE_SUB, w), lambda i, *_: (i, 0))
    return pl.pallas_call(
        _combine_kernel,
        grid_spec=pltpu.PrefetchScalarGridSpec(
            num_scalar_prefetch=3,
            grid=(t // MOE_SUB,),
            in_specs=[tok(d), tok(TOP_K), tok(TOP_K), pl.BlockSpec(memory_space=pl.ANY)],
            out_specs=tok(d),
            scratch_shapes=[pltpu.VMEM((2, LOCAL_SLOTS, d), ROW_DTYPE), pltpu.SemaphoreType.DMA((2,))]),
        out_shape=jax.ShapeDtypeStruct((t, d), F32),
        compiler_params=_cparams(("arbitrary",)),
        name="moe_combine",
    )(seg, loff, gdst, x1, slot_tk, gates_tk, ys)


def _rope_tables(s):
    pos = np.arange(s, dtype=np.float64)
    inv = ROPE_BASE ** (-np.arange(0, MLA_ROPE, 2, dtype=np.float64) / MLA_ROPE)
    ang = pos[:, None] * inv[None, :]
    cos, sin = jnp.asarray(np.cos(ang), F32), jnp.asarray(np.sin(ang), F32)
    return _spread_rope(jnp.concatenate([cos, cos], axis=-1)), _spread_rope(jnp.concatenate([-sin, sin], axis=-1))


def _spread_rope(w, gap=None):
    half = MLA_ROPE // 2
    zeros = jnp.zeros(w.shape[:-1] + (half,), w.dtype)
    return jnp.concatenate([w[..., :half], zeros if gap is None else gap, w[..., half:], zeros], axis=-1)


def _pad_heads(w):
    lead = w.shape[:-1]
    w = w.reshape(lead + (MLA_HEADS, MLA_QK_DIM))
    w = jnp.concatenate([w[..., :MLA_NOPE], _spread_rope(w[..., MLA_NOPE:])], axis=-1)
    return w.reshape(lead + (MLA_HEADS * 2 * LANES,))


def _layer(x, g_mix, w_in, conv_w, conv_b, dt_bias, a_log, d_skip, g_ssd_out, w_ssd_out,
           g_q_lat, w_q_up, g_kv_lat, w_kv_up, g_qk_q, g_qk_k, w_mla_out, w_o,
           g_ffn, w_router, b_router, w_gate_up, b_gate_up, w_down, b_down):
    b, s, d = x.shape
    t = b * s
    x2 = x.reshape(t, d)

    off_xbc = SSD_D_INNER
    off_dt = off_xbc + SSD_CONV_DIM
    off_ql = off_dt + SSD_HEADS
    off_kvl = off_ql + MLA_Q_RANK
    off_kr = off_kvl + MLA_KV_RANK
    off_gate = off_kr + MLA_ROPE
    dt_gap = jnp.pad(w_in[:, off_dt:off_ql], ((0, 0), (0, MLA_ROPE // 2 - SSD_HEADS)))
    w_streams = [w_in[:, :off_xbc], w_in[:, off_xbc:off_dt], w_in[:, off_ql:off_kvl], w_in[:, off_kvl:off_kr],
                 _spread_rope(w_in[:, off_kr:off_gate], gap=dt_gap), w_in[:, off_gate:]]

    tiles = _tile_sizes(s)
    z, xbc, ql, kvl, krdt, gate = _in_proj(x2, g_mix, [w.astype(BF16) for w in w_streams], tiles.tokens)

    lane_pad = lambda vec: jnp.pad(vec, (DT_LO, LANES - DT_LO - SSD_HEADS)).reshape(1, LANES)
    lt = tiles.ssd
    y_ssd = _ssd(xbc.reshape(b, s, SSD_CONV_DIM), z.reshape(b, s, SSD_D_INNER), krdt.reshape(b, s, LANES),
                 conv_w, conv_b.reshape(1, -1), lane_pad(dt_bias), lane_pad(a_log),
                 jnp.repeat(d_skip, SSD_HEAD_DIM).reshape(1, -1), g_ssd_out.reshape(1, -1), lt)

    cos_t, sin_t = _rope_tables(s)
    wq_pad = _pad_heads(w_q_up).astype(BF16)
    spread_gain = lambda g: jnp.concatenate([g[:MLA_NOPE], _spread_rope(g[MLA_NOPE:])]).reshape(1, -1)
    gq_pad = spread_gain(g_qk_q)
    gk_pad = spread_gain(g_qk_k)
    tq = tiles.tokens
    w_kv_h = w_kv_up.reshape(MLA_KV_RANK, MLA_HEADS, MLA_NOPE + MLA_V)
    wk = w_kv_h[:, :, :MLA_NOPE].reshape(MLA_KV_RANK, MLA_HEADS * MLA_NOPE).astype(BF16)
    wvt = w_kv_h[:, :, MLA_NOPE:].reshape(MLA_KV_RANK, MLA_HEADS * MLA_V).T.astype(BF16)
    q, k, v = _qkv_prep(ql, kvl, krdt, cos_t, sin_t, g_q_lat.reshape(1, -1), wq_pad,
                        g_kv_lat.reshape(1, -1), wk, wvt, gq_pad, gk_pad, b, s, tq)
    o_mla = _flash(q, k, v, tiles.queries)

    x1, h2, idx_t, gates_t, rank_t, cnt = _merge(
        x2, y_ssd.reshape(t, d), o_mla.reshape(t, d), gate,
        w_ssd_out.astype(BF16), w_mla_out.astype(BF16), w_o.astype(BF16),
        g_ffn.reshape(1, -1), w_router.T, b_router.reshape(-1, 1), tiles.tokens)

    nsub = t // MOE_SUB
    assert t % MOE_SUB == 0 and nsub <= LANES
    cap = t * TOP_K + nsub * N_EXPERTS * (SEG_ALIGN - 1) + N_EXPERTS * (MOE_BLOCK - 1)
    nblk = -(-cap // MOE_BLOCK)
    slot_t, seg, loff, gdst, tail = _tables(cnt, idx_t, rank_t)
    used = (tail[0:1, 2] // MOE_BLOCK).astype(I32)
    seg, loff, gdst = seg.reshape(-1), loff.reshape(-1), gdst.reshape(-1)
    xs = _dispatch(seg, loff, gdst, tail[:, 0], tail[:, 1], used, h2, slot_t, nblk)
    ys = _experts(tail[:, 3], tail[:, 1], used, xs, w_gate_up, b_gate_up, w_down, b_down)
    out = _combine(seg, loff, gdst, x1, slot_t.T, gates_t.T, ys)
    return out.reshape(b, s, d)


def kernel(x, g_mix, w_in, conv_w, conv_b, dt_bias, a_log, d_skip, g_ssd_out, w_ssd_out, g_q_lat, w_q_up, g_kv_lat, w_kv_up, g_qk_q, g_qk_k, w_mla_out, w_o, g_ffn, w_router, b_router, w_gate_up, b_gate_up, w_down, b_down):
    params = (g_mix, w_in, conv_w, conv_b, dt_bias, a_log, d_skip, g_ssd_out, w_ssd_out, g_q_lat, w_q_up,
              g_kv_lat, w_kv_up, g_qk_q, g_qk_k, w_mla_out, w_o, g_ffn, w_router, b_router,
              w_gate_up, b_gate_up, w_down, b_down)
    for l in range(g_mix.shape[0]):
        x = _layer(x, *(p[l] for p in params))
    return x
```

```python
import functools
import math
from typing import NamedTuple

import jax
import jax.numpy as jnp
import numpy as np
from jax import lax
from jax.experimental import pallas as pl
from jax.experimental.pallas import tpu as pltpu

F32 = jnp.float32
BF16 = jnp.bfloat16
I32 = jnp.int32

EPS = 1e-6
CHUNK = 64

SSD_HEADS = 16
SSD_HEAD_DIM = 64
SSD_GROUPS = 2
SSD_STATE = 128
SSD_CONV = 4
SSD_D_INNER = SSD_HEADS * SSD_HEAD_DIM
SSD_CONV_DIM = SSD_D_INNER + 2 * SSD_GROUPS * SSD_STATE

MLA_HEADS = 8
MLA_Q_RANK = 384
MLA_KV_RANK = 256
MLA_NOPE = 128
MLA_ROPE = 64
MLA_QK_DIM = MLA_NOPE + MLA_ROPE
MLA_V = 128
V_ROWS = MLA_V + 16
DT_LO = MLA_ROPE // 2
SSD_SCAN_CHUNK = 128
ROPE_BASE = 10000.0

N_EXPERTS = 32
TOP_K = 4
SWIGLU_ALPHA = 1.702
SWIGLU_LIMIT = 7.0

LANES = 128
VMEM_LIMIT = 56 * 1024 * 1024
NEG_BIG = -1e30
LOG2E = math.log2(math.e)

MOE_BLOCK = 256
MOE_SUB = 256
ROW_DTYPE = F32
SEG_ALIGN = 8 * 4 // jnp.dtype(ROW_DTYPE).itemsize
LOCAL_SLOTS = -(-(TOP_K * MOE_SUB + N_EXPERTS * (SEG_ALIGN - 1)) // MOE_BLOCK) * MOE_BLOCK


class _Tiles(NamedTuple):
    tokens: int
    ssd: int
    queries: int


def _tile_sizes(seq):
    return _Tiles(tokens=min(512, seq), ssd=min(256, seq), queries=min(1024, seq))


def _cparams(semantics, **kw):
    return pltpu.CompilerParams(dimension_semantics=semantics,
                                vmem_limit_bytes=VMEM_LIMIT, **kw)


def _sigmoid(v):
    return 1.0 / (1.0 + jnp.exp(-v))


def _silu(v):
    return v * _sigmoid(v)


def _bf16_pieces(t, n):
    pieces = []
    for _ in range(n - 1):
        p = t.astype(BF16)
        pieces.append(p)
        t = t - p.astype(F32)
    pieces.append(t.astype(BF16))
    return pieces


def _inproj_kernel(x_ref, g_ref, wz_ref, wxbc_ref, wql_ref, wkvl_ref, wkrdt_ref, wgate_ref,
                   z_ref, xbc_ref, ql_ref, kvl_ref, krdt_ref, gate_ref):
    x = x_ref[...]
    ms = jnp.mean(x * x, axis=-1, keepdims=True)
    h = (x * lax.rsqrt(ms + EPS) * g_ref[...]).astype(BF16)
    streams = ((wz_ref, z_ref), (wxbc_ref, xbc_ref), (wql_ref, ql_ref), (wkvl_ref, kvl_ref),
               (wkrdt_ref, krdt_ref), (wgate_ref, gate_ref))
    for w_ref, ref in streams:
        p = jnp.dot(h, w_ref[...], preferred_element_type=F32)
        if ref is gate_ref:
            p = _sigmoid(p)
        ref[...] = p.astype(ref.dtype)


def _in_proj(x2, g_mix, weights, tm):
    t, d = x2.shape
    widths = tuple(w.shape[1] for w in weights)
    dts = (BF16, BF16, BF16, BF16, F32, BF16)
    out_shape = tuple(jax.ShapeDtypeStruct((t, w), dt) for w, dt in zip(widths, dts))
    out_specs = tuple(pl.BlockSpec((tm, w), lambda i: (i, 0)) for w in widths)
    return pl.pallas_call(
        _inproj_kernel,
        grid=(t // tm,),
        in_specs=[pl.BlockSpec((tm, d), lambda i: (i, 0)),
                  pl.BlockSpec((1, d), lambda i: (0, 0))]
                 + [pl.BlockSpec((d, w), lambda i: (0, 0)) for w in widths],
        out_specs=out_specs,
        out_shape=out_shape,
        compiler_params=_cparams(("parallel",)),
        name="in_proj",
    )(x2, g_mix.reshape(1, d), *weights)


def _ssd_kernel(xbc_ref, z_ref, krdt_ref, convw_ref, convb_ref, dtb_ref, alog_ref, dskip_ref, gout_ref,
                y_ref, xext_scr, state_scr, *, lt):
    i = pl.program_id(1)
    halo = 8
    dt_lo = DT_LO
    gw = SSD_D_INNER // SSD_GROUPS
    hpg = SSD_HEADS // SSD_GROUPS

    @pl.when(i == 0)
    def _():
        state_scr[...] = jnp.zeros_like(state_scr)
        xext_scr[0:halo, :] = jnp.zeros((halo, SSD_CONV_DIM), F32)

    xin = xbc_ref[0]
    xext_scr[halo:2 * halo, :] = xin[0:halo, :].astype(F32)
    srow = lax.broadcasted_iota(I32, (lt, lt), 0)
    scol = lax.broadcasted_iota(I32, (lt, lt), 1)
    acc = convb_ref[...] + convw_ref[SSD_CONV - 1:SSD_CONV, :] * xin.astype(F32)
    head = jnp.broadcast_to(convb_ref[...], (halo, SSD_CONV_DIM))
    for j in range(SSD_CONV):
        shift = SSD_CONV - 1 - j
        head = head + convw_ref[j:j + 1, :] * xext_scr[pl.ds(halo - shift, halo), :]
        if shift:
            shifted = jnp.dot((srow == scol + shift).astype(BF16), xin, preferred_element_type=F32)
            acc = acc + convw_ref[j:j + 1, :] * shifted
    acc = jnp.concatenate([head, acc[halo:, :]], axis=0)
    xext_scr[0:halo, :] = xin[lt - halo:lt, :].astype(F32)
    xbc = _silu(acc)
    xs = xbc[:, :SSD_D_INNER]
    bm = xbc[:, SSD_D_INNER:SSD_D_INNER + SSD_GROUPS * SSD_STATE]
    cm = xbc[:, SSD_D_INNER + SSD_GROUPS * SSD_STATE:]

    lane = lax.broadcasted_iota(I32, (1, LANES), 1)
    head_lane = (lane >= dt_lo) & (lane < dt_lo + SSD_HEADS)
    v = krdt_ref[0] + dtb_ref[...]
    dt = jnp.maximum(v, 0.0) + jnp.log(1.0 + jnp.exp(-jnp.abs(v)))
    dt = jnp.where(head_lane, dt, 0.0)
    a = jnp.where(head_lane, -jnp.exp(alog_ref[...]) * LOG2E, 0.0)
    da = dt * a
    lc = min(SSD_SCAN_CHUNK, lt)
    nchunk = lt // lc
    row = lax.broadcasted_iota(I32, (lt, lt), 0)
    col = lax.broadcasted_iota(I32, (lt, lt), 1)
    tri_b = ((row >= col) & ((row // lc) == (col // lc))).astype(BF16)
    a_cum = jnp.zeros((lt, LANES), F32)
    for piece in _bf16_pieces(da, 3):
        a_cum = a_cum + jnp.dot(tri_b, piece, preferred_element_type=F32)
    a_cum_t = a_cum.T
    a_last = jnp.concatenate([jnp.broadcast_to(a_cum[(c + 1) * lc - 1:(c + 1) * lc, :], (lc, LANES))
                              for c in range(nchunk)], axis=0)
    exp_a = jnp.exp2(a_cum)
    dec = jnp.exp2(a_last - a_cum)
    tri = lax.broadcasted_iota(I32, (lc, lc), 0) >= lax.broadcasted_iota(I32, (lc, lc), 1)

    er = lax.broadcasted_iota(I32, (LANES, SSD_D_INNER), 0)
    ec = lax.broadcasted_iota(I32, (LANES, SSD_D_INNER), 1)
    expand = ((er - dt_lo) == (ec // SSD_HEAD_DIM)).astype(BF16)

    def widen(t, pieces):
        out = jnp.zeros((lt, SSD_D_INNER), F32)
        for piece in _bf16_pieces(t, pieces):
            out = out + jnp.dot(piece, expand, preferred_element_type=F32)
        return out

    dt_w = widen(dt, 2)
    exp_a_w = widen(exp_a, 1)
    dec_w = widen(dec, 1)
    xdt = xs * dt_w
    xdt_b = xdt.astype(BF16)
    xdec_b = (xdt * dec_w).astype(BF16)
    lane_pair = lax.broadcasted_iota(I32, (lc, LANES), 1)

    states = [state_scr[g] for g in range(SSD_GROUPS)]
    y_rows = [[] for _ in range(SSD_GROUPS)]
    for c in range(nchunk):
        rows = slice(c * lc, (c + 1) * lc)
        for g in range(SSD_GROUPS):
            cols = slice(g * gw, (g + 1) * gw)
            bg = bm[rows, g * SSD_STATE:(g + 1) * SSD_STATE].astype(BF16)
            cg = cm[rows, g * SSD_STATE:(g + 1) * SSD_STATE].astype(BF16)
            cb = lax.dot_general(cg, bg, (((1,), (1,)), ((), ())), preferred_element_type=F32)
            parts = []
            for pair in range(hpg // 2):
                h0 = g * hpg + 2 * pair
                rhs = xdt_b[rows, h0 * SSD_HEAD_DIM:(h0 + 2) * SSD_HEAD_DIM]
                res = []
                for h in (h0, h0 + 1):
                    seg = a_cum[rows, dt_lo + h:dt_lo + h + 1] - a_cum_t[dt_lo + h:dt_lo + h + 1, rows]
                    lmat = jnp.exp2(jnp.where(tri, seg, NEG_BIG))
                    res.append(jnp.dot((cb * lmat).astype(BF16), rhs, preferred_element_type=F32))
                parts.append(jnp.where(lane_pair < SSD_HEAD_DIM, res[0], res[1]))
            st = states[g]
            y_off = jnp.dot(cg, st.astype(BF16), preferred_element_type=F32)
            y_rows[g].append(jnp.concatenate(parts, axis=-1) + y_off * exp_a_w[rows, cols])
            new = lax.dot_general(bg, xdec_b[rows, cols], (((0,), (0,)), ((), ())),
                                  preferred_element_type=F32)
            states[g] = st * exp_a_w[(c + 1) * lc - 1:(c + 1) * lc, cols] + new
    for g in range(SSD_GROUPS):
        state_scr[g] = states[g]

    zs = _silu(z_ref[0].astype(F32))
    outs = []
    for g in range(SSD_GROUPS):
        yg = (jnp.concatenate(y_rows[g], axis=0)
              + dskip_ref[:, g * gw:(g + 1) * gw] * xs[:, g * gw:(g + 1) * gw])
        yg = yg * zs[:, g * gw:(g + 1) * gw]
        ms = jnp.mean(yg * yg, axis=-1, keepdims=True)
        outs.append(yg * lax.rsqrt(ms + EPS) * gout_ref[:, g * gw:(g + 1) * gw])
    y_ref[0] = jnp.concatenate(outs, axis=-1).astype(y_ref.dtype)


def _ssd(xbc, z, krdt, conv_w, conv_b, dtb, alog, dskip_w, g_out, lt):
    b, s, _ = xbc.shape
    gw = SSD_D_INNER // SSD_GROUPS
    full = lambda shape: pl.BlockSpec(shape, lambda bi, i: (0,) * len(shape))
    return pl.pallas_call(
        functools.partial(_ssd_kernel, lt=lt),
        grid=(b, s // lt),
        in_specs=[pl.BlockSpec((1, lt, SSD_CONV_DIM), lambda bi, i: (bi, i, 0)),
                  pl.BlockSpec((1, lt, SSD_D_INNER), lambda bi, i: (bi, i, 0)),
                  pl.BlockSpec((1, lt, LANES), lambda bi, i: (bi, i, 0)),
                  full((SSD_CONV, SSD_CONV_DIM)), full((1, SSD_CONV_DIM)),
                  full((1, LANES)), full((1, LANES)),
                  full((1, SSD_D_INNER)), full((1, SSD_D_INNER))],
        out_specs=pl.BlockSpec((1, lt, SSD_D_INNER), lambda bi, i: (bi, i, 0)),
        out_shape=jax.ShapeDtypeStruct((b, s, SSD_D_INNER), BF16),
        scratch_shapes=[pltpu.VMEM((16, SSD_CONV_DIM), F32),
                        pltpu.VMEM((SSD_GROUPS, SSD_STATE, gw), F32)],
        compiler_params=_cparams(("parallel", "arbitrary")),
        name="ssd_scan",
    )(xbc, z, krdt, conv_w, conv_b, dtb, alog, dskip_w, g_out)


def _swap_halves(t):
    return pltpu.roll(t, LANES // 2, 1)


def _lane_sums(sq, width):
    ones = jnp.ones((sq.shape[1], width), BF16)
    return jnp.dot(sq.astype(BF16), ones, preferred_element_type=F32)


def _qkv_kernel(ql_ref, kvl_ref, krdt_ref, cos_ref, sin_ref, gql_ref, wq_ref, gkvl_ref, wk_ref, wvt_ref,
                gq_ref, gk_ref, q_ref, k_ref, v_ref, *, scale):
    hw = 2 * LANES
    cosv = cos_ref[...]
    sinv = sin_ref[...]

    ql = ql_ref[...].astype(F32)
    rq = lax.rsqrt(_lane_sums(ql * ql, LANES) * (1.0 / MLA_Q_RANK) + EPS)
    qn = (ql * jnp.concatenate([rq] * (MLA_Q_RANK // LANES), axis=1) * gql_ref[...]).astype(BF16)

    kvl = kvl_ref[...].astype(F32)
    rkv = lax.rsqrt(_lane_sums(kvl * kvl, LANES) * (1.0 / MLA_KV_RANK) + EPS)
    kvn = (kvl * jnp.concatenate([rkv] * (MLA_KV_RANK // LANES), axis=1) * gkvl_ref[...]).astype(BF16)
    tm = kvn.shape[0]
    ones_rows = (lax.broadcasted_iota(I32, (V_ROWS - MLA_V, tm), 0) == 0).astype(v_ref.dtype)

    lane = lax.broadcasted_iota(I32, (1, LANES), 1)
    kr = jnp.where((lane % (LANES // 2)) < MLA_ROPE // 2, krdt_ref[...], 0.0)
    ss_r = _lane_sums(kr * kr, LANES)
    krg = kr * gk_ref[:, LANES:]
    kr_rot = krg * cosv + _swap_halves(krg) * sinv

    qf = jnp.dot(qn, wq_ref[...], preferred_element_type=F32)
    kf = jnp.dot(kvn, wk_ref[...], preferred_element_type=F32)
    vt = lax.dot_general(wvt_ref[...], kvn, (((1,), (1,)), ((), ())), preferred_element_type=F32)
    for h in range(MLA_HEADS):
        qh = qf[:, h * hw:(h + 1) * hw]
        r = lax.rsqrt(_lane_sums(qh * qh, hw) * (1.0 / MLA_QK_DIM) + EPS)
        qs = qh * (r * scale) * gq_ref[...]
        qr = qs[:, LANES:]
        qh_t = jnp.concatenate([qs[:, :LANES], qr * cosv + _swap_halves(qr) * sinv], axis=1).T
        q_ref[0, h] = qh_t.astype(q_ref.dtype)

        kn = kf[:, h * LANES:(h + 1) * LANES]
        rk = lax.rsqrt((_lane_sums(kn * kn, LANES) + ss_r) * (1.0 / MLA_QK_DIM) + EPS)
        k_ref[0, h, :, 0:LANES] = (kn * rk * gk_ref[:, :LANES]).astype(k_ref.dtype)
        k_ref[0, h, :, LANES:hw] = (kr_rot * rk).astype(k_ref.dtype)
        v_ref[0, h, 0:MLA_V, :] = vt[h * MLA_V:(h + 1) * MLA_V, :].astype(v_ref.dtype)
        v_ref[0, h, MLA_V:V_ROWS, :] = ones_rows


def _qkv_prep(ql, kvl, krdt, cos_t, sin_t, g_q_lat, wq_pad, g_kv_lat, wk, wvt, gq_pad, gk_pad, b, s, tm):
    t = b * s
    nst = s // tm
    full = lambda shape: pl.BlockSpec(shape, lambda bi, i: (0,) * len(shape))
    tok = lambda w: pl.BlockSpec((tm, w), lambda bi, i: (bi * nst + i, 0))
    hs = lambda w: pl.BlockSpec((1, MLA_HEADS, tm, w), lambda bi, i: (bi, 0, i, 0))
    return pl.pallas_call(
        functools.partial(_qkv_kernel, scale=MLA_QK_DIM ** -0.5 * math.log2(math.e)),
        grid=(b, nst),
        in_specs=[tok(MLA_Q_RANK), tok(MLA_KV_RANK), tok(LANES),
                  pl.BlockSpec((tm, LANES), lambda bi, i: (i, 0)),
                  pl.BlockSpec((tm, LANES), lambda bi, i: (i, 0)),
                  full((1, MLA_Q_RANK)), full(wq_pad.shape), full((1, MLA_KV_RANK)), full(wk.shape),
                  full(wvt.shape), full((1, 2 * LANES)), full((1, 2 * LANES))],
        out_specs=(pl.BlockSpec((1, MLA_HEADS, 2 * LANES, tm), lambda bi, i: (bi, 0, 0, i)), hs(2 * LANES),
                   pl.BlockSpec((1, MLA_HEADS, V_ROWS, tm), lambda bi, i: (bi, 0, 0, i))),
        out_shape=(jax.ShapeDtypeStruct((b, MLA_HEADS, 2 * LANES, s), BF16),
                   jax.ShapeDtypeStruct((b, MLA_HEADS, s, 2 * LANES), BF16),
                   jax.ShapeDtypeStruct((b, MLA_HEADS, V_ROWS, s), BF16)),
        compiler_params=_cparams(("parallel", "parallel")),
        name="qkv_prep",
    )(ql, kvl, krdt, cos_t, sin_t, g_q_lat, wq_pad, g_kv_lat, wk, wvt, gq_pad, gk_pad)


def _flash_kernel(qt_ref, k_ref, vt_ref, o_ref, m_scr, acc_scr, s_scr, *, tq):
    i = pl.program_id(2)
    tk = tq // 2
    qt = qt_ref[0, 0]
    m_scr[...] = jnp.full(m_scr.shape, NEG_BIG, F32)
    acc_scr[...] = jnp.zeros(acc_scr.shape, F32)

    def scores(j, slot, lo=0):
        start = pl.multiple_of(j * tk, tk)
        ks = k_ref[0, 0, pl.ds(start, tk), :]
        s_scr[slot, :, lo:] = jnp.dot(ks, qt[:, lo:], preferred_element_type=F32)

    def softmax_pv(j, slot, masked, lo=0):
        start = pl.multiple_of(j * tk, tk)
        vt = vt_ref[0, 0, :, pl.ds(start, tk)]
        st = s_scr[slot, :, lo:]
        if masked:
            kc = lax.broadcasted_iota(I32, (tk, tk), 0) // CHUNK
            qc = lax.broadcasted_iota(I32, (tk, tk), 1) // CHUNK
            diag = jnp.where(kc <= qc, st[:, :tk], NEG_BIG)
            st = diag if st.shape[1] == tk else jnp.concatenate([diag, st[:, tk:]], axis=1)
        m_prev = m_scr[:, lo:]
        m_new = jnp.maximum(m_prev, jnp.max(st, axis=0, keepdims=True))
        alpha = jnp.exp2(m_prev - m_new)
        pt = jnp.exp2(st - m_new)
        acc_scr[:, lo:] = alpha * acc_scr[:, lo:] + jnp.dot(vt, pt.astype(BF16), preferred_element_type=F32)
        m_scr[:, lo:] = m_new

    scores(0, 0)

    def body(jj, carry):
        j = 2 * jj
        scores(j + 1, 1)
        softmax_pv(j, 0, False)
        scores(j + 2, 0)
        softmax_pv(j + 1, 1, False)
        return carry

    lax.fori_loop(0, i, body, 0)
    scores(2 * i + 1, 1, lo=tk)
    softmax_pv(2 * i, 0, True)
    softmax_pv(2 * i + 1, 1, True, lo=tk)

    o_ref[0] = (acc_scr[0:MLA_V, :] / acc_scr[MLA_V:MLA_V + 1, :]).T.astype(o_ref.dtype)


def _flash(qt, k, v, tq):
    b, nh, s, _ = k.shape
    return pl.pallas_call(
        functools.partial(_flash_kernel, tq=tq),
        grid=(b, nh, s // tq),
        in_specs=[pl.BlockSpec((1, 1, qt.shape[2], tq), lambda bi, h, i: (bi, h, 0, i)),
                  pl.BlockSpec((1, 1, s, k.shape[-1]), lambda bi, h, i: (bi, h, 0, 0)),
                  pl.BlockSpec((1, 1, V_ROWS, s), lambda bi, h, i: (bi, h, 0, 0))],
        out_specs=pl.BlockSpec((1, tq, MLA_V), lambda bi, h, i: (bi, i, h)),
        out_shape=jax.ShapeDtypeStruct((b, s, nh * MLA_V), BF16),
        scratch_shapes=[pltpu.VMEM((1, tq), F32), pltpu.VMEM((V_ROWS, tq), F32),
                        pltpu.VMEM((2, tq // 2, tq), F32)],
        compiler_params=_cparams(("parallel", "parallel", "arbitrary")),
        name="flash_attn",
    )(qt, k, v)


def _merge_kernel(x_ref, ys_ref, om_ref, gate_ref, wss_ref, wml_ref, wo_ref, gffn_ref, wrt_ref, brt_ref,
                  x1_ref, h2_ref, idx_ref, gates_ref, rank_ref, cnt_ref, *, tm):
    d = x_ref.shape[-1]
    step = pl.program_id(0)

    @pl.when(step == 0)
    def _():
        cnt_ref[...] = jnp.zeros_like(cnt_ref)

    y1 = jnp.dot(ys_ref[...], wss_ref[...], preferred_element_type=F32)
    y2 = jnp.dot(om_ref[...], wml_ref[...], preferred_element_type=F32)
    g = gate_ref[...].astype(F32)
    merged = (g[:, :d] * y1 + g[:, d:] * y2).astype(BF16)
    x1 = x_ref[...] + jnp.dot(merged, wo_ref[...], preferred_element_type=F32)
    x1_ref[...] = x1
    ms = jnp.mean(x1 * x1, axis=-1, keepdims=True)
    h2 = x1 * lax.rsqrt(ms + EPS) * gffn_ref[...]
    h2_ref[...] = h2.astype(h2_ref.dtype)

    w_hi, w_lo = _bf16_pieces(wrt_ref[...], 2)
    h_hi, h_lo = _bf16_pieces(h2, 2)
    nt = (((1,), (1,)), ((), ()))
    logits = (lax.dot_general(w_hi, h_hi, nt, preferred_element_type=F32)
              + lax.dot_general(w_hi, h_lo, nt, preferred_element_type=F32)
              + lax.dot_general(w_lo, h_hi, nt, preferred_element_type=F32)) + brt_ref[...]
    eid = lax.broadcasted_iota(I32, (N_EXPERTS, tm), 0)
    cur = logits
    onehot = jnp.zeros((N_EXPERTS, tm), F32)
    vals, sels = [], []
    for k in range(TOP_K):
        mx = jnp.max(cur, axis=0, keepdims=True)
        idx = jnp.min(jnp.where(cur == mx, eid, N_EXPERTS), axis=0, keepdims=True)
        sel = eid == idx
        vals.append(mx)
        sels.append(sel)
        idx_ref[k:k + 1, :] = idx
        cur = jnp.where(sel, -jnp.inf, cur)
        onehot = onehot + sel.astype(F32)
    es = [jnp.exp(vk - vals[0]) for vk in vals]
    den = es[0] + es[1] + es[2] + es[3]
    for k in range(TOP_K):
        gates_ref[k:k + 1, :] = es[k] / den

    r = lax.broadcasted_iota(I32, (tm, tm), 0)
    c = lax.broadcasted_iota(I32, (tm, tm), 1)
    before = ((r < c) & ((r // MOE_SUB) == (c // MOE_SUB))).astype(BF16)
    prefix = jnp.dot(onehot.astype(BF16), before, preferred_element_type=F32)
    for k in range(TOP_K):
        rank_ref[k:k + 1, :] = jnp.sum(jnp.where(sels[k], prefix, 0.0), axis=0, keepdims=True).astype(I32)
    lane = lax.broadcasted_iota(I32, (N_EXPERTS, LANES), 1)
    cnt = cnt_ref[...]
    for g in range(tm // MOE_SUB):
        c_g = jnp.sum(onehot[:, g * MOE_SUB:(g + 1) * MOE_SUB], axis=1, keepdims=True)
        cnt = jnp.where(lane == step * (tm // MOE_SUB) + g, c_g, cnt)
    cnt_ref[...] = cnt


def _merge(x2, y_ssd, o_mla, gate, w_ss, w_ml, w_o, g_ffn, w_rt, b_rt, tm):
    t, d = x2.shape
    full = lambda shape: pl.BlockSpec(shape, lambda i: (0,) * len(shape))
    tok = lambda w: pl.BlockSpec((tm, w), lambda i: (i, 0))
    sel = pl.BlockSpec((TOP_K, tm), lambda i: (0, i))
    return pl.pallas_call(
        functools.partial(_merge_kernel, tm=tm),
        grid=(t // tm,),
        in_specs=[tok(d), tok(d), tok(d), tok(2 * d), full((d, d)), full((d, d)), full((d, d)),
                  full((1, d)), full((N_EXPERTS, d)), full((N_EXPERTS, 1))],
        out_specs=(tok(d), tok(d), sel, sel, sel, full((N_EXPERTS, LANES))),
        out_shape=(jax.ShapeDtypeStruct((t, d), F32), jax.ShapeDtypeStruct((t, d), BF16),
                   jax.ShapeDtypeStruct((TOP_K, t), I32), jax.ShapeDtypeStruct((TOP_K, t), F32),
                   jax.ShapeDtypeStruct((TOP_K, t), I32), jax.ShapeDtypeStruct((N_EXPERTS, LANES), F32)),
        compiler_params=_cparams(("arbitrary",)),
        name="merge_route",
    )(x2, y_ssd, o_mla, gate, w_ss, w_ml, w_o, g_ffn, w_rt, b_rt)


def _excl_cumsum_rows(v):
    eid = lax.broadcasted_iota(I32, v.shape, 0)
    out = jnp.zeros(v.shape, F32)
    for e in range(N_EXPERTS - 1):
        out = out + jnp.where(eid > e, v[e:e + 1, :], 0.0)
    return out


def _tables_kernel(cnt_ref, idx_ref, rank_ref, slot_ref, seg_ref, loff_ref, gdst_ref, tail_ref):
    t = idx_ref.shape[1]
    cnt = cnt_ref[...]
    seg = jnp.ceil(cnt * (1.0 / SEG_ALIGN)) * SEG_ALIGN
    loff = _excl_cumsum_rows(seg)
    r = lax.broadcasted_iota(I32, (LANES, LANES), 0)
    c = lax.broadcasted_iota(I32, (LANES, LANES), 1)
    run = jnp.dot((seg * (1.0 / SEG_ALIGN)).astype(BF16), (r < c).astype(BF16),
                  preferred_element_type=F32) * SEG_ALIGN
    tot = jnp.sum(seg, axis=1, keepdims=True)
    padded = jnp.broadcast_to(jnp.ceil(tot * (1.0 / MOE_BLOCK)) * MOE_BLOCK, (N_EXPERTS, LANES))
    start = _excl_cumsum_rows(padded)
    end = start + padded
    seg_ref[...] = seg.astype(I32)
    loff_ref[...] = loff.astype(I32)
    gdst_ref[...] = (start + run).astype(I32)

    gr = lax.broadcasted_iota(I32, (LANES, t), 0)
    gc = lax.broadcasted_iota(I32, (LANES, t), 1) // MOE_SUB
    loff_tok = jnp.dot((loff * (1.0 / SEG_ALIGN)).astype(BF16), (gr == gc).astype(BF16),
                       preferred_element_type=F32) * SEG_ALIGN
    idx = idx_ref[...]
    slot = rank_ref[...]
    for e in range(N_EXPERTS):
        slot = slot + jnp.where(idx == e, loff_tok[e:e + 1, :].astype(I32), 0)
    slot_ref[...] = slot

    lane = lax.broadcasted_iota(I32, (N_EXPERTS, LANES), 1)
    total = jnp.max(end, axis=0, keepdims=True)
    tail = jnp.where(lane == 0, start + tot, jnp.where(lane == 1, end, jnp.where(lane == 3, start, total)))
    tail_ref[...] = tail.astype(I32)


def _tables(cnt, idx_t, rank_t):
    t = idx_t.shape[1]
    tab = jax.ShapeDtypeStruct((N_EXPERTS, LANES), I32)
    return pl.pallas_call(
        _tables_kernel,
        out_shape=(jax.ShapeDtypeStruct((TOP_K, t), I32), tab, tab, tab, tab),
        compiler_params=pltpu.CompilerParams(vmem_limit_bytes=VMEM_LIMIT),
        name="route_tables",
    )(cnt, idx_t, rank_t)


def _pow2_sizes(limit):
    size = SEG_ALIGN
    while size * 2 <= limit:
        size *= 2
    sizes = []
    while size >= SEG_ALIGN:
        sizes.append(size)
        size //= 2
    return tuple(sizes)


def _piece_copies(src_ref, src_off, dst_ref, dst_off, n, limit, sem, wait=False):
    off = 0
    for size in _pow2_sizes(limit):
        take = n & size

        @pl.when(take != 0)
        def _(off=off, size=size):
            cp = pltpu.make_async_copy(
                src_ref.at[pl.ds(pl.multiple_of(src_off + off, SEG_ALIGN), size), :],
                dst_ref.at[pl.ds(pl.multiple_of(dst_off + off, SEG_ALIGN), size), :], sem)
            if wait:
                cp.wait()
            else:
                cp.start()

        off = off + take


def _dispatch_kernel(seg_ref, loff_ref, gdst_ref, taillo_ref, tailhi_ref, used_ref,
                     h_ref, slot_ref, xs_ref, loc_scr, zero_scr, sem, zsem, *, nblk):
    i = pl.program_id(0)
    par = i % 2

    @pl.when(i == 0)
    def _():
        nblk_all = xs_ref.shape[0] // MOE_BLOCK
        zero_scr[...] = jnp.zeros_like(zero_scr)

        def blk_copy(b):
            start = pl.multiple_of(b * MOE_BLOCK, MOE_BLOCK)
            return pltpu.make_async_copy(zero_scr, xs_ref.at[pl.ds(start, MOE_BLOCK), :], zsem)

        def issue_blk(b, carry):
            blk_copy(b).start()
            return carry

        def drain_blk(b, carry):
            blk_copy(b).wait()
            return carry

        lax.fori_loop(used_ref[0], nblk_all, issue_blk, 0)
        lax.fori_loop(used_ref[0], nblk_all, drain_blk, 0)

        def tails(wait):
            def body(e, carry):
                lo = taillo_ref[e]
                _piece_copies(zero_scr, 0, xs_ref, lo, tailhi_ref[e] - lo, MOE_BLOCK - 1, zsem, wait=wait)
                return carry
            return body

        lax.fori_loop(0, N_EXPERTS, tails(False), 0)
        lax.fori_loop(0, N_EXPERTS, tails(True), 0)

    slots = slot_ref[...]
    rid = lax.broadcasted_iota(I32, (LOCAL_SLOTS, MOE_SUB), 0)
    hit = rid == slots[0:1, :]
    for k in range(1, TOP_K):
        hit = hit | (rid == slots[k:k + 1, :])
    loc_scr[par] = jnp.dot(hit.astype(BF16), h_ref[...], preferred_element_type=F32).astype(ROW_DTYPE)

    loc = loc_scr.at[par]

    def seg_body(e, total):
        n = seg_ref[e * LANES + i]
        _piece_copies(loc, loff_ref[e * LANES + i], xs_ref, gdst_ref[e * LANES + i], n, MOE_SUB, sem.at[par])
        return total + n

    total = lax.fori_loop(0, N_EXPERTS, seg_body, 0)
    spare = (nblk + par * (LOCAL_SLOTS // MOE_BLOCK)) * MOE_BLOCK
    _piece_copies(loc, total, xs_ref, spare + total, LOCAL_SLOTS - total, LOCAL_SLOTS, sem.at[par])

    def wait_step(p):
        pltpu.make_async_copy(loc_scr.at[p], xs_ref.at[pl.ds(0, LOCAL_SLOTS), :], sem.at[p]).wait()

    @pl.when(i > 0)
    def _():
        wait_step(1 - par)

    @pl.when(i == pl.num_programs(0) - 1)
    def _():
        wait_step(par)


def _dispatch(seg, loff, gdst, tail_lo, tail_hi, used, h2, slot_t, nblk):
    t, d = h2.shape
    rows = (nblk + 2 * (LOCAL_SLOTS // MOE_BLOCK)) * MOE_BLOCK
    return pl.pallas_call(
        functools.partial(_dispatch_kernel, nblk=nblk),
        grid_spec=pltpu.PrefetchScalarGridSpec(
            num_scalar_prefetch=6,
            grid=(t // MOE_SUB,),
            in_specs=[pl.BlockSpec((MOE_SUB, d), lambda i, *_: (i, 0)),
                      pl.BlockSpec((TOP_K, MOE_SUB), lambda i, *_: (0, i))],
            out_specs=pl.BlockSpec(memory_space=pl.ANY),
            scratch_shapes=[pltpu.VMEM((2, LOCAL_SLOTS, d), ROW_DTYPE), pltpu.VMEM((MOE_BLOCK, d), ROW_DTYPE),
                            pltpu.SemaphoreType.DMA((2,)), pltpu.SemaphoreType.DMA(())]),
        out_shape=jax.ShapeDtypeStruct((rows, d), ROW_DTYPE),
        compiler_params=_cparams(("arbitrary",)),
        name="moe_dispatch",
    )(seg, loff, gdst, tail_lo, tail_hi, used, h2, slot_t)


def _expert_kernel(lo_ref, hi_ref, used_ref, x_ref, wgu_ref, bgu_ref, wd_ref, bd_ref, y_ref,
                   wgu_scr, wd_scr, xbuf, ybuf, xsem, ysem):
    e = pl.program_id(0)
    ne = pl.num_programs(0)
    ff = wd_ref.shape[1]
    big = 2 * MOE_BLOCK

    def layout(ex):
        first = lo_ref[ex] // MOE_BLOCK
        n = (hi_ref[ex] - lo_ref[ex]) // MOE_BLOCK
        return first, n // 2, n // 2 + n % 2

    cur = layout(e)
    first, npair, items = cur

    def x_copy(lay, k, rows):
        start = pl.multiple_of((lay[0] + 2 * k) * MOE_BLOCK, MOE_BLOCK)
        return pltpu.make_async_copy(x_ref.at[pl.ds(start, rows), :], xbuf.at[k % 2, pl.ds(0, rows), :],
                                     xsem.at[k % 2])

    def y_copy(k, rows):
        start = pl.multiple_of((first + 2 * k) * MOE_BLOCK, MOE_BLOCK)
        return pltpu.make_async_copy(ybuf.at[k % 2, pl.ds(0, rows), :], y_ref.at[pl.ds(start, rows), :],
                                     ysem.at[k % 2])

    def start_x(lay, k):
        @pl.when(k < lay[1])
        def _():
            x_copy(lay, k, big).start()

        @pl.when((k == lay[1]) & (k < lay[2]))
        def _():
            x_copy(lay, k, MOE_BLOCK).start()

    def compute(k, rows):
        slot = k % 2
        xb = xbuf[slot, 0:rows, :].astype(BF16)
        gu = jnp.dot(xb, wgu_scr[...], preferred_element_type=F32) + bgu_ref[0]
        glu = jnp.minimum(gu[:, :ff], SWIGLU_LIMIT)
        lin = jnp.clip(gu[:, ff:], -SWIGLU_LIMIT, SWIGLU_LIMIT)
        act = glu * _sigmoid(SWIGLU_ALPHA * glu) * (lin + 1.0)
        y = jnp.dot(act.astype(BF16), wd_scr[...], preferred_element_type=F32) + bd_ref[0]
        ybuf[slot, 0:rows, :] = y.astype(ybuf.dtype)

    def item(k, rows):
        x_copy(cur, k, rows).wait()

        @pl.when(k >= 1)
        def _():
            start_x(cur, k + 1)

        @pl.when(k >= 2)
        def _():
            y_copy(k - 2, big).wait()

        compute(k, rows)
        y_copy(k, rows).start()

    @pl.when(e == 0)
    def _():
        start_x(cur, 0)
        start_x(cur, 1)

    @pl.when(items > 0)
    def _():
        wgu_scr[...] = wgu_ref[0].astype(BF16)
        wd_scr[...] = wd_ref[0].astype(BF16)

    def body(k, carry):
        item(k, big)
        return carry

    lax.fori_loop(0, npair, body, 0)

    @pl.when(items > npair)
    def _():
        item(npair, MOE_BLOCK)

    @pl.when(items >= 2)
    def _():
        y_copy(items - 2, big).wait()

    @pl.when(items > npair)
    def _():
        y_copy(items - 1, MOE_BLOCK).wait()

    @pl.when((items == npair) & (items >= 1))
    def _():
        y_copy(items - 1, big).wait()

    @pl.when(e + 1 < ne)
    def _():
        nxt = layout(jnp.minimum(e + 1, ne - 1))
        start_x(nxt, 0)
        start_x(nxt, 1)

    @pl.when(e == pl.num_programs(0) - 1)
    def _():
        nblk_all = y_ref.shape[0] // MOE_BLOCK
        ybuf[0] = jnp.zeros(ybuf.shape[1:], ybuf.dtype)

        def zero_copy(b):
            start = pl.multiple_of(b * MOE_BLOCK, MOE_BLOCK)
            return pltpu.make_async_copy(ybuf.at[0, pl.ds(0, MOE_BLOCK), :],
                                         y_ref.at[pl.ds(start, MOE_BLOCK), :], ysem.at[0])

        def issue(b, carry):
            zero_copy(b).start()
            return carry

        def drain(b, carry):
            zero_copy(b).wait()
            return carry

        lax.fori_loop(used_ref[0], nblk_all, issue, 0)
        lax.fori_loop(used_ref[0], nblk_all, drain, 0)


def _experts(row_lo, row_hi, used, xs, w_gate_up, b_gate_up, w_down, b_down):
    rows, d = xs.shape
    ne, _, ff2 = w_gate_up.shape
    ff = ff2 // 2
    return pl.pallas_call(
        _expert_kernel,
        grid_spec=pltpu.PrefetchScalarGridSpec(
            num_scalar_prefetch=3,
            grid=(ne,),
            in_specs=[pl.BlockSpec(memory_space=pl.ANY),
                      pl.BlockSpec((1, d, ff2), lambda e, *_: (e, 0, 0)),
                      pl.BlockSpec((1, 1, ff2), lambda e, *_: (e, 0, 0)),
                      pl.BlockSpec((1, ff, d), lambda e, *_: (e, 0, 0)),
                      pl.BlockSpec((1, 1, d), lambda e, *_: (e, 0, 0))],
            out_specs=pl.BlockSpec(memory_space=pl.ANY),
            scratch_shapes=[pltpu.VMEM((d, ff2), BF16), pltpu.VMEM((ff, d), BF16),
                            pltpu.VMEM((2, 2 * MOE_BLOCK, d), ROW_DTYPE),
                            pltpu.VMEM((2, 2 * MOE_BLOCK, d), ROW_DTYPE),
                            pltpu.SemaphoreType.DMA((2,)), pltpu.SemaphoreType.DMA((2,))]),
        out_shape=jax.ShapeDtypeStruct((rows, d), ROW_DTYPE),
        compiler_params=_cparams(("arbitrary",)),
        name="moe_experts",
    )(row_lo, row_hi, used, xs, w_gate_up, b_gate_up.reshape(ne, 1, ff2), w_down, b_down.reshape(ne, 1, d))


def _combine_kernel(seg_ref, loff_ref, gdst_ref, x1_ref, slot_ref, gates_ref, ys_ref, o_ref, loc_scr, sem):
    i = pl.program_id(0)
    par = i % 2

    def fetch(g, p):
        loc = loc_scr.at[p]

        def seg_body(e, total):
            n = seg_ref[e * LANES + g]
            _piece_copies(ys_ref, gdst_ref[e * LANES + g], loc, loff_ref[e * LANES + g], n, MOE_SUB, sem.at[p])
            return total + n

        total = lax.fori_loop(0, N_EXPERTS, seg_body, 0)
        _piece_copies(ys_ref, total, loc, total, LOCAL_SLOTS - total, LOCAL_SLOTS, sem.at[p])

    @pl.when(i == 0)
    def _():
        fetch(0, 0)

    @pl.when(i + 1 < pl.num_programs(0))
    def _():
        fetch(i + 1, 1 - par)

    pltpu.make_async_copy(ys_ref.at[pl.ds(0, LOCAL_SLOTS), :], loc_scr.at[par], sem.at[par]).wait()

    slots = slot_ref[...]
    g = gates_ref[...]
    rid = lax.broadcasted_iota(I32, (MOE_SUB, LOCAL_SLOTS), 1)
    gmat = jnp.where(rid == slots[:, 0:1], g[:, 0:1], 0.0)
    for k in range(1, TOP_K):
        gmat = gmat + jnp.where(rid == slots[:, k:k + 1], g[:, k:k + 1], 0.0)
    o_ref[...] = x1_ref[...] + jnp.dot(gmat.astype(BF16), loc_scr[par].astype(BF16),
                                       preferred_element_type=F32)


def _combine(seg, loff, gdst, x1, slot_tk, gates_tk, ys):
    t, d = x1.shape
    tok = lambda w: pl.BlockSpec((MOE_SUB, w), lambda i, *_: (i, 0))
    return pl.pallas_call(
        _combine_kernel,
        grid_spec=pltpu.PrefetchScalarGridSpec(
            num_scalar_prefetch=3,
            grid=(t // MOE_SUB,),
            in_specs=[tok(d), tok(TOP_K), tok(TOP_K), pl.BlockSpec(memory_space=pl.ANY)],
            out_specs=tok(d),
            scratch_shapes=[pltpu.VMEM((2, LOCAL_SLOTS, d), ROW_DTYPE), pltpu.SemaphoreType.DMA((2,))]),
        out_shape=jax.ShapeDtypeStruct((t, d), F32),
        compiler_params=_cparams(("arbitrary",)),
        name="moe_combine",
    )(seg, loff, gdst, x1, slot_tk, gates_tk, ys)


def _rope_tables(s):
    pos = np.arange(s, dtype=np.float64)
    inv = ROPE_BASE ** (-np.arange(0, MLA_ROPE, 2, dtype=np.float64) / MLA_ROPE)
    ang = pos[:, None] * inv[None, :]
    cos, sin = jnp.asarray(np.cos(ang), F32), jnp.asarray(np.sin(ang), F32)
    return _spread_rope(jnp.concatenate([cos, cos], axis=-1)), _spread_rope(jnp.concatenate([-sin, sin], axis=-1))


def _spread_rope(w, gap=None):
    half = MLA_ROPE // 2
    zeros = jnp.zeros(w.shape[:-1] + (half,), w.dtype)
    return jnp.concatenate([w[..., :half], zeros if gap is None else gap, w[..., half:], zeros], axis=-1)


def _pad_heads(w):
    lead = w.shape[:-1]
    w = w.reshape(lead + (MLA_HEADS, MLA_QK_DIM))
    w = jnp.concatenate([w[..., :MLA_NOPE], _spread_rope(w[..., MLA_NOPE:])], axis=-1)
    return w.reshape(lead + (MLA_HEADS * 2 * LANES,))


def _layer(x, g_mix, w_in, conv_w, conv_b, dt_bias, a_log, d_skip, g_ssd_out, w_ssd_out,
           g_q_lat, w_q_up, g_kv_lat, w_kv_up, g_qk_q, g_qk_k, w_mla_out, w_o,
           g_ffn, w_router, b_router, w_gate_up, b_gate_up, w_down, b_down):
    b, s, d = x.shape
    t = b * s
    x2 = x.reshape(t, d)

    off_xbc = SSD_D_INNER
    off_dt = off_xbc + SSD_CONV_DIM
    off_ql = off_dt + SSD_HEADS
    off_kvl = off_ql + MLA_Q_RANK
    off_kr = off_kvl + MLA_KV_RANK
    off_gate = off_kr + MLA_ROPE
    dt_gap = jnp.pad(w_in[:, off_dt:off_ql], ((0, 0), (0, MLA_ROPE // 2 - SSD_HEADS)))
    w_streams = [w_in[:, :off_xbc], w_in[:, off_xbc:off_dt], w_in[:, off_ql:off_kvl], w_in[:, off_kvl:off_kr],
                 _spread_rope(w_in[:, off_kr:off_gate], gap=dt_gap), w_in[:, off_gate:]]

    tiles = _tile_sizes(s)
    z, xbc, ql, kvl, krdt, gate = _in_proj(x2, g_mix, [w.astype(BF16) for w in w_streams], tiles.tokens)

    lane_pad = lambda vec: jnp.pad(vec, (DT_LO, LANES - DT_LO - SSD_HEADS)).reshape(1, LANES)
    lt = tiles.ssd
    y_ssd = _ssd(xbc.reshape(b, s, SSD_CONV_DIM), z.reshape(b, s, SSD_D_INNER), krdt.reshape(b, s, LANES),
                 conv_w, conv_b.reshape(1, -1), lane_pad(dt_bias), lane_pad(a_log),
                 jnp.repeat(d_skip, SSD_HEAD_DIM).reshape(1, -1), g_ssd_out.reshape(1, -1), lt)

    cos_t, sin_t = _rope_tables(s)
    wq_pad = _pad_heads(w_q_up).astype(BF16)
    spread_gain = lambda g: jnp.concatenate([g[:MLA_NOPE], _spread_rope(g[MLA_NOPE:])]).reshape(1, -1)
    gq_pad = spread_gain(g_qk_q)
    gk_pad = spread_gain(g_qk_k)
    tq = tiles.tokens
    w_kv_h = w_kv_up.reshape(MLA_KV_RANK, MLA_HEADS, MLA_NOPE + MLA_V)
    wk = w_kv_h[:, :, :MLA_NOPE].reshape(MLA_KV_RANK, MLA_HEADS * MLA_NOPE).astype(BF16)
    wvt = w_kv_h[:, :, MLA_NOPE:].reshape(MLA_KV_RANK, MLA_HEADS * MLA_V).T.astype(BF16)
    q, k, v = _qkv_prep(ql, kvl, krdt, cos_t, sin_t, g_q_lat.reshape(1, -1), wq_pad,
                        g_kv_lat.reshape(1, -1), wk, wvt, gq_pad, gk_pad, b, s, tq)
    o_mla = _flash(q, k, v, tiles.queries)

    x1, h2, idx_t, gates_t, rank_t, cnt = _merge(
        x2, y_ssd.reshape(t, d), o_mla.reshape(t, d), gate,
        w_ssd_out.astype(BF16), w_mla_out.astype(BF16), w_o.astype(BF16),
        g_ffn.reshape(1, -1), w_router.T, b_router.reshape(-1, 1), tiles.tokens)

    nsub = t // MOE_SUB
    assert t % MOE_SUB == 0 and nsub <= LANES
    cap = t * TOP_K + nsub * N_EXPERTS * (SEG_ALIGN - 1) + N_EXPERTS * (MOE_BLOCK - 1)
    nblk = -(-cap // MOE_BLOCK)
    slot_t, seg, loff, gdst, tail = _tables(cnt, idx_t, rank_t)
    used = (tail[0:1, 2] // MOE_BLOCK).astype(I32)
    seg, loff, gdst = seg.reshape(-1), loff.reshape(-1), gdst.reshape(-1)
    xs = _dispatch(seg, loff, gdst, tail[:, 0], tail[:, 1], used, h2, slot_t, nblk)
    ys = _experts(tail[:, 3], tail[:, 1], used, xs, w_gate_up, b_gate_up, w_down, b_down)
    out = _combine(seg, loff, gdst, x1, slot_t.T, gates_t.T, ys)
    return out.reshape(b, s, d)


def kernel(x, g_mix, w_in, conv_w, conv_b, dt_bias, a_log, d_skip, g_ssd_out, w_ssd_out, g_q_lat, w_q_up, g_kv_lat, w_kv_up, g_qk_q, g_qk_k, w_mla_out, w_o, g_ffn, w_router, b_router, w_gate_up, b_gate_up, w_down, b_down):
    params = (g_mix, w_in, conv_w, conv_b, dt_bias, a_log, d_skip, g_ssd_out, w_ssd_out, g_q_lat, w_q_up,
              g_kv_lat, w_kv_up, g_qk_q, g_qk_k, w_mla_out, w_o, g_ffn, w_router, b_router,
              w_gate_up, b_gate_up, w_down, b_down)
    for l in range(g_mix.shape[0]):
        x = _layer(x, *(p[l] for p in params))
    return x
```

```python
import functools
import math
from typing import NamedTuple

import jax
import jax.numpy as jnp
import numpy as np
from jax import lax
from jax.experimental import pallas as pl
from jax.experimental.pallas import tpu as pltpu

F32 = jnp.float32
BF16 = jnp.bfloat16
I32 = jnp.int32

EPS = 1e-6
CHUNK = 64

SSD_HEADS = 16
SSD_HEAD_DIM = 64
SSD_GROUPS = 2
SSD_STATE = 128
SSD_CONV = 4
SSD_D_INNER = SSD_HEADS * SSD_HEAD_DIM
SSD_CONV_DIM = SSD_D_INNER + 2 * SSD_GROUPS * SSD_STATE

MLA_HEADS = 8
MLA_Q_RANK = 384
MLA_KV_RANK = 256
MLA_NOPE = 128
MLA_ROPE = 64
MLA_QK_DIM = MLA_NOPE + MLA_ROPE
MLA_V = 128
V_ROWS = MLA_V + 16
DT_LO = MLA_ROPE // 2
SSD_SCAN_CHUNK = 128
ROPE_BASE = 10000.0

N_EXPERTS = 32
TOP_K = 4
SWIGLU_ALPHA = 1.702
SWIGLU_LIMIT = 7.0

LANES = 128
VMEM_LIMIT = 56 * 1024 * 1024
NEG_BIG = -1e30
LOG2E = math.log2(math.e)

MOE_BLOCK = 256
MOE_SUB = 256
ROW_DTYPE = F32
SEG_ALIGN = 8 * 4 // jnp.dtype(ROW_DTYPE).itemsize
LOCAL_SLOTS = -(-(TOP_K * MOE_SUB + N_EXPERTS * (SEG_ALIGN - 1)) // MOE_BLOCK) * MOE_BLOCK


class _Tiles(NamedTuple):
    tokens: int
    ssd: int
    queries: int


def _tile_sizes(seq):
    return _Tiles(tokens=min(512, seq), ssd=min(256, seq), queries=min(1024, seq))


def _cparams(semantics, **kw):
    return pltpu.CompilerParams(dimension_semantics=semantics,
                                vmem_limit_bytes=VMEM_LIMIT, **kw)


def _sigmoid(v):
    return 1.0 / (1.0 + jnp.exp(-v))


def _silu(v):
    return v * _sigmoid(v)


def _bf16_pieces(t, n):
    pieces = []
    for _ in range(n - 1):
        p = t.astype(BF16)
        pieces.append(p)
        t = t - p.astype(F32)
    pieces.append(t.astype(BF16))
    return pieces


def _inproj_kernel(x_ref, g_ref, wz_ref, wxbc_ref, wql_ref, wkvl_ref, wkrdt_ref, wgate_ref,
                   z_ref, xbc_ref, ql_ref, kvl_ref, krdt_ref, gate_ref):
    x = x_ref[...]
    ms = jnp.mean(x * x, axis=-1, keepdims=True)
    h = (x * lax.rsqrt(ms + EPS) * g_ref[...]).astype(BF16)
    streams = ((wz_ref, z_ref), (wxbc_ref, xbc_ref), (wql_ref, ql_ref), (wkvl_ref, kvl_ref),
               (wkrdt_ref, krdt_ref), (wgate_ref, gate_ref))
    for w_ref, ref in streams:
        p = jnp.dot(h, w_ref[...], preferred_element_type=F32)
        if ref is gate_ref:
            p = _sigmoid(p)
        ref[...] = p.astype(ref.dtype)


def _in_proj(x2, g_mix, weights, tm):
    t, d = x2.shape
    widths = tuple(w.shape[1] for w in weights)
    dts = (BF16, BF16, BF16, BF16, F32, BF16)
    out_shape = tuple(jax.ShapeDtypeStruct((t, w), dt) for w, dt in zip(widths, dts))
    out_specs = tuple(pl.BlockSpec((tm, w), lambda i: (i, 0)) for w in widths)
    return pl.pallas_call(
        _inproj_kernel,
        grid=(t // tm,),
        in_specs=[pl.BlockSpec((tm, d), lambda i: (i, 0)),
                  pl.BlockSpec((1, d), lambda i: (0, 0))]
                 + [pl.BlockSpec((d, w), lambda i: (0, 0)) for w in widths],
        out_specs=out_specs,
        out_shape=out_shape,
        compiler_params=_cparams(("parallel",)),
        name="in_proj",
    )(x2, g_mix.reshape(1, d), *weights)


def _ssd_kernel(xbc_ref, z_ref, krdt_ref, convw_ref, convb_ref, dtb_ref, alog_ref, dskip_ref, gout_ref,
                y_ref, xext_scr, state_scr, *, lt):
    i = pl.program_id(1)
    halo = 8
    dt_lo = DT_LO
    gw = SSD_D_INNER // SSD_GROUPS
    hpg = SSD_HEADS // SSD_GROUPS

    @pl.when(i == 0)
    def _():
        state_scr[...] = jnp.zeros_like(state_scr)
        xext_scr[0:halo, :] = jnp.zeros((halo, SSD_CONV_DIM), F32)

    xin = xbc_ref[0]
    xext_scr[halo:2 * halo, :] = xin[0:halo, :].astype(F32)
    srow = lax.broadcasted_iota(I32, (lt, lt), 0)
    scol = lax.broadcasted_iota(I32, (lt, lt), 1)
    acc = convb_ref[...] + convw_ref[SSD_CONV - 1:SSD_CONV, :] * xin.astype(F32)
    head = jnp.broadcast_to(convb_ref[...], (halo, SSD_CONV_DIM))
    for j in range(SSD_CONV):
        shift = SSD_CONV - 1 - j
        head = head + convw_ref[j:j + 1, :] * xext_scr[pl.ds(halo - shift, halo), :]
        if shift:
            shifted = jnp.dot((srow == scol + shift).astype(BF16), xin, preferred_element_type=F32)
            acc = acc + convw_ref[j:j + 1, :] * shifted
    acc = jnp.concatenate([head, acc[halo:, :]], axis=0)
    xext_scr[0:halo, :] = xin[lt - halo:lt, :].astype(F32)
    xbc = _silu(acc)
    xs = xbc[:, :SSD_D_INNER]
    bm = xbc[:, SSD_D_INNER:SSD_D_INNER + SSD_GROUPS * SSD_STATE]
    cm = xbc[:, SSD_D_INNER + SSD_GROUPS * SSD_STATE:]

    lane = lax.broadcasted_iota(I32, (1, LANES), 1)
    head_lane = (lane >= dt_lo) & (lane < dt_lo + SSD_HEADS)
    v = krdt_ref[0] + dtb_ref[...]
    dt = jnp.maximum(v, 0.0) + jnp.log(1.0 + jnp.exp(-jnp.abs(v)))
    dt = jnp.where(head_lane, dt, 0.0)
    a = jnp.where(head_lane, -jnp.exp(alog_ref[...]) * LOG2E, 0.0)
    da = dt * a
    lc = min(SSD_SCAN_CHUNK, lt)
    nchunk = lt // lc
    row = lax.broadcasted_iota(I32, (lt, lt), 0)
    col = lax.broadcasted_iota(I32, (lt, lt), 1)
    tri_b = ((row >= col) & ((row // lc) == (col // lc))).astype(BF16)
    a_cum = jnp.zeros((lt, LANES), F32)
    for piece in _bf16_pieces(da, 3):
        a_cum = a_cum + jnp.dot(tri_b, piece, preferred_element_type=F32)
    a_cum_t = a_cum.T
    a_last = jnp.concatenate([jnp.broadcast_to(a_cum[(c + 1) * lc - 1:(c + 1) * lc, :], (lc, LANES))
                              for c in range(nchunk)], axis=0)
    exp_a = jnp.exp2(a_cum)
    dec = jnp.exp2(a_last - a_cum)
    tri = lax.broadcasted_iota(I32, (lc, lc), 0) >= lax.broadcasted_iota(I32, (lc, lc), 1)

    er = lax.broadcasted_iota(I32, (LANES, SSD_D_INNER), 0)
    ec = lax.broadcasted_iota(I32, (LANES, SSD_D_INNER), 1)
    expand = ((er - dt_lo) == (ec // SSD_HEAD_DIM)).astype(BF16)

    def widen(t, pieces):
        out = jnp.zeros((lt, SSD_D_INNER), F32)
        for piece in _bf16_pieces(t, pieces):
            out = out + jnp.dot(piece, expand, preferred_element_type=F32)
        return out

    dt_w = widen(dt, 2)
    exp_a_w = widen(exp_a, 1)
    dec_w = widen(dec, 1)
    xdt = xs * dt_w
    xdt_b = xdt.astype(BF16)
    xdec_b = (xdt * dec_w).astype(BF16)
    lane_pair = lax.broadcasted_iota(I32, (lc, LANES), 1)

    states = [state_scr[g] for g in range(SSD_GROUPS)]
    y_rows = [[] for _ in range(SSD_GROUPS)]
    for c in range(nchunk):
        rows = slice(c * lc, (c + 1) * lc)
        for g in range(SSD_GROUPS):
            cols = slice(g * gw, (g + 1) * gw)
            bg = bm[rows, g * SSD_STATE:(g + 1) * SSD_STATE].astype(BF16)
            cg = cm[rows, g * SSD_STATE:(g + 1) * SSD_STATE].astype(BF16)
            cb = lax.dot_general(cg, bg, (((1,), (1,)), ((), ())), preferred_element_type=F32)
            parts = []
            for pair in range(hpg // 2):
                h0 = g * hpg + 2 * pair
                rhs = xdt_b[rows, h0 * SSD_HEAD_DIM:(h0 + 2) * SSD_HEAD_DIM]
                res = []
                for h in (h0, h0 + 1):
                    seg = a_cum[rows, dt_lo + h:dt_lo + h + 1] - a_cum_t[dt_lo + h:dt_lo + h + 1, rows]
                    lmat = jnp.exp2(jnp.where(tri, seg, NEG_BIG))
                    res.append(jnp.dot((cb * lmat).astype(BF16), rhs, preferred_element_type=F32))
                parts.append(jnp.where(lane_pair < SSD_HEAD_DIM, res[0], res[1]))
            st = states[g]
            y_off = jnp.dot(cg, st.astype(BF16), preferred_element_type=F32)
            y_rows[g].append(jnp.concatenate(parts, axis=-1) + y_off * exp_a_w[rows, cols])
            new = lax.dot_general(bg, xdec_b[rows, cols], (((0,), (0,)), ((), ())),
                                  preferred_element_type=F32)
            states[g] = st * exp_a_w[(c + 1) * lc - 1:(c + 1) * lc, cols] + new
    for g in range(SSD_GROUPS):
        state_scr[g] = states[g]

    zs = _silu(z_ref[0].astype(F32))
    outs = []
    for g in range(SSD_GROUPS):
        yg = (jnp.concatenate(y_rows[g], axis=0)
              + dskip_ref[:, g * gw:(g + 1) * gw] * xs[:, g * gw:(g + 1) * gw])
        yg = yg * zs[:, g * gw:(g + 1) * gw]
        ms = jnp.mean(yg * yg, axis=-1, keepdims=True)
        outs.append(yg * lax.rsqrt(ms + EPS) * gout_ref[:, g * gw:(g + 1) * gw])
    y_ref[0] = jnp.concatenate(outs, axis=-1).astype(y_ref.dtype)


def _ssd(xbc, z, krdt, conv_w, conv_b, dtb, alog, dskip_w, g_out, lt):
    b, s, _ = xbc.shape
    gw = SSD_D_INNER // SSD_GROUPS
    full = lambda shape: pl.BlockSpec(shape, lambda bi, i: (0,) * len(shape))
    return pl.pallas_call(
        functools.partial(_ssd_kernel, lt=lt),
        grid=(b, s // lt),
        in_specs=[pl.BlockSpec((1, lt, SSD_CONV_DIM), lambda bi, i: (bi, i, 0)),
                  pl.BlockSpec((1, lt, SSD_D_INNER), lambda bi, i: (bi, i, 0)),
                  pl.BlockSpec((1, lt, LANES), lambda bi, i: (bi, i, 0)),
                  full((SSD_CONV, SSD_CONV_DIM)), full((1, SSD_CONV_DIM)),
                  full((1, LANES)), full((1, LANES)),
                  full((1, SSD_D_INNER)), full((1, SSD_D_INNER))],
        out_specs=pl.BlockSpec((1, lt, SSD_D_INNER), lambda bi, i: (bi, i, 0)),
        out_shape=jax.ShapeDtypeStruct((b, s, SSD_D_INNER), BF16),
        scratch_shapes=[pltpu.VMEM((16, SSD_CONV_DIM), F32),
                        pltpu.VMEM((SSD_GROUPS, SSD_STATE, gw), F32)],
        compiler_params=_cparams(("parallel", "arbitrary")),
        name="ssd_scan",
    )(xbc, z, krdt, conv_w, conv_b, dtb, alog, dskip_w, g_out)


def _swap_halves(t):
    return pltpu.roll(t, LANES // 2, 1)


def _lane_sums(sq, width):
    ones = jnp.ones((sq.shape[1], width), BF16)
    return jnp.dot(sq.astype(BF16), ones, preferred_element_type=F32)


def _qkv_kernel(ql_ref, kvl_ref, krdt_ref, cos_ref, sin_ref, cost_ref, sint_ref, gql_ref, wqt_ref, gkvl_ref,
                wk_ref, wvt_ref, gqt_ref, gk_ref, q_ref, k_ref, v_ref, *, scale):
    hw = 2 * LANES
    cosv = cos_ref[...]
    sinv = sin_ref[...]
    cost = cost_ref[...]
    sint = sint_ref[...]

    ql = ql_ref[...].astype(F32)
    rq = lax.rsqrt(_lane_sums(ql * ql, LANES) * (1.0 / MLA_Q_RANK) + EPS)
    qn = (ql * jnp.concatenate([rq] * (MLA_Q_RANK // LANES), axis=1) * gql_ref[...]).astype(BF16)

    kvl = kvl_ref[...].astype(F32)
    rkv = lax.rsqrt(_lane_sums(kvl * kvl, LANES) * (1.0 / MLA_KV_RANK) + EPS)
    kvn = (kvl * jnp.concatenate([rkv] * (MLA_KV_RANK // LANES), axis=1) * gkvl_ref[...]).astype(BF16)
    tm = kvn.shape[0]
    ones_rows = (lax.broadcasted_iota(I32, (V_ROWS - MLA_V, tm), 0) == 0).astype(v_ref.dtype)

    lane = lax.broadcasted_iota(I32, (1, LANES), 1)
    kr = jnp.where((lane % (LANES // 2)) < MLA_ROPE // 2, krdt_ref[...], 0.0)
    ss_r = _lane_sums(kr * kr, LANES)
    krg = kr * gk_ref[:, LANES:]
    kr_rot = krg * cosv + _swap_halves(krg) * sinv

    nt = (((1,), (1,)), ((), ()))
    qt = lax.dot_general(wqt_ref[...], qn, nt, preferred_element_type=F32)
    kf = jnp.dot(kvn, wk_ref[...], preferred_element_type=F32)
    vt = lax.dot_general(wvt_ref[...], kvn, nt, preferred_element_type=F32)
    gq_full = jnp.concatenate([gqt_ref[...]] * (tm // LANES), axis=1)
    half = LANES // 2
    for h in range(MLA_HEADS):
        qh = qt[h * hw:(h + 1) * hw, :]
        r = lax.rsqrt(jnp.sum(qh * qh, axis=0, keepdims=True) * (1.0 / MLA_QK_DIM) + EPS)
        qs = qh * (r * scale) * gq_full
        xr = qs[LANES:, :]
        rot = xr * cost + jnp.concatenate([xr[half:, :], xr[:half, :]], axis=0) * sint
        q_ref[0, h] = jnp.concatenate([qs[:LANES, :], rot], axis=0).astype(q_ref.dtype)

        kn = kf[:, h * LANES:(h + 1) * LANES]
        rk = lax.rsqrt((_lane_sums(kn * kn, LANES) + ss_r) * (1.0 / MLA_QK_DIM) + EPS)
        k_ref[0, h, :, 0:LANES] = (kn * rk * gk_ref[:, :LANES]).astype(k_ref.dtype)
        k_ref[0, h, :, LANES:hw] = (kr_rot * rk).astype(k_ref.dtype)
        v_ref[0, h, 0:MLA_V, :] = vt[h * MLA_V:(h + 1) * MLA_V, :].astype(v_ref.dtype)
        v_ref[0, h, MLA_V:V_ROWS, :] = ones_rows


def _qkv_prep(ql, kvl, krdt, cos_t, sin_t, g_q_lat, wq_pad, g_kv_lat, wk, wvt, gq_pad, gk_pad, b, s, tm):
    nst = s // tm
    wqt = wq_pad.T
    gqt = jnp.broadcast_to(gq_pad.reshape(-1, 1), (2 * LANES, LANES))
    cos_tt, sin_tt = cos_t.T, sin_t.T
    full = lambda shape: pl.BlockSpec(shape, lambda bi, i: (0,) * len(shape))
    tok = lambda w: pl.BlockSpec((tm, w), lambda bi, i: (bi * nst + i, 0))
    hs = lambda w: pl.BlockSpec((1, MLA_HEADS, tm, w), lambda bi, i: (bi, 0, i, 0))
    return pl.pallas_call(
        functools.partial(_qkv_kernel, scale=MLA_QK_DIM ** -0.5 * math.log2(math.e)),
        grid=(b, nst),
        in_specs=[tok(MLA_Q_RANK), tok(MLA_KV_RANK), tok(LANES),
                  pl.BlockSpec((tm, LANES), lambda bi, i: (i, 0)),
                  pl.BlockSpec((tm, LANES), lambda bi, i: (i, 0)),
                  pl.BlockSpec((LANES, tm), lambda bi, i: (0, i)),
                  pl.BlockSpec((LANES, tm), lambda bi, i: (0, i)),
                  full((1, MLA_Q_RANK)), full(wqt.shape), full((1, MLA_KV_RANK)), full(wk.shape),
                  full(wvt.shape), full(gqt.shape), full((1, 2 * LANES))],
        out_specs=(pl.BlockSpec((1, MLA_HEADS, 2 * LANES, tm), lambda bi, i: (bi, 0, 0, i)), hs(2 * LANES),
                   pl.BlockSpec((1, MLA_HEADS, V_ROWS, tm), lambda bi, i: (bi, 0, 0, i))),
        out_shape=(jax.ShapeDtypeStruct((b, MLA_HEADS, 2 * LANES, s), BF16),
                   jax.ShapeDtypeStruct((b, MLA_HEADS, s, 2 * LANES), BF16),
                   jax.ShapeDtypeStruct((b, MLA_HEADS, V_ROWS, s), BF16)),
        compiler_params=_cparams(("parallel", "parallel")),
        name="qkv_prep",
    )(ql, kvl, krdt, cos_t, sin_t, cos_tt, sin_tt, g_q_lat, wqt, g_kv_lat, wk, wvt, gqt, gk_pad)


def _flash_kernel(qt_ref, k_ref, vt_ref, o_ref, m_scr, acc_scr, s_scr, *, tq):
    i = pl.program_id(2)
    tk = tq // 2
    qt = qt_ref[0, 0]
    m_scr[...] = jnp.full(m_scr.shape, NEG_BIG, F32)
    acc_scr[...] = jnp.zeros(acc_scr.shape, F32)

    def scores(j, slot, lo=0):
        start = pl.multiple_of(j * tk, tk)
        ks = k_ref[0, 0, pl.ds(start, tk), :]
        s_scr[slot, :, lo:] = jnp.dot(ks, qt[:, lo:], preferred_element_type=F32)

    def softmax_pv(j, slot, masked, lo=0):
        start = pl.multiple_of(j * tk, tk)
        vt = vt_ref[0, 0, :, pl.ds(start, tk)]
        st = s_scr[slot, :, lo:]
        if masked:
            kc = lax.broadcasted_iota(I32, (tk, tk), 0) // CHUNK
            qc = lax.broadcasted_iota(I32, (tk, tk), 1) // CHUNK
            diag = jnp.where(kc <= qc, st[:, :tk], NEG_BIG)
            st = diag if st.shape[1] == tk else jnp.concatenate([diag, st[:, tk:]], axis=1)
        m_prev = m_scr[:, lo:]
        m_new = jnp.maximum(m_prev, jnp.max(st, axis=0, keepdims=True))
        alpha = jnp.exp2(m_prev - m_new)
        pt = jnp.exp2(st - m_new)
        acc_scr[:, lo:] = alpha * acc_scr[:, lo:] + jnp.dot(vt, pt.astype(BF16), preferred_element_type=F32)
        m_scr[:, lo:] = m_new

    scores(0, 0)

    def body(jj, carry):
        j = 2 * jj
        scores(j + 1, 1)
        softmax_pv(j, 0, False)
        scores(j + 2, 0)
        softmax_pv(j + 1, 1, False)
        return carry

    lax.fori_loop(0, i, body, 0)
    scores(2 * i + 1, 1, lo=tk)
    softmax_pv(2 * i, 0, True)
    softmax_pv(2 * i + 1, 1, True, lo=tk)

    o_ref[0] = (acc_scr[0:MLA_V, :] / acc_scr[MLA_V:MLA_V + 1, :]).T.astype(o_ref.dtype)


def _flash(qt, k, v, tq):
    b, nh, s, _ = k.shape
    return pl.pallas_call(
        functools.partial(_flash_kernel, tq=tq),
        grid=(b, nh, s // tq),
        in_specs=[pl.BlockSpec((1, 1, qt.shape[2], tq), lambda bi, h, i: (bi, h, 0, i)),
                  pl.BlockSpec((1, 1, s, k.shape[-1]), lambda bi, h, i: (bi, h, 0, 0)),
                  pl.BlockSpec((1, 1, V_ROWS, s), lambda bi, h, i: (bi, h, 0, 0))],
        out_specs=pl.BlockSpec((1, tq, MLA_V), lambda bi, h, i: (bi, i, h)),
        out_shape=jax.ShapeDtypeStruct((b, s, nh * MLA_V), BF16),
        scratch_shapes=[pltpu.VMEM((1, tq), F32), pltpu.VMEM((V_ROWS, tq), F32),
                        pltpu.VMEM((2, tq // 2, tq), F32)],
        compiler_params=_cparams(("parallel", "parallel", "arbitrary")),
        name="flash_attn",
    )(qt, k, v)


def _merge_kernel(x_ref, ys_ref, om_ref, gate_ref, wss_ref, wml_ref, wo_ref, gffn_ref, wrt_ref, brt_ref,
                  x1_ref, h2_ref, idx_ref, gates_ref, rank_ref, cnt_ref, *, tm):
    d = x_ref.shape[-1]
    step = pl.program_id(0)

    @pl.when(step == 0)
    def _():
        cnt_ref[...] = jnp.zeros_like(cnt_ref)

    y1 = jnp.dot(ys_ref[...], wss_ref[...], preferred_element_type=F32)
    y2 = jnp.dot(om_ref[...], wml_ref[...], preferred_element_type=F32)
    g = gate_ref[...].astype(F32)
    merged = (g[:, :d] * y1 + g[:, d:] * y2).astype(BF16)
    x1 = x_ref[...] + jnp.dot(merged, wo_ref[...], preferred_element_type=F32)
    x1_ref[...] = x1
    ms = jnp.mean(x1 * x1, axis=-1, keepdims=True)
    h2 = x1 * lax.rsqrt(ms + EPS) * gffn_ref[...]
    h2_ref[...] = h2.astype(h2_ref.dtype)

    w_hi, w_lo = _bf16_pieces(wrt_ref[...], 2)
    h_hi, h_lo = _bf16_pieces(h2, 2)
    nt = (((1,), (1,)), ((), ()))
    logits = (lax.dot_general(w_hi, h_hi, nt, preferred_element_type=F32)
              + lax.dot_general(w_hi, h_lo, nt, preferred_element_type=F32)
              + lax.dot_general(w_lo, h_hi, nt, preferred_element_type=F32)) + brt_ref[...]
    eid = lax.broadcasted_iota(I32, (N_EXPERTS, tm), 0)
    cur = logits
    onehot = jnp.zeros((N_EXPERTS, tm), F32)
    vals, sels = [], []
    for k in range(TOP_K):
        mx = jnp.max(cur, axis=0, keepdims=True)
        idx = jnp.min(jnp.where(cur == mx, eid, N_EXPERTS), axis=0, keepdims=True)
        sel = eid == idx
        vals.append(mx)
        sels.append(sel)
        idx_ref[k:k + 1, :] = idx
        cur = jnp.where(sel, -jnp.inf, cur)
        onehot = onehot + sel.astype(F32)
    es = [jnp.exp(vk - vals[0]) for vk in vals]
    den = es[0] + es[1] + es[2] + es[3]
    for k in range(TOP_K):
        gates_ref[k:k + 1, :] = es[k] / den

    r = lax.broadcasted_iota(I32, (tm, tm), 0)
    c = lax.broadcasted_iota(I32, (tm, tm), 1)
    before = ((r < c) & ((r // MOE_SUB) == (c // MOE_SUB))).astype(BF16)
    prefix = jnp.dot(onehot.astype(BF16), before, preferred_element_type=F32)
    for k in range(TOP_K):
        rank_ref[k:k + 1, :] = jnp.sum(jnp.where(sels[k], prefix, 0.0), axis=0, keepdims=True).astype(I32)
    lane = lax.broadcasted_iota(I32, (N_EXPERTS, LANES), 1)
    cnt = cnt_ref[...]
    for g in range(tm // MOE_SUB):
        c_g = jnp.sum(onehot[:, g * MOE_SUB:(g + 1) * MOE_SUB], axis=1, keepdims=True)
        cnt = jnp.where(lane == step * (tm // MOE_SUB) + g, c_g, cnt)
    cnt_ref[...] = cnt


def _merge(x2, y_ssd, o_mla, gate, w_ss, w_ml, w_o, g_ffn, w_rt, b_rt, tm):
    t, d = x2.shape
    full = lambda shape: pl.BlockSpec(shape, lambda i: (0,) * len(shape))
    tok = lambda w: pl.BlockSpec((tm, w), lambda i: (i, 0))
    sel = pl.BlockSpec((TOP_K, tm), lambda i: (0, i))
    return pl.pallas_call(
        functools.partial(_merge_kernel, tm=tm),
        grid=(t // tm,),
        in_specs=[tok(d), tok(d), tok(d), tok(2 * d), full((d, d)), full((d, d)), full((d, d)),
                  full((1, d)), full((N_EXPERTS, d)), full((N_EXPERTS, 1))],
        out_specs=(tok(d), tok(d), sel, sel, sel, full((N_EXPERTS, LANES))),
        out_shape=(jax.ShapeDtypeStruct((t, d), F32), jax.ShapeDtypeStruct((t, d), BF16),
                   jax.ShapeDtypeStruct((TOP_K, t), I32), jax.ShapeDtypeStruct((TOP_K, t), F32),
                   jax.ShapeDtypeStruct((TOP_K, t), I32), jax.ShapeDtypeStruct((N_EXPERTS, LANES), F32)),
        compiler_params=_cparams(("arbitrary",)),
        name="merge_route",
    )(x2, y_ssd, o_mla, gate, w_ss, w_ml, w_o, g_ffn, w_rt, b_rt)


def _excl_cumsum_rows(v):
    eid = lax.broadcasted_iota(I32, v.shape, 0)
    out = jnp.zeros(v.shape, F32)
    for e in range(N_EXPERTS - 1):
        out = out + jnp.where(eid > e, v[e:e + 1, :], 0.0)
    return out


def _tables_kernel(cnt_ref, idx_ref, rank_ref, slot_ref, seg_ref, loff_ref, gdst_ref, tail_ref):
    t = idx_ref.shape[1]
    cnt = cnt_ref[...]
    seg = jnp.ceil(cnt * (1.0 / SEG_ALIGN)) * SEG_ALIGN
    loff = _excl_cumsum_rows(seg)
    r = lax.broadcasted_iota(I32, (LANES, LANES), 0)
    c = lax.broadcasted_iota(I32, (LANES, LANES), 1)
    run = jnp.dot((seg * (1.0 / SEG_ALIGN)).astype(BF16), (r < c).astype(BF16),
                  preferred_element_type=F32) * SEG_ALIGN
    tot = jnp.sum(seg, axis=1, keepdims=True)
    padded = jnp.broadcast_to(jnp.ceil(tot * (1.0 / MOE_BLOCK)) * MOE_BLOCK, (N_EXPERTS, LANES))
    start = _excl_cumsum_rows(padded)
    end = start + padded
    seg_ref[...] = seg.astype(I32)
    loff_ref[...] = loff.astype(I32)
    gdst_ref[...] = (start + run).astype(I32)

    gr = lax.broadcasted_iota(I32, (LANES, t), 0)
    gc = lax.broadcasted_iota(I32, (LANES, t), 1) // MOE_SUB
    loff_tok = jnp.dot((loff * (1.0 / SEG_ALIGN)).astype(BF16), (gr == gc).astype(BF16),
                       preferred_element_type=F32) * SEG_ALIGN
    idx = idx_ref[...]
    slot = rank_ref[...]
    for e in range(N_EXPERTS):
        slot = slot + jnp.where(idx == e, loff_tok[e:e + 1, :].astype(I32), 0)
    slot_ref[...] = slot

    lane = lax.broadcasted_iota(I32, (N_EXPERTS, LANES), 1)
    total = jnp.max(end, axis=0, keepdims=True)
    tail = jnp.where(lane == 0, start + tot, jnp.where(lane == 1, end, jnp.where(lane == 3, start, total)))
    tail_ref[...] = tail.astype(I32)


def _tables(cnt, idx_t, rank_t):
    t = idx_t.shape[1]
    tab = jax.ShapeDtypeStruct((N_EXPERTS, LANES), I32)
    return pl.pallas_call(
        _tables_kernel,
        out_shape=(jax.ShapeDtypeStruct((TOP_K, t), I32), tab, tab, tab, tab),
        compiler_params=pltpu.CompilerParams(vmem_limit_bytes=VMEM_LIMIT),
        name="route_tables",
    )(cnt, idx_t, rank_t)


def _pow2_sizes(limit):
    size = SEG_ALIGN
    while size * 2 <= limit:
        size *= 2
    sizes = []
    while size >= SEG_ALIGN:
        sizes.append(size)
        size //= 2
    return tuple(sizes)


def _piece_copies(src_ref, src_off, dst_ref, dst_off, n, limit, sem, wait=False):
    off = 0
    for size in _pow2_sizes(limit):
        take = n & size

        @pl.when(take != 0)
        def _(off=off, size=size):
            cp = pltpu.make_async_copy(
                src_ref.at[pl.ds(pl.multiple_of(src_off + off, SEG_ALIGN), size), :],
                dst_ref.at[pl.ds(pl.multiple_of(dst_off + off, SEG_ALIGN), size), :], sem)
            if wait:
                cp.wait()
            else:
                cp.start()

        off = off + take


def _dispatch_kernel(seg_ref, loff_ref, gdst_ref, taillo_ref, tailhi_ref, used_ref,
                     h_ref, slot_ref, xs_ref, loc_scr, zero_scr, sem, zsem, *, nblk):
    i = pl.program_id(0)
    par = i % 2

    @pl.when(i == 0)
    def _():
        nblk_all = xs_ref.shape[0] // MOE_BLOCK
        zero_scr[...] = jnp.zeros_like(zero_scr)

        def blk_copy(b):
            start = pl.multiple_of(b * MOE_BLOCK, MOE_BLOCK)
            return pltpu.make_async_copy(zero_scr, xs_ref.at[pl.ds(start, MOE_BLOCK), :], zsem)

        def issue_blk(b, carry):
            blk_copy(b).start()
            return carry

        def drain_blk(b, carry):
            blk_copy(b).wait()
            return carry

        lax.fori_loop(used_ref[0], nblk_all, issue_blk, 0)
        lax.fori_loop(used_ref[0], nblk_all, drain_blk, 0)

        def tails(wait):
            def body(e, carry):
                lo = taillo_ref[e]
                _piece_copies(zero_scr, 0, xs_ref, lo, tailhi_ref[e] - lo, MOE_BLOCK - 1, zsem, wait=wait)
                return carry
            return body

        lax.fori_loop(0, N_EXPERTS, tails(False), 0)
        lax.fori_loop(0, N_EXPERTS, tails(True), 0)

    slots = slot_ref[...]
    rid = lax.broadcasted_iota(I32, (LOCAL_SLOTS, MOE_SUB), 0)
    hit = rid == slots[0:1, :]
    for k in range(1, TOP_K):
        hit = hit | (rid == slots[k:k + 1, :])
    loc_scr[par] = jnp.dot(hit.astype(BF16), h_ref[...], preferred_element_type=F32).astype(ROW_DTYPE)

    loc = loc_scr.at[par]

    def seg_body(e, total):
        n = seg_ref[e * LANES + i]
        _piece_copies(loc, loff_ref[e * LANES + i], xs_ref, gdst_ref[e * LANES + i], n, MOE_SUB, sem.at[par])
        return total + n

    total = lax.fori_loop(0, N_EXPERTS, seg_body, 0)
    spare = (nblk + par * (LOCAL_SLOTS // MOE_BLOCK)) * MOE_BLOCK
    _piece_copies(loc, total, xs_ref, spare + total, LOCAL_SLOTS - total, LOCAL_SLOTS, sem.at[par])

    def wait_step(p):
        pltpu.make_async_copy(loc_scr.at[p], xs_ref.at[pl.ds(0, LOCAL_SLOTS), :], sem.at[p]).wait()

    @pl.when(i > 0)
    def _():
        wait_step(1 - par)

    @pl.when(i == pl.num_programs(0) - 1)
    def _():
        wait_step(par)


def _dispatch(seg, loff, gdst, tail_lo, tail_hi, used, h2, slot_t, nblk):
    t, d = h2.shape
    rows = (nblk + 2 * (LOCAL_SLOTS // MOE_BLOCK)) * MOE_BLOCK
    return pl.pallas_call(
        functools.partial(_dispatch_kernel, nblk=nblk),
        grid_spec=pltpu.PrefetchScalarGridSpec(
            num_scalar_prefetch=6,
            grid=(t // MOE_SUB,),
            in_specs=[pl.BlockSpec((MOE_SUB, d), lambda i, *_: (i, 0)),
                      pl.BlockSpec((TOP_K, MOE_SUB), lambda i, *_: (0, i))],
            out_specs=pl.BlockSpec(memory_space=pl.ANY),
            scratch_shapes=[pltpu.VMEM((2, LOCAL_SLOTS, d), ROW_DTYPE), pltpu.VMEM((MOE_BLOCK, d), ROW_DTYPE),
                            pltpu.SemaphoreType.DMA((2,)), pltpu.SemaphoreType.DMA(())]),
        out_shape=jax.ShapeDtypeStruct((rows, d), ROW_DTYPE),
        compiler_params=_cparams(("arbitrary",)),
        name="moe_dispatch",
    )(seg, loff, gdst, tail_lo, tail_hi, used, h2, slot_t)


def _expert_kernel(lo_ref, hi_ref, used_ref, x_ref, wgu_ref, bgu_ref, wd_ref, bd_ref, y_ref,
                   wgu_scr, wd_scr, xbuf, ybuf, xsem, ysem):
    e = pl.program_id(0)
    ne = pl.num_programs(0)
    ff = wd_ref.shape[1]
    big = 2 * MOE_BLOCK

    def layout(ex):
        first = lo_ref[ex] // MOE_BLOCK
        n = (hi_ref[ex] - lo_ref[ex]) // MOE_BLOCK
        return first, n // 2, n // 2 + n % 2

    cur = layout(e)
    first, npair, items = cur

    def x_copy(lay, k, rows):
        start = pl.multiple_of((lay[0] + 2 * k) * MOE_BLOCK, MOE_BLOCK)
        return pltpu.make_async_copy(x_ref.at[pl.ds(start, rows), :], xbuf.at[k % 2, pl.ds(0, rows), :],
                                     xsem.at[k % 2])

    def y_copy(k, rows):
        start = pl.multiple_of((first + 2 * k) * MOE_BLOCK, MOE_BLOCK)
        return pltpu.make_async_copy(ybuf.at[k % 2, pl.ds(0, rows), :], y_ref.at[pl.ds(start, rows), :],
                                     ysem.at[k % 2])

    def start_x(lay, k):
        @pl.when(k < lay[1])
        def _():
            x_copy(lay, k, big).start()

        @pl.when((k == lay[1]) & (k < lay[2]))
        def _():
            x_copy(lay, k, MOE_BLOCK).start()

    def compute(k, rows):
        slot = k % 2
        xb = xbuf[slot, 0:rows, :].astype(BF16)
        gu = jnp.dot(xb, wgu_scr[...], preferred_element_type=F32) + bgu_ref[0]
        glu = jnp.minimum(gu[:, :ff], SWIGLU_LIMIT)
        lin = jnp.clip(gu[:, ff:], -SWIGLU_LIMIT, SWIGLU_LIMIT)
        act = glu * _sigmoid(SWIGLU_ALPHA * glu) * (lin + 1.0)
        y = jnp.dot(act.astype(BF16), wd_scr[...], preferred_element_type=F32) + bd_ref[0]
        ybuf[slot, 0:rows, :] = y.astype(ybuf.dtype)

    def item(k, rows):
        x_copy(cur, k, rows).wait()

        @pl.when(k >= 1)
        def _():
            start_x(cur, k + 1)

        @pl.when(k >= 2)
        def _():
            y_copy(k - 2, big).wait()

        compute(k, rows)
        y_copy(k, rows).start()

    @pl.when(e == 0)
    def _():
        start_x(cur, 0)
        start_x(cur, 1)

    @pl.when(items > 0)
    def _():
        wgu_scr[...] = wgu_ref[0].astype(BF16)
        wd_scr[...] = wd_ref[0].astype(BF16)

    def body(k, carry):
        item(k, big)
        return carry

    lax.fori_loop(0, npair, body, 0)

    @pl.when(items > npair)
    def _():
        item(npair, MOE_BLOCK)

    @pl.when(items >= 2)
    def _():
        y_copy(items - 2, big).wait()

    @pl.when(items > npair)
    def _():
        y_copy(items - 1, MOE_BLOCK).wait()

    @pl.when((items == npair) & (items >= 1))
    def _():
        y_copy(items - 1, big).wait()

    @pl.when(e + 1 < ne)
    def _():
        nxt = layout(jnp.minimum(e + 1, ne - 1))
        start_x(nxt, 0)
        start_x(nxt, 1)

    @pl.when(e == pl.num_programs(0) - 1)
    def _():
        nblk_all = y_ref.shape[0] // MOE_BLOCK
        ybuf[0] = jnp.zeros(ybuf.shape[1:], ybuf.dtype)

        def zero_copy(b):
            start = pl.multiple_of(b * MOE_BLOCK, MOE_BLOCK)
            return pltpu.make_async_copy(ybuf.at[0, pl.ds(0, MOE_BLOCK), :],
                                         y_ref.at[pl.ds(start, MOE_BLOCK), :], ysem.at[0])

        def issue(b, carry):
            zero_copy(b).start()
            return carry

        def drain(b, carry):
            zero_copy(b).wait()
            return carry

        lax.fori_loop(used_ref[0], nblk_all, issue, 0)
        lax.fori_loop(used_ref[0], nblk_all, drain, 0)


def _experts(row_lo, row_hi, used, xs, w_gate_up, b_gate_up, w_down, b_down):
    rows, d = xs.shape
    ne, _, ff2 = w_gate_up.shape
    ff = ff2 // 2
    return pl.pallas_call(
        _expert_kernel,
        grid_spec=pltpu.PrefetchScalarGridSpec(
            num_scalar_prefetch=3,
            grid=(ne,),
            in_specs=[pl.BlockSpec(memory_space=pl.ANY),
                      pl.BlockSpec((1, d, ff2), lambda e, *_: (e, 0, 0)),
                      pl.BlockSpec((1, 1, ff2), lambda e, *_: (e, 0, 0)),
                      pl.BlockSpec((1, ff, d), lambda e, *_: (e, 0, 0)),
                      pl.BlockSpec((1, 1, d), lambda e, *_: (e, 0, 0))],
            out_specs=pl.BlockSpec(memory_space=pl.ANY),
            scratch_shapes=[pltpu.VMEM((d, ff2), BF16), pltpu.VMEM((ff, d), BF16),
                            pltpu.VMEM((2, 2 * MOE_BLOCK, d), ROW_DTYPE),
                            pltpu.VMEM((2, 2 * MOE_BLOCK, d), ROW_DTYPE),
                            pltpu.SemaphoreType.DMA((2,)), pltpu.SemaphoreType.DMA((2,))]),
        out_shape=jax.ShapeDtypeStruct((rows, d), ROW_DTYPE),
        compiler_params=_cparams(("arbitrary",)),
        name="moe_experts",
    )(row_lo, row_hi, used, xs, w_gate_up, b_gate_up.reshape(ne, 1, ff2), w_down, b_down.reshape(ne, 1, d))


def _combine_kernel(seg_ref, loff_ref, gdst_ref, x1_ref, slot_ref, gates_ref, ys_ref, o_ref, loc_scr, sem):
    i = pl.program_id(0)
    par = i % 2

    def fetch(g, p):
        loc = loc_scr.at[p]

        def seg_body(e, total):
            n = seg_ref[e * LANES + g]
            _piece_copies(ys_ref, gdst_ref[e * LANES + g], loc, loff_ref[e * LANES + g], n, MOE_SUB, sem.at[p])
            return total + n

        total = lax.fori_loop(0, N_EXPERTS, seg_body, 0)
        _piece_copies(ys_ref, total, loc, total, LOCAL_SLOTS - total, LOCAL_SLOTS, sem.at[p])

    @pl.when(i == 0)
    def _():
        fetch(0, 0)

    @pl.when(i + 1 < pl.num_programs(0))
    def _():
        fetch(i + 1, 1 - par)

    pltpu.make_async_copy(ys_ref.at[pl.ds(0, LOCAL_SLOTS), :], loc_scr.at[par], sem.at[par]).wait()

    slots = slot_ref[...]
    g = gates_ref[...]
    rid = lax.broadcasted_iota(I32, (MOE_SUB, LOCAL_SLOTS), 1)
    gmat = jnp.where(rid == slots[:, 0:1], g[:, 0:1], 0.0)
    for k in range(1, TOP_K):
        gmat = gmat + jnp.where(rid == slots[:, k:k + 1], g[:, k:k + 1], 0.0)
    o_ref[...] = x1_ref[...] + jnp.dot(gmat.astype(BF16), loc_scr[par].astype(BF16),
                                       preferred_element_type=F32)


def _combine(seg, loff, gdst, x1, slot_tk, gates_tk, ys):
    t, d = x1.shape
    tok = lambda w: pl.BlockSpec((MOE_SUB, w), lambda i, *_: (i, 0))
    return pl.pallas_call(
        _combine_kernel,
        grid_spec=pltpu.PrefetchScalarGridSpec(
            num_scalar_prefetch=3,
            grid=(t // MOE_SUB,),
            in_specs=[tok(d), tok(TOP_K), tok(TOP_K), pl.BlockSpec(memory_space=pl.ANY)],
            out_specs=tok(d),
            scratch_shapes=[pltpu.VMEM((2, LOCAL_SLOTS, d), ROW_DTYPE), pltpu.SemaphoreType.DMA((2,))]),
        out_shape=jax.ShapeDtypeStruct((t, d), F32),
        compiler_params=_cparams(("arbitrary",)),
        name="moe_combine",
    )(seg, loff, gdst, x1, slot_tk, gates_tk, ys)


def _rope_tables(s):
    pos = np.arange(s, dtype=np.float64)
    inv = ROPE_BASE ** (-np.arange(0, MLA_ROPE, 2, dtype=np.float64) / MLA_ROPE)
    ang = pos[:, None] * inv[None, :]
    cos, sin = jnp.asarray(np.cos(ang), F32), jnp.asarray(np.sin(ang), F32)
    return _spread_rope(jnp.concatenate([cos, cos], axis=-1)), _spread_rope(jnp.concatenate([-sin, sin], axis=-1))


def _spread_rope(w, gap=None):
    half = MLA_ROPE // 2
    zeros = jnp.zeros(w.shape[:-1] + (half,), w.dtype)
    return jnp.concatenate([w[..., :half], zeros if gap is None else gap, w[..., half:], zeros], axis=-1)


def _pad_heads(w):
    lead = w.shape[:-1]
    w = w.reshape(lead + (MLA_HEADS, MLA_QK_DIM))
    w = jnp.concatenate([w[..., :MLA_NOPE], _spread_rope(w[..., MLA_NOPE:])], axis=-1)
    return w.reshape(lead + (MLA_HEADS * 2 * LANES,))


def _layer(x, g_mix, w_in, conv_w, conv_b, dt_bias, a_log, d_skip, g_ssd_out, w_ssd_out,
           g_q_lat, w_q_up, g_kv_lat, w_kv_up, g_qk_q, g_qk_k, w_mla_out, w_o,
           g_ffn, w_router, b_router, w_gate_up, b_gate_up, w_down, b_down):
    b, s, d = x.shape
    t = b * s
    x2 = x.reshape(t, d)

    off_xbc = SSD_D_INNER
    off_dt = off_xbc + SSD_CONV_DIM
    off_ql = off_dt + SSD_HEADS
    off_kvl = off_ql + MLA_Q_RANK
    off_kr = off_kvl + MLA_KV_RANK
    off_gate = off_kr + MLA_ROPE
    dt_gap = jnp.pad(w_in[:, off_dt:off_ql], ((0, 0), (0, MLA_ROPE // 2 - SSD_HEADS)))
    w_streams = [w_in[:, :off_xbc], w_in[:, off_xbc:off_dt], w_in[:, off_ql:off_kvl], w_in[:, off_kvl:off_kr],
                 _spread_rope(w_in[:, off_kr:off_gate], gap=dt_gap), w_in[:, off_gate:]]

    tiles = _tile_sizes(s)
    z, xbc, ql, kvl, krdt, gate = _in_proj(x2, g_mix, [w.astype(BF16) for w in w_streams], tiles.tokens)

    lane_pad = lambda vec: jnp.pad(vec, (DT_LO, LANES - DT_LO - SSD_HEADS)).reshape(1, LANES)
    lt = tiles.ssd
    y_ssd = _ssd(xbc.reshape(b, s, SSD_CONV_DIM), z.reshape(b, s, SSD_D_INNER), krdt.reshape(b, s, LANES),
                 conv_w, conv_b.reshape(1, -1), lane_pad(dt_bias), lane_pad(a_log),
                 jnp.repeat(d_skip, SSD_HEAD_DIM).reshape(1, -1), g_ssd_out.reshape(1, -1), lt)

    cos_t, sin_t = _rope_tables(s)
    wq_pad = _pad_heads(w_q_up).astype(BF16)
    spread_gain = lambda g: jnp.concatenate([g[:MLA_NOPE], _spread_rope(g[MLA_NOPE:])]).reshape(1, -1)
    gq_pad = spread_gain(g_qk_q)
    gk_pad = spread_gain(g_qk_k)
    tq = tiles.tokens
    w_kv_h = w_kv_up.reshape(MLA_KV_RANK, MLA_HEADS, MLA_NOPE + MLA_V)
    wk = w_kv_h[:, :, :MLA_NOPE].reshape(MLA_KV_RANK, MLA_HEADS * MLA_NOPE).astype(BF16)
    wvt = w_kv_h[:, :, MLA_NOPE:].reshape(MLA_KV_RANK, MLA_HEADS * MLA_V).T.astype(BF16)
    q, k, v = _qkv_prep(ql, kvl, krdt, cos_t, sin_t, g_q_lat.reshape(1, -1), wq_pad,
                        g_kv_lat.reshape(1, -1), wk, wvt, gq_pad, gk_pad, b, s, tq)
    o_mla = _flash(q, k, v, tiles.queries)

    x1, h2, idx_t, gates_t, rank_t, cnt = _merge(
        x2, y_ssd.reshape(t, d), o_mla.reshape(t, d), gate,
        w_ssd_out.astype(BF16), w_mla_out.astype(BF16), w_o.astype(BF16),
        g_ffn.reshape(1, -1), w_router.T, b_router.reshape(-1, 1), tiles.tokens)

    nsub = t // MOE_SUB
    assert t % MOE_SUB == 0 and nsub <= LANES
    cap = t * TOP_K + nsub * N_EXPERTS * (SEG_ALIGN - 1) + N_EXPERTS * (MOE_BLOCK - 1)
    nblk = -(-cap // MOE_BLOCK)
    slot_t, seg, loff, gdst, tail = _tables(cnt, idx_t, rank_t)
    used = (tail[0:1, 2] // MOE_BLOCK).astype(I32)
    seg, loff, gdst = seg.reshape(-1), loff.reshape(-1), gdst.reshape(-1)
    xs = _dispatch(seg, loff, gdst, tail[:, 0], tail[:, 1], used, h2, slot_t, nblk)
    ys = _experts(tail[:, 3], tail[:, 1], used, xs, w_gate_up, b_gate_up, w_down, b_down)
    out = _combine(seg, loff, gdst, x1, slot_t.T, gates_t.T, ys)
    return out.reshape(b, s, d)


def kernel(x, g_mix, w_in, conv_w, conv_b, dt_bias, a_log, d_skip, g_ssd_out, w_ssd_out, g_q_lat, w_q_up, g_kv_lat, w_kv_up, g_qk_q, g_qk_k, w_mla_out, w_o, g_ffn, w_router, b_router, w_gate_up, b_gate_up, w_down, b_down):
    params = (g_mix, w_in, conv_w, conv_b, dt_bias, a_log, d_skip, g_ssd_out, w_ssd_out, g_q_lat, w_q_up,
              g_kv_lat, w_kv_up, g_qk_q, g_qk_k, w_mla_out, w_o, g_ffn, w_router, b_router,
              w_gate_up, b_gate_up, w_down, b_down)
    for l in range(g_mix.shape[0]):
        x = _layer(x, *(p[l] for p in params))
    return x
```

```python
import functools
import math
from typing import NamedTuple

import jax
import jax.numpy as jnp
import numpy as np
from jax import lax
from jax.experimental import pallas as pl
from jax.experimental.pallas import tpu as pltpu

F32 = jnp.float32
BF16 = jnp.bfloat16
I32 = jnp.int32

EPS = 1e-6
CHUNK = 64

SSD_HEADS = 16
SSD_HEAD_DIM = 64
SSD_GROUPS = 2
SSD_STATE = 128
SSD_CONV = 4
SSD_D_INNER = SSD_HEADS * SSD_HEAD_DIM
SSD_CONV_DIM = SSD_D_INNER + 2 * SSD_GROUPS * SSD_STATE

MLA_HEADS = 8
MLA_Q_RANK = 384
MLA_KV_RANK = 256
MLA_NOPE = 128
MLA_ROPE = 64
MLA_QK_DIM = MLA_NOPE + MLA_ROPE
MLA_V = 128
V_ROWS = MLA_V + 16
DT_LO = MLA_ROPE // 2
SSD_SCAN_CHUNK = 128
ROPE_BASE = 10000.0

N_EXPERTS = 32
TOP_K = 4
SWIGLU_ALPHA = 1.702
SWIGLU_LIMIT = 7.0

LANES = 128
VMEM_LIMIT = 56 * 1024 * 1024
NEG_BIG = -1e30
LOG2E = math.log2(math.e)

MOE_BLOCK = 256
MOE_SUB = 256
ROW_DTYPE = F32
SEG_ALIGN = 8 * 4 // jnp.dtype(ROW_DTYPE).itemsize
LOCAL_SLOTS = -(-(TOP_K * MOE_SUB + N_EXPERTS * (SEG_ALIGN - 1)) // MOE_BLOCK) * MOE_BLOCK


class _Tiles(NamedTuple):
    tokens: int
    ssd: int
    queries: int


def _tile_sizes(seq):
    return _Tiles(tokens=min(512, seq), ssd=min(256, seq), queries=min(1024, seq))


def _cparams(semantics, **kw):
    return pltpu.CompilerParams(dimension_semantics=semantics,
                                vmem_limit_bytes=VMEM_LIMIT, **kw)


def _sigmoid(v):
    return 1.0 / (1.0 + jnp.exp(-v))


def _silu(v):
    return v * _sigmoid(v)


def _bf16_pieces(t, n):
    pieces = []
    for _ in range(n - 1):
        p = t.astype(BF16)
        pieces.append(p)
        t = t - p.astype(F32)
    pieces.append(t.astype(BF16))
    return pieces


def _inproj_kernel(x_ref, g_ref, wz_ref, wxbc_ref, wql_ref, wkvl_ref, wkrdt_ref, wgate_ref,
                   z_ref, xbc_ref, ql_ref, kvl_ref, krdt_ref, gate_ref):
    x = x_ref[...]
    ms = jnp.mean(x * x, axis=-1, keepdims=True)
    h = (x * lax.rsqrt(ms + EPS) * g_ref[...]).astype(BF16)
    streams = ((wz_ref, z_ref), (wxbc_ref, xbc_ref), (wql_ref, ql_ref), (wkvl_ref, kvl_ref),
               (wkrdt_ref, krdt_ref), (wgate_ref, gate_ref))
    for w_ref, ref in streams:
        p = jnp.dot(h, w_ref[...], preferred_element_type=F32)
        if ref is gate_ref:
            p = _sigmoid(p)
        ref[...] = p.astype(ref.dtype)


def _in_proj(x2, g_mix, weights, tm):
    t, d = x2.shape
    widths = tuple(w.shape[1] for w in weights)
    dts = (BF16, BF16, BF16, BF16, F32, BF16)
    out_shape = tuple(jax.ShapeDtypeStruct((t, w), dt) for w, dt in zip(widths, dts))
    out_specs = tuple(pl.BlockSpec((tm, w), lambda i: (i, 0)) for w in widths)
    return pl.pallas_call(
        _inproj_kernel,
        grid=(t // tm,),
        in_specs=[pl.BlockSpec((tm, d), lambda i: (i, 0)),
                  pl.BlockSpec((1, d), lambda i: (0, 0))]
                 + [pl.BlockSpec((d, w), lambda i: (0, 0)) for w in widths],
        out_specs=out_specs,
        out_shape=out_shape,
        compiler_params=_cparams(("parallel",)),
        name="in_proj",
    )(x2, g_mix.reshape(1, d), *weights)


def _ssd_kernel(xbc_ref, z_ref, krdt_ref, convw_ref, convb_ref, dtb_ref, alog_ref, dskip_ref, gout_ref,
                y_ref, xext_scr, state_scr, *, lt):
    i = pl.program_id(1)
    halo = 8
    dt_lo = DT_LO
    gw = SSD_D_INNER // SSD_GROUPS
    hpg = SSD_HEADS // SSD_GROUPS

    @pl.when(i == 0)
    def _():
        state_scr[...] = jnp.zeros_like(state_scr)
        xext_scr[0:halo, :] = jnp.zeros((halo, SSD_CONV_DIM), F32)

    xin = xbc_ref[0]
    xext_scr[halo:2 * halo, :] = xin[0:halo, :].astype(F32)
    srow = lax.broadcasted_iota(I32, (lt, lt), 0)
    scol = lax.broadcasted_iota(I32, (lt, lt), 1)
    acc = convb_ref[...] + convw_ref[SSD_CONV - 1:SSD_CONV, :] * xin.astype(F32)
    head = jnp.broadcast_to(convb_ref[...], (halo, SSD_CONV_DIM))
    for j in range(SSD_CONV):
        shift = SSD_CONV - 1 - j
        head = head + convw_ref[j:j + 1, :] * xext_scr[pl.ds(halo - shift, halo), :]
        if shift:
            shifted = jnp.dot((srow == scol + shift).astype(BF16), xin, preferred_element_type=F32)
            acc = acc + convw_ref[j:j + 1, :] * shifted
    acc = jnp.concatenate([head, acc[halo:, :]], axis=0)
    xext_scr[0:halo, :] = xin[lt - halo:lt, :].astype(F32)
    xbc = _silu(acc)
    xs = xbc[:, :SSD_D_INNER]
    bm = xbc[:, SSD_D_INNER:SSD_D_INNER + SSD_GROUPS * SSD_STATE]
    cm = xbc[:, SSD_D_INNER + SSD_GROUPS * SSD_STATE:]

    lane = lax.broadcasted_iota(I32, (1, LANES), 1)
    head_lane = (lane >= dt_lo) & (lane < dt_lo + SSD_HEADS)
    v = krdt_ref[0] + dtb_ref[...]
    dt = jnp.maximum(v, 0.0) + jnp.log(1.0 + jnp.exp(-jnp.abs(v)))
    dt = jnp.where(head_lane, dt, 0.0)
    a = jnp.where(head_lane, -jnp.exp(alog_ref[...]) * LOG2E, 0.0)
    da = dt * a
    lc = min(SSD_SCAN_CHUNK, lt)
    nchunk = lt // lc
    row = lax.broadcasted_iota(I32, (lt, lt), 0)
    col = lax.broadcasted_iota(I32, (lt, lt), 1)
    tri_b = ((row >= col) & ((row // lc) == (col // lc))).astype(BF16)
    a_cum = jnp.zeros((lt, LANES), F32)
    for piece in _bf16_pieces(da, 3):
        a_cum = a_cum + jnp.dot(tri_b, piece, preferred_element_type=F32)
    a_cum_t = a_cum.T
    a_last = jnp.concatenate([jnp.broadcast_to(a_cum[(c + 1) * lc - 1:(c + 1) * lc, :], (lc, LANES))
                              for c in range(nchunk)], axis=0)
    exp_a = jnp.exp2(a_cum)
    dec = jnp.exp2(a_last - a_cum)
    tri = lax.broadcasted_iota(I32, (lc, lc), 0) >= lax.broadcasted_iota(I32, (lc, lc), 1)

    er = lax.broadcasted_iota(I32, (LANES, SSD_D_INNER), 0)
    ec = lax.broadcasted_iota(I32, (LANES, SSD_D_INNER), 1)
    expand = ((er - dt_lo) == (ec // SSD_HEAD_DIM)).astype(BF16)

    def widen(t, pieces):
        out = jnp.zeros((lt, SSD_D_INNER), F32)
        for piece in _bf16_pieces(t, pieces):
            out = out + jnp.dot(piece, expand, preferred_element_type=F32)
        return out

    dt_w = widen(dt, 2)
    exp_a_w = widen(exp_a, 1)
    dec_w = widen(dec, 1)
    xdt = xs * dt_w
    xdt_b = xdt.astype(BF16)
    xdec_b = (xdt * dec_w).astype(BF16)
    lane_pair = lax.broadcasted_iota(I32, (lc, LANES), 1)

    states = [state_scr[g] for g in range(SSD_GROUPS)]
    y_rows = [[] for _ in range(SSD_GROUPS)]
    for c in range(nchunk):
        rows = slice(c * lc, (c + 1) * lc)
        for g in range(SSD_GROUPS):
            cols = slice(g * gw, (g + 1) * gw)
            bg = bm[rows, g * SSD_STATE:(g + 1) * SSD_STATE].astype(BF16)
            cg = cm[rows, g * SSD_STATE:(g + 1) * SSD_STATE].astype(BF16)
            cb = lax.dot_general(cg, bg, (((1,), (1,)), ((), ())), preferred_element_type=F32)
            parts = []
            for pair in range(hpg // 2):
                h0 = g * hpg + 2 * pair
                rhs = xdt_b[rows, h0 * SSD_HEAD_DIM:(h0 + 2) * SSD_HEAD_DIM]
                res = []
                for h in (h0, h0 + 1):
                    seg = a_cum[rows, dt_lo + h:dt_lo + h + 1] - a_cum_t[dt_lo + h:dt_lo + h + 1, rows]
                    lmat = jnp.exp2(jnp.where(tri, seg, NEG_BIG))
                    res.append(jnp.dot((cb * lmat).astype(BF16), rhs, preferred_element_type=F32))
                parts.append(jnp.where(lane_pair < SSD_HEAD_DIM, res[0], res[1]))
            st = states[g]
            y_off = jnp.dot(cg, st.astype(BF16), preferred_element_type=F32)
            y_rows[g].append(jnp.concatenate(parts, axis=-1) + y_off * exp_a_w[rows, cols])
            new = lax.dot_general(bg, xdec_b[rows, cols], (((0,), (0,)), ((), ())),
                                  preferred_element_type=F32)
            states[g] = st * exp_a_w[(c + 1) * lc - 1:(c + 1) * lc, cols] + new
    for g in range(SSD_GROUPS):
        state_scr[g] = states[g]

    zs = _silu(z_ref[0].astype(F32))
    outs = []
    for g in range(SSD_GROUPS):
        yg = (jnp.concatenate(y_rows[g], axis=0)
              + dskip_ref[:, g * gw:(g + 1) * gw] * xs[:, g * gw:(g + 1) * gw])
        yg = yg * zs[:, g * gw:(g + 1) * gw]
        ms = jnp.mean(yg * yg, axis=-1, keepdims=True)
        outs.append(yg * lax.rsqrt(ms + EPS) * gout_ref[:, g * gw:(g + 1) * gw])
    y_ref[0] = jnp.concatenate(outs, axis=-1).astype(y_ref.dtype)


def _ssd(xbc, z, krdt, conv_w, conv_b, dtb, alog, dskip_w, g_out, lt):
    b, s, _ = xbc.shape
    gw = SSD_D_INNER // SSD_GROUPS
    full = lambda shape: pl.BlockSpec(shape, lambda bi, i: (0,) * len(shape))
    return pl.pallas_call(
        functools.partial(_ssd_kernel, lt=lt),
        grid=(b, s // lt),
        in_specs=[pl.BlockSpec((1, lt, SSD_CONV_DIM), lambda bi, i: (bi, i, 0)),
                  pl.BlockSpec((1, lt, SSD_D_INNER), lambda bi, i: (bi, i, 0)),
                  pl.BlockSpec((1, lt, LANES), lambda bi, i: (bi, i, 0)),
                  full((SSD_CONV, SSD_CONV_DIM)), full((1, SSD_CONV_DIM)),
                  full((1, LANES)), full((1, LANES)),
                  full((1, SSD_D_INNER)), full((1, SSD_D_INNER))],
        out_specs=pl.BlockSpec((1, lt, SSD_D_INNER), lambda bi, i: (bi, i, 0)),
        out_shape=jax.ShapeDtypeStruct((b, s, SSD_D_INNER), BF16),
        scratch_shapes=[pltpu.VMEM((16, SSD_CONV_DIM), F32),
                        pltpu.VMEM((SSD_GROUPS, SSD_STATE, gw), F32)],
        compiler_params=_cparams(("parallel", "arbitrary")),
        name="ssd_scan",
    )(xbc, z, krdt, conv_w, conv_b, dtb, alog, dskip_w, g_out)


def _swap_halves(t):
    return pltpu.roll(t, LANES // 2, 1)


def _lane_sums(sq, width):
    ones = jnp.ones((sq.shape[1], width), BF16)
    return jnp.dot(sq.astype(BF16), ones, preferred_element_type=F32)


def _qkv_kernel(ql_ref, kvl_ref, krdt_ref, cos_ref, sin_ref, cost_ref, sint_ref, gql_ref, wqt_ref, gkvl_ref,
                wk_ref, wvt_ref, gqt_ref, gk_ref, q_ref, k_ref, v_ref, *, scale):
    hw = 2 * LANES
    cosv = cos_ref[...]
    sinv = sin_ref[...]
    cost = cost_ref[...]
    sint = sint_ref[...]

    ql = ql_ref[...].astype(F32)
    rq = lax.rsqrt(_lane_sums(ql * ql, LANES) * (1.0 / MLA_Q_RANK) + EPS)
    qn = (ql * jnp.concatenate([rq] * (MLA_Q_RANK // LANES), axis=1) * gql_ref[...]).astype(BF16)

    kvl = kvl_ref[...].astype(F32)
    rkv = lax.rsqrt(_lane_sums(kvl * kvl, LANES) * (1.0 / MLA_KV_RANK) + EPS)
    kvn = (kvl * jnp.concatenate([rkv] * (MLA_KV_RANK // LANES), axis=1) * gkvl_ref[...]).astype(BF16)
    tm = kvn.shape[0]
    ones_rows = (lax.broadcasted_iota(I32, (V_ROWS - MLA_V, tm), 0) == 0).astype(v_ref.dtype)

    lane = lax.broadcasted_iota(I32, (1, LANES), 1)
    kr = jnp.where((lane % (LANES // 2)) < MLA_ROPE // 2, krdt_ref[...], 0.0)
    ss_r = _lane_sums(kr * kr, LANES)
    krg = kr * gk_ref[:, LANES:]
    kr_rot = krg * cosv + _swap_halves(krg) * sinv

    nt = (((1,), (1,)), ((), ()))
    qt = lax.dot_general(wqt_ref[...], qn, nt, preferred_element_type=F32)
    kf = jnp.dot(kvn, wk_ref[...], preferred_element_type=F32)
    vt = lax.dot_general(wvt_ref[...], kvn, nt, preferred_element_type=F32)
    gq_full = jnp.concatenate([gqt_ref[...]] * (tm // LANES), axis=1)
    half = LANES // 2
    for h in range(MLA_HEADS):
        qh = qt[h * hw:(h + 1) * hw, :]
        r = lax.rsqrt(jnp.sum(qh * qh, axis=0, keepdims=True) * (1.0 / MLA_QK_DIM) + EPS)
        qs = qh * (r * scale) * gq_full
        xr = qs[LANES:, :]
        rot = xr * cost + jnp.concatenate([xr[half:, :], xr[:half, :]], axis=0) * sint
        q_ref[0, h] = jnp.concatenate([qs[:LANES, :], rot], axis=0).astype(q_ref.dtype)

        kn = kf[:, h * LANES:(h + 1) * LANES]
        rk = lax.rsqrt((_lane_sums(kn * kn, LANES) + ss_r) * (1.0 / MLA_QK_DIM) + EPS)
        k_ref[0, h, :, 0:LANES] = (kn * rk * gk_ref[:, :LANES]).astype(k_ref.dtype)
        k_ref[0, h, :, LANES:hw] = (kr_rot * rk).astype(k_ref.dtype)
        v_ref[0, h, 0:MLA_V, :] = vt[h * MLA_V:(h + 1) * MLA_V, :].astype(v_ref.dtype)
        v_ref[0, h, MLA_V:V_ROWS, :] = ones_rows


def _qkv_prep(ql, kvl, krdt, cos_t, sin_t, g_q_lat, wq_pad, g_kv_lat, wk, wvt, gq_pad, gk_pad, b, s, tm):
    nst = s // tm
    wqt = wq_pad.T
    gqt = jnp.broadcast_to(gq_pad.reshape(-1, 1), (2 * LANES, LANES))
    cos_tt, sin_tt = cos_t.T, sin_t.T
    full = lambda shape: pl.BlockSpec(shape, lambda bi, i: (0,) * len(shape))
    tok = lambda w: pl.BlockSpec((tm, w), lambda bi, i: (bi * nst + i, 0))
    hs = lambda w: pl.BlockSpec((1, MLA_HEADS, tm, w), lambda bi, i: (bi, 0, i, 0))
    return pl.pallas_call(
        functools.partial(_qkv_kernel, scale=MLA_QK_DIM ** -0.5 * math.log2(math.e)),
        grid=(b, nst),
        in_specs=[tok(MLA_Q_RANK), tok(MLA_KV_RANK), tok(LANES),
                  pl.BlockSpec((tm, LANES), lambda bi, i: (i, 0)),
                  pl.BlockSpec((tm, LANES), lambda bi, i: (i, 0)),
                  pl.BlockSpec((LANES, tm), lambda bi, i: (0, i)),
                  pl.BlockSpec((LANES, tm), lambda bi, i: (0, i)),
                  full((1, MLA_Q_RANK)), full(wqt.shape), full((1, MLA_KV_RANK)), full(wk.shape),
                  full(wvt.shape), full(gqt.shape), full((1, 2 * LANES))],
        out_specs=(pl.BlockSpec((1, MLA_HEADS, 2 * LANES, tm), lambda bi, i: (bi, 0, 0, i)), hs(2 * LANES),
                   pl.BlockSpec((1, MLA_HEADS, V_ROWS, tm), lambda bi, i: (bi, 0, 0, i))),
        out_shape=(jax.ShapeDtypeStruct((b, MLA_HEADS, 2 * LANES, s), BF16),
                   jax.ShapeDtypeStruct((b, MLA_HEADS, s, 2 * LANES), BF16),
                   jax.ShapeDtypeStruct((b, MLA_HEADS, V_ROWS, s), BF16)),
        compiler_params=_cparams(("parallel", "parallel")),
        name="qkv_prep",
    )(ql, kvl, krdt, cos_t, sin_t, cos_tt, sin_tt, g_q_lat, wqt, g_kv_lat, wk, wvt, gqt, gk_pad)


def _flash_kernel(qt_ref, k_ref, vt_ref, o_ref, m_scr, acc_scr, s_scr, *, tq):
    i = pl.program_id(2)
    tk = tq // 2
    qt = qt_ref[0, 0]
    m_scr[...] = jnp.full(m_scr.shape, NEG_BIG, F32)
    acc_scr[...] = jnp.zeros(acc_scr.shape, F32)

    def scores(j, slot, lo=0):
        start = pl.multiple_of(j * tk, tk)
        ks = k_ref[0, 0, pl.ds(start, tk), :]
        s_scr[slot, :, lo:] = jnp.dot(ks, qt[:, lo:], preferred_element_type=F32)

    def softmax_pv(j, slot, masked, lo=0):
        start = pl.multiple_of(j * tk, tk)
        vt = vt_ref[0, 0, :, pl.ds(start, tk)]
        st = s_scr[slot, :, lo:]
        if masked:
            kc = lax.broadcasted_iota(I32, (tk, tk), 0) // CHUNK
            qc = lax.broadcasted_iota(I32, (tk, tk), 1) // CHUNK
            diag = jnp.where(kc <= qc, st[:, :tk], NEG_BIG)
            st = diag if st.shape[1] == tk else jnp.concatenate([diag, st[:, tk:]], axis=1)
        m_prev = m_scr[:, lo:]
        m_new = jnp.maximum(m_prev, jnp.max(st, axis=0, keepdims=True))
        alpha = jnp.exp2(m_prev - m_new)
        pt = jnp.exp2(st - m_new)
        acc_scr[:, lo:] = alpha * acc_scr[:, lo:] + jnp.dot(vt, pt.astype(BF16), preferred_element_type=F32)
        m_scr[:, lo:] = m_new

    scores(0, 0)

    def pair(j):
        scores(j + 1, 1)
        softmax_pv(j, 0, False)
        scores(j + 2, 0)
        softmax_pv(j + 1, 1, False)

    def body(jj, carry):
        pair(4 * jj)
        pair(4 * jj + 2)
        return carry

    lax.fori_loop(0, i // 2, body, 0)

    @pl.when(i % 2 == 1)
    def _():
        pair(2 * i - 2)

    scores(2 * i + 1, 1, lo=tk)
    softmax_pv(2 * i, 0, True)
    softmax_pv(2 * i + 1, 1, True, lo=tk)

    o_ref[0] = (acc_scr[0:MLA_V, :] / acc_scr[MLA_V:MLA_V + 1, :]).T.astype(o_ref.dtype)


def _flash(qt, k, v, tq):
    b, nh, s, _ = k.shape
    return pl.pallas_call(
        functools.partial(_flash_kernel, tq=tq),
        grid=(b, nh, s // tq),
        in_specs=[pl.BlockSpec((1, 1, qt.shape[2], tq), lambda bi, h, i: (bi, h, 0, i)),
                  pl.BlockSpec((1, 1, s, k.shape[-1]), lambda bi, h, i: (bi, h, 0, 0)),
                  pl.BlockSpec((1, 1, V_ROWS, s), lambda bi, h, i: (bi, h, 0, 0))],
        out_specs=pl.BlockSpec((1, tq, MLA_V), lambda bi, h, i: (bi, i, h)),
        out_shape=jax.ShapeDtypeStruct((b, s, nh * MLA_V), BF16),
        scratch_shapes=[pltpu.VMEM((1, tq), F32), pltpu.VMEM((V_ROWS, tq), F32),
                        pltpu.VMEM((2, tq // 2, tq), F32)],
        compiler_params=_cparams(("parallel", "parallel", "arbitrary")),
        name="flash_attn",
    )(qt, k, v)


def _merge_kernel(x_ref, ys_ref, om_ref, gate_ref, wss_ref, wml_ref, wo_ref, gffn_ref, wrt_ref, brt_ref,
                  x1_ref, h2_ref, idx_ref, gates_ref, rank_ref, cnt_ref, *, tm):
    d = x_ref.shape[-1]
    step = pl.program_id(0)

    @pl.when(step == 0)
    def _():
        cnt_ref[...] = jnp.zeros_like(cnt_ref)

    y1 = jnp.dot(ys_ref[...], wss_ref[...], preferred_element_type=F32)
    y2 = jnp.dot(om_ref[...], wml_ref[...], preferred_element_type=F32)
    g = gate_ref[...].astype(F32)
    merged = (g[:, :d] * y1 + g[:, d:] * y2).astype(BF16)
    x1 = x_ref[...] + jnp.dot(merged, wo_ref[...], preferred_element_type=F32)
    x1_ref[...] = x1
    ms = jnp.mean(x1 * x1, axis=-1, keepdims=True)
    h2 = x1 * lax.rsqrt(ms + EPS) * gffn_ref[...]
    h2_ref[...] = h2.astype(h2_ref.dtype)

    w_hi, w_lo = _bf16_pieces(wrt_ref[...], 2)
    h_hi, h_lo = _bf16_pieces(h2, 2)
    nt = (((1,), (1,)), ((), ()))
    logits = (lax.dot_general(w_hi, h_hi, nt, preferred_element_type=F32)
              + lax.dot_general(w_hi, h_lo, nt, preferred_element_type=F32)
              + lax.dot_general(w_lo, h_hi, nt, preferred_element_type=F32)) + brt_ref[...]
    eid = lax.broadcasted_iota(I32, (N_EXPERTS, tm), 0)
    cur = logits
    onehot = jnp.zeros((N_EXPERTS, tm), F32)
    vals, sels = [], []
    for k in range(TOP_K):
        mx = jnp.max(cur, axis=0, keepdims=True)
        idx = jnp.min(jnp.where(cur == mx, eid, N_EXPERTS), axis=0, keepdims=True)
        sel = eid == idx
        vals.append(mx)
        sels.append(sel)
        idx_ref[k:k + 1, :] = idx
        cur = jnp.where(sel, -jnp.inf, cur)
        onehot = onehot + sel.astype(F32)
    es = [jnp.exp(vk - vals[0]) for vk in vals]
    den = es[0] + es[1] + es[2] + es[3]
    for k in range(TOP_K):
        gates_ref[k:k + 1, :] = es[k] / den

    r = lax.broadcasted_iota(I32, (tm, tm), 0)
    c = lax.broadcasted_iota(I32, (tm, tm), 1)
    before = ((r < c) & ((r // MOE_SUB) == (c // MOE_SUB))).astype(BF16)
    prefix = jnp.dot(onehot.astype(BF16), before, preferred_element_type=F32)
    for k in range(TOP_K):
        rank_ref[k:k + 1, :] = jnp.sum(jnp.where(sels[k], prefix, 0.0), axis=0, keepdims=True).astype(I32)
    lane = lax.broadcasted_iota(I32, (N_EXPERTS, LANES), 1)
    cnt = cnt_ref[...]
    for g in range(tm // MOE_SUB):
        c_g = jnp.sum(onehot[:, g * MOE_SUB:(g + 1) * MOE_SUB], axis=1, keepdims=True)
        cnt = jnp.where(lane == step * (tm // MOE_SUB) + g, c_g, cnt)
    cnt_ref[...] = cnt


def _merge(x2, y_ssd, o_mla, gate, w_ss, w_ml, w_o, g_ffn, w_rt, b_rt, tm):
    t, d = x2.shape
    full = lambda shape: pl.BlockSpec(shape, lambda i: (0,) * len(shape))
    tok = lambda w: pl.BlockSpec((tm, w), lambda i: (i, 0))
    sel = pl.BlockSpec((TOP_K, tm), lambda i: (0, i))
    return pl.pallas_call(
        functools.partial(_merge_kernel, tm=tm),
        grid=(t // tm,),
        in_specs=[tok(d), tok(d), tok(d), tok(2 * d), full((d, d)), full((d, d)), full((d, d)),
                  full((1, d)), full((N_EXPERTS, d)), full((N_EXPERTS, 1))],
        out_specs=(tok(d), tok(d), sel, sel, sel, full((N_EXPERTS, LANES))),
        out_shape=(jax.ShapeDtypeStruct((t, d), F32), jax.ShapeDtypeStruct((t, d), BF16),
                   jax.ShapeDtypeStruct((TOP_K, t), I32), jax.ShapeDtypeStruct((TOP_K, t), F32),
                   jax.ShapeDtypeStruct((TOP_K, t), I32), jax.ShapeDtypeStruct((N_EXPERTS, LANES), F32)),
        compiler_params=_cparams(("arbitrary",)),
        name="merge_route",
    )(x2, y_ssd, o_mla, gate, w_ss, w_ml, w_o, g_ffn, w_rt, b_rt)


def _excl_cumsum_rows(v):
    eid = lax.broadcasted_iota(I32, v.shape, 0)
    out = jnp.zeros(v.shape, F32)
    for e in range(N_EXPERTS - 1):
        out = out + jnp.where(eid > e, v[e:e + 1, :], 0.0)
    return out


def _tables_kernel(cnt_ref, idx_ref, rank_ref, slot_ref, seg_ref, loff_ref, gdst_ref, tail_ref):
    t = idx_ref.shape[1]
    cnt = cnt_ref[...]
    seg = jnp.ceil(cnt * (1.0 / SEG_ALIGN)) * SEG_ALIGN
    loff = _excl_cumsum_rows(seg)
    r = lax.broadcasted_iota(I32, (LANES, LANES), 0)
    c = lax.broadcasted_iota(I32, (LANES, LANES), 1)
    run = jnp.dot((seg * (1.0 / SEG_ALIGN)).astype(BF16), (r < c).astype(BF16),
                  preferred_element_type=F32) * SEG_ALIGN
    tot = jnp.sum(seg, axis=1, keepdims=True)
    padded = jnp.broadcast_to(jnp.ceil(tot * (1.0 / MOE_BLOCK)) * MOE_BLOCK, (N_EXPERTS, LANES))
    start = _excl_cumsum_rows(padded)
    end = start + padded
    seg_ref[...] = seg.astype(I32)
    loff_ref[...] = loff.astype(I32)
    gdst_ref[...] = (start + run).astype(I32)

    gr = lax.broadcasted_iota(I32, (LANES, t), 0)
    gc = lax.broadcasted_iota(I32, (LANES, t), 1) // MOE_SUB
    loff_tok = jnp.dot((loff * (1.0 / SEG_ALIGN)).astype(BF16), (gr == gc).astype(BF16),
                       preferred_element_type=F32) * SEG_ALIGN
    idx = idx_ref[...]
    slot = rank_ref[...]
    for e in range(N_EXPERTS):
        slot = slot + jnp.where(idx == e, loff_tok[e:e + 1, :].astype(I32), 0)
    slot_ref[...] = slot

    lane = lax.broadcasted_iota(I32, (N_EXPERTS, LANES), 1)
    total = jnp.max(end, axis=0, keepdims=True)
    tail = jnp.where(lane == 0, start + tot, jnp.where(lane == 1, end, jnp.where(lane == 3, start, total)))
    tail_ref[...] = tail.astype(I32)


def _tables(cnt, idx_t, rank_t):
    t = idx_t.shape[1]
    tab = jax.ShapeDtypeStruct((N_EXPERTS, LANES), I32)
    return pl.pallas_call(
        _tables_kernel,
        out_shape=(jax.ShapeDtypeStruct((TOP_K, t), I32), tab, tab, tab, tab),
        compiler_params=pltpu.CompilerParams(vmem_limit_bytes=VMEM_LIMIT),
        name="route_tables",
    )(cnt, idx_t, rank_t)


def _pow2_sizes(limit):
    size = SEG_ALIGN
    while size * 2 <= limit:
        size *= 2
    sizes = []
    while size >= SEG_ALIGN:
        sizes.append(size)
        size //= 2
    return tuple(sizes)


def _piece_copies(src_ref, src_off, dst_ref, dst_off, n, limit, sem, wait=False):
    off = 0
    for size in _pow2_sizes(limit):
        take = n & size

        @pl.when(take != 0)
        def _(off=off, size=size):
            cp = pltpu.make_async_copy(
                src_ref.at[pl.ds(pl.multiple_of(src_off + off, SEG_ALIGN), size), :],
                dst_ref.at[pl.ds(pl.multiple_of(dst_off + off, SEG_ALIGN), size), :], sem)
            if wait:
                cp.wait()
            else:
                cp.start()

        off = off + take


def _dispatch_kernel(seg_ref, loff_ref, gdst_ref, taillo_ref, tailhi_ref, used_ref,
                     h_ref, slot_ref, xs_ref, loc_scr, zero_scr, sem, zsem, *, nblk):
    i = pl.program_id(0)
    par = i % 2

    @pl.when(i == 0)
    def _():
        nblk_all = xs_ref.shape[0] // MOE_BLOCK
        zero_scr[...] = jnp.zeros_like(zero_scr)

        def blk_copy(b):
            start = pl.multiple_of(b * MOE_BLOCK, MOE_BLOCK)
            return pltpu.make_async_copy(zero_scr, xs_ref.at[pl.ds(start, MOE_BLOCK), :], zsem)

        def issue_blk(b, carry):
            blk_copy(b).start()
            return carry

        def drain_blk(b, carry):
            blk_copy(b).wait()
            return carry

        lax.fori_loop(used_ref[0], nblk_all, issue_blk, 0)
        lax.fori_loop(used_ref[0], nblk_all, drain_blk, 0)

        def tails(wait):
            def body(e, carry):
                lo = taillo_ref[e]
                _piece_copies(zero_scr, 0, xs_ref, lo, tailhi_ref[e] - lo, MOE_BLOCK - 1, zsem, wait=wait)
                return carry
            return body

        lax.fori_loop(0, N_EXPERTS, tails(False), 0)
        lax.fori_loop(0, N_EXPERTS, tails(True), 0)

    slots = slot_ref[...]
    rid = lax.broadcasted_iota(I32, (LOCAL_SLOTS, MOE_SUB), 0)
    hit = rid == slots[0:1, :]
    for k in range(1, TOP_K):
        hit = hit | (rid == slots[k:k + 1, :])
    loc_scr[par] = jnp.dot(hit.astype(BF16), h_ref[...], preferred_element_type=F32).astype(ROW_DTYPE)

    loc = loc_scr.at[par]

    def seg_body(e, total):
        n = seg_ref[e * LANES + i]
        _piece_copies(loc, loff_ref[e * LANES + i], xs_ref, gdst_ref[e * LANES + i], n, MOE_SUB, sem.at[par])
        return total + n

    total = lax.fori_loop(0, N_EXPERTS, seg_body, 0)
    spare = (nblk + par * (LOCAL_SLOTS // MOE_BLOCK)) * MOE_BLOCK
    _piece_copies(loc, total, xs_ref, spare + total, LOCAL_SLOTS - total, LOCAL_SLOTS, sem.at[par])

    def wait_step(p):
        pltpu.make_async_copy(loc_scr.at[p], xs_ref.at[pl.ds(0, LOCAL_SLOTS), :], sem.at[p]).wait()

    @pl.when(i > 0)
    def _():
        wait_step(1 - par)

    @pl.when(i == pl.num_programs(0) - 1)
    def _():
        wait_step(par)


def _dispatch(seg, loff, gdst, tail_lo, tail_hi, used, h2, slot_t, nblk):
    t, d = h2.shape
    rows = (nblk + 2 * (LOCAL_SLOTS // MOE_BLOCK)) * MOE_BLOCK
    return pl.pallas_call(
        functools.partial(_dispatch_kernel, nblk=nblk),
        grid_spec=pltpu.PrefetchScalarGridSpec(
            num_scalar_prefetch=6,
            grid=(t // MOE_SUB,),
            in_specs=[pl.BlockSpec((MOE_SUB, d), lambda i, *_: (i, 0)),
                      pl.BlockSpec((TOP_K, MOE_SUB), lambda i, *_: (0, i))],
            out_specs=pl.BlockSpec(memory_space=pl.ANY),
            scratch_shapes=[pltpu.VMEM((2, LOCAL_SLOTS, d), ROW_DTYPE), pltpu.VMEM((MOE_BLOCK, d), ROW_DTYPE),
                            pltpu.SemaphoreType.DMA((2,)), pltpu.SemaphoreType.DMA(())]),
        out_shape=jax.ShapeDtypeStruct((rows, d), ROW_DTYPE),
        compiler_params=_cparams(("arbitrary",)),
        name="moe_dispatch",
    )(seg, loff, gdst, tail_lo, tail_hi, used, h2, slot_t)


def _expert_kernel(lo_ref, hi_ref, used_ref, x_ref, wgu_ref, bgu_ref, wd_ref, bd_ref, y_ref,
                   wgu_scr, wd_scr, xbuf, ybuf, xsem, ysem):
    e = pl.program_id(0)
    ne = pl.num_programs(0)
    ff = wd_ref.shape[1]
    big = 2 * MOE_BLOCK

    def layout(ex):
        first = lo_ref[ex] // MOE_BLOCK
        n = (hi_ref[ex] - lo_ref[ex]) // MOE_BLOCK
        return first, n // 2, n // 2 + n % 2

    cur = layout(e)
    first, npair, items = cur

    def x_copy(lay, k, rows):
        start = pl.multiple_of((lay[0] + 2 * k) * MOE_BLOCK, MOE_BLOCK)
        return pltpu.make_async_copy(x_ref.at[pl.ds(start, rows), :], xbuf.at[k % 2, pl.ds(0, rows), :],
                                     xsem.at[k % 2])

    def y_copy(k, rows):
        start = pl.multiple_of((first + 2 * k) * MOE_BLOCK, MOE_BLOCK)
        return pltpu.make_async_copy(ybuf.at[k % 2, pl.ds(0, rows), :], y_ref.at[pl.ds(start, rows), :],
                                     ysem.at[k % 2])

    def start_x(lay, k):
        @pl.when(k < lay[1])
        def _():
            x_copy(lay, k, big).start()

        @pl.when((k == lay[1]) & (k < lay[2]))
        def _():
            x_copy(lay, k, MOE_BLOCK).start()

    def compute(k, rows):
        slot = k % 2
        xb = xbuf[slot, 0:rows, :].astype(BF16)
        gu = jnp.dot(xb, wgu_scr[...], preferred_element_type=F32) + bgu_ref[0]
        glu = jnp.minimum(gu[:, :ff], SWIGLU_LIMIT)
        lin = jnp.clip(gu[:, ff:], -SWIGLU_LIMIT, SWIGLU_LIMIT)
        act = glu * _sigmoid(SWIGLU_ALPHA * glu) * (lin + 1.0)
        y = jnp.dot(act.astype(BF16), wd_scr[...], preferred_element_type=F32) + bd_ref[0]
        ybuf[slot, 0:rows, :] = y.astype(ybuf.dtype)

    def item(k, rows):
        x_copy(cur, k, rows).wait()

        @pl.when(k >= 1)
        def _():
            start_x(cur, k + 1)

        @pl.when(k >= 2)
        def _():
            y_copy(k - 2, big).wait()

        compute(k, rows)
        y_copy(k, rows).start()

    @pl.when(e == 0)
    def _():
        start_x(cur, 0)
        start_x(cur, 1)

    @pl.when(items > 0)
    def _():
        wgu_scr[...] = wgu_ref[0].astype(BF16)
        wd_scr[...] = wd_ref[0].astype(BF16)

    def body(k, carry):
        item(k, big)
        return carry

    lax.fori_loop(0, npair, body, 0)

    @pl.when(items > npair)
    def _():
        item(npair, MOE_BLOCK)

    @pl.when(items >= 2)
    def _():
        y_copy(items - 2, big).wait()

    @pl.when(items > npair)
    def _():
        y_copy(items - 1, MOE_BLOCK).wait()

    @pl.when((items == npair) & (items >= 1))
    def _():
        y_copy(items - 1, big).wait()

    @pl.when(e + 1 < ne)
    def _():
        nxt = layout(jnp.minimum(e + 1, ne - 1))
        start_x(nxt, 0)
        start_x(nxt, 1)

    @pl.when(e == pl.num_programs(0) - 1)
    def _():
        nblk_all = y_ref.shape[0] // MOE_BLOCK
        ybuf[0] = jnp.zeros(ybuf.shape[1:], ybuf.dtype)

        def zero_copy(b):
            start = pl.multiple_of(b * MOE_BLOCK, MOE_BLOCK)
            return pltpu.make_async_copy(ybuf.at[0, pl.ds(0, MOE_BLOCK), :],
                                         y_ref.at[pl.ds(start, MOE_BLOCK), :], ysem.at[0])

        def issue(b, carry):
            zero_copy(b).start()
            return carry

        def drain(b, carry):
            zero_copy(b).wait()
            return carry

        lax.fori_loop(used_ref[0], nblk_all, issue, 0)
        lax.fori_loop(used_ref[0], nblk_all, drain, 0)


def _experts(row_lo, row_hi, used, xs, w_gate_up, b_gate_up, w_down, b_down):
    rows, d = xs.shape
    ne, _, ff2 = w_gate_up.shape
    ff = ff2 // 2
    return pl.pallas_call(
        _expert_kernel,
        grid_spec=pltpu.PrefetchScalarGridSpec(
            num_scalar_prefetch=3,
            grid=(ne,),
            in_specs=[pl.BlockSpec(memory_space=pl.ANY),
                      pl.BlockSpec((1, d, ff2), lambda e, *_: (e, 0, 0)),
                      pl.BlockSpec((1, 1, ff2), lambda e, *_: (e, 0, 0)),
                      pl.BlockSpec((1, ff, d), lambda e, *_: (e, 0, 0)),
                      pl.BlockSpec((1, 1, d), lambda e, *_: (e, 0, 0))],
            out_specs=pl.BlockSpec(memory_space=pl.ANY),
            scratch_shapes=[pltpu.VMEM((d, ff2), BF16), pltpu.VMEM((ff, d), BF16),
                            pltpu.VMEM((2, 2 * MOE_BLOCK, d), ROW_DTYPE),
                            pltpu.VMEM((2, 2 * MOE_BLOCK, d), ROW_DTYPE),
                            pltpu.SemaphoreType.DMA((2,)), pltpu.SemaphoreType.DMA((2,))]),
        out_shape=jax.ShapeDtypeStruct((rows, d), ROW_DTYPE),
        compiler_params=_cparams(("arbitrary",)),
        name="moe_experts",
    )(row_lo, row_hi, used, xs, w_gate_up, b_gate_up.reshape(ne, 1, ff2), w_down, b_down.reshape(ne, 1, d))


def _combine_kernel(seg_ref, loff_ref, gdst_ref, x1_ref, slot_ref, gates_ref, ys_ref, o_ref, loc_scr, sem):
    i = pl.program_id(0)
    par = i % 2

    def fetch(g, p):
        loc = loc_scr.at[p]

        def seg_body(e, total):
            n = seg_ref[e * LANES + g]
            _piece_copies(ys_ref, gdst_ref[e * LANES + g], loc, loff_ref[e * LANES + g], n, MOE_SUB, sem.at[p])
            return total + n

        total = lax.fori_loop(0, N_EXPERTS, seg_body, 0)
        _piece_copies(ys_ref, total, loc, total, LOCAL_SLOTS - total, LOCAL_SLOTS, sem.at[p])

    @pl.when(i == 0)
    def _():
        fetch(0, 0)

    @pl.when(i + 1 < pl.num_programs(0))
    def _():
        fetch(i + 1, 1 - par)

    pltpu.make_async_copy(ys_ref.at[pl.ds(0, LOCAL_SLOTS), :], loc_scr.at[par], sem.at[par]).wait()

    slots = slot_ref[...]
    g = gates_ref[...]
    rid = lax.broadcasted_iota(I32, (MOE_SUB, LOCAL_SLOTS), 1)
    gmat = jnp.where(rid == slots[:, 0:1], g[:, 0:1], 0.0)
    for k in range(1, TOP_K):
        gmat = gmat + jnp.where(rid == slots[:, k:k + 1], g[:, k:k + 1], 0.0)
    o_ref[...] = x1_ref[...] + jnp.dot(gmat.astype(BF16), loc_scr[par].astype(BF16),
                                       preferred_element_type=F32)


def _combine(seg, loff, gdst, x1, slot_tk, gates_tk, ys):
    t, d = x1.shape
    tok = lambda w: pl.BlockSpec((MOE_SUB, w), lambda i, *_: (i, 0))
    return pl.pallas_call(
        _combine_kernel,
        grid_spec=pltpu.PrefetchScalarGridSpec(
            num_scalar_prefetch=3,
            grid=(t // MOE_SUB,),
            in_specs=[tok(d), tok(TOP_K), tok(TOP_K), pl.BlockSpec(memory_space=pl.ANY)],
            out_specs=tok(d),
            scratch_shapes=[pltpu.VMEM((2, LOCAL_SLOTS, d), ROW_DTYPE), pltpu.SemaphoreType.DMA((2,))]),
        out_shape=jax.ShapeDtypeStruct((t, d), F32),
        compiler_params=_cparams(("arbitrary",)),
        name="moe_combine",
    )(seg, loff, gdst, x1, slot_tk, gates_tk, ys)


def _rope_tables(s):
    pos = np.arange(s, dtype=np.float64)
    inv = ROPE_BASE ** (-np.arange(0, MLA_ROPE, 2, dtype=np.float64) / MLA_ROPE)
    ang = pos[:, None] * inv[None, :]
    cos, sin = jnp.asarray(np.cos(ang), F32), jnp.asarray(np.sin(ang), F32)
    return _spread_rope(jnp.concatenate([cos, cos], axis=-1)), _spread_rope(jnp.concatenate([-sin, sin], axis=-1))


def _spread_rope(w, gap=None):
    half = MLA_ROPE // 2
    zeros = jnp.zeros(w.shape[:-1] + (half,), w.dtype)
    return jnp.concatenate([w[..., :half], zeros if gap is None else gap, w[..., half:], zeros], axis=-1)


def _pad_heads(w):
    lead = w.shape[:-1]
    w = w.reshape(lead + (MLA_HEADS, MLA_QK_DIM))
    w = jnp.concatenate([w[..., :MLA_NOPE], _spread_rope(w[..., MLA_NOPE:])], axis=-1)
    return w.reshape(lead + (MLA_HEADS * 2 * LANES,))


def _layer(x, g_mix, w_in, conv_w, conv_b, dt_bias, a_log, d_skip, g_ssd_out, w_ssd_out,
           g_q_lat, w_q_up, g_kv_lat, w_kv_up, g_qk_q, g_qk_k, w_mla_out, w_o,
           g_ffn, w_router, b_router, w_gate_up, b_gate_up, w_down, b_down):
    b, s, d = x.shape
    t = b * s
    x2 = x.reshape(t, d)

    off_xbc = SSD_D_INNER
    off_dt = off_xbc + SSD_CONV_DIM
    off_ql = off_dt + SSD_HEADS
    off_kvl = off_ql + MLA_Q_RANK
    off_kr = off_kvl + MLA_KV_RANK
    off_gate = off_kr + MLA_ROPE
    dt_gap = jnp.pad(w_in[:, off_dt:off_ql], ((0, 0), (0, MLA_ROPE // 2 - SSD_HEADS)))
    w_streams = [w_in[:, :off_xbc], w_in[:, off_xbc:off_dt], w_in[:, off_ql:off_kvl], w_in[:, off_kvl:off_kr],
                 _spread_rope(w_in[:, off_kr:off_gate], gap=dt_gap), w_in[:, off_gate:]]

    tiles = _tile_sizes(s)
    z, xbc, ql, kvl, krdt, gate = _in_proj(x2, g_mix, [w.astype(BF16) for w in w_streams], tiles.tokens)

    lane_pad = lambda vec: jnp.pad(vec, (DT_LO, LANES - DT_LO - SSD_HEADS)).reshape(1, LANES)
    lt = tiles.ssd
    y_ssd = _ssd(xbc.reshape(b, s, SSD_CONV_DIM), z.reshape(b, s, SSD_D_INNER), krdt.reshape(b, s, LANES),
                 conv_w, conv_b.reshape(1, -1), lane_pad(dt_bias), lane_pad(a_log),
                 jnp.repeat(d_skip, SSD_HEAD_DIM).reshape(1, -1), g_ssd_out.reshape(1, -1), lt)

    cos_t, sin_t = _rope_tables(s)
    wq_pad = _pad_heads(w_q_up).astype(BF16)
    spread_gain = lambda g: jnp.concatenate([g[:MLA_NOPE], _spread_rope(g[MLA_NOPE:])]).reshape(1, -1)
    gq_pad = spread_gain(g_qk_q)
    gk_pad = spread_gain(g_qk_k)
    tq = tiles.tokens
    w_kv_h = w_kv_up.reshape(MLA_KV_RANK, MLA_HEADS, MLA_NOPE + MLA_V)
    wk = w_kv_h[:, :, :MLA_NOPE].reshape(MLA_KV_RANK, MLA_HEADS * MLA_NOPE).astype(BF16)
    wvt = w_kv_h[:, :, MLA_NOPE:].reshape(MLA_KV_RANK, MLA_HEADS * MLA_V).T.astype(BF16)
    q, k, v = _qkv_prep(ql, kvl, krdt, cos_t, sin_t, g_q_lat.reshape(1, -1), wq_pad,
                        g_kv_lat.reshape(1, -1), wk, wvt, gq_pad, gk_pad, b, s, tq)
    o_mla = _flash(q, k, v, tiles.queries)

    x1, h2, idx_t, gates_t, rank_t, cnt = _merge(
        x2, y_ssd.reshape(t, d), o_mla.reshape(t, d), gate,
        w_ssd_out.astype(BF16), w_mla_out.astype(BF16), w_o.astype(BF16),
        g_ffn.reshape(1, -1), w_router.T, b_router.reshape(-1, 1), tiles.tokens)

    nsub = t // MOE_SUB
    assert t % MOE_SUB == 0 and nsub <= LANES
    cap = t * TOP_K + nsub * N_EXPERTS * (SEG_ALIGN - 1) + N_EXPERTS * (MOE_BLOCK - 1)
    nblk = -(-cap // MOE_BLOCK)
    slot_t, seg, loff, gdst, tail = _tables(cnt, idx_t, rank_t)
    used = (tail[0:1, 2] // MOE_BLOCK).astype(I32)
    seg, loff, gdst = seg.reshape(-1), loff.reshape(-1), gdst.reshape(-1)
    xs = _dispatch(seg, loff, gdst, tail[:, 0], tail[:, 1], used, h2, slot_t, nblk)
    ys = _experts(tail[:, 3], tail[:, 1], used, xs, w_gate_up, b_gate_up, w_down, b_down)
    out = _combine(seg, loff, gdst, x1, slot_t.T, gates_t.T, ys)
    return out.reshape(b, s, d)


def kernel(x, g_mix, w_in, conv_w, conv_b, dt_bias, a_log, d_skip, g_ssd_out, w_ssd_out, g_q_lat, w_q_up, g_kv_lat, w_kv_up, g_qk_q, g_qk_k, w_mla_out, w_o, g_ffn, w_router, b_router, w_gate_up, b_gate_up, w_down, b_down):
    params = (g_mix, w_in, conv_w, conv_b, dt_bias, a_log, d_skip, g_ssd_out, w_ssd_out, g_q_lat, w_q_up,
              g_kv_lat, w_kv_up, g_qk_q, g_qk_k, w_mla_out, w_o, g_ffn, w_router, b_router,
              w_gate_up, b_gate_up, w_down, b_down)
    for l in range(g_mix.shape[0]):
        x = _layer(x, *(p[l] for p in params))
    return x
```

```python
import functools
import math
from typing import NamedTuple

import jax
import jax.numpy as jnp
import numpy as np
from jax import lax
from jax.experimental import pallas as pl
from jax.experimental.pallas import tpu as pltpu

F32 = jnp.float32
BF16 = jnp.bfloat16
I32 = jnp.int32

EPS = 1e-6
CHUNK = 64

SSD_HEADS = 16
SSD_HEAD_DIM = 64
SSD_GROUPS = 2
SSD_STATE = 128
SSD_CONV = 4
SSD_D_INNER = SSD_HEADS * SSD_HEAD_DIM
SSD_CONV_DIM = SSD_D_INNER + 2 * SSD_GROUPS * SSD_STATE

MLA_HEADS = 8
MLA_Q_RANK = 384
MLA_KV_RANK = 256
MLA_NOPE = 128
MLA_ROPE = 64
MLA_QK_DIM = MLA_NOPE + MLA_ROPE
MLA_V = 128
V_ROWS = MLA_V + 16
DT_LO = MLA_ROPE // 2
SSD_SCAN_CHUNK = 128
ROPE_BASE = 10000.0

N_EXPERTS = 32
TOP_K = 4
SWIGLU_ALPHA = 1.702
SWIGLU_LIMIT = 7.0

LANES = 128
VMEM_LIMIT = 56 * 1024 * 1024
NEG_BIG = -1e30
LOG2E = math.log2(math.e)

MOE_BLOCK = 256
MOE_SUB = 256
ROW_DTYPE = F32
SEG_ALIGN = 8 * 4 // jnp.dtype(ROW_DTYPE).itemsize
LOCAL_SLOTS = -(-(TOP_K * MOE_SUB + N_EXPERTS * (SEG_ALIGN - 1)) // MOE_BLOCK) * MOE_BLOCK
ROW_COPY_PRIORITY = 1


class _Tiles(NamedTuple):
    tokens: int
    ssd: int
    queries: int


def _tile_sizes(seq):
    return _Tiles(tokens=min(512, seq), ssd=min(256, seq), queries=min(1024, seq))


def _cparams(semantics, **kw):
    return pltpu.CompilerParams(dimension_semantics=semantics,
                                vmem_limit_bytes=VMEM_LIMIT, **kw)


def _sigmoid(v):
    return 1.0 / (1.0 + jnp.exp(-v))


def _silu(v):
    return v * _sigmoid(v)


def _bf16_pieces(t, n):
    pieces = []
    for _ in range(n - 1):
        p = t.astype(BF16)
        pieces.append(p)
        t = t - p.astype(F32)
    pieces.append(t.astype(BF16))
    return pieces


def _inproj_kernel(x_ref, g_ref, wz_ref, wxbc_ref, wql_ref, wkvl_ref, wkrdt_ref, wgate_ref,
                   z_ref, xbc_ref, ql_ref, kvl_ref, krdt_ref, gate_ref):
    x = x_ref[...]
    ms = jnp.mean(x * x, axis=-1, keepdims=True)
    h = (x * lax.rsqrt(ms + EPS) * g_ref[...]).astype(BF16)
    streams = ((wz_ref, z_ref), (wxbc_ref, xbc_ref), (wql_ref, ql_ref), (wkvl_ref, kvl_ref),
               (wkrdt_ref, krdt_ref), (wgate_ref, gate_ref))
    for w_ref, ref in streams:
        p = jnp.dot(h, w_ref[...], preferred_element_type=F32)
        if ref is gate_ref:
            p = _sigmoid(p)
        ref[...] = p.astype(ref.dtype)


def _in_proj(x2, g_mix, weights, tm):
    t, d = x2.shape
    widths = tuple(w.shape[1] for w in weights)
    dts = (BF16, BF16, BF16, BF16, F32, BF16)
    out_shape = tuple(jax.ShapeDtypeStruct((t, w), dt) for w, dt in zip(widths, dts))
    out_specs = tuple(pl.BlockSpec((tm, w), lambda i: (i, 0)) for w in widths)
    return pl.pallas_call(
        _inproj_kernel,
        grid=(t // tm,),
        in_specs=[pl.BlockSpec((tm, d), lambda i: (i, 0)),
                  pl.BlockSpec((1, d), lambda i: (0, 0))]
                 + [pl.BlockSpec((d, w), lambda i: (0, 0)) for w in widths],
        out_specs=out_specs,
        out_shape=out_shape,
        compiler_params=_cparams(("parallel",)),
        name="in_proj",
    )(x2, g_mix.reshape(1, d), *weights)


def _ssd_kernel(xbc_ref, z_ref, krdt_ref, convw_ref, convb_ref, dtb_ref, alog_ref, dskip_ref, gout_ref,
                y_ref, xext_scr, state_scr, *, lt):
    i = pl.program_id(1)
    halo = 8
    dt_lo = DT_LO
    gw = SSD_D_INNER // SSD_GROUPS
    hpg = SSD_HEADS // SSD_GROUPS

    @pl.when(i == 0)
    def _():
        state_scr[...] = jnp.zeros_like(state_scr)
        xext_scr[0:halo, :] = jnp.zeros((halo, SSD_CONV_DIM), F32)

    xin = xbc_ref[0]
    xext_scr[halo:2 * halo, :] = xin[0:halo, :].astype(F32)
    srow = lax.broadcasted_iota(I32, (lt, lt), 0)
    scol = lax.broadcasted_iota(I32, (lt, lt), 1)
    acc = convb_ref[...] + convw_ref[SSD_CONV - 1:SSD_CONV, :] * xin.astype(F32)
    head = jnp.broadcast_to(convb_ref[...], (halo, SSD_CONV_DIM))
    for j in range(SSD_CONV):
        shift = SSD_CONV - 1 - j
        head = head + convw_ref[j:j + 1, :] * xext_scr[pl.ds(halo - shift, halo), :]
        if shift:
            shifted = jnp.dot((srow == scol + shift).astype(BF16), xin, preferred_element_type=F32)
            acc = acc + convw_ref[j:j + 1, :] * shifted
    acc = jnp.concatenate([head, acc[halo:, :]], axis=0)
    xext_scr[0:halo, :] = xin[lt - halo:lt, :].astype(F32)
    xbc = _silu(acc)
    xs = xbc[:, :SSD_D_INNER]
    bm = xbc[:, SSD_D_INNER:SSD_D_INNER + SSD_GROUPS * SSD_STATE]
    cm = xbc[:, SSD_D_INNER + SSD_GROUPS * SSD_STATE:]

    lane = lax.broadcasted_iota(I32, (1, LANES), 1)
    head_lane = (lane >= dt_lo) & (lane < dt_lo + SSD_HEADS)
    v = krdt_ref[0] + dtb_ref[...]
    dt = jnp.maximum(v, 0.0) + jnp.log(1.0 + jnp.exp(-jnp.abs(v)))
    dt = jnp.where(head_lane, dt, 0.0)
    a = jnp.where(head_lane, -jnp.exp(alog_ref[...]) * LOG2E, 0.0)
    da = dt * a
    lc = min(SSD_SCAN_CHUNK, lt)
    nchunk = lt // lc
    row = lax.broadcasted_iota(I32, (lt, lt), 0)
    col = lax.broadcasted_iota(I32, (lt, lt), 1)
    tri_b = ((row >= col) & ((row // lc) == (col // lc))).astype(BF16)
    a_cum = jnp.zeros((lt, LANES), F32)
    for piece in _bf16_pieces(da, 3):
        a_cum = a_cum + jnp.dot(tri_b, piece, preferred_element_type=F32)
    a_cum_t = a_cum.T
    a_last = jnp.concatenate([jnp.broadcast_to(a_cum[(c + 1) * lc - 1:(c + 1) * lc, :], (lc, LANES))
                              for c in range(nchunk)], axis=0)
    exp_a = jnp.exp2(a_cum)
    dec = jnp.exp2(a_last - a_cum)
    tri = lax.broadcasted_iota(I32, (lc, lc), 0) >= lax.broadcasted_iota(I32, (lc, lc), 1)

    er = lax.broadcasted_iota(I32, (LANES, SSD_D_INNER), 0)
    ec = lax.broadcasted_iota(I32, (LANES, SSD_D_INNER), 1)
    expand = ((er - dt_lo) == (ec // SSD_HEAD_DIM)).astype(BF16)

    def widen(t, pieces):
        out = jnp.zeros((lt, SSD_D_INNER), F32)
        for piece in _bf16_pieces(t, pieces):
            out = out + jnp.dot(piece, expand, preferred_element_type=F32)
        return out

    dt_w = widen(dt, 2)
    exp_a_w = widen(exp_a, 1)
    dec_w = widen(dec, 1)
    xdt = xs * dt_w
    xdt_b = xdt.astype(BF16)
    xdec_b = (xdt * dec_w).astype(BF16)
    lane_pair = lax.broadcasted_iota(I32, (lc, LANES), 1)

    states = [state_scr[g] for g in range(SSD_GROUPS)]
    y_rows = [[] for _ in range(SSD_GROUPS)]
    for c in range(nchunk):
        rows = slice(c * lc, (c + 1) * lc)
        for g in range(SSD_GROUPS):
            cols = slice(g * gw, (g + 1) * gw)
            bg = bm[rows, g * SSD_STATE:(g + 1) * SSD_STATE].astype(BF16)
            cg = cm[rows, g * SSD_STATE:(g + 1) * SSD_STATE].astype(BF16)
            cb = lax.dot_general(cg, bg, (((1,), (1,)), ((), ())), preferred_element_type=F32)
            parts = []
            for pair in range(hpg // 2):
                h0 = g * hpg + 2 * pair
                rhs = xdt_b[rows, h0 * SSD_HEAD_DIM:(h0 + 2) * SSD_HEAD_DIM]
                res = []
                for h in (h0, h0 + 1):
                    seg = a_cum[rows, dt_lo + h:dt_lo + h + 1] - a_cum_t[dt_lo + h:dt_lo + h + 1, rows]
                    lmat = jnp.exp2(jnp.where(tri, seg, NEG_BIG))
                    res.append(jnp.dot((cb * lmat).astype(BF16), rhs, preferred_element_type=F32))
                parts.append(jnp.where(lane_pair < SSD_HEAD_DIM, res[0], res[1]))
            st = states[g]
            y_off = jnp.dot(cg, st.astype(BF16), preferred_element_type=F32)
            y_rows[g].append(jnp.concatenate(parts, axis=-1) + y_off * exp_a_w[rows, cols])
            new = lax.dot_general(bg, xdec_b[rows, cols], (((0,), (0,)), ((), ())),
                                  preferred_element_type=F32)
            states[g] = st * exp_a_w[(c + 1) * lc - 1:(c + 1) * lc, cols] + new
    for g in range(SSD_GROUPS):
        state_scr[g] = states[g]

    zs = _silu(z_ref[0].astype(F32))
    outs = []
    for g in range(SSD_GROUPS):
        yg = (jnp.concatenate(y_rows[g], axis=0)
              + dskip_ref[:, g * gw:(g + 1) * gw] * xs[:, g * gw:(g + 1) * gw])
        yg = yg * zs[:, g * gw:(g + 1) * gw]
        ms = jnp.mean(yg * yg, axis=-1, keepdims=True)
        outs.append(yg * lax.rsqrt(ms + EPS) * gout_ref[:, g * gw:(g + 1) * gw])
    y_ref[0] = jnp.concatenate(outs, axis=-1).astype(y_ref.dtype)


def _ssd(xbc, z, krdt, conv_w, conv_b, dtb, alog, dskip_w, g_out, lt):
    b, s, _ = xbc.shape
    gw = SSD_D_INNER // SSD_GROUPS
    full = lambda shape: pl.BlockSpec(shape, lambda bi, i: (0,) * len(shape))
    return pl.pallas_call(
        functools.partial(_ssd_kernel, lt=lt),
        grid=(b, s // lt),
        in_specs=[pl.BlockSpec((1, lt, SSD_CONV_DIM), lambda bi, i: (bi, i, 0)),
                  pl.BlockSpec((1, lt, SSD_D_INNER), lambda bi, i: (bi, i, 0)),
                  pl.BlockSpec((1, lt, LANES), lambda bi, i: (bi, i, 0)),
                  full((SSD_CONV, SSD_CONV_DIM)), full((1, SSD_CONV_DIM)),
                  full((1, LANES)), full((1, LANES)),
                  full((1, SSD_D_INNER)), full((1, SSD_D_INNER))],
        out_specs=pl.BlockSpec((1, lt, SSD_D_INNER), lambda bi, i: (bi, i, 0)),
        out_shape=jax.ShapeDtypeStruct((b, s, SSD_D_INNER), BF16),
        scratch_shapes=[pltpu.VMEM((16, SSD_CONV_DIM), F32),
                        pltpu.VMEM((SSD_GROUPS, SSD_STATE, gw), F32)],
        compiler_params=_cparams(("parallel", "arbitrary")),
        name="ssd_scan",
    )(xbc, z, krdt, conv_w, conv_b, dtb, alog, dskip_w, g_out)


def _swap_halves(t):
    return pltpu.roll(t, LANES // 2, 1)


def _lane_sums(sq, width):
    ones = jnp.ones((sq.shape[1], width), BF16)
    return jnp.dot(sq.astype(BF16), ones, preferred_element_type=F32)


def _qkv_kernel(ql_ref, kvl_ref, krdt_ref, cos_ref, sin_ref, cost_ref, sint_ref, gql_ref, wqt_ref, gkvl_ref,
                wk_ref, wvt_ref, gqt_ref, gk_ref, q_ref, k_ref, v_ref, *, scale):
    hw = 2 * LANES
    cosv = cos_ref[...]
    sinv = sin_ref[...]
    cost = cost_ref[...]
    sint = sint_ref[...]

    ql = ql_ref[...].astype(F32)
    rq = lax.rsqrt(_lane_sums(ql * ql, LANES) * (1.0 / MLA_Q_RANK) + EPS)
    qn = (ql * jnp.concatenate([rq] * (MLA_Q_RANK // LANES), axis=1) * gql_ref[...]).astype(BF16)

    kvl = kvl_ref[...].astype(F32)
    rkv = lax.rsqrt(_lane_sums(kvl * kvl, LANES) * (1.0 / MLA_KV_RANK) + EPS)
    kvn = (kvl * jnp.concatenate([rkv] * (MLA_KV_RANK // LANES), axis=1) * gkvl_ref[...]).astype(BF16)
    tm = kvn.shape[0]
    ones_rows = (lax.broadcasted_iota(I32, (V_ROWS - MLA_V, tm), 0) == 0).astype(v_ref.dtype)

    lane = lax.broadcasted_iota(I32, (1, LANES), 1)
    kr = jnp.where((lane % (LANES // 2)) < MLA_ROPE // 2, krdt_ref[...], 0.0)
    ss_r = _lane_sums(kr * kr, LANES)
    krg = kr * gk_ref[:, LANES:]
    kr_rot = krg * cosv + _swap_halves(krg) * sinv

    nt = (((1,), (1,)), ((), ()))
    qt = lax.dot_general(wqt_ref[...], qn, nt, preferred_element_type=F32)
    kf = jnp.dot(kvn, wk_ref[...], preferred_element_type=F32)
    vt = lax.dot_general(wvt_ref[...], kvn, nt, preferred_element_type=F32)
    gq_full = jnp.concatenate([gqt_ref[...]] * (tm // LANES), axis=1)
    half = LANES // 2
    for h in range(MLA_HEADS):
        qh = qt[h * hw:(h + 1) * hw, :]
        r = lax.rsqrt(jnp.sum(qh * qh, axis=0, keepdims=True) * (1.0 / MLA_QK_DIM) + EPS)
        qs = qh * (r * scale) * gq_full
        xr = qs[LANES:, :]
        rot = xr * cost + jnp.concatenate([xr[half:, :], xr[:half, :]], axis=0) * sint
        q_ref[0, h] = jnp.concatenate([qs[:LANES, :], rot], axis=0).astype(q_ref.dtype)

        kn = kf[:, h * LANES:(h + 1) * LANES]
        rk = lax.rsqrt((_lane_sums(kn * kn, LANES) + ss_r) * (1.0 / MLA_QK_DIM) + EPS)
        k_ref[0, h, :, 0:LANES] = (kn * rk * gk_ref[:, :LANES]).astype(k_ref.dtype)
        k_ref[0, h, :, LANES:hw] = (kr_rot * rk).astype(k_ref.dtype)
        v_ref[0, h, 0:MLA_V, :] = vt[h * MLA_V:(h + 1) * MLA_V, :].astype(v_ref.dtype)
        v_ref[0, h, MLA_V:V_ROWS, :] = ones_rows


def _qkv_prep(ql, kvl, krdt, cos_t, sin_t, g_q_lat, wq_pad, g_kv_lat, wk, wvt, gq_pad, gk_pad, b, s, tm):
    nst = s // tm
    wqt = wq_pad.T
    gqt = jnp.broadcast_to(gq_pad.reshape(-1, 1), (2 * LANES, LANES))
    cos_tt, sin_tt = cos_t.T, sin_t.T
    full = lambda shape: pl.BlockSpec(shape, lambda bi, i: (0,) * len(shape))
    tok = lambda w: pl.BlockSpec((tm, w), lambda bi, i: (bi * nst + i, 0))
    hs = lambda w: pl.BlockSpec((1, MLA_HEADS, tm, w), lambda bi, i: (bi, 0, i, 0))
    return pl.pallas_call(
        functools.partial(_qkv_kernel, scale=MLA_QK_DIM ** -0.5 * math.log2(math.e)),
        grid=(b, nst),
        in_specs=[tok(MLA_Q_RANK), tok(MLA_KV_RANK), tok(LANES),
                  pl.BlockSpec((tm, LANES), lambda bi, i: (i, 0)),
                  pl.BlockSpec((tm, LANES), lambda bi, i: (i, 0)),
                  pl.BlockSpec((LANES, tm), lambda bi, i: (0, i)),
                  pl.BlockSpec((LANES, tm), lambda bi, i: (0, i)),
                  full((1, MLA_Q_RANK)), full(wqt.shape), full((1, MLA_KV_RANK)), full(wk.shape),
                  full(wvt.shape), full(gqt.shape), full((1, 2 * LANES))],
        out_specs=(pl.BlockSpec((1, MLA_HEADS, 2 * LANES, tm), lambda bi, i: (bi, 0, 0, i)), hs(2 * LANES),
                   pl.BlockSpec((1, MLA_HEADS, V_ROWS, tm), lambda bi, i: (bi, 0, 0, i))),
        out_shape=(jax.ShapeDtypeStruct((b, MLA_HEADS, 2 * LANES, s), BF16),
                   jax.ShapeDtypeStruct((b, MLA_HEADS, s, 2 * LANES), BF16),
                   jax.ShapeDtypeStruct((b, MLA_HEADS, V_ROWS, s), BF16)),
        compiler_params=_cparams(("parallel", "parallel")),
        name="qkv_prep",
    )(ql, kvl, krdt, cos_t, sin_t, cos_tt, sin_tt, g_q_lat, wqt, g_kv_lat, wk, wvt, gqt, gk_pad)


def _flash_kernel(qt_ref, k_ref, vt_ref, o_ref, m_scr, acc_scr, s_scr, *, tq):
    i = pl.program_id(2)
    tk = tq // 2
    qt = qt_ref[0, 0]
    m_scr[...] = jnp.full(m_scr.shape, NEG_BIG, F32)
    acc_scr[...] = jnp.zeros(acc_scr.shape, F32)

    def scores(j, slot, lo=0):
        start = pl.multiple_of(j * tk, tk)
        ks = k_ref[0, 0, pl.ds(start, tk), :]
        s_scr[slot, :, lo:] = jnp.dot(ks, qt[:, lo:], preferred_element_type=F32)

    def softmax_pv(j, slot, masked, lo=0):
        start = pl.multiple_of(j * tk, tk)
        vt = vt_ref[0, 0, :, pl.ds(start, tk)]
        st = s_scr[slot, :, lo:]
        if masked:
            kc = lax.broadcasted_iota(I32, (tk, tk), 0) // CHUNK
            qc = lax.broadcasted_iota(I32, (tk, tk), 1) // CHUNK
            diag = jnp.where(kc <= qc, st[:, :tk], NEG_BIG)
            st = diag if st.shape[1] == tk else jnp.concatenate([diag, st[:, tk:]], axis=1)
        m_prev = m_scr[:, lo:]
        m_new = jnp.maximum(m_prev, jnp.max(st, axis=0, keepdims=True))
        alpha = jnp.exp2(m_prev - m_new)
        pt = jnp.exp2(st - m_new)
        acc_scr[:, lo:] = alpha * acc_scr[:, lo:] + jnp.dot(vt, pt.astype(BF16), preferred_element_type=F32)
        m_scr[:, lo:] = m_new

    scores(0, 0)

    def pair(j):
        scores(j + 1, 1)
        softmax_pv(j, 0, False)
        scores(j + 2, 0)
        softmax_pv(j + 1, 1, False)

    def body(jj, carry):
        pair(4 * jj)
        pair(4 * jj + 2)
        return carry

    lax.fori_loop(0, i // 2, body, 0)

    @pl.when(i % 2 == 1)
    def _():
        pair(2 * i - 2)

    scores(2 * i + 1, 1, lo=tk)
    softmax_pv(2 * i, 0, True)
    softmax_pv(2 * i + 1, 1, True, lo=tk)

    o_ref[0] = (acc_scr[0:MLA_V, :] / acc_scr[MLA_V:MLA_V + 1, :]).T.astype(o_ref.dtype)


def _flash(qt, k, v, tq):
    b, nh, s, _ = k.shape
    return pl.pallas_call(
        functools.partial(_flash_kernel, tq=tq),
        grid=(b, nh, s // tq),
        in_specs=[pl.BlockSpec((1, 1, qt.shape[2], tq), lambda bi, h, i: (bi, h, 0, i)),
                  pl.BlockSpec((1, 1, s, k.shape[-1]), lambda bi, h, i: (bi, h, 0, 0)),
                  pl.BlockSpec((1, 1, V_ROWS, s), lambda bi, h, i: (bi, h, 0, 0))],
        out_specs=pl.BlockSpec((1, tq, MLA_V), lambda bi, h, i: (bi, i, h)),
        out_shape=jax.ShapeDtypeStruct((b, s, nh * MLA_V), BF16),
        scratch_shapes=[pltpu.VMEM((1, tq), F32), pltpu.VMEM((V_ROWS, tq), F32),
                        pltpu.VMEM((2, tq // 2, tq), F32)],
        compiler_params=_cparams(("parallel", "parallel", "arbitrary")),
        name="flash_attn",
    )(qt, k, v)


def _merge_kernel(x_ref, ys_ref, om_ref, gate_ref, wss_ref, wml_ref, wo_ref, gffn_ref, wrt_ref, brt_ref,
                  x1_ref, h2_ref, idx_ref, gates_ref, rank_ref, cnt_ref, *, tm):
    d = x_ref.shape[-1]
    step = pl.program_id(0)

    @pl.when(step == 0)
    def _():
        cnt_ref[...] = jnp.zeros_like(cnt_ref)

    y1 = jnp.dot(ys_ref[...], wss_ref[...], preferred_element_type=F32)
    y2 = jnp.dot(om_ref[...], wml_ref[...], preferred_element_type=F32)
    g = gate_ref[...].astype(F32)
    merged = (g[:, :d] * y1 + g[:, d:] * y2).astype(BF16)
    x1 = x_ref[...] + jnp.dot(merged, wo_ref[...], preferred_element_type=F32)
    x1_ref[...] = x1
    ms = jnp.mean(x1 * x1, axis=-1, keepdims=True)
    h2 = x1 * lax.rsqrt(ms + EPS) * gffn_ref[...]
    h2_ref[...] = h2.astype(h2_ref.dtype)

    w_hi, w_lo = _bf16_pieces(wrt_ref[...], 2)
    h_hi, h_lo = _bf16_pieces(h2, 2)
    nt = (((1,), (1,)), ((), ()))
    logits = (lax.dot_general(w_hi, h_hi, nt, preferred_element_type=F32)
              + lax.dot_general(w_hi, h_lo, nt, preferred_element_type=F32)
              + lax.dot_general(w_lo, h_hi, nt, preferred_element_type=F32)) + brt_ref[...]
    eid = lax.broadcasted_iota(I32, (N_EXPERTS, tm), 0)
    cur = logits
    onehot = jnp.zeros((N_EXPERTS, tm), F32)
    vals, sels = [], []
    for k in range(TOP_K):
        mx = jnp.max(cur, axis=0, keepdims=True)
        idx = jnp.min(jnp.where(cur == mx, eid, N_EXPERTS), axis=0, keepdims=True)
        sel = eid == idx
        vals.append(mx)
        sels.append(sel)
        idx_ref[k:k + 1, :] = idx
        cur = jnp.where(sel, -jnp.inf, cur)
        onehot = onehot + sel.astype(F32)
    es = [jnp.exp(vk - vals[0]) for vk in vals]
    den = es[0] + es[1] + es[2] + es[3]
    for k in range(TOP_K):
        gates_ref[k:k + 1, :] = es[k] / den

    r = lax.broadcasted_iota(I32, (tm, tm), 0)
    c = lax.broadcasted_iota(I32, (tm, tm), 1)
    before = ((r < c) & ((r // MOE_SUB) == (c // MOE_SUB))).astype(BF16)
    prefix = jnp.dot(onehot.astype(BF16), before, preferred_element_type=F32)
    for k in range(TOP_K):
        rank_ref[k:k + 1, :] = jnp.sum(jnp.where(sels[k], prefix, 0.0), axis=0, keepdims=True).astype(I32)
    lane = lax.broadcasted_iota(I32, (N_EXPERTS, LANES), 1)
    cnt = cnt_ref[...]
    for g in range(tm // MOE_SUB):
        c_g = jnp.sum(onehot[:, g * MOE_SUB:(g + 1) * MOE_SUB], axis=1, keepdims=True)
        cnt = jnp.where(lane == step * (tm // MOE_SUB) + g, c_g, cnt)
    cnt_ref[...] = cnt


def _merge(x2, y_ssd, o_mla, gate, w_ss, w_ml, w_o, g_ffn, w_rt, b_rt, tm):
    t, d = x2.shape
    full = lambda shape: pl.BlockSpec(shape, lambda i: (0,) * len(shape))
    tok = lambda w: pl.BlockSpec((tm, w), lambda i: (i, 0))
    sel = pl.BlockSpec((TOP_K, tm), lambda i: (0, i))
    return pl.pallas_call(
        functools.partial(_merge_kernel, tm=tm),
        grid=(t // tm,),
        in_specs=[tok(d), tok(d), tok(d), tok(2 * d), full((d, d)), full((d, d)), full((d, d)),
                  full((1, d)), full((N_EXPERTS, d)), full((N_EXPERTS, 1))],
        out_specs=(tok(d), tok(d), sel, sel, sel, full((N_EXPERTS, LANES))),
        out_shape=(jax.ShapeDtypeStruct((t, d), F32), jax.ShapeDtypeStruct((t, d), BF16),
                   jax.ShapeDtypeStruct((TOP_K, t), I32), jax.ShapeDtypeStruct((TOP_K, t), F32),
                   jax.ShapeDtypeStruct((TOP_K, t), I32), jax.ShapeDtypeStruct((N_EXPERTS, LANES), F32)),
        compiler_params=_cparams(("arbitrary",)),
        name="merge_route",
    )(x2, y_ssd, o_mla, gate, w_ss, w_ml, w_o, g_ffn, w_rt, b_rt)


def _excl_cumsum_rows(v):
    eid = lax.broadcasted_iota(I32, v.shape, 0)
    out = jnp.zeros(v.shape, F32)
    for e in range(N_EXPERTS - 1):
        out = out + jnp.where(eid > e, v[e:e + 1, :], 0.0)
    return out


def _tables_kernel(cnt_ref, idx_ref, rank_ref, slot_ref, seg_ref, loff_ref, gdst_ref, tail_ref):
    t = idx_ref.shape[1]
    cnt = cnt_ref[...]
    seg = jnp.ceil(cnt * (1.0 / SEG_ALIGN)) * SEG_ALIGN
    loff = _excl_cumsum_rows(seg)
    r = lax.broadcasted_iota(I32, (LANES, LANES), 0)
    c = lax.broadcasted_iota(I32, (LANES, LANES), 1)
    run = jnp.dot((seg * (1.0 / SEG_ALIGN)).astype(BF16), (r < c).astype(BF16),
                  preferred_element_type=F32) * SEG_ALIGN
    tot = jnp.sum(seg, axis=1, keepdims=True)
    padded = jnp.broadcast_to(jnp.ceil(tot * (1.0 / MOE_BLOCK)) * MOE_BLOCK, (N_EXPERTS, LANES))
    start = _excl_cumsum_rows(padded)
    end = start + padded
    seg_ref[...] = seg.astype(I32)
    loff_ref[...] = loff.astype(I32)
    gdst_ref[...] = (start + run).astype(I32)

    gr = lax.broadcasted_iota(I32, (LANES, t), 0)
    gc = lax.broadcasted_iota(I32, (LANES, t), 1) // MOE_SUB
    loff_tok = jnp.dot((loff * (1.0 / SEG_ALIGN)).astype(BF16), (gr == gc).astype(BF16),
                       preferred_element_type=F32) * SEG_ALIGN
    idx = idx_ref[...]
    slot = rank_ref[...]
    for e in range(N_EXPERTS):
        slot = slot + jnp.where(idx == e, loff_tok[e:e + 1, :].astype(I32), 0)
    slot_ref[...] = slot

    lane = lax.broadcasted_iota(I32, (N_EXPERTS, LANES), 1)
    total = jnp.max(end, axis=0, keepdims=True)
    tail = jnp.where(lane == 0, start + tot, jnp.where(lane == 1, end, jnp.where(lane == 3, start, total)))
    tail_ref[...] = tail.astype(I32)


def _tables(cnt, idx_t, rank_t):
    t = idx_t.shape[1]
    tab = jax.ShapeDtypeStruct((N_EXPERTS, LANES), I32)
    return pl.pallas_call(
        _tables_kernel,
        out_shape=(jax.ShapeDtypeStruct((TOP_K, t), I32), tab, tab, tab, tab),
        compiler_params=pltpu.CompilerParams(vmem_limit_bytes=VMEM_LIMIT),
        name="route_tables",
    )(cnt, idx_t, rank_t)


def _pow2_sizes(limit):
    size = SEG_ALIGN
    while size * 2 <= limit:
        size *= 2
    sizes = []
    while size >= SEG_ALIGN:
        sizes.append(size)
        size //= 2
    return tuple(sizes)


def _piece_copies(src_ref, src_off, dst_ref, dst_off, n, limit, sem, wait=False):
    off = 0
    for piece, size in enumerate(_pow2_sizes(limit)):
        take = n & size

        @pl.when(take != 0)
        def _(off=off, size=size, piece=piece):
            cp = pltpu.make_async_copy(
                src_ref.at[pl.ds(pl.multiple_of(src_off + off, SEG_ALIGN), size), :],
                dst_ref.at[pl.ds(pl.multiple_of(dst_off + off, SEG_ALIGN), size), :], sem)
            if wait:
                cp.wait()
            else:
                cp.start(priority=piece % 2)

        off = off + take


def _dispatch_kernel(seg_ref, loff_ref, gdst_ref, taillo_ref, tailhi_ref, used_ref,
                     h_ref, slot_ref, xs_ref, loc_scr, zero_scr, sem, zsem, *, nblk):
    i = pl.program_id(0)
    par = i % 2

    @pl.when(i == 0)
    def _():
        nblk_all = xs_ref.shape[0] // MOE_BLOCK
        zero_scr[...] = jnp.zeros_like(zero_scr)

        def blk_copy(b):
            start = pl.multiple_of(b * MOE_BLOCK, MOE_BLOCK)
            return pltpu.make_async_copy(zero_scr, xs_ref.at[pl.ds(start, MOE_BLOCK), :], zsem)

        def issue_blk(b, carry):
            blk_copy(b).start()
            return carry

        def drain_blk(b, carry):
            blk_copy(b).wait()
            return carry

        lax.fori_loop(used_ref[0], nblk_all, issue_blk, 0)
        lax.fori_loop(used_ref[0], nblk_all, drain_blk, 0)

        def tails(wait):
            def body(e, carry):
                lo = taillo_ref[e]
                _piece_copies(zero_scr, 0, xs_ref, lo, tailhi_ref[e] - lo, MOE_BLOCK - 1, zsem, wait=wait)
                return carry
            return body

        lax.fori_loop(0, N_EXPERTS, tails(False), 0)
        lax.fori_loop(0, N_EXPERTS, tails(True), 0)

    slots = slot_ref[...]
    rid = lax.broadcasted_iota(I32, (LOCAL_SLOTS, MOE_SUB), 0)
    hit = rid == slots[0:1, :]
    for k in range(1, TOP_K):
        hit = hit | (rid == slots[k:k + 1, :])
    loc_scr[par] = jnp.dot(hit.astype(BF16), h_ref[...], preferred_element_type=F32).astype(ROW_DTYPE)

    loc = loc_scr.at[par]

    def seg_body(e, total):
        n = seg_ref[e * LANES + i]
        _piece_copies(loc, loff_ref[e * LANES + i], xs_ref, gdst_ref[e * LANES + i], n, MOE_SUB, sem.at[par])
        return total + n

    total = lax.fori_loop(0, N_EXPERTS, seg_body, 0)
    spare = (nblk + par * (LOCAL_SLOTS // MOE_BLOCK)) * MOE_BLOCK
    _piece_copies(loc, total, xs_ref, spare + total, LOCAL_SLOTS - total, LOCAL_SLOTS, sem.at[par])

    def wait_step(p):
        pltpu.make_async_copy(loc_scr.at[p], xs_ref.at[pl.ds(0, LOCAL_SLOTS), :], sem.at[p]).wait()

    @pl.when(i > 0)
    def _():
        wait_step(1 - par)

    @pl.when(i == pl.num_programs(0) - 1)
    def _():
        wait_step(par)


def _dispatch(seg, loff, gdst, tail_lo, tail_hi, used, h2, slot_t, nblk):
    t, d = h2.shape
    rows = (nblk + 2 * (LOCAL_SLOTS // MOE_BLOCK)) * MOE_BLOCK
    return pl.pallas_call(
        functools.partial(_dispatch_kernel, nblk=nblk),
        grid_spec=pltpu.PrefetchScalarGridSpec(
            num_scalar_prefetch=6,
            grid=(t // MOE_SUB,),
            in_specs=[pl.BlockSpec((MOE_SUB, d), lambda i, *_: (i, 0)),
                      pl.BlockSpec((TOP_K, MOE_SUB), lambda i, *_: (0, i))],
            out_specs=pl.BlockSpec(memory_space=pl.ANY),
            scratch_shapes=[pltpu.VMEM((2, LOCAL_SLOTS, d), ROW_DTYPE), pltpu.VMEM((MOE_BLOCK, d), ROW_DTYPE),
                            pltpu.SemaphoreType.DMA((2,)), pltpu.SemaphoreType.DMA(())]),
        out_shape=jax.ShapeDtypeStruct((rows, d), ROW_DTYPE),
        compiler_params=_cparams(("arbitrary",)),
        name="moe_dispatch",
    )(seg, loff, gdst, tail_lo, tail_hi, used, h2, slot_t)


def _expert_kernel(lo_ref, hi_ref, used_ref, x_ref, wgu_ref, bgu_ref, wd_ref, bd_ref, y_ref,
                   wgu_scr, wd_scr, xbuf, ybuf, xsem, ysem):
    e = pl.program_id(0)
    ne = pl.num_programs(0)
    ff = wd_ref.shape[1]
    big = 2 * MOE_BLOCK

    def layout(ex):
        first = lo_ref[ex] // MOE_BLOCK
        n = (hi_ref[ex] - lo_ref[ex]) // MOE_BLOCK
        return first, n // 2, n // 2 + n % 2

    cur = layout(e)
    first, npair, items = cur

    def x_copy(lay, k, rows):
        start = pl.multiple_of((lay[0] + 2 * k) * MOE_BLOCK, MOE_BLOCK)
        return pltpu.make_async_copy(x_ref.at[pl.ds(start, rows), :], xbuf.at[k % 2, pl.ds(0, rows), :],
                                     xsem.at[k % 2])

    def y_copy(k, rows):
        start = pl.multiple_of((first + 2 * k) * MOE_BLOCK, MOE_BLOCK)
        return pltpu.make_async_copy(ybuf.at[k % 2, pl.ds(0, rows), :], y_ref.at[pl.ds(start, rows), :],
                                     ysem.at[k % 2])

    def start_x(lay, k):
        @pl.when(k < lay[1])
        def _():
            x_copy(lay, k, big).start(priority=ROW_COPY_PRIORITY)

        @pl.when((k == lay[1]) & (k < lay[2]))
        def _():
            x_copy(lay, k, MOE_BLOCK).start(priority=ROW_COPY_PRIORITY)

    def compute(k, rows):
        slot = k % 2
        xb = xbuf[slot, 0:rows, :].astype(BF16)
        gu = jnp.dot(xb, wgu_scr[...], preferred_element_type=F32) + bgu_ref[0]
        glu = jnp.minimum(gu[:, :ff], SWIGLU_LIMIT)
        lin = jnp.clip(gu[:, ff:], -SWIGLU_LIMIT, SWIGLU_LIMIT)
        act = glu * _sigmoid(SWIGLU_ALPHA * glu) * (lin + 1.0)
        y = jnp.dot(act.astype(BF16), wd_scr[...], preferred_element_type=F32) + bd_ref[0]
        ybuf[slot, 0:rows, :] = y.astype(ybuf.dtype)

    def item(k, rows):
        x_copy(cur, k, rows).wait()

        @pl.when(k >= 1)
        def _():
            start_x(cur, k + 1)

        @pl.when(k >= 2)
        def _():
            y_copy(k - 2, big).wait()

        compute(k, rows)
        y_copy(k, rows).start(priority=ROW_COPY_PRIORITY)

    @pl.when(e == 0)
    def _():
        start_x(cur, 0)
        start_x(cur, 1)

    @pl.when(items > 0)
    def _():
        wgu_scr[...] = wgu_ref[0].astype(BF16)
        wd_scr[...] = wd_ref[0].astype(BF16)

    def body(k, carry):
        item(k, big)
        return carry

    lax.fori_loop(0, npair, body, 0)

    @pl.when(items > npair)
    def _():
        item(npair, MOE_BLOCK)

    @pl.when(items >= 2)
    def _():
        y_copy(items - 2, big).wait()

    @pl.when(items > npair)
    def _():
        y_copy(items - 1, MOE_BLOCK).wait()

    @pl.when((items == npair) & (items >= 1))
    def _():
        y_copy(items - 1, big).wait()

    @pl.when(e + 1 < ne)
    def _():
        nxt = layout(jnp.minimum(e + 1, ne - 1))
        start_x(nxt, 0)
        start_x(nxt, 1)

    @pl.when(e == pl.num_programs(0) - 1)
    def _():
        nblk_all = y_ref.shape[0] // MOE_BLOCK
        ybuf[0] = jnp.zeros(ybuf.shape[1:], ybuf.dtype)

        def zero_copy(b):
            start = pl.multiple_of(b * MOE_BLOCK, MOE_BLOCK)
            return pltpu.make_async_copy(ybuf.at[0, pl.ds(0, MOE_BLOCK), :],
                                         y_ref.at[pl.ds(start, MOE_BLOCK), :], ysem.at[0])

        def issue(b, carry):
            zero_copy(b).start()
            return carry

        def drain(b, carry):
            zero_copy(b).wait()
            return carry

        lax.fori_loop(used_ref[0], nblk_all, issue, 0)
        lax.fori_loop(used_ref[0], nblk_all, drain, 0)


def _experts(row_lo, row_hi, used, xs, w_gate_up, b_gate_up, w_down, b_down):
    rows, d = xs.shape
    ne, _, ff2 = w_gate_up.shape
    ff = ff2 // 2
    return pl.pallas_call(
        _expert_kernel,
        grid_spec=pltpu.PrefetchScalarGridSpec(
            num_scalar_prefetch=3,
            grid=(ne,),
            in_specs=[pl.BlockSpec(memory_space=pl.ANY),
                      pl.BlockSpec((1, d, ff2), lambda e, *_: (e, 0, 0)),
                      pl.BlockSpec((1, 1, ff2), lambda e, *_: (e, 0, 0)),
                      pl.BlockSpec((1, ff, d), lambda e, *_: (e, 0, 0)),
                      pl.BlockSpec((1, 1, d), lambda e, *_: (e, 0, 0))],
            out_specs=pl.BlockSpec(memory_space=pl.ANY),
            scratch_shapes=[pltpu.VMEM((d, ff2), BF16), pltpu.VMEM((ff, d), BF16),
                            pltpu.VMEM((2, 2 * MOE_BLOCK, d), ROW_DTYPE),
                            pltpu.VMEM((2, 2 * MOE_BLOCK, d), ROW_DTYPE),
                            pltpu.SemaphoreType.DMA((2,)), pltpu.SemaphoreType.DMA((2,))]),
        out_shape=jax.ShapeDtypeStruct((rows, d), ROW_DTYPE),
        compiler_params=_cparams(("arbitrary",)),
        name="moe_experts",
    )(row_lo, row_hi, used, xs, w_gate_up, b_gate_up.reshape(ne, 1, ff2), w_down, b_down.reshape(ne, 1, d))


def _combine_kernel(seg_ref, loff_ref, gdst_ref, x1_ref, slot_ref, gates_ref, ys_ref, o_ref, loc_scr, sem):
    i = pl.program_id(0)
    par = i % 2

    def fetch(g, p):
        loc = loc_scr.at[p]

        def seg_body(e, total):
            n = seg_ref[e * LANES + g]
            _piece_copies(ys_ref, gdst_ref[e * LANES + g], loc, loff_ref[e * LANES + g], n, MOE_SUB, sem.at[p])
            return total + n

        total = lax.fori_loop(0, N_EXPERTS, seg_body, 0)
        _piece_copies(ys_ref, total, loc, total, LOCAL_SLOTS - total, LOCAL_SLOTS, sem.at[p])

    @pl.when(i == 0)
    def _():
        fetch(0, 0)

    @pl.when(i + 1 < pl.num_programs(0))
    def _():
        fetch(i + 1, 1 - par)

    pltpu.make_async_copy(ys_ref.at[pl.ds(0, LOCAL_SLOTS), :], loc_scr.at[par], sem.at[par]).wait()

    slots = slot_ref[...]
    g = gates_ref[...]
    rid = lax.broadcasted_iota(I32, (MOE_SUB, LOCAL_SLOTS), 1)
    gmat = jnp.where(rid == slots[:, 0:1], g[:, 0:1], 0.0)
    for k in range(1, TOP_K):
        gmat = gmat + jnp.where(rid == slots[:, k:k + 1], g[:, k:k + 1], 0.0)
    o_ref[...] = x1_ref[...] + jnp.dot(gmat.astype(BF16), loc_scr[par].astype(BF16),
                                       preferred_element_type=F32)


def _combine(seg, loff, gdst, x1, slot_tk, gates_tk, ys):
    t, d = x1.shape
    tok = lambda w: pl.BlockSpec((MOE_SUB, w), lambda i, *_: (i, 0))
    return pl.pallas_call(
        _combine_kernel,
        grid_spec=pltpu.PrefetchScalarGridSpec(
            num_scalar_prefetch=3,
            grid=(t // MOE_SUB,),
            in_specs=[tok(d), tok(TOP_K), tok(TOP_K), pl.BlockSpec(memory_space=pl.ANY)],
            out_specs=tok(d),
            scratch_shapes=[pltpu.VMEM((2, LOCAL_SLOTS, d), ROW_DTYPE), pltpu.SemaphoreType.DMA((2,))]),
        out_shape=jax.ShapeDtypeStruct((t, d), F32),
        compiler_params=_cparams(("arbitrary",)),
        name="moe_combine",
    )(seg, loff, gdst, x1, slot_tk, gates_tk, ys)


def _rope_tables(s):
    pos = np.arange(s, dtype=np.float64)
    inv = ROPE_BASE ** (-np.arange(0, MLA_ROPE, 2, dtype=np.float64) / MLA_ROPE)
    ang = pos[:, None] * inv[None, :]
    cos, sin = jnp.asarray(np.cos(ang), F32), jnp.asarray(np.sin(ang), F32)
    return _spread_rope(jnp.concatenate([cos, cos], axis=-1)), _spread_rope(jnp.concatenate([-sin, sin], axis=-1))


def _spread_rope(w, gap=None):
    half = MLA_ROPE // 2
    zeros = jnp.zeros(w.shape[:-1] + (half,), w.dtype)
    return jnp.concatenate([w[..., :half], zeros if gap is None else gap, w[..., half:], zeros], axis=-1)


def _pad_heads(w):
    lead = w.shape[:-1]
    w = w.reshape(lead + (MLA_HEADS, MLA_QK_DIM))
    w = jnp.concatenate([w[..., :MLA_NOPE], _spread_rope(w[..., MLA_NOPE:])], axis=-1)
    return w.reshape(lead + (MLA_HEADS * 2 * LANES,))


def _layer(x, g_mix, w_in, conv_w, conv_b, dt_bias, a_log, d_skip, g_ssd_out, w_ssd_out,
           g_q_lat, w_q_up, g_kv_lat, w_kv_up, g_qk_q, g_qk_k, w_mla_out, w_o,
           g_ffn, w_router, b_router, w_gate_up, b_gate_up, w_down, b_down):
    b, s, d = x.shape
    t = b * s
    x2 = x.reshape(t, d)

    off_xbc = SSD_D_INNER
    off_dt = off_xbc + SSD_CONV_DIM
    off_ql = off_dt + SSD_HEADS
    off_kvl = off_ql + MLA_Q_RANK
    off_kr = off_kvl + MLA_KV_RANK
    off_gate = off_kr + MLA_ROPE
    dt_gap = jnp.pad(w_in[:, off_dt:off_ql], ((0, 0), (0, MLA_ROPE // 2 - SSD_HEADS)))
    w_streams = [w_in[:, :off_xbc], w_in[:, off_xbc:off_dt], w_in[:, off_ql:off_kvl], w_in[:, off_kvl:off_kr],
                 _spread_rope(w_in[:, off_kr:off_gate], gap=dt_gap), w_in[:, off_gate:]]

    tiles = _tile_sizes(s)
    z, xbc, ql, kvl, krdt, gate = _in_proj(x2, g_mix, [w.astype(BF16) for w in w_streams], tiles.tokens)

    lane_pad = lambda vec: jnp.pad(vec, (DT_LO, LANES - DT_LO - SSD_HEADS)).reshape(1, LANES)
    lt = tiles.ssd
    y_ssd = _ssd(xbc.reshape(b, s, SSD_CONV_DIM), z.reshape(b, s, SSD_D_INNER), krdt.reshape(b, s, LANES),
                 conv_w, conv_b.reshape(1, -1), lane_pad(dt_bias), lane_pad(a_log),
                 jnp.repeat(d_skip, SSD_HEAD_DIM).reshape(1, -1), g_ssd_out.reshape(1, -1), lt)

    cos_t, sin_t = _rope_tables(s)
    wq_pad = _pad_heads(w_q_up).astype(BF16)
    spread_gain = lambda g: jnp.concatenate([g[:MLA_NOPE], _spread_rope(g[MLA_NOPE:])]).reshape(1, -1)
    gq_pad = spread_gain(g_qk_q)
    gk_pad = spread_gain(g_qk_k)
    tq = tiles.tokens
    w_kv_h = w_kv_up.reshape(MLA_KV_RANK, MLA_HEADS, MLA_NOPE + MLA_V)
    wk = w_kv_h[:, :, :MLA_NOPE].reshape(MLA_KV_RANK, MLA_HEADS * MLA_NOPE).astype(BF16)
    wvt = w_kv_h[:, :, MLA_NOPE:].reshape(MLA_KV_RANK, MLA_HEADS * MLA_V).T.astype(BF16)
    q, k, v = _qkv_prep(ql, kvl, krdt, cos_t, sin_t, g_q_lat.reshape(1, -1), wq_pad,
                        g_kv_lat.reshape(1, -1), wk, wvt, gq_pad, gk_pad, b, s, tq)
    o_mla = _flash(q, k, v, tiles.queries)

    x1, h2, idx_t, gates_t, rank_t, cnt = _merge(
        x2, y_ssd.reshape(t, d), o_mla.reshape(t, d), gate,
        w_ssd_out.astype(BF16), w_mla_out.astype(BF16), w_o.astype(BF16),
        g_ffn.reshape(1, -1), w_router.T, b_router.reshape(-1, 1), tiles.tokens)

    nsub = t // MOE_SUB
    assert t % MOE_SUB == 0 and nsub <= LANES
    cap = t * TOP_K + nsub * N_EXPERTS * (SEG_ALIGN - 1) + N_EXPERTS * (MOE_BLOCK - 1)
    nblk = -(-cap // MOE_BLOCK)
    slot_t, seg, loff, gdst, tail = _tables(cnt, idx_t, rank_t)
    used = (tail[0:1, 2] // MOE_BLOCK).astype(I32)
    seg, loff, gdst = seg.reshape(-1), loff.reshape(-1), gdst.reshape(-1)
    xs = _dispatch(seg, loff, gdst, tail[:, 0], tail[:, 1], used, h2, slot_t, nblk)
    ys = _experts(tail[:, 3], tail[:, 1], used, xs, w_gate_up, b_gate_up, w_down, b_down)
    out = _combine(seg, loff, gdst, x1, slot_t.T, gates_t.T, ys)
    return out.reshape(b, s, d)


def kernel(x, g_mix, w_in, conv_w, conv_b, dt_bias, a_log, d_skip, g_ssd_out, w_ssd_out, g_q_lat, w_q_up, g_kv_lat, w_kv_up, g_qk_q, g_qk_k, w_mla_out, w_o, g_ffn, w_router, b_router, w_gate_up, b_gate_up, w_down, b_down):
    params = (g_mix, w_in, conv_w, conv_b, dt_bias, a_log, d_skip, g_ssd_out, w_ssd_out, g_q_lat, w_q_up,
              g_kv_lat, w_kv_up, g_qk_q, g_qk_k, w_mla_out, w_o, g_ffn, w_router, b_router,
              w_gate_up, b_gate_up, w_down, b_down)
    for l in range(g_mix.shape[0]):
        x = _layer(x, *(p[l] for p in params))
    return x
```

```python
import functools
import math
from typing import NamedTuple

import jax
import jax.numpy as jnp
import numpy as np
from jax import lax
from jax.experimental import pallas as pl
from jax.experimental.pallas import tpu as pltpu

F32 = jnp.float32
BF16 = jnp.bfloat16
I32 = jnp.int32

EPS = 1e-6
CHUNK = 64

SSD_HEADS = 16
SSD_HEAD_DIM = 64
SSD_GROUPS = 2
SSD_STATE = 128
SSD_CONV = 4
SSD_D_INNER = SSD_HEADS * SSD_HEAD_DIM
SSD_CONV_DIM = SSD_D_INNER + 2 * SSD_GROUPS * SSD_STATE

MLA_HEADS = 8
MLA_Q_RANK = 384
MLA_KV_RANK = 256
MLA_NOPE = 128
MLA_ROPE = 64
MLA_QK_DIM = MLA_NOPE + MLA_ROPE
MLA_V = 128
V_ROWS = MLA_V + 16
DT_LO = MLA_ROPE // 2
SSD_SCAN_CHUNK = 128
ROPE_BASE = 10000.0

N_EXPERTS = 32
TOP_K = 4
SWIGLU_ALPHA = 1.702
SWIGLU_LIMIT = 7.0

LANES = 128
VMEM_LIMIT = 56 * 1024 * 1024
NEG_BIG = -1e30
LOG2E = math.log2(math.e)

MOE_BLOCK = 256
MOE_SUB = 256
ROW_DTYPE = F32
SEG_ALIGN = 8 * 4 // jnp.dtype(ROW_DTYPE).itemsize
LOCAL_SLOTS = -(-(TOP_K * MOE_SUB + N_EXPERTS * (SEG_ALIGN - 1)) // MOE_BLOCK) * MOE_BLOCK


class _Tiles(NamedTuple):
    tokens: int
    ssd: int
    queries: int


def _tile_sizes(seq):
    return _Tiles(tokens=min(512, seq), ssd=min(256, seq), queries=min(1024, seq))


def _cparams(semantics, **kw):
    return pltpu.CompilerParams(dimension_semantics=semantics,
                                vmem_limit_bytes=VMEM_LIMIT, **kw)


def _sigmoid(v):
    return 1.0 / (1.0 + jnp.exp(-v))


def _silu(v):
    return v * _sigmoid(v)


def _bf16_pieces(t, n):
    pieces = []
    for _ in range(n - 1):
        p = t.astype(BF16)
        pieces.append(p)
        t = t - p.astype(F32)
    pieces.append(t.astype(BF16))
    return pieces


def _inproj_kernel(x_ref, g_ref, wz_ref, wxbc_ref, wql_ref, wkvl_ref, wkrdt_ref, wgate_ref,
                   z_ref, xbc_ref, ql_ref, kvl_ref, krdt_ref, gate_ref):
    x = x_ref[...]
    ms = jnp.mean(x * x, axis=-1, keepdims=True)
    h = (x * lax.rsqrt(ms + EPS) * g_ref[...]).astype(BF16)
    streams = ((wz_ref, z_ref), (wxbc_ref, xbc_ref), (wql_ref, ql_ref), (wkvl_ref, kvl_ref),
               (wkrdt_ref, krdt_ref), (wgate_ref, gate_ref))
    for w_ref, ref in streams:
        p = jnp.dot(h, w_ref[...], preferred_element_type=F32)
        if ref is gate_ref:
            p = _sigmoid(p)
        ref[...] = p.astype(ref.dtype)


def _in_proj(x2, g_mix, weights, tm):
    t, d = x2.shape
    widths = tuple(w.shape[1] for w in weights)
    dts = (BF16, BF16, BF16, BF16, F32, BF16)
    out_shape = tuple(jax.ShapeDtypeStruct((t, w), dt) for w, dt in zip(widths, dts))
    out_specs = tuple(pl.BlockSpec((tm, w), lambda i: (i, 0)) for w in widths)
    return pl.pallas_call(
        _inproj_kernel,
        grid=(t // tm,),
        in_specs=[pl.BlockSpec((tm, d), lambda i: (i, 0)),
                  pl.BlockSpec((1, d), lambda i: (0, 0))]
                 + [pl.BlockSpec((d, w), lambda i: (0, 0)) for w in widths],
        out_specs=out_specs,
        out_shape=out_shape,
        compiler_params=_cparams(("parallel",)),
        name="in_proj",
    )(x2, g_mix.reshape(1, d), *weights)


def _ssd_kernel(xbc_ref, z_ref, krdt_ref, convw_ref, convb_ref, dtb_ref, alog_ref, dskip_ref, gout_ref,
                y_ref, xext_scr, state_scr, *, lt):
    i = pl.program_id(1)
    halo = 8
    dt_lo = DT_LO
    gw = SSD_D_INNER // SSD_GROUPS
    hpg = SSD_HEADS // SSD_GROUPS

    @pl.when(i == 0)
    def _():
        state_scr[...] = jnp.zeros_like(state_scr)
        xext_scr[0:halo, :] = jnp.zeros((halo, SSD_CONV_DIM), F32)

    xin = xbc_ref[0]
    xext_scr[halo:2 * halo, :] = xin[0:halo, :].astype(F32)
    srow = lax.broadcasted_iota(I32, (lt, lt), 0)
    scol = lax.broadcasted_iota(I32, (lt, lt), 1)
    acc = convb_ref[...] + convw_ref[SSD_CONV - 1:SSD_CONV, :] * xin.astype(F32)
    head = jnp.broadcast_to(convb_ref[...], (halo, SSD_CONV_DIM))
    for j in range(SSD_CONV):
        shift = SSD_CONV - 1 - j
        head = head + convw_ref[j:j + 1, :] * xext_scr[pl.ds(halo - shift, halo), :]
        if shift:
            shifted = jnp.dot((srow == scol + shift).astype(BF16), xin, preferred_element_type=F32)
            acc = acc + convw_ref[j:j + 1, :] * shifted
    acc = jnp.concatenate([head, acc[halo:, :]], axis=0)
    xext_scr[0:halo, :] = xin[lt - halo:lt, :].astype(F32)
    xbc = _silu(acc)
    xs = xbc[:, :SSD_D_INNER]
    bm = xbc[:, SSD_D_INNER:SSD_D_INNER + SSD_GROUPS * SSD_STATE]
    cm = xbc[:, SSD_D_INNER + SSD_GROUPS * SSD_STATE:]

    lane = lax.broadcasted_iota(I32, (1, LANES), 1)
    head_lane = (lane >= dt_lo) & (lane < dt_lo + SSD_HEADS)
    v = krdt_ref[0] + dtb_ref[...]
    dt = jnp.maximum(v, 0.0) + jnp.log(1.0 + jnp.exp(-jnp.abs(v)))
    dt = jnp.where(head_lane, dt, 0.0)
    a = jnp.where(head_lane, -jnp.exp(alog_ref[...]) * LOG2E, 0.0)
    da = dt * a
    lc = min(SSD_SCAN_CHUNK, lt)
    nchunk = lt // lc
    row = lax.broadcasted_iota(I32, (lt, lt), 0)
    col = lax.broadcasted_iota(I32, (lt, lt), 1)
    tri_b = ((row >= col) & ((row // lc) == (col // lc))).astype(BF16)
    a_cum = jnp.zeros((lt, LANES), F32)
    for piece in _bf16_pieces(da, 3):
        a_cum = a_cum + jnp.dot(tri_b, piece, preferred_element_type=F32)
    a_cum_t = a_cum.T
    a_last = jnp.concatenate([jnp.broadcast_to(a_cum[(c + 1) * lc - 1:(c + 1) * lc, :], (lc, LANES))
                              for c in range(nchunk)], axis=0)
    exp_a = jnp.exp2(a_cum)
    dec = jnp.exp2(a_last - a_cum)
    tri = lax.broadcasted_iota(I32, (lc, lc), 0) >= lax.broadcasted_iota(I32, (lc, lc), 1)

    er = lax.broadcasted_iota(I32, (LANES, SSD_D_INNER), 0)
    ec = lax.broadcasted_iota(I32, (LANES, SSD_D_INNER), 1)
    expand = ((er - dt_lo) == (ec // SSD_HEAD_DIM)).astype(BF16)

    def widen(t, pieces):
        out = jnp.zeros((lt, SSD_D_INNER), F32)
        for piece in _bf16_pieces(t, pieces):
            out = out + jnp.dot(piece, expand, preferred_element_type=F32)
        return out

    dt_w = widen(dt, 2)
    exp_a_w = widen(exp_a, 1)
    dec_w = widen(dec, 1)
    xdt = xs * dt_w
    xdt_b = xdt.astype(BF16)
    xdec_b = (xdt * dec_w).astype(BF16)
    lane_pair = lax.broadcasted_iota(I32, (lc, LANES), 1)

    states = [state_scr[g] for g in range(SSD_GROUPS)]
    y_rows = [[] for _ in range(SSD_GROUPS)]
    for c in range(nchunk):
        rows = slice(c * lc, (c + 1) * lc)
        for g in range(SSD_GROUPS):
            cols = slice(g * gw, (g + 1) * gw)
            bg = bm[rows, g * SSD_STATE:(g + 1) * SSD_STATE].astype(BF16)
            cg = cm[rows, g * SSD_STATE:(g + 1) * SSD_STATE].astype(BF16)
            cb = lax.dot_general(cg, bg, (((1,), (1,)), ((), ())), preferred_element_type=F32)
            parts = []
            for pair in range(hpg // 2):
                h0 = g * hpg + 2 * pair
                rhs = xdt_b[rows, h0 * SSD_HEAD_DIM:(h0 + 2) * SSD_HEAD_DIM]
                res = []
                for h in (h0, h0 + 1):
                    seg = a_cum[rows, dt_lo + h:dt_lo + h + 1] - a_cum_t[dt_lo + h:dt_lo + h + 1, rows]
                    lmat = jnp.exp2(jnp.where(tri, seg, NEG_BIG))
                    res.append(jnp.dot((cb * lmat).astype(BF16), rhs, preferred_element_type=F32))
                parts.append(jnp.where(lane_pair < SSD_HEAD_DIM, res[0], res[1]))
            st = states[g]
            y_off = jnp.dot(cg, st.astype(BF16), preferred_element_type=F32)
            y_rows[g].append(jnp.concatenate(parts, axis=-1) + y_off * exp_a_w[rows, cols])
            new = lax.dot_general(bg, xdec_b[rows, cols], (((0,), (0,)), ((), ())),
                                  preferred_element_type=F32)
            states[g] = st * exp_a_w[(c + 1) * lc - 1:(c + 1) * lc, cols] + new
    for g in range(SSD_GROUPS):
        state_scr[g] = states[g]

    zs = _silu(z_ref[0].astype(F32))
    outs = []
    for g in range(SSD_GROUPS):
        yg = (jnp.concatenate(y_rows[g], axis=0)
              + dskip_ref[:, g * gw:(g + 1) * gw] * xs[:, g * gw:(g + 1) * gw])
        yg = yg * zs[:, g * gw:(g + 1) * gw]
        ms = jnp.mean(yg * yg, axis=-1, keepdims=True)
        outs.append(yg * lax.rsqrt(ms + EPS) * gout_ref[:, g * gw:(g + 1) * gw])
    y_ref[0] = jnp.concatenate(outs, axis=-1).astype(y_ref.dtype)


def _ssd(xbc, z, krdt, conv_w, conv_b, dtb, alog, dskip_w, g_out, lt):
    b, s, _ = xbc.shape
    gw = SSD_D_INNER // SSD_GROUPS
    full = lambda shape: pl.BlockSpec(shape, lambda bi, i: (0,) * len(shape))
    return pl.pallas_call(
        functools.partial(_ssd_kernel, lt=lt),
        grid=(b, s // lt),
        in_specs=[pl.BlockSpec((1, lt, SSD_CONV_DIM), lambda bi, i: (bi, i, 0)),
                  pl.BlockSpec((1, lt, SSD_D_INNER), lambda bi, i: (bi, i, 0)),
                  pl.BlockSpec((1, lt, LANES), lambda bi, i: (bi, i, 0)),
                  full((SSD_CONV, SSD_CONV_DIM)), full((1, SSD_CONV_DIM)),
                  full((1, LANES)), full((1, LANES)),
                  full((1, SSD_D_INNER)), full((1, SSD_D_INNER))],
        out_specs=pl.BlockSpec((1, lt, SSD_D_INNER), lambda bi, i: (bi, i, 0)),
        out_shape=jax.ShapeDtypeStruct((b, s, SSD_D_INNER), BF16),
        scratch_shapes=[pltpu.VMEM((16, SSD_CONV_DIM), F32),
                        pltpu.VMEM((SSD_GROUPS, SSD_STATE, gw), F32)],
        compiler_params=_cparams(("parallel", "arbitrary")),
        name="ssd_scan",
    )(xbc, z, krdt, conv_w, conv_b, dtb, alog, dskip_w, g_out)


def _swap_halves(t):
    return pltpu.roll(t, LANES // 2, 1)


def _lane_sums(sq, width):
    ones = jnp.ones((sq.shape[1], width), BF16)
    return jnp.dot(sq.astype(BF16), ones, preferred_element_type=F32)


def _qkv_kernel(ql_ref, kvl_ref, krdt_ref, cos_ref, sin_ref, cost_ref, sint_ref, gql_ref, wqt_ref, gkvl_ref,
                wk_ref, wvt_ref, gqt_ref, gk_ref, q_ref, k_ref, v_ref, *, scale):
    hw = 2 * LANES
    cosv = cos_ref[...]
    sinv = sin_ref[...]
    cost = cost_ref[...]
    sint = sint_ref[...]

    ql = ql_ref[...].astype(F32)
    rq = lax.rsqrt(_lane_sums(ql * ql, LANES) * (1.0 / MLA_Q_RANK) + EPS)
    qn = (ql * jnp.concatenate([rq] * (MLA_Q_RANK // LANES), axis=1) * gql_ref[...]).astype(BF16)

    kvl = kvl_ref[...].astype(F32)
    rkv = lax.rsqrt(_lane_sums(kvl * kvl, LANES) * (1.0 / MLA_KV_RANK) + EPS)
    kvn = (kvl * jnp.concatenate([rkv] * (MLA_KV_RANK // LANES), axis=1) * gkvl_ref[...]).astype(BF16)
    tm = kvn.shape[0]
    ones_rows = (lax.broadcasted_iota(I32, (V_ROWS - MLA_V, tm), 0) == 0).astype(v_ref.dtype)

    lane = lax.broadcasted_iota(I32, (1, LANES), 1)
    kr = jnp.where((lane % (LANES // 2)) < MLA_ROPE // 2, krdt_ref[...], 0.0)
    ss_r = _lane_sums(kr * kr, LANES)
    krg = kr * gk_ref[:, LANES:]
    kr_rot = krg * cosv + _swap_halves(krg) * sinv

    nt = (((1,), (1,)), ((), ()))
    qt = lax.dot_general(wqt_ref[...], qn, nt, preferred_element_type=F32)
    kf = jnp.dot(kvn, wk_ref[...], preferred_element_type=F32)
    vt = lax.dot_general(wvt_ref[...], kvn, nt, preferred_element_type=F32)
    gq_full = jnp.concatenate([gqt_ref[...]] * (tm // LANES), axis=1)
    half = LANES // 2
    for h in range(MLA_HEADS):
        qh = qt[h * hw:(h + 1) * hw, :]
        r = lax.rsqrt(jnp.sum(qh * qh, axis=0, keepdims=True) * (1.0 / MLA_QK_DIM) + EPS)
        qs = qh * (r * scale) * gq_full
        xr = qs[LANES:, :]
        rot = xr * cost + jnp.concatenate([xr[half:, :], xr[:half, :]], axis=0) * sint
        q_ref[0, h] = jnp.concatenate([qs[:LANES, :], rot], axis=0).astype(q_ref.dtype)

        kn = kf[:, h * LANES:(h + 1) * LANES]
        rk = lax.rsqrt((_lane_sums(kn * kn, LANES) + ss_r) * (1.0 / MLA_QK_DIM) + EPS)
        k_ref[0, h, :, 0:LANES] = (kn * rk * gk_ref[:, :LANES]).astype(k_ref.dtype)
        k_ref[0, h, :, LANES:hw] = (kr_rot * rk).astype(k_ref.dtype)
        v_ref[0, h, 0:MLA_V, :] = vt[h * MLA_V:(h + 1) * MLA_V, :].astype(v_ref.dtype)
        v_ref[0, h, MLA_V:V_ROWS, :] = ones_rows


def _qkv_prep(ql, kvl, krdt, cos_t, sin_t, g_q_lat, wq_pad, g_kv_lat, wk, wvt, gq_pad, gk_pad, b, s, tm):
    nst = s // tm
    wqt = wq_pad.T
    gqt = jnp.broadcast_to(gq_pad.reshape(-1, 1), (2 * LANES, LANES))
    cos_tt, sin_tt = cos_t.T, sin_t.T
    full = lambda shape: pl.BlockSpec(shape, lambda bi, i: (0,) * len(shape))
    tok = lambda w: pl.BlockSpec((tm, w), lambda bi, i: (bi * nst + i, 0))
    hs = lambda w: pl.BlockSpec((1, MLA_HEADS, tm, w), lambda bi, i: (bi, 0, i, 0))
    return pl.pallas_call(
        functools.partial(_qkv_kernel, scale=MLA_QK_DIM ** -0.5 * math.log2(math.e)),
        grid=(b, nst),
        in_specs=[tok(MLA_Q_RANK), tok(MLA_KV_RANK), tok(LANES),
                  pl.BlockSpec((tm, LANES), lambda bi, i: (i, 0)),
                  pl.BlockSpec((tm, LANES), lambda bi, i: (i, 0)),
                  pl.BlockSpec((LANES, tm), lambda bi, i: (0, i)),
                  pl.BlockSpec((LANES, tm), lambda bi, i: (0, i)),
                  full((1, MLA_Q_RANK)), full(wqt.shape), full((1, MLA_KV_RANK)), full(wk.shape),
                  full(wvt.shape), full(gqt.shape), full((1, 2 * LANES))],
        out_specs=(pl.BlockSpec((1, MLA_HEADS, 2 * LANES, tm), lambda bi, i: (bi, 0, 0, i)), hs(2 * LANES),
                   pl.BlockSpec((1, MLA_HEADS, V_ROWS, tm), lambda bi, i: (bi, 0, 0, i))),
        out_shape=(jax.ShapeDtypeStruct((b, MLA_HEADS, 2 * LANES, s), BF16),
                   jax.ShapeDtypeStruct((b, MLA_HEADS, s, 2 * LANES), BF16),
                   jax.ShapeDtypeStruct((b, MLA_HEADS, V_ROWS, s), BF16)),
        compiler_params=_cparams(("parallel", "parallel")),
        name="qkv_prep",
    )(ql, kvl, krdt, cos_t, sin_t, cos_tt, sin_tt, g_q_lat, wqt, g_kv_lat, wk, wvt, gqt, gk_pad)


def _flash_kernel(qt_ref, k_ref, vt_ref, o_ref, m_scr, acc_scr, s_scr, *, tq):
    i = pl.program_id(2)
    tk = tq // 2
    qt = qt_ref[0, 0]
    m_scr[...] = jnp.full(m_scr.shape, NEG_BIG, F32)
    acc_scr[...] = jnp.zeros(acc_scr.shape, F32)

    def scores(j, slot, lo=0):
        start = pl.multiple_of(j * tk, tk)
        ks = k_ref[0, 0, pl.ds(start, tk), :]
        s_scr[slot, :, lo:] = jnp.dot(ks, qt[:, lo:], preferred_element_type=F32)

    def softmax_pv(j, slot, masked, lo=0):
        start = pl.multiple_of(j * tk, tk)
        vt = vt_ref[0, 0, :, pl.ds(start, tk)]
        st = s_scr[slot, :, lo:]
        if masked:
            kc = lax.broadcasted_iota(I32, (tk, tk), 0) // CHUNK
            qc = lax.broadcasted_iota(I32, (tk, tk), 1) // CHUNK
            diag = jnp.where(kc <= qc, st[:, :tk], NEG_BIG)
            st = diag if st.shape[1] == tk else jnp.concatenate([diag, st[:, tk:]], axis=1)
        m_prev = m_scr[:, lo:]
        m_new = jnp.maximum(m_prev, jnp.max(st, axis=0, keepdims=True))
        alpha = jnp.exp2(m_prev - m_new)
        pt = jnp.exp2(st - m_new)
        acc_scr[:, lo:] = alpha * acc_scr[:, lo:] + jnp.dot(vt, pt.astype(BF16), preferred_element_type=F32)
        m_scr[:, lo:] = m_new

    scores(0, 0)

    def pair(j):
        scores(j + 1, 1)
        softmax_pv(j, 0, False)
        scores(j + 2, 0)
        softmax_pv(j + 1, 1, False)

    def body(jj, carry):
        pair(4 * jj)
        pair(4 * jj + 2)
        return carry

    lax.fori_loop(0, i // 2, body, 0)

    @pl.when(i % 2 == 1)
    def _():
        pair(2 * i - 2)

    scores(2 * i + 1, 1, lo=tk)
    softmax_pv(2 * i, 0, True)
    softmax_pv(2 * i + 1, 1, True, lo=tk)

    o_ref[0] = (acc_scr[0:MLA_V, :] / acc_scr[MLA_V:MLA_V + 1, :]).T.astype(o_ref.dtype)


def _flash(qt, k, v, tq):
    b, nh, s, _ = k.shape
    return pl.pallas_call(
        functools.partial(_flash_kernel, tq=tq),
        grid=(b, nh, s // tq),
        in_specs=[pl.BlockSpec((1, 1, qt.shape[2], tq), lambda bi, h, i: (bi, h, 0, i)),
                  pl.BlockSpec((1, 1, s, k.shape[-1]), lambda bi, h, i: (bi, h, 0, 0)),
                  pl.BlockSpec((1, 1, V_ROWS, s), lambda bi, h, i: (bi, h, 0, 0))],
        out_specs=pl.BlockSpec((1, tq, MLA_V), lambda bi, h, i: (bi, i, h)),
        out_shape=jax.ShapeDtypeStruct((b, s, nh * MLA_V), BF16),
        scratch_shapes=[pltpu.VMEM((1, tq), F32), pltpu.VMEM((V_ROWS, tq), F32),
                        pltpu.VMEM((2, tq // 2, tq), F32)],
        compiler_params=_cparams(("parallel", "parallel", "arbitrary")),
        name="flash_attn",
    )(qt, k, v)


def _merge_kernel(x_ref, ys_ref, om_ref, gate_ref, wss_ref, wml_ref, wo_ref, gffn_ref, wrt_ref, brt_ref,
                  x1_ref, h2_ref, idx_ref, gates_ref, rank_ref, cnt_ref, *, tm):
    d = x_ref.shape[-1]
    step = pl.program_id(0)

    @pl.when(step == 0)
    def _():
        cnt_ref[...] = jnp.zeros_like(cnt_ref)

    y1 = jnp.dot(ys_ref[...], wss_ref[...], preferred_element_type=F32)
    y2 = jnp.dot(om_ref[...], wml_ref[...], preferred_element_type=F32)
    g = gate_ref[...].astype(F32)
    merged = (g[:, :d] * y1 + g[:, d:] * y2).astype(BF16)
    x1 = x_ref[...] + jnp.dot(merged, wo_ref[...], preferred_element_type=F32)
    x1_ref[...] = x1
    ms = jnp.mean(x1 * x1, axis=-1, keepdims=True)
    h2 = x1 * lax.rsqrt(ms + EPS) * gffn_ref[...]
    h2_ref[...] = h2.astype(h2_ref.dtype)

    w_hi, w_lo = _bf16_pieces(wrt_ref[...], 2)
    h_hi, h_lo = _bf16_pieces(h2, 2)
    nt = (((1,), (1,)), ((), ()))
    logits = (lax.dot_general(w_hi, h_hi, nt, preferred_element_type=F32)
              + lax.dot_general(w_hi, h_lo, nt, preferred_element_type=F32)
              + lax.dot_general(w_lo, h_hi, nt, preferred_element_type=F32)) + brt_ref[...]
    eid = lax.broadcasted_iota(I32, (N_EXPERTS, tm), 0)
    cur = logits
    onehot = jnp.zeros((N_EXPERTS, tm), F32)
    vals, sels = [], []
    for k in range(TOP_K):
        mx = jnp.max(cur, axis=0, keepdims=True)
        idx = jnp.min(jnp.where(cur == mx, eid, N_EXPERTS), axis=0, keepdims=True)
        sel = eid == idx
        vals.append(mx)
        sels.append(sel)
        idx_ref[k:k + 1, :] = idx
        cur = jnp.where(sel, -jnp.inf, cur)
        onehot = onehot + sel.astype(F32)
    es = [jnp.exp(vk - vals[0]) for vk in vals]
    den = es[0] + es[1] + es[2] + es[3]
    for k in range(TOP_K):
        gates_ref[k:k + 1, :] = es[k] / den

    r = lax.broadcasted_iota(I32, (tm, tm), 0)
    c = lax.broadcasted_iota(I32, (tm, tm), 1)
    before = ((r < c) & ((r // MOE_SUB) == (c // MOE_SUB))).astype(BF16)
    prefix = jnp.dot(onehot.astype(BF16), before, preferred_element_type=F32)
    for k in range(TOP_K):
        rank_ref[k:k + 1, :] = jnp.sum(jnp.where(sels[k], prefix, 0.0), axis=0, keepdims=True).astype(I32)
    lane = lax.broadcasted_iota(I32, (N_EXPERTS, LANES), 1)
    cnt = cnt_ref[...]
    for g in range(tm // MOE_SUB):
        c_g = jnp.sum(onehot[:, g * MOE_SUB:(g + 1) * MOE_SUB], axis=1, keepdims=True)
        cnt = jnp.where(lane == step * (tm // MOE_SUB) + g, c_g, cnt)
    cnt_ref[...] = cnt


def _merge(x2, y_ssd, o_mla, gate, w_ss, w_ml, w_o, g_ffn, w_rt, b_rt, tm):
    t, d = x2.shape
    full = lambda shape: pl.BlockSpec(shape, lambda i: (0,) * len(shape))
    tok = lambda w: pl.BlockSpec((tm, w), lambda i: (i, 0))
    sel = pl.BlockSpec((TOP_K, tm), lambda i: (0, i))
    return pl.pallas_call(
        functools.partial(_merge_kernel, tm=tm),
        grid=(t // tm,),
        in_specs=[tok(d), tok(d), tok(d), tok(2 * d), full((d, d)), full((d, d)), full((d, d)),
                  full((1, d)), full((N_EXPERTS, d)), full((N_EXPERTS, 1))],
        out_specs=(tok(d), tok(d), sel, sel, sel, full((N_EXPERTS, LANES))),
        out_shape=(jax.ShapeDtypeStruct((t, d), F32), jax.ShapeDtypeStruct((t, d), BF16),
                   jax.ShapeDtypeStruct((TOP_K, t), I32), jax.ShapeDtypeStruct((TOP_K, t), F32),
                   jax.ShapeDtypeStruct((TOP_K, t), I32), jax.ShapeDtypeStruct((N_EXPERTS, LANES), F32)),
        compiler_params=_cparams(("arbitrary",)),
        name="merge_route",
    )(x2, y_ssd, o_mla, gate, w_ss, w_ml, w_o, g_ffn, w_rt, b_rt)


def _excl_cumsum_rows(v):
    eid = lax.broadcasted_iota(I32, v.shape, 0)
    out = jnp.zeros(v.shape, F32)
    for e in range(N_EXPERTS - 1):
        out = out + jnp.where(eid > e, v[e:e + 1, :], 0.0)
    return out


def _tables_kernel(cnt_ref, idx_ref, rank_ref, slot_ref, seg_ref, loff_ref, gdst_ref, tail_ref):
    t = idx_ref.shape[1]
    cnt = cnt_ref[...]
    seg = jnp.ceil(cnt * (1.0 / SEG_ALIGN)) * SEG_ALIGN
    loff = _excl_cumsum_rows(seg)
    r = lax.broadcasted_iota(I32, (LANES, LANES), 0)
    c = lax.broadcasted_iota(I32, (LANES, LANES), 1)
    run = jnp.dot((seg * (1.0 / SEG_ALIGN)).astype(BF16), (r < c).astype(BF16),
                  preferred_element_type=F32) * SEG_ALIGN
    tot = jnp.sum(seg, axis=1, keepdims=True)
    padded = jnp.broadcast_to(jnp.ceil(tot * (1.0 / MOE_BLOCK)) * MOE_BLOCK, (N_EXPERTS, LANES))
    start = _excl_cumsum_rows(padded)
    end = start + padded
    seg_ref[...] = seg.astype(I32)
    loff_ref[...] = loff.astype(I32)
    gdst_ref[...] = (start + run).astype(I32)

    gr = lax.broadcasted_iota(I32, (LANES, t), 0)
    gc = lax.broadcasted_iota(I32, (LANES, t), 1) // MOE_SUB
    loff_tok = jnp.dot((loff * (1.0 / SEG_ALIGN)).astype(BF16), (gr == gc).astype(BF16),
                       preferred_element_type=F32) * SEG_ALIGN
    idx = idx_ref[...]
    slot = rank_ref[...]
    for e in range(N_EXPERTS):
        slot = slot + jnp.where(idx == e, loff_tok[e:e + 1, :].astype(I32), 0)
    slot_ref[...] = slot

    lane = lax.broadcasted_iota(I32, (N_EXPERTS, LANES), 1)
    total = jnp.max(end, axis=0, keepdims=True)
    tail = jnp.where(lane == 0, start + tot, jnp.where(lane == 1, end, jnp.where(lane == 3, start, total)))
    tail_ref[...] = tail.astype(I32)


def _tables(cnt, idx_t, rank_t):
    t = idx_t.shape[1]
    tab = jax.ShapeDtypeStruct((N_EXPERTS, LANES), I32)
    return pl.pallas_call(
        _tables_kernel,
        out_shape=(jax.ShapeDtypeStruct((TOP_K, t), I32), tab, tab, tab, tab),
        compiler_params=pltpu.CompilerParams(vmem_limit_bytes=VMEM_LIMIT),
        name="route_tables",
    )(cnt, idx_t, rank_t)


def _pow2_sizes(limit):
    size = SEG_ALIGN
    while size * 2 <= limit:
        size *= 2
    sizes = []
    while size >= SEG_ALIGN:
        sizes.append(size)
        size //= 2
    return tuple(sizes)


def _piece_copies(src_ref, src_off, dst_ref, dst_off, n, limit, sem, wait=False):
    off = 0
    for size in _pow2_sizes(limit):
        take = n & size

        @pl.when(take != 0)
        def _(off=off, size=size):
            cp = pltpu.make_async_copy(
                src_ref.at[pl.ds(pl.multiple_of(src_off + off, SEG_ALIGN), size), :],
                dst_ref.at[pl.ds(pl.multiple_of(dst_off + off, SEG_ALIGN), size), :], sem)
            if wait:
                cp.wait()
            else:
                cp.start()

        off = off + take


def _dispatch_kernel(seg_ref, loff_ref, gdst_ref, taillo_ref, tailhi_ref, used_ref,
                     h_ref, slot_ref, xs_ref, loc_scr, zero_scr, sem, zsem, *, nblk):
    i = pl.program_id(0)
    par = i % 2

    @pl.when(i == 0)
    def _():
        nblk_all = xs_ref.shape[0] // MOE_BLOCK
        zero_scr[...] = jnp.zeros_like(zero_scr)

        def blk_copy(b):
            start = pl.multiple_of(b * MOE_BLOCK, MOE_BLOCK)
            return pltpu.make_async_copy(zero_scr, xs_ref.at[pl.ds(start, MOE_BLOCK), :], zsem)

        def issue_blk(b, carry):
            blk_copy(b).start()
            return carry

        def drain_blk(b, carry):
            blk_copy(b).wait()
            return carry

        lax.fori_loop(used_ref[0], nblk_all, issue_blk, 0)
        lax.fori_loop(used_ref[0], nblk_all, drain_blk, 0)

        def tails(wait):
            def body(e, carry):
                lo = taillo_ref[e]
                _piece_copies(zero_scr, 0, xs_ref, lo, tailhi_ref[e] - lo, MOE_BLOCK - 1, zsem, wait=wait)
                return carry
            return body

        lax.fori_loop(0, N_EXPERTS, tails(False), 0)
        lax.fori_loop(0, N_EXPERTS, tails(True), 0)

    slots = slot_ref[...]
    rid = lax.broadcasted_iota(I32, (LOCAL_SLOTS, MOE_SUB), 0)
    hit = rid == slots[0:1, :]
    for k in range(1, TOP_K):
        hit = hit | (rid == slots[k:k + 1, :])
    loc_scr[par] = jnp.dot(hit.astype(BF16), h_ref[...], preferred_element_type=F32).astype(ROW_DTYPE)

    loc = loc_scr.at[par]

    def seg_body(e, total):
        n = seg_ref[e * LANES + i]
        _piece_copies(loc, loff_ref[e * LANES + i], xs_ref, gdst_ref[e * LANES + i], n, MOE_SUB, sem.at[par])
        return total + n

    total = lax.fori_loop(0, N_EXPERTS, seg_body, 0)
    spare = (nblk + par * (LOCAL_SLOTS // MOE_BLOCK)) * MOE_BLOCK
    _piece_copies(loc, total, xs_ref, spare + total, LOCAL_SLOTS - total, LOCAL_SLOTS, sem.at[par])

    def wait_step(p):
        pltpu.make_async_copy(loc_scr.at[p], xs_ref.at[pl.ds(0, LOCAL_SLOTS), :], sem.at[p]).wait()

    @pl.when(i > 0)
    def _():
        wait_step(1 - par)

    @pl.when(i == pl.num_programs(0) - 1)
    def _():
        wait_step(par)


def _dispatch(seg, loff, gdst, tail_lo, tail_hi, used, h2, slot_t, nblk):
    t, d = h2.shape
    rows = (nblk + 2 * (LOCAL_SLOTS // MOE_BLOCK)) * MOE_BLOCK
    return pl.pallas_call(
        functools.partial(_dispatch_kernel, nblk=nblk),
        grid_spec=pltpu.PrefetchScalarGridSpec(
            num_scalar_prefetch=6,
            grid=(t // MOE_SUB,),
            in_specs=[pl.BlockSpec((MOE_SUB, d), lambda i, *_: (i, 0)),
                      pl.BlockSpec((TOP_K, MOE_SUB), lambda i, *_: (0, i))],
            out_specs=pl.BlockSpec(memory_space=pl.ANY),
            scratch_shapes=[pltpu.VMEM((2, LOCAL_SLOTS, d), ROW_DTYPE), pltpu.VMEM((MOE_BLOCK, d), ROW_DTYPE),
                            pltpu.SemaphoreType.DMA((2,)), pltpu.SemaphoreType.DMA(())]),
        out_shape=jax.ShapeDtypeStruct((rows, d), ROW_DTYPE),
        compiler_params=_cparams(("arbitrary",)),
        name="moe_dispatch",
    )(seg, loff, gdst, tail_lo, tail_hi, used, h2, slot_t)


def _expert_kernel(lo_ref, hi_ref, used_ref, x_ref, wgu_ref, bgu_ref, wd_ref, bd_ref, y_ref,
                   wgu_scr, wd_scr, xbuf, ybuf, xsem, ysem):
    e = pl.program_id(0)
    ne = pl.num_programs(0)
    ff = wd_ref.shape[1]
    big = 2 * MOE_BLOCK

    def layout(ex):
        first = lo_ref[ex] // MOE_BLOCK
        n = (hi_ref[ex] - lo_ref[ex]) // MOE_BLOCK
        return first, n // 2, n // 2 + n % 2

    cur = layout(e)
    first, npair, items = cur

    def x_copy(lay, k, rows):
        start = pl.multiple_of((lay[0] + 2 * k) * MOE_BLOCK, MOE_BLOCK)
        return pltpu.make_async_copy(x_ref.at[pl.ds(start, rows), :], xbuf.at[k % 2, pl.ds(0, rows), :],
                                     xsem.at[k % 2])

    def y_copy(k, rows):
        start = pl.multiple_of((first + 2 * k) * MOE_BLOCK, MOE_BLOCK)
        return pltpu.make_async_copy(ybuf.at[k % 2, pl.ds(0, rows), :], y_ref.at[pl.ds(start, rows), :],
                                     ysem.at[k % 2])

    def start_x(lay, k):
        @pl.when(k < lay[1])
        def _():
            x_copy(lay, k, big).start()

        @pl.when((k == lay[1]) & (k < lay[2]))
        def _():
            x_copy(lay, k, MOE_BLOCK).start()

    def compute(k, rows):
        slot = k % 2
        xb = xbuf[slot, 0:rows, :].astype(BF16)
        gu = jnp.dot(xb, wgu_scr[...], preferred_element_type=F32) + bgu_ref[0]
        glu = jnp.minimum(gu[:, :ff], SWIGLU_LIMIT)
        lin = jnp.clip(gu[:, ff:], -SWIGLU_LIMIT, SWIGLU_LIMIT)
        act = glu * _sigmoid(SWIGLU_ALPHA * glu) * (lin + 1.0)
        y = jnp.dot(act.astype(BF16), wd_scr[...], preferred_element_type=F32) + bd_ref[0]
        ybuf[slot, 0:rows, :] = y.astype(ybuf.dtype)

    def item(k, rows):
        x_copy(cur, k, rows).wait()

        @pl.when(k >= 1)
        def _():
            start_x(cur, k + 1)

        @pl.when(k >= 2)
        def _():
            y_copy(k - 2, big).wait()

        compute(k, rows)
        y_copy(k, rows).start(priority=1)

    @pl.when(e == 0)
    def _():
        start_x(cur, 0)
        start_x(cur, 1)

    @pl.when(items > 0)
    def _():
        wgu_scr[...] = wgu_ref[0].astype(BF16)
        wd_scr[...] = wd_ref[0].astype(BF16)

    def body(k, carry):
        item(k, big)
        return carry

    lax.fori_loop(0, npair, body, 0)

    @pl.when(items > npair)
    def _():
        item(npair, MOE_BLOCK)

    @pl.when(items >= 2)
    def _():
        y_copy(items - 2, big).wait()

    @pl.when(items > npair)
    def _():
        y_copy(items - 1, MOE_BLOCK).wait()

    @pl.when((items == npair) & (items >= 1))
    def _():
        y_copy(items - 1, big).wait()

    @pl.when(e + 1 < ne)
    def _():
        nxt = layout(jnp.minimum(e + 1, ne - 1))
        start_x(nxt, 0)
        start_x(nxt, 1)

    @pl.when(e == pl.num_programs(0) - 1)
    def _():
        nblk_all = y_ref.shape[0] // MOE_BLOCK
        ybuf[0] = jnp.zeros(ybuf.shape[1:], ybuf.dtype)

        def zero_copy(b):
            start = pl.multiple_of(b * MOE_BLOCK, MOE_BLOCK)
            return pltpu.make_async_copy(ybuf.at[0, pl.ds(0, MOE_BLOCK), :],
                                         y_ref.at[pl.ds(start, MOE_BLOCK), :], ysem.at[0])

        def issue(b, carry):
            zero_copy(b).start()
            return carry

        def drain(b, carry):
            zero_copy(b).wait()
            return carry

        lax.fori_loop(used_ref[0], nblk_all, issue, 0)
        lax.fori_loop(used_ref[0], nblk_all, drain, 0)


def _experts(row_lo, row_hi, used, xs, w_gate_up, b_gate_up, w_down, b_down):
    rows, d = xs.shape
    ne, _, ff2 = w_gate_up.shape
    ff = ff2 // 2
    return pl.pallas_call(
        _expert_kernel,
        grid_spec=pltpu.PrefetchScalarGridSpec(
            num_scalar_prefetch=3,
            grid=(ne,),
            in_specs=[pl.BlockSpec(memory_space=pl.ANY),
                      pl.BlockSpec((1, d, ff2), lambda e, *_: (e, 0, 0)),
                      pl.BlockSpec((1, 1, ff2), lambda e, *_: (e, 0, 0)),
                      pl.BlockSpec((1, ff, d), lambda e, *_: (e, 0, 0)),
                      pl.BlockSpec((1, 1, d), lambda e, *_: (e, 0, 0))],
            out_specs=pl.BlockSpec(memory_space=pl.ANY),
            scratch_shapes=[pltpu.VMEM((d, ff2), BF16), pltpu.VMEM((ff, d), BF16),
                            pltpu.VMEM((2, 2 * MOE_BLOCK, d), ROW_DTYPE),
                            pltpu.VMEM((2, 2 * MOE_BLOCK, d), ROW_DTYPE),
                            pltpu.SemaphoreType.DMA((2,)), pltpu.SemaphoreType.DMA((2,))]),
        out_shape=jax.ShapeDtypeStruct((rows, d), ROW_DTYPE),
        compiler_params=_cparams(("arbitrary",)),
        name="moe_experts",
    )(row_lo, row_hi, used, xs, w_gate_up, b_gate_up.reshape(ne, 1, ff2), w_down, b_down.reshape(ne, 1, d))


def _combine_kernel(seg_ref, loff_ref, gdst_ref, x1_ref, slot_ref, gates_ref, ys_ref, o_ref, loc_scr, sem):
    i = pl.program_id(0)
    par = i % 2

    def fetch(g, p):
        loc = loc_scr.at[p]

        def seg_body(e, total):
            n = seg_ref[e * LANES + g]
            _piece_copies(ys_ref, gdst_ref[e * LANES + g], loc, loff_ref[e * LANES + g], n, MOE_SUB, sem.at[p])
            return total + n

        total = lax.fori_loop(0, N_EXPERTS, seg_body, 0)
        _piece_copies(ys_ref, total, loc, total, LOCAL_SLOTS - total, LOCAL_SLOTS, sem.at[p])

    @pl.when(i == 0)
    def _():
        fetch(0, 0)

    @pl.when(i + 1 < pl.num_programs(0))
    def _():
        fetch(i + 1, 1 - par)

    pltpu.make_async_copy(ys_ref.at[pl.ds(0, LOCAL_SLOTS), :], loc_scr.at[par], sem.at[par]).wait()

    slots = slot_ref[...]
    g = gates_ref[...]
    rid = lax.broadcasted_iota(I32, (MOE_SUB, LOCAL_SLOTS), 1)
    gmat = jnp.where(rid == slots[:, 0:1], g[:, 0:1], 0.0)
    for k in range(1, TOP_K):
        gmat = gmat + jnp.where(rid == slots[:, k:k + 1], g[:, k:k + 1], 0.0)
    o_ref[...] = x1_ref[...] + jnp.dot(gmat.astype(BF16), loc_scr[par].astype(BF16),
                                       preferred_element_type=F32)


def _combine(seg, loff, gdst, x1, slot_tk, gates_tk, ys):
    t, d = x1.shape
    tok = lambda w: pl.BlockSpec((MOE_SUB, w), lambda i, *_: (i, 0))
    return pl.pallas_call(
        _combine_kernel,
        grid_spec=pltpu.PrefetchScalarGridSpec(
            num_scalar_prefetch=3,
            grid=(t // MOE_SUB,),
            in_specs=[tok(d), tok(TOP_K), tok(TOP_K), pl.BlockSpec(memory_space=pl.ANY)],
            out_specs=tok(d),
            scratch_shapes=[pltpu.VMEM((2, LOCAL_SLOTS, d), ROW_DTYPE), pltpu.SemaphoreType.DMA((2,))]),
        out_shape=jax.ShapeDtypeStruct((t, d), F32),
        compiler_params=_cparams(("arbitrary",)),
        name="moe_combine",
    )(seg, loff, gdst, x1, slot_tk, gates_tk, ys)


def _rope_tables(s):
    pos = np.arange(s, dtype=np.float64)
    inv = ROPE_BASE ** (-np.arange(0, MLA_ROPE, 2, dtype=np.float64) / MLA_ROPE)
    ang = pos[:, None] * inv[None, :]
    cos, sin = jnp.asarray(np.cos(ang), F32), jnp.asarray(np.sin(ang), F32)
    return _spread_rope(jnp.concatenate([cos, cos], axis=-1)), _spread_rope(jnp.concatenate([-sin, sin], axis=-1))


def _spread_rope(w, gap=None):
    half = MLA_ROPE // 2
    zeros = jnp.zeros(w.shape[:-1] + (half,), w.dtype)
    return jnp.concatenate([w[..., :half], zeros if gap is None else gap, w[..., half:], zeros], axis=-1)


def _pad_heads(w):
    lead = w.shape[:-1]
    w = w.reshape(lead + (MLA_HEADS, MLA_QK_DIM))
    w = jnp.concatenate([w[..., :MLA_NOPE], _spread_rope(w[..., MLA_NOPE:])], axis=-1)
    return w.reshape(lead + (MLA_HEADS * 2 * LANES,))


def _layer(x, g_mix, w_in, conv_w, conv_b, dt_bias, a_log, d_skip, g_ssd_out, w_ssd_out,
           g_q_lat, w_q_up, g_kv_lat, w_kv_up, g_qk_q, g_qk_k, w_mla_out, w_o,
           g_ffn, w_router, b_router, w_gate_up, b_gate_up, w_down, b_down):
    b, s, d = x.shape
    t = b * s
    x2 = x.reshape(t, d)

    off_xbc = SSD_D_INNER
    off_dt = off_xbc + SSD_CONV_DIM
    off_ql = off_dt + SSD_HEADS
    off_kvl = off_ql + MLA_Q_RANK
    off_kr = off_kvl + MLA_KV_RANK
    off_gate = off_kr + MLA_ROPE
    dt_gap = jnp.pad(w_in[:, off_dt:off_ql], ((0, 0), (0, MLA_ROPE // 2 - SSD_HEADS)))
    w_streams = [w_in[:, :off_xbc], w_in[:, off_xbc:off_dt], w_in[:, off_ql:off_kvl], w_in[:, off_kvl:off_kr],
                 _spread_rope(w_in[:, off_kr:off_gate], gap=dt_gap), w_in[:, off_gate:]]

    tiles = _tile_sizes(s)
    z, xbc, ql, kvl, krdt, gate = _in_proj(x2, g_mix, [w.astype(BF16) for w in w_streams], tiles.tokens)

    lane_pad = lambda vec: jnp.pad(vec, (DT_LO, LANES - DT_LO - SSD_HEADS)).reshape(1, LANES)
    lt = tiles.ssd
    y_ssd = _ssd(xbc.reshape(b, s, SSD_CONV_DIM), z.reshape(b, s, SSD_D_INNER), krdt.reshape(b, s, LANES),
                 conv_w, conv_b.reshape(1, -1), lane_pad(dt_bias), lane_pad(a_log),
                 jnp.repeat(d_skip, SSD_HEAD_DIM).reshape(1, -1), g_ssd_out.reshape(1, -1), lt)

    cos_t, sin_t = _rope_tables(s)
    wq_pad = _pad_heads(w_q_up).astype(BF16)
    spread_gain = lambda g: jnp.concatenate([g[:MLA_NOPE], _spread_rope(g[MLA_NOPE:])]).reshape(1, -1)
    gq_pad = spread_gain(g_qk_q)
    gk_pad = spread_gain(g_qk_k)
    tq = tiles.tokens
    w_kv_h = w_kv_up.reshape(MLA_KV_RANK, MLA_HEADS, MLA_NOPE + MLA_V)
    wk = w_kv_h[:, :, :MLA_NOPE].reshape(MLA_KV_RANK, MLA_HEADS * MLA_NOPE).astype(BF16)
    wvt = w_kv_h[:, :, MLA_NOPE:].reshape(MLA_KV_RANK, MLA_HEADS * MLA_V).T.astype(BF16)
    q, k, v = _qkv_prep(ql, kvl, krdt, cos_t, sin_t, g_q_lat.reshape(1, -1), wq_pad,
                        g_kv_lat.reshape(1, -1), wk, wvt, gq_pad, gk_pad, b, s, tq)
    o_mla = _flash(q, k, v, tiles.queries)

    x1, h2, idx_t, gates_t, rank_t, cnt = _merge(
        x2, y_ssd.reshape(t, d), o_mla.reshape(t, d), gate,
        w_ssd_out.astype(BF16), w_mla_out.astype(BF16), w_o.astype(BF16),
        g_ffn.reshape(1, -1), w_router.T, b_router.reshape(-1, 1), tiles.tokens)

    nsub = t // MOE_SUB
    assert t % MOE_SUB == 0 and nsub <= LANES
    cap = t * TOP_K + nsub * N_EXPERTS * (SEG_ALIGN - 1) + N_EXPERTS * (MOE_BLOCK - 1)
    nblk = -(-cap // MOE_BLOCK)
    slot_t, seg, loff, gdst, tail = _tables(cnt, idx_t, rank_t)
    used = (tail[0:1, 2] // MOE_BLOCK).astype(I32)
    seg, loff, gdst = seg.reshape(-1), loff.reshape(-1), gdst.reshape(-1)
    xs = _dispatch(seg, loff, gdst, tail[:, 0], tail[:, 1], used, h2, slot_t, nblk)
    ys = _experts(tail[:, 3], tail[:, 1], used, xs, w_gate_up, b_gate_up, w_down, b_down)
    out = _combine(seg, loff, gdst, x1, slot_t.T, gates_t.T, ys)
    return out.reshape(b, s, d)


def kernel(x, g_mix, w_in, conv_w, conv_b, dt_bias, a_log, d_skip, g_ssd_out, w_ssd_out, g_q_lat, w_q_up, g_kv_lat, w_kv_up, g_qk_q, g_qk_k, w_mla_out, w_o, g_ffn, w_router, b_router, w_gate_up, b_gate_up, w_down, b_down):
    params = (g_mix, w_in, conv_w, conv_b, dt_bias, a_log, d_skip, g_ssd_out, w_ssd_out, g_q_lat, w_q_up,
              g_kv_lat, w_kv_up, g_qk_q, g_qk_k, w_mla_out, w_o, g_ffn, w_router, b_router,
              w_gate_up, b_gate_up, w_down, b_down)
    for l in range(g_mix.shape[0]):
        x = _layer(x, *(p[l] for p in params))
    return x
```
